```python
import math
import jax, jax.numpy as jnp
from jax import lax
import numpy as np

D_MODEL = 1024
BATCH = 8
SEQ = 2048
DEPTH = 2

N_MIXERS = 2
HEAD_DIM = 64
ATTN_HEADS = D_MODEL // HEAD_DIM
ATTN_WIDTH = ATTN_HEADS * HEAD_DIM
DILATED_GROUPS = ((128, 1), (512, 4), (2048, 16))
N_GROUPS_A = len(DILATED_GROUPS)
ATTN_IN_COLS = 3 * N_GROUPS_A * ATTN_WIDTH + ATTN_WIDTH
REL_BUCKETS = 32
REL_MAX_DIST = 2048
SSM_WIDTH = D_MODEL
SSM_GROUP = 16
SSM_N_GROUPS = SSM_WIDTH // SSM_GROUP
SSM_STATE = 64
N_ATTN_LAYERS = (DEPTH + 1) // 2
N_SSM_LAYERS = DEPTH // 2
EPS = 1e-6
NEG = -1e30

kernel_name = "hybrid_dilated_attn_s5_interleaved"


def rms_norm(x, g):
    xf = x.astype(jnp.float32)
    y = xf * lax.rsqrt(jnp.mean(xf * xf, axis=-1, keepdims=True) + EPS) * g.astype(jnp.float32)
    return y.astype(x.dtype)


def t5_bucket(dist):
    max_exact = REL_BUCKETS // 2
    n = jnp.maximum(dist, 1).astype(jnp.float32)
    large = max_exact + (jnp.log(n / max_exact) / math.log(REL_MAX_DIST / max_exact)
                         * (REL_BUCKETS - max_exact)).astype(jnp.int32)
    large = jnp.minimum(large, REL_BUCKETS - 1)
    return jnp.where(dist < max_exact, dist, large)


def dilated_group_attention(q, k, v, rel_bias, window, dilation):
    B, S, H, E = q.shape
    steps = window // dilation
    blk = steps
    L = S // dilation
    nb = -(-L // blk)
    Lp = nb * blk

    def to_sub(t):
        t = t.reshape(B, L, dilation, H, E)
        return jnp.pad(t, ((0, 0), (0, Lp - L), (0, 0), (0, 0), (0, 0)))

    qs = to_sub(q).reshape(B, nb, blk, dilation, H, E)
    front = ((0, 0), (blk, 0), (0, 0), (0, 0), (0, 0))
    ks = jnp.pad(to_sub(k), front).reshape(B, nb + 1, blk, dilation, H, E)
    vs = jnp.pad(to_sub(v), front).reshape(B, nb + 1, blk, dilation, H, E)
    kw = jnp.concatenate([ks[:, :-1], ks[:, 1:]], axis=2)
    vw = jnp.concatenate([vs[:, :-1], vs[:, 1:]], axis=2)

    i = jnp.arange(blk)[:, None]
    j = jnp.arange(2 * blk)[None, :]
    back = blk + i - j
    band = (back >= 0) & (back <= steps)
    valid = band[None] & ((jnp.arange(nb)[:, None, None] > 0) | (j[None] >= blk))
    bias = rel_bias[t5_bucket(jnp.maximum(back, 0) * dilation)]
    bias = jnp.transpose(bias, (2, 0, 1)).astype(jnp.float32)

    scale = HEAD_DIM ** -0.5
    logits = jnp.einsum('bnqrhe,bnkrhe->bnrhqk', qs, kw,
                        preferred_element_type=jnp.float32) * scale + bias
    logits = jnp.where(valid[None, :, None, None], logits, NEG)
    lse = jax.nn.logsumexp(logits, axis=-1)
    p = jnp.exp(logits - lse[..., None])
    o = jnp.einsum('bnrhqk,bnkrhe->bnqrhe', p.astype(v.dtype), vw)
    o = o.reshape(B, Lp, dilation, H, E)[:, :L].reshape(B, S, H, E)
    lse = jnp.transpose(lse, (0, 1, 4, 2, 3)).reshape(B, Lp, dilation, H)[:, :L].reshape(B, S, H)
    return o, lse


def dilated_attention_mixer(h, w_in, w_out, rel_bias):
    B, S, _ = h.shape
    proj = h @ w_in
    n_qkv = 3 * N_GROUPS_A * ATTN_WIDTH
    qkv = proj[..., :n_qkv].reshape(B, S, 3, N_GROUPS_A, ATTN_HEADS, HEAD_DIM)
    z = proj[..., n_qkv:]
    outs, lses = [], []
    for g, (window, dilation) in enumerate(DILATED_GROUPS):
        o, l = dilated_group_attention(qkv[:, :, 0, g], qkv[:, :, 1, g], qkv[:, :, 2, g],
                                       rel_bias, window, dilation)
        outs.append(o)
        lses.append(l)
    wts = jax.nn.softmax(jnp.stack(lses), axis=0)
    o = jnp.sum(wts[..., None] * jnp.stack(outs).astype(jnp.float32), axis=0)
    o = o.reshape(B, S, ATTN_WIDTH).astype(h.dtype)
    return (o * jax.nn.silu(z)) @ w_out


def s5_scan(u, a_re, a_im, log_dt, b_re, b_im, c_re, c_im, d_skip):
    B, S, _ = u.shape
    f = jnp.float32
    uf = u.astype(f).reshape(B, S, SSM_N_GROUPS, SSM_GROUP)
    A = lax.complex(a_re.astype(f), a_im.astype(f))
    dt = jnp.exp(log_dt.astype(f))[:, None]
    a_bar = jnp.exp(A * dt)
    Bm = lax.complex(b_re.astype(f), b_im.astype(f))
    b_bar = ((a_bar - 1.0) / A)[..., None] * Bm
    bu = jnp.einsum('bsgc,gpc->bsgp', uf.astype(jnp.complex64), b_bar)
    a_elems = jnp.broadcast_to(a_bar, bu.shape)

    def combine(left, right):
        return (right[0] * left[0], right[0] * left[1] + right[1])

    _, states = lax.associative_scan(combine, (a_elems, bu), axis=1)
    Cm = lax.complex(c_re.astype(f), c_im.astype(f))
    y = jnp.einsum('bsgp,gcp->bsgc', states, Cm).real
    y = y + d_skip.astype(f).reshape(SSM_N_GROUPS, SSM_GROUP) * uf
    return y.reshape(B, S, SSM_WIDTH).astype(u.dtype)


def s5_mixer(h, w_in, a_re, a_im, log_dt, b_re, b_im, c_re, c_im, d_skip, w_glu, b_glu, w_out):
    proj = h @ w_in
    u, z = proj[..., :SSM_WIDTH], proj[..., SSM_WIDTH:]
    y = s5_scan(u, a_re, a_im, log_dt, b_re, b_im, c_re, c_im, d_skip)
    g = jax.nn.gelu(y)
    y = g * jax.nn.sigmoid(g @ w_glu + b_glu)
    return (y * jax.nn.silu(z)) @ w_out


def setup_inputs(seed: int = 0) -> dict:
    key = jax.random.key(seed)
    ks = jax.random.split(key, 24)
    nrm = jax.random.normal
    f = jnp.float32
    NA, NS, D = N_ATTN_LAYERS, N_SSM_LAYERS, D_MODEL
    G, P, C, E = SSM_N_GROUPS, SSM_STATE, SSM_GROUP, SSM_WIDTH
    x = nrm(ks[0], (BATCH, SEQ, D), f)
    rel_bias = 0.5 * nrm(ks[1], (REL_BUCKETS, ATTN_HEADS), f)
    attn_pre_norm = 1.0 + 0.05 * nrm(ks[2], (NA, D), f)
    attn_w_in = nrm(ks[3], (NA, D, ATTN_IN_COLS), f) * D ** -0.5
    attn_w_out = nrm(ks[4], (NA, ATTN_WIDTH, D), f) * ATTN_WIDTH ** -0.5
    attn_post_norm = 1.0 + 0.05 * nrm(ks[5], (NA, D), f)
    ssm_pre_norm = 1.0 + 0.05 * nrm(ks[6], (NS, D), f)
    ssm_w_in = nrm(ks[7], (NS, D, 2 * E), f) * D ** -0.5
    ssm_a_re = -0.5 + 0.01 * nrm(ks[8], (NS, G, P), f)
    ssm_a_im = math.pi * jnp.arange(P, dtype=f)[None, None, :] + 0.01 * nrm(ks[9], (NS, G, P), f)
    ssm_log_dt = jax.random.uniform(ks[10], (NS, G), f, math.log(1e-3), math.log(1e-1))
    ssm_b_re = nrm(ks[11], (NS, G, P, C), f) * (2 * C) ** -0.5
    ssm_b_im = nrm(ks[12], (NS, G, P, C), f) * (2 * C) ** -0.5
    ssm_c_re = nrm(ks[13], (NS, G, C, P), f) * (2 * P) ** -0.5
    ssm_c_im = nrm(ks[14], (NS, G, C, P), f) * (2 * P) ** -0.5
    ssm_d = nrm(ks[15], (NS, E), f)
    ssm_w_glu = nrm(ks[16], (NS, E, E), f) * E ** -0.5
    ssm_b_glu = 0.02 * nrm(ks[17], (NS, E), f)
    ssm_w_out = nrm(ks[18], (NS, E, D), f) * E ** -0.5
    ssm_post_norm = 1.0 + 0.05 * nrm(ks[19], (NS, D), f)
    return {"x": x, "rel_bias": rel_bias,
            "attn_pre_norm": attn_pre_norm, "attn_w_in": attn_w_in,
            "attn_w_out": attn_w_out, "attn_post_norm": attn_post_norm,
            "ssm_pre_norm": ssm_pre_norm, "ssm_w_in": ssm_w_in,
            "ssm_a_re": ssm_a_re, "ssm_a_im": ssm_a_im, "ssm_log_dt": ssm_log_dt,
            "ssm_b_re": ssm_b_re, "ssm_b_im": ssm_b_im,
            "ssm_c_re": ssm_c_re, "ssm_c_im": ssm_c_im, "ssm_d": ssm_d,
            "ssm_w_glu": ssm_w_glu, "ssm_b_glu": ssm_b_glu,
            "ssm_w_out": ssm_w_out, "ssm_post_norm": ssm_post_norm}


def reference(x, rel_bias, attn_pre_norm, attn_w_in, attn_w_out, attn_post_norm,
              ssm_pre_norm, ssm_w_in, ssm_a_re, ssm_a_im, ssm_log_dt,
              ssm_b_re, ssm_b_im, ssm_c_re, ssm_c_im, ssm_d,
              ssm_w_glu, ssm_b_glu, ssm_w_out, ssm_post_norm):
    for i in range(DEPTH):
        j = i // N_MIXERS
        if i % N_MIXERS == 0:
            h = rms_norm(x, attn_pre_norm[j])
            h = dilated_attention_mixer(h, attn_w_in[j], attn_w_out[j], rel_bias)
            x = x + rms_norm(h, attn_post_norm[j])
        else:
            h = rms_norm(x, ssm_pre_norm[j])
            h = s5_mixer(h, ssm_w_in[j], ssm_a_re[j], ssm_a_im[j], ssm_log_dt[j],
                         ssm_b_re[j], ssm_b_im[j], ssm_c_re[j], ssm_c_im[j], ssm_d[j],
                         ssm_w_glu[j], ssm_b_glu[j], ssm_w_out[j])
            x = x + rms_norm(h, ssm_post_norm[j])
    return x
```

```python
import functools
import math

import numpy as np
import jax
import jax.numpy as jnp
from jax import lax
from jax.experimental import pallas as pl
from jax.experimental.pallas import tpu as pltpu

F32 = jnp.float32
BF16 = jnp.bfloat16

D_MODEL = 1024
HEAD_DIM = 64
HEADS = 16
N_PHASE = 16
BLK = 128
DILATIONS = (1, 4, 16)
REL_BUCKETS = 32
REL_MAX_DIST = 2048
SSM_GROUP = 16
SSM_N_GROUPS = 64
SSM_STATE = 64
EPS = 1e-6
NEG = -1e30
VMEM_LIMIT = 56 * 1024 * 1024


def _cparams(sem):
    return pltpu.CompilerParams(dimension_semantics=sem, vmem_limit_bytes=VMEM_LIMIT)


def _norm_proj_kernel(x_ref, g_ref, w_ref, o_ref, h_ref, *, mh):
    @pl.when(pl.program_id(2) == 0)
    def _():
        g = g_ref[...]
        for r in range(N_PHASE):
            xr = x_ref[:, r * D_MODEL:(r + 1) * D_MODEL]
            ms = jnp.mean(xr * xr, axis=-1, keepdims=True)
            h_ref[r * mh:(r + 1) * mh, :] = (xr * lax.rsqrt(ms + EPS) * g).astype(BF16)

    res = jnp.dot(h_ref[...], w_ref[...], preferred_element_type=F32)
    o_ref[...] = res.reshape(N_PHASE, mh, res.shape[-1]).astype(BF16)


def _norm_proj(x, g, w, *, mh=64, tn=1024):
    B, S, D = x.shape
    M = S // N_PHASE
    N = w.shape[1]
    xv = x.reshape(B, M, N_PHASE * D)
    return pl.pallas_call(
        functools.partial(_norm_proj_kernel, mh=mh),
        grid=(B, M // mh, N // tn),
        in_specs=[
            pl.BlockSpec((None, mh, N_PHASE * D), lambda b, m, j: (b, m, 0)),
            pl.BlockSpec((1, D), lambda b, m, j: (0, 0)),
            pl.BlockSpec((D, tn), lambda b, m, j: (0, j)),
        ],
        out_specs=pl.BlockSpec((None, N_PHASE, mh, tn), lambda b, m, j: (b, 0, m, j)),
        out_shape=jax.ShapeDtypeStruct((B, N_PHASE, M, N), BF16),
        scratch_shapes=[pltpu.VMEM((N_PHASE * mh, D), BF16)],
        compiler_params=_cparams(("parallel", "parallel", "arbitrary")),
        name="attn_norm_proj",
    )(xv, g.reshape(1, D), w)


def _attn_kernel(q0_ref, q1_ref, q2_ref, k0_ref, k1_ref, k2_ref, v0_ref, v1_ref, v2_ref,
                 z_ref, bm_ref, o_ref, qf_ref, kf_ref, vf_ref, og_ref, lse_ref):
    lane = lax.broadcasted_iota(jnp.int32, (BLK, 2 * HEAD_DIM), 1)
    first_head = lane < HEAD_DIM

    def block_attn(q, k, v, g, cur_only):
        outs, lses = [], []
        for hh in range(2):
            sel = first_head if hh == 0 else jnp.logical_not(first_head)
            qm = jnp.where(sel, q, jnp.zeros_like(q))
            s = lax.dot_general(qm, k, (((1,), (1,)), ((), ())), preferred_element_type=F32)
            if cur_only:
                s = s + bm_ref[g, hh, :, BLK:2 * BLK]
            else:
                s = s + bm_ref[g, hh]
            m = jnp.max(s, axis=-1, keepdims=True)
            p = jnp.exp(s - m)
            l = jnp.sum(p, axis=-1, keepdims=True)
            o = jnp.dot(p.astype(BF16), v, preferred_element_type=F32)
            outs.append(o / l)
            lses.append(m + jnp.log(l))
        o = jnp.where(first_head, outs[0], outs[1])
        lse = jnp.where(first_head, lses[0], lses[1])
        return o, lse

    scale = jnp.asarray(HEAD_DIM ** -0.5, F32)

    def g2_body(r, carry):
        q = (q2_ref[r].astype(F32) * scale).astype(BF16)
        o, lse = block_attn(q, k2_ref[r], v2_ref[r], 2, True)
        og_ref[2, r] = o
        lse_ref[2, r] = lse
        return carry

    lax.fori_loop(0, N_PHASE, g2_body, 0)

    def g1_rows(ref, r4, n):
        off = pl.multiple_of(n * 32, 32)
        return jnp.concatenate([ref[4 * q4 + r4, pl.ds(off, 32), :] for q4 in range(4)], axis=0)

    def g1_store(r4, n, o, lse):
        off = pl.multiple_of(n * 32, 32)
        for q4 in range(4):
            og_ref[1, 4 * q4 + r4, pl.ds(off, 32), :] = o[32 * q4:32 * q4 + 32]
            lse_ref[1, 4 * q4 + r4, pl.ds(off, 32), :] = lse[32 * q4:32 * q4 + 32]

    def g1_first(r4, carry):
        q = (g1_rows(q1_ref, r4, 0).astype(F32) * scale).astype(BF16)
        o, lse = block_attn(q, g1_rows(k1_ref, r4, 0), g1_rows(v1_ref, r4, 0), 1, True)
        g1_store(r4, 0, o, lse)
        return carry

    lax.fori_loop(0, 4, g1_first, 0)

    def g1_body(i, carry):
        r4 = i % 4
        n = i // 4 + 1
        q = (g1_rows(q1_ref, r4, n).astype(F32) * scale).astype(BF16)
        k = jnp.concatenate([g1_rows(k1_ref, r4, n - 1), g1_rows(k1_ref, r4, n)], axis=0)
        v = jnp.concatenate([g1_rows(v1_ref, r4, n - 1), g1_rows(v1_ref, r4, n)], axis=0)
        o, lse = block_attn(q, k, v, 1, False)
        g1_store(r4, n, o, lse)
        return carry

    lax.fori_loop(0, 12, g1_body, 0)

    def upcast(r, carry):
        qf_ref[r] = q0_ref[r].astype(F32) * scale
        kf_ref[r] = k0_ref[r].astype(F32)
        vf_ref[r] = v0_ref[r].astype(F32)
        return carry

    lax.fori_loop(0, N_PHASE, upcast, 0)

    def g0_rows(ref, n):
        off = pl.multiple_of(n * 8, 8)
        return jnp.concatenate([ref[r, pl.ds(off, 8), :] for r in range(N_PHASE)], axis=0)

    def g0_store(n, o, lse):
        off = pl.multiple_of(n * 8, 8)
        for r in range(N_PHASE):
            og_ref[0, r, pl.ds(off, 8), :] = o[8 * r:8 * r + 8]
            lse_ref[0, r, pl.ds(off, 8), :] = lse[8 * r:8 * r + 8]

    o, lse = block_attn(g0_rows(qf_ref, 0).astype(BF16), g0_rows(kf_ref, 0).astype(BF16),
                        g0_rows(vf_ref, 0).astype(BF16), 0, True)
    g0_store(0, o, lse)

    def g0_body(n, carry):
        q = g0_rows(qf_ref, n).astype(BF16)
        k = jnp.concatenate([g0_rows(kf_ref, n - 1), g0_rows(kf_ref, n)], axis=0).astype(BF16)
        v = jnp.concatenate([g0_rows(vf_ref, n - 1), g0_rows(vf_ref, n)], axis=0).astype(BF16)
        o, lse = block_attn(q, k, v, 0, False)
        g0_store(n, o, lse)
        return carry

    lax.fori_loop(1, N_PHASE, g0_body, 0)

    def combine(r, carry):
        l0, l1, l2 = lse_ref[0, r], lse_ref[1, r], lse_ref[2, r]
        mx = jnp.maximum(jnp.maximum(l0, l1), l2)
        w0, w1, w2 = jnp.exp(l0 - mx), jnp.exp(l1 - mx), jnp.exp(l2 - mx)
        o = (w0 * og_ref[0, r] + w1 * og_ref[1, r] + w2 * og_ref[2, r]) / (w0 + w1 + w2)
        z = z_ref[r].astype(F32)
        o_ref[r] = (o * (z * jax.nn.sigmoid(z))).astype(BF16)
        return carry

    lax.fori_loop(0, N_PHASE, combine, 0)


def _attention(P, bm):
    B, _, M, _ = P.shape
    HP = HEADS // 2
    W = 2 * HEAD_DIM

    def spec(kind, g):
        base = (kind * 3 + g) * HP
        return pl.BlockSpec((None, N_PHASE, M, W), lambda b, hp, base=base: (b, 0, 0, base + hp))

    in_specs = [spec(kind, g) for kind in range(3) for g in range(3)]
    in_specs.append(pl.BlockSpec((None, N_PHASE, M, W), lambda b, hp: (b, 0, 0, 9 * HP + hp)))
    in_specs.append(pl.BlockSpec((3, 2, BLK, 2 * BLK), lambda b, hp: (0, hp, 0, 0)))
    return pl.pallas_call(
        _attn_kernel,
        grid=(B, HP),
        in_specs=in_specs,
        out_specs=pl.BlockSpec((None, N_PHASE, M, W), lambda b, hp: (b, 0, 0, hp)),
        out_shape=jax.ShapeDtypeStruct((B, N_PHASE, M, HEADS * HEAD_DIM), BF16),
        scratch_shapes=[pltpu.VMEM((N_PHASE, M, W), F32) for _ in range(3)]
        + [pltpu.VMEM((3, N_PHASE, M, W), F32) for _ in range(2)],
        compiler_params=_cparams(("parallel", "parallel")),
        name="dilated_attention",
    )(*([P] * 10), bm)


def _t5_bucket(dist):
    max_exact = REL_BUCKETS // 2
    n = jnp.maximum(dist, 1).astype(F32)
    large = max_exact + (jnp.log(n / max_exact) / math.log(REL_MAX_DIST / max_exact)
                         * (REL_BUCKETS - max_exact)).astype(jnp.int32)
    large = jnp.minimum(large, REL_BUCKETS - 1)
    return jnp.where(dist < max_exact, dist, large)


def _bias_tables(rel_bias):
    a = np.arange(BLK)
    pos = (16 * (a % 8) + a // 8, 4 * (a % 32) + a // 32, a)
    tables = []
    for g, d in enumerate(DILATIONS):
        p = pos[g]
        back = np.concatenate([BLK + p[:, None] - p[None, :], p[:, None] - p[None, :]], axis=1)
        valid = (back >= 0) & (back <= BLK)
        tb = rel_bias[_t5_bucket(jnp.arange(BLK + 1, dtype=jnp.int32) * d)]
        t = jnp.take(tb, jnp.asarray(np.clip(back, 0, BLK).reshape(-1)), axis=0)
        t = t.reshape(BLK, 2 * BLK, HEADS).astype(F32)
        t = jnp.where(jnp.asarray(valid)[:, :, None], t, NEG)
        tables.append(jnp.transpose(t, (2, 0, 1)))
    return jnp.stack(tables)


def _out_proj_kernel(o_ref, w_ref, g_ref, x_ref, y_ref, *, mh):
    o = o_ref[...].reshape(N_PHASE * mh, o_ref.shape[-1])
    h = jnp.dot(o, w_ref[...], preferred_element_type=F32)
    ms = jnp.mean(h * h, axis=-1, keepdims=True)
    y = h * lax.rsqrt(ms + EPS) * g_ref[...]
    for r in range(N_PHASE):
        sl = slice(r * D_MODEL, (r + 1) * D_MODEL)
        y_ref[:, sl] = x_ref[:, sl] + y[r * mh:(r + 1) * mh]


def _out_proj(O, w, g, x, *, mh=64):
    B, S, D = x.shape
    M = S // N_PHASE
    xv = x.reshape(B, M, N_PHASE * D)
    out = pl.pallas_call(
        functools.partial(_out_proj_kernel, mh=mh),
        grid=(B, M // mh),
        in_specs=[
            pl.BlockSpec((None, N_PHASE, mh, O.shape[-1]), lambda b, m: (b, 0, m, 0)),
            pl.BlockSpec(w.shape, lambda b, m: (0, 0)),
            pl.BlockSpec((1, D), lambda b, m: (0, 0)),
            pl.BlockSpec((None, mh, N_PHASE * D), lambda b, m: (b, m, 0)),
        ],
        out_specs=pl.BlockSpec((None, mh, N_PHASE * D), lambda b, m: (b, m, 0)),
        out_shape=jax.ShapeDtypeStruct(xv.shape, F32),
        compiler_params=_cparams(("parallel", "parallel")),
        name="attn_out_proj",
    )(O, w, g.reshape(1, D), xv)
    return out.reshape(B, S, D)


def _ssm_in_proj_kernel(x_ref, g_ref, wt_ref, o_ref, h_ref):
    M = x_ref.shape[0]

    @pl.when(pl.program_id(1) == 0)
    def _():
        g = g_ref[...]
        for s in range(N_PHASE):
            xs = x_ref[:, s * D_MODEL:(s + 1) * D_MODEL]
            ms = jnp.mean(xs * xs, axis=-1, keepdims=True)
            h_ref[s * M:(s + 1) * M, :] = (xs * lax.rsqrt(ms + EPS) * g).astype(BF16)

    res = lax.dot_general(wt_ref[...], h_ref[...], (((1,), (1,)), ((), ())),
                          preferred_element_type=F32)
    for s in range(N_PHASE):
        o_ref[s] = res[:, s * M:(s + 1) * M].astype(BF16)


def _ssm_in_proj(x, g, wt, *, tn=512):
    B, S, D = x.shape
    M = S // N_PHASE
    N = wt.shape[0]
    xv = x.reshape(B, M, N_PHASE * D)
    return pl.pallas_call(
        _ssm_in_proj_kernel,
        grid=(B, N // tn),
        in_specs=[
            pl.BlockSpec((None, M, N_PHASE * D), lambda b, j: (b, 0, 0)),
            pl.BlockSpec((1, D), lambda b, j: (0, 0)),
            pl.BlockSpec((tn, D), lambda b, j: (j, 0)),
        ],
        out_specs=pl.BlockSpec((N_PHASE, tn, M), lambda b, j: (0, j, b)),
        out_shape=jax.ShapeDtypeStruct((N_PHASE, N, B * M), BF16),
        scratch_shapes=[pltpu.VMEM((S, D), BF16)],
        compiler_params=_cparams(("parallel", "arbitrary")),
        name="ssm_in_proj",
    )(xv, g.reshape(1, D), wt)


def _ssm_core_kernel(u_ref, m_ref, bre_ref, bim_ref, cre_ref, cim_ref, pwre_ref, pwim_ref,
                     d_ref, y_ref, *, n_chunk):
    C = SSM_GROUP
    QC = N_PHASE * C
    P = SSM_STATE
    u = u_ref[...]
    N = u.shape[-1]
    us = [u[:, i * C:(i + 1) * C, :].reshape(QC, N) for i in range(2)]

    xre = jnp.concatenate([jnp.dot(bre_ref[i], us[i], preferred_element_type=F32)
                           for i in range(2)], axis=0)
    xim = jnp.concatenate([jnp.dot(bim_ref[i], us[i], preferred_element_type=F32)
                           for i in range(2)], axis=0)

    c_idx = lax.broadcasted_iota(jnp.int32, xre.shape, 1) % n_chunk
    k = 0
    while (1 << k) < n_chunk:
        sh = 1 << k
        rre = pltpu.roll(xre, sh, axis=1)
        rim = pltpu.roll(xim, sh, axis=1)
        ar = pwre_ref[k]
        ai = pwim_ref[k]
        ok = c_idx >= sh
        xre, xim = (xre + jnp.where(ok, ar * rre - ai * rim, 0.0),
                    xim + jnp.where(ok, ar * rim + ai * rre, 0.0))
        k += 1
    ok = c_idx >= 1
    sre = jnp.where(ok, pltpu.roll(xre, 1, axis=1), 0.0).astype(BF16)
    sim = jnp.where(ok, pltpu.roll(xim, 1, axis=1), 0.0).astype(BF16)

    for i in range(2):
        y = jnp.dot(m_ref[i], us[i], preferred_element_type=F32)
        y = y + jnp.dot(cre_ref[i], sre[i * P:(i + 1) * P], preferred_element_type=F32)
        y = y + jnp.dot(cim_ref[i], sim[i * P:(i + 1) * P], preferred_element_type=F32)
        y = y.reshape(N_PHASE, C, N)
        y = y + d_ref[i * C:(i + 1) * C] * u[:, i * C:(i + 1) * C, :].astype(F32)
        y_ref[:, i * C:(i + 1) * C, :] = y.astype(BF16)


def _ssm_core(UT, prm, n_chunk):
    _, _, N = UT.shape
    G2 = SSM_N_GROUPS // 2
    C2 = 2 * SSM_GROUP
    QC = N_PHASE * SSM_GROUP
    P = SSM_STATE
    n_steps = prm["pw_re"].shape[1]
    return pl.pallas_call(
        functools.partial(_ssm_core_kernel, n_chunk=n_chunk),
        grid=(G2,),
        in_specs=[
            pl.BlockSpec((N_PHASE, C2, N), lambda g: (0, g, 0)),
            pl.BlockSpec((2, QC, QC), lambda g: (g, 0, 0)),
            pl.BlockSpec((2, P, QC), lambda g: (g, 0, 0)),
            pl.BlockSpec((2, P, QC), lambda g: (g, 0, 0)),
            pl.BlockSpec((2, QC, P), lambda g: (g, 0, 0)),
            pl.BlockSpec((2, QC, P), lambda g: (g, 0, 0)),
            pl.BlockSpec((None, n_steps, 2 * P, 1), lambda g: (g, 0, 0, 0)),
            pl.BlockSpec((None, n_steps, 2 * P, 1), lambda g: (g, 0, 0, 0)),
            pl.BlockSpec((None, C2, 1), lambda g: (g, 0, 0)),
        ],
        out_specs=pl.BlockSpec((N_PHASE, C2, N), lambda g: (0, g, 0)),
        out_shape=jax.ShapeDtypeStruct((N_PHASE, SSM_N_GROUPS * SSM_GROUP, N), BF16),
        compiler_params=_cparams(("parallel",)),
        name="ssm_core",
    )(UT, prm["m"], prm["b_re"], prm["b_im"], prm["c_re"], prm["c_im"],
      prm["pw_re"], prm["pw_im"], prm["d"])


def _ssm_params(a_re, a_im, log_dt, b_re, b_im, c_re, c_im, d_skip, n_chunk):
    G, P, C, Q = SSM_N_GROUPS, SSM_STATE, SSM_GROUP, N_PHASE
    hi = lax.Precision.HIGHEST
    dt = jnp.exp(log_dt.astype(F32))[:, None]
    a_re = a_re.astype(F32)
    a_im = a_im.astype(F32)
    lre, lim = a_re * dt, a_im * dt
    n = jnp.arange(Q + 1, dtype=F32)[None, :, None]
    mag = jnp.exp(lre[:, None, :] * n)
    pre, pim = mag * jnp.cos(lim[:, None, :] * n), mag * jnp.sin(lim[:, None, :] * n)
    abr, abi = pre[:, 1], pim[:, 1]
    den = a_re * a_re + a_im * a_im
    cfr = ((abr - 1.0) * a_re + abi * a_im) / den
    cfi = (abi * a_re - (abr - 1.0) * a_im) / den
    b_re = b_re.astype(F32)
    b_im = b_im.astype(F32)
    bbr = cfr[..., None] * b_re - cfi[..., None] * b_im
    bbi = cfr[..., None] * b_im + cfi[..., None] * b_re
    c_re = c_re.astype(F32)
    c_im = c_im.astype(F32)

    wr = pre[:, :Q, :, None] * bbr[:, None] - pim[:, :Q, :, None] * bbi[:, None]
    wi = pre[:, :Q, :, None] * bbi[:, None] + pim[:, :Q, :, None] * bbr[:, None]
    K = (jnp.einsum("gop,gdpi->gdoi", c_re, wr, precision=hi)
         - jnp.einsum("gop,gdpi->gdoi", c_im, wi, precision=hi))
    lag = np.arange(Q)[:, None] - np.arange(Q)[None, :]
    Kt = jnp.take(K, jnp.asarray(np.clip(lag, 0, Q - 1).reshape(-1)), axis=1)
    Kt = Kt.reshape(G, Q, Q, C, C)
    Kt = jnp.where(jnp.asarray(lag >= 0)[None, :, :, None, None], Kt, 0.0)
    m = jnp.transpose(Kt, (0, 1, 3, 2, 4)).reshape(G, Q * C, Q * C)

    rev = np.arange(Q - 1, -1, -1)
    er, ei = pre[:, rev], pim[:, rev]
    bst_re = er[..., None] * bbr[:, None] - ei[..., None] * bbi[:, None]
    bst_im = er[..., None] * bbi[:, None] + ei[..., None] * bbr[:, None]
    bst_re = jnp.transpose(bst_re, (0, 2, 1, 3)).reshape(G, P, Q * C)
    bst_im = jnp.transpose(bst_im, (0, 2, 1, 3)).reshape(G, P, Q * C)

    fr, fi = pre[:, 1:], pim[:, 1:]
    cst_re = c_re[:, None] * fr[:, :, None, :] - c_im[:, None] * fi[:, :, None, :]
    cst_im = -(c_re[:, None] * fi[:, :, None, :] + c_im[:, None] * fr[:, :, None, :])
    cst_re = cst_re.reshape(G, Q * C, P)
    cst_im = cst_im.reshape(G, Q * C, P)

    sr, si = [pre[:, Q]], [pim[:, Q]]
    k = 1
    while (1 << k) < n_chunk:
        r0, i0 = sr[-1], si[-1]
        sr.append(r0 * r0 - i0 * i0)
        si.append(2.0 * r0 * i0)
        k += 1

    def cols(lst):
        a = jnp.stack(lst, axis=1)
        a = a.reshape(G // 2, 2, len(lst), P)
        return jnp.transpose(a, (0, 2, 1, 3)).reshape(G // 2, len(lst), 2 * P, 1)

    return {
        "m": m.astype(BF16), "b_re": bst_re.astype(BF16), "b_im": bst_im.astype(BF16),
        "c_re": cst_re.astype(BF16), "c_im": cst_im.astype(BF16),
        "pw_re": cols(sr), "pw_im": cols(si),
        "d": d_skip.astype(F32).reshape(G // 2, 2 * C, 1),
    }


def _ssm_out_kernel(y_ref, z_ref, wg_ref, bg_ref, wo_ref, gp_ref, x_ref, o_ref):
    y = y_ref[...].astype(F32)
    g = jax.nn.gelu(y)
    gl = jnp.dot(wg_ref[...], g.astype(BF16), preferred_element_type=F32) + bg_ref[...]
    y2 = g * jax.nn.sigmoid(gl)
    z = z_ref[...].astype(F32)
    gated = (y2 * (z * jax.nn.sigmoid(z))).astype(BF16)
    ot = jnp.dot(wo_ref[...], gated, preferred_element_type=F32)
    ms = jnp.mean(ot * ot, axis=0, keepdims=True)
    nt = (ot * lax.rsqrt(ms + EPS)).T
    o_ref[...] = x_ref[...] + (nt * gp_ref[...]).reshape(o_ref.shape)


def _ssm_out(YT, UT, wgt, bg, wot, gp, x, *, nb=2):
    B, S, D = x.shape
    M = S // N_PHASE
    E = YT.shape[1]
    xv = x.reshape(B, M, N_PHASE * D)
    out = pl.pallas_call(
        _ssm_out_kernel,
        grid=(N_PHASE, B // nb),
        in_specs=[
            pl.BlockSpec((None, E, nb * M), lambda s, j: (s, 0, j)),
            pl.BlockSpec((None, E, nb * M), lambda s, j: (s, 1, j)),
            pl.BlockSpec((E, E), lambda s, j: (0, 0)),
            pl.BlockSpec((E, 1), lambda s, j: (0, 0)),
            pl.BlockSpec((D, E), lambda s, j: (0, 0)),
            pl.BlockSpec((1, D), lambda s, j: (0, 0)),
            pl.BlockSpec((nb, M, D), lambda s, j: (j, 0, s)),
        ],
        out_specs=pl.BlockSpec((nb, M, D), lambda s, j: (j, 0, s)),
        out_shape=jax.ShapeDtypeStruct(xv.shape, F32),
        compiler_params=_cparams(("parallel", "parallel")),
        name="ssm_out",
    )(YT, UT, wgt, bg.reshape(E, 1), wot, gp.reshape(1, D), xv)
    return out.reshape(B, S, D)


def kernel(x, rel_bias, attn_pre_norm, attn_w_in, attn_w_out, attn_post_norm, ssm_pre_norm, ssm_w_in, ssm_a_re, ssm_a_im, ssm_log_dt, ssm_b_re, ssm_b_im, ssm_c_re, ssm_c_im, ssm_d, ssm_w_glu, ssm_b_glu, ssm_w_out, ssm_post_norm):
    B, S, D = x.shape
    n_chunk = S // N_PHASE

    P = _norm_proj(x, attn_pre_norm[0], attn_w_in[0].astype(BF16))
    O = _attention(P, _bias_tables(rel_bias))
    x = _out_proj(O, attn_w_out[0].astype(BF16), attn_post_norm[0], x)

    prm = _ssm_params(ssm_a_re[0], ssm_a_im[0], ssm_log_dt[0], ssm_b_re[0], ssm_b_im[0],
                      ssm_c_re[0], ssm_c_im[0], ssm_d[0], n_chunk)
    UT = _ssm_in_proj(x, ssm_pre_norm[0], ssm_w_in[0].T.astype(BF16))
    YT = _ssm_core(UT, prm, n_chunk)
    x = _ssm_out(YT, UT, ssm_w_glu[0].T.astype(BF16), ssm_b_glu[0],
                 ssm_w_out[0].T.astype(BF16), ssm_post_norm[0], x)
    return x
```

```python
import functools
import math

import numpy as np
import jax
import jax.numpy as jnp
from jax import lax
from jax.experimental import pallas as pl
from jax.experimental.pallas import tpu as pltpu

F32 = jnp.float32
BF16 = jnp.bfloat16

D_MODEL = 1024
HEAD_DIM = 64
HEADS = 16
N_PHASE = 16
BLK = 128
DILATIONS = (1, 4, 16)
REL_BUCKETS = 32
REL_MAX_DIST = 2048
SSM_GROUP = 16
SSM_N_GROUPS = 64
SSM_STATE = 64
EPS = 1e-6
NEG = -1e30
VMEM_LIMIT = 56 * 1024 * 1024


def _cparams(sem):
    return pltpu.CompilerParams(dimension_semantics=sem, vmem_limit_bytes=VMEM_LIMIT)


def _norm_proj_kernel(x_ref, g_ref, w_ref, o_ref, h_ref, *, mh):
    @pl.when(pl.program_id(2) == 0)
    def _():
        x = x_ref[...].reshape(N_PHASE * mh, x_ref.shape[-1])
        ms = jnp.mean(x * x, axis=-1, keepdims=True)
        h_ref[...] = (x * lax.rsqrt(ms + EPS) * g_ref[...]).astype(BF16)

    res = jnp.dot(h_ref[...], w_ref[...], preferred_element_type=F32)
    o_ref[...] = res.reshape(N_PHASE, mh, res.shape[-1]).astype(BF16)


def _norm_proj(xp, g, w, *, mh=64, tn=1024):
    B, _, M, D = xp.shape
    N = w.shape[1]
    return pl.pallas_call(
        functools.partial(_norm_proj_kernel, mh=mh),
        grid=(B, M // mh, N // tn),
        in_specs=[
            pl.BlockSpec((None, N_PHASE, mh, D), lambda b, m, j: (b, 0, m, 0)),
            pl.BlockSpec((1, D), lambda b, m, j: (0, 0)),
            pl.BlockSpec((D, tn), lambda b, m, j: (0, j)),
        ],
        out_specs=pl.BlockSpec((None, N_PHASE, mh, tn), lambda b, m, j: (b, 0, m, j)),
        out_shape=jax.ShapeDtypeStruct((B, N_PHASE, M, N), BF16),
        scratch_shapes=[pltpu.VMEM((N_PHASE * mh, D), BF16)],
        compiler_params=_cparams(("parallel", "parallel", "arbitrary")),
        name="attn_norm_proj",
    )(xp, g.reshape(1, D), w)


def _attn_kernel(q0_ref, q1_ref, q2_ref, k0_ref, k1_ref, k2_ref, v0_ref, v1_ref, v2_ref,
                 z_ref, bm_ref, o_ref, qf_ref, kf_ref, vf_ref, acc_ref, l_ref, m_ref):
    W = 2 * HEAD_DIM
    lane = lax.broadcasted_iota(jnp.int32, (BLK, W), 1)
    first_head = lane < HEAD_DIM

    def block_attn(q, k, v, g, cur_only):
        nk = k.shape[0]
        zq = jnp.zeros_like(q)
        qs = jnp.concatenate([jnp.where(first_head, q, zq), jnp.where(first_head, zq, q)], axis=0)
        s = lax.dot_general(qs, k, (((1,), (1,)), ((), ())), preferred_element_type=F32)
        s = s + (bm_ref[g, :, BLK:2 * BLK] if cur_only else bm_ref[g])
        m = jnp.max(s, axis=-1, keepdims=True)
        p = jnp.exp2(s - m).astype(BF16)
        va = jnp.concatenate([v, jnp.ones((nk, W), BF16)], axis=1)
        pv = jnp.dot(p, va, preferred_element_type=F32)
        acc = jnp.where(first_head, pv[:BLK, :W], pv[BLK:, :W])
        l = jnp.where(first_head, pv[:BLK, W:], pv[BLK:, W:])
        mm = jnp.where(first_head, m[:BLK], m[BLK:])
        return acc, l, mm

    def store(g, ph, rows, vals, sl):
        for ref, val in zip((acc_ref, l_ref, m_ref), vals):
            ref[g, ph, rows, :] = val[sl]

    def g2_body(r, carry):
        vals = block_attn(q2_ref[r], k2_ref[r], v2_ref[r], 2, True)
        store(2, r, slice(None), vals, slice(None))
        return carry

    lax.fori_loop(0, N_PHASE, g2_body, 0, unroll=2)

    def g1_rows(ref, r4, n):
        off = pl.multiple_of(n * 32, 32)
        return jnp.concatenate([ref[4 * q4 + r4, pl.ds(off, 32), :] for q4 in range(4)], axis=0)

    def g1_store(r4, n, vals):
        off = pl.multiple_of(n * 32, 32)
        for q4 in range(4):
            store(1, 4 * q4 + r4, pl.ds(off, 32), vals, slice(32 * q4, 32 * q4 + 32))

    def g1_first(r4, carry):
        vals = block_attn(g1_rows(q1_ref, r4, 0), g1_rows(k1_ref, r4, 0), g1_rows(v1_ref, r4, 0),
                          1, True)
        g1_store(r4, 0, vals)
        return carry

    lax.fori_loop(0, 4, g1_first, 0, unroll=2)

    def g1_body(i, carry):
        r4 = i % 4
        n = i // 4 + 1
        k = jnp.concatenate([g1_rows(k1_ref, r4, n - 1), g1_rows(k1_ref, r4, n)], axis=0)
        v = jnp.concatenate([g1_rows(v1_ref, r4, n - 1), g1_rows(v1_ref, r4, n)], axis=0)
        g1_store(r4, n, block_attn(g1_rows(q1_ref, r4, n), k, v, 1, False))
        return carry

    lax.fori_loop(0, 12, g1_body, 0, unroll=2)

    def upcast(r, carry):
        qf_ref[r] = q0_ref[r].astype(F32)
        kf_ref[r] = k0_ref[r].astype(F32)
        vf_ref[r] = v0_ref[r].astype(F32)
        return carry

    lax.fori_loop(0, N_PHASE, upcast, 0)

    def g0_rows(ref, n):
        off = pl.multiple_of(n * 8, 8)
        return jnp.concatenate([ref[r, pl.ds(off, 8), :] for r in range(N_PHASE)], axis=0)

    def g0_store(n, vals):
        off = pl.multiple_of(n * 8, 8)
        for r in range(N_PHASE):
            store(0, r, pl.ds(off, 8), vals, slice(8 * r, 8 * r + 8))

    g0_store(0, block_attn(g0_rows(qf_ref, 0).astype(BF16), g0_rows(kf_ref, 0).astype(BF16),
                           g0_rows(vf_ref, 0).astype(BF16), 0, True))

    def g0_body(n, carry):
        q = g0_rows(qf_ref, n).astype(BF16)
        k = jnp.concatenate([g0_rows(kf_ref, n - 1), g0_rows(kf_ref, n)], axis=0).astype(BF16)
        v = jnp.concatenate([g0_rows(vf_ref, n - 1), g0_rows(vf_ref, n)], axis=0).astype(BF16)
        g0_store(n, block_attn(q, k, v, 0, False))
        return carry

    lax.fori_loop(1, N_PHASE, g0_body, 0, unroll=3)

    def combine(r, carry):
        m0, m1, m2 = m_ref[0, r], m_ref[1, r], m_ref[2, r]
        mx = jnp.maximum(jnp.maximum(m0, m1), m2)
        w0, w1, w2 = jnp.exp2(m0 - mx), jnp.exp2(m1 - mx), jnp.exp2(m2 - mx)
        num = w0 * acc_ref[0, r] + w1 * acc_ref[1, r] + w2 * acc_ref[2, r]
        den = w0 * l_ref[0, r] + w1 * l_ref[1, r] + w2 * l_ref[2, r]
        z = z_ref[r].astype(F32)
        o_ref[r] = (num / den * (z * jax.nn.sigmoid(z))).astype(BF16)
        return carry

    lax.fori_loop(0, N_PHASE, combine, 0, unroll=2)


def _attention(P, bm):
    B, _, M, _ = P.shape
    HP = HEADS // 2
    W = 2 * HEAD_DIM

    def spec(kind, g):
        base = (kind * 3 + g) * HP
        return pl.BlockSpec((None, N_PHASE, M, W), lambda b, hp, base=base: (b, 0, 0, base + hp))

    in_specs = [spec(kind, g) for kind in range(3) for g in range(3)]
    in_specs.append(pl.BlockSpec((None, N_PHASE, M, W), lambda b, hp: (b, 0, 0, 9 * HP + hp)))
    in_specs.append(pl.BlockSpec((3, None, 2 * BLK, 2 * BLK), lambda b, hp: (0, hp, 0, 0)))
    return pl.pallas_call(
        _attn_kernel,
        grid=(B, HP),
        in_specs=in_specs,
        out_specs=pl.BlockSpec((None, N_PHASE, M, W), lambda b, hp: (b, 0, 0, hp)),
        out_shape=jax.ShapeDtypeStruct((B, N_PHASE, M, HEADS * HEAD_DIM), BF16),
        scratch_shapes=[pltpu.VMEM((N_PHASE, M, W), F32) for _ in range(3)]
        + [pltpu.VMEM((3, N_PHASE, M, W), F32) for _ in range(3)],
        compiler_params=_cparams(("parallel", "parallel")),
        name="dilated_attention",
    )(*([P] * 10), bm)


def _t5_bucket(dist):
    max_exact = REL_BUCKETS // 2
    n = jnp.maximum(dist, 1).astype(F32)
    large = max_exact + (jnp.log(n / max_exact) / math.log(REL_MAX_DIST / max_exact)
                         * (REL_BUCKETS - max_exact)).astype(jnp.int32)
    large = jnp.minimum(large, REL_BUCKETS - 1)
    return jnp.where(dist < max_exact, dist, large)


def _bias_tables(rel_bias):
    a = np.arange(BLK)
    pos = (16 * (a % 8) + a // 8, 4 * (a % 32) + a // 32, a)
    back = np.stack([np.concatenate([BLK + p[:, None] - p[None, :], p[:, None] - p[None, :]], axis=1)
                     for p in pos])
    valid = (back >= 0) & (back <= BLK)
    dist = np.clip(back, 0, BLK) * np.asarray(DILATIONS)[:, None, None]
    bucket = jnp.where(jnp.asarray(valid), _t5_bucket(jnp.asarray(dist, jnp.int32)), REL_BUCKETS)
    onehot = (bucket[..., None] == jnp.arange(REL_BUCKETS + 1)).astype(F32)
    ext = jnp.concatenate([rel_bias.astype(F32), jnp.full((1, HEADS), NEG, F32)], axis=0)
    t = jnp.einsum("gijc,ch->ghij", onehot, ext * math.log2(math.e),
                   precision=lax.Precision.HIGHEST)
    return t.reshape(3, HEADS // 2, 2 * BLK, 2 * BLK)


def _out_proj_kernel(o_ref, w_ref, g_ref, x_ref, y_ref, *, mh):
    o = o_ref[...].reshape(N_PHASE * mh, o_ref.shape[-1])
    h = jnp.dot(o, w_ref[...], preferred_element_type=F32)
    ms = jnp.mean(h * h, axis=-1, keepdims=True)
    y = h * lax.rsqrt(ms + EPS) * g_ref[...]
    y_ref[...] = x_ref[...] + y.reshape(y_ref.shape)


def _out_proj(O, w, g, xp, *, mh=64):
    B, _, M, D = xp.shape
    return pl.pallas_call(
        functools.partial(_out_proj_kernel, mh=mh),
        grid=(B, M // mh),
        in_specs=[
            pl.BlockSpec((None, N_PHASE, mh, O.shape[-1]), lambda b, m: (b, 0, m, 0)),
            pl.BlockSpec(w.shape, lambda b, m: (0, 0)),
            pl.BlockSpec((1, D), lambda b, m: (0, 0)),
            pl.BlockSpec((None, N_PHASE, mh, D), lambda b, m: (b, 0, m, 0)),
        ],
        out_specs=pl.BlockSpec((None, N_PHASE, mh, D), lambda b, m: (b, 0, m, 0)),
        out_shape=jax.ShapeDtypeStruct(xp.shape, F32),
        compiler_params=_cparams(("parallel", "parallel")),
        name="attn_out_proj",
    )(O, w, g.reshape(1, D), xp)


def _ssm_in_proj_kernel(x_ref, g_ref, wt_ref, o_ref, h_ref):
    M = x_ref.shape[1]

    @pl.when(pl.program_id(1) == 0)
    def _():
        g = g_ref[...]
        for s in range(N_PHASE):
            xs = x_ref[s]
            ms = jnp.mean(xs * xs, axis=-1, keepdims=True)
            h_ref[s * M:(s + 1) * M, :] = (xs * lax.rsqrt(ms + EPS) * g).astype(BF16)

    res = lax.dot_general(wt_ref[...], h_ref[...], (((1,), (1,)), ((), ())),
                          preferred_element_type=F32)
    for s in range(N_PHASE):
        o_ref[s] = res[:, s * M:(s + 1) * M].astype(BF16)


def _ssm_in_proj(xp, g, wt, *, tn=512):
    B, _, M, D = xp.shape
    S = N_PHASE * M
    N = wt.shape[0]
    return pl.pallas_call(
        _ssm_in_proj_kernel,
        grid=(B, N // tn),
        in_specs=[
            pl.BlockSpec((None, N_PHASE, M, D), lambda b, j: (b, 0, 0, 0)),
            pl.BlockSpec((1, D), lambda b, j: (0, 0)),
            pl.BlockSpec((tn, D), lambda b, j: (j, 0)),
        ],
        out_specs=pl.BlockSpec((N_PHASE, tn, M), lambda b, j: (0, j, b)),
        out_shape=jax.ShapeDtypeStruct((N_PHASE, N, B * M), BF16),
        scratch_shapes=[pltpu.VMEM((S, D), BF16)],
        compiler_params=_cparams(("parallel", "arbitrary")),
        name="ssm_in_proj",
    )(xp, g.reshape(1, D), wt)


def _ssm_core_kernel(u_ref, m_ref, bre_ref, bim_ref, cre_ref, cim_ref, pwre_ref, pwim_ref,
                     d_ref, y_ref, *, n_chunk):
    C = SSM_GROUP
    QC = N_PHASE * C
    P = SSM_STATE
    u = u_ref[...]
    N = u.shape[-1]
    us = [u[:, i * C:(i + 1) * C, :].reshape(QC, N) for i in range(2)]

    xre = jnp.concatenate([jnp.dot(bre_ref[i], us[i], preferred_element_type=F32)
                           for i in range(2)], axis=0)
    xim = jnp.concatenate([jnp.dot(bim_ref[i], us[i], preferred_element_type=F32)
                           for i in range(2)], axis=0)

    c_idx = lax.broadcasted_iota(jnp.int32, xre.shape, 1) % n_chunk
    k = 0
    while (1 << k) < n_chunk:
        sh = 1 << k
        rre = pltpu.roll(xre, sh, axis=1)
        rim = pltpu.roll(xim, sh, axis=1)
        ar = pwre_ref[k]
        ai = pwim_ref[k]
        ok = c_idx >= sh
        xre, xim = (xre + jnp.where(ok, ar * rre - ai * rim, 0.0),
                    xim + jnp.where(ok, ar * rim + ai * rre, 0.0))
        k += 1
    ok = c_idx >= 1
    sre = jnp.where(ok, pltpu.roll(xre, 1, axis=1), 0.0).astype(BF16)
    sim = jnp.where(ok, pltpu.roll(xim, 1, axis=1), 0.0).astype(BF16)

    for i in range(2):
        y = jnp.dot(m_ref[i], us[i], preferred_element_type=F32)
        y = y + jnp.dot(cre_ref[i], sre[i * P:(i + 1) * P], preferred_element_type=F32)
        y = y + jnp.dot(cim_ref[i], sim[i * P:(i + 1) * P], preferred_element_type=F32)
        y = y.reshape(N_PHASE, C, N)
        y = y + d_ref[i * C:(i + 1) * C] * u[:, i * C:(i + 1) * C, :].astype(F32)
        y_ref[:, i * C:(i + 1) * C, :] = y.astype(BF16)


def _ssm_core(UT, prm, n_chunk):
    _, _, N = UT.shape
    G2 = SSM_N_GROUPS // 2
    C2 = 2 * SSM_GROUP
    QC = N_PHASE * SSM_GROUP
    P = SSM_STATE
    n_steps = prm["pw_re"].shape[1]
    return pl.pallas_call(
        functools.partial(_ssm_core_kernel, n_chunk=n_chunk),
        grid=(G2,),
        in_specs=[
            pl.BlockSpec((N_PHASE, C2, N), lambda g: (0, g, 0)),
            pl.BlockSpec((2, QC, QC), lambda g: (g, 0, 0)),
            pl.BlockSpec((2, P, QC), lambda g: (g, 0, 0)),
            pl.BlockSpec((2, P, QC), lambda g: (g, 0, 0)),
            pl.BlockSpec((2, QC, P), lambda g: (g, 0, 0)),
            pl.BlockSpec((2, QC, P), lambda g: (g, 0, 0)),
            pl.BlockSpec((None, n_steps, 2 * P, 1), lambda g: (g, 0, 0, 0)),
            pl.BlockSpec((None, n_steps, 2 * P, 1), lambda g: (g, 0, 0, 0)),
            pl.BlockSpec((None, C2, 1), lambda g: (g, 0, 0)),
        ],
        out_specs=pl.BlockSpec((N_PHASE, C2, N), lambda g: (0, g, 0)),
        out_shape=jax.ShapeDtypeStruct((N_PHASE, SSM_N_GROUPS * SSM_GROUP, N), BF16),
        compiler_params=_cparams(("parallel",)),
        name="ssm_core",
    )(UT, prm["m"], prm["b_re"], prm["b_im"], prm["c_re"], prm["c_im"],
      prm["pw_re"], prm["pw_im"], prm["d"])


def _ssm_params(a_re, a_im, log_dt, b_re, b_im, c_re, c_im, d_skip, n_chunk):
    G, P, C, Q = SSM_N_GROUPS, SSM_STATE, SSM_GROUP, N_PHASE
    hi = lax.Precision.HIGHEST
    dt = jnp.exp(log_dt.astype(F32))[:, None]
    a_re = a_re.astype(F32)
    a_im = a_im.astype(F32)
    lre, lim = a_re * dt, a_im * dt
    n = jnp.arange(Q + 1, dtype=F32)[None, :, None]
    mag = jnp.exp(lre[:, None, :] * n)
    pre, pim = mag * jnp.cos(lim[:, None, :] * n), mag * jnp.sin(lim[:, None, :] * n)
    abr, abi = pre[:, 1], pim[:, 1]
    den = a_re * a_re + a_im * a_im
    cfr = ((abr - 1.0) * a_re + abi * a_im) / den
    cfi = (abi * a_re - (abr - 1.0) * a_im) / den
    b_re = b_re.astype(F32)
    b_im = b_im.astype(F32)
    bbr = cfr[..., None] * b_re - cfi[..., None] * b_im
    bbi = cfr[..., None] * b_im + cfi[..., None] * b_re
    c_re = c_re.astype(F32)
    c_im = c_im.astype(F32)

    wr = pre[:, :Q, :, None] * bbr[:, None] - pim[:, :Q, :, None] * bbi[:, None]
    wi = pre[:, :Q, :, None] * bbi[:, None] + pim[:, :Q, :, None] * bbr[:, None]
    K = (jnp.einsum("gop,gdpi->gdoi", c_re, wr, precision=hi)
         - jnp.einsum("gop,gdpi->gdoi", c_im, wi, precision=hi))
    lag = np.arange(Q)[:, None] - np.arange(Q)[None, :]
    sel = (lag[:, :, None] == np.arange(Q)[None, None, :]).astype(np.float32)
    m = jnp.einsum("tsd,gdoi->gtosi", jnp.asarray(sel), K, precision=hi)
    m = m.reshape(G, Q * C, Q * C)

    er, ei = pre[:, Q - 1::-1], pim[:, Q - 1::-1]
    bst_re = er[..., None] * bbr[:, None] - ei[..., None] * bbi[:, None]
    bst_im = er[..., None] * bbi[:, None] + ei[..., None] * bbr[:, None]
    bst_re = jnp.transpose(bst_re, (0, 2, 1, 3)).reshape(G, P, Q * C)
    bst_im = jnp.transpose(bst_im, (0, 2, 1, 3)).reshape(G, P, Q * C)

    fr, fi = pre[:, 1:], pim[:, 1:]
    cst_re = c_re[:, None] * fr[:, :, None, :] - c_im[:, None] * fi[:, :, None, :]
    cst_im = -(c_re[:, None] * fi[:, :, None, :] + c_im[:, None] * fr[:, :, None, :])
    cst_re = cst_re.reshape(G, Q * C, P)
    cst_im = cst_im.reshape(G, Q * C, P)

    sr, si = [pre[:, Q]], [pim[:, Q]]
    k = 1
    while (1 << k) < n_chunk:
        r0, i0 = sr[-1], si[-1]
        sr.append(r0 * r0 - i0 * i0)
        si.append(2.0 * r0 * i0)
        k += 1

    def cols(lst):
        a = jnp.stack(lst, axis=1)
        a = a.reshape(G // 2, 2, len(lst), P)
        return jnp.transpose(a, (0, 2, 1, 3)).reshape(G // 2, len(lst), 2 * P, 1)

    return {
        "m": m.astype(BF16), "b_re": bst_re.astype(BF16), "b_im": bst_im.astype(BF16),
        "c_re": cst_re.astype(BF16), "c_im": cst_im.astype(BF16),
        "pw_re": cols(sr), "pw_im": cols(si),
        "d": d_skip.astype(F32).reshape(G // 2, 2 * C, 1),
    }


def _ssm_out_kernel(y_ref, z_ref, wg_ref, bg_ref, wo_ref, gp_ref, x_ref, o_hbm, obuf, sem, *, nb):
    s, j = pl.program_id(0), pl.program_id(1)
    nj = pl.num_programs(1)
    step = s * nj + j
    last = pl.num_programs(0) * nj - 1
    slot = step % 2

    def out_copy(slot_, s_, j_):
        return pltpu.make_async_copy(obuf.at[slot_], o_hbm.at[pl.ds(j_ * nb, nb), :, s_, :],
                                     sem.at[slot_])

    @pl.when(step >= 2)
    def _():
        out_copy(slot, (step - 2) // nj, (step - 2) % nj).wait()

    y = y_ref[...].astype(F32)
    g = jax.nn.gelu(y)
    gl = jnp.dot(wg_ref[...], g.astype(BF16), preferred_element_type=F32) + bg_ref[...]
    y2 = g * jax.nn.sigmoid(gl)
    z = z_ref[...].astype(F32)
    gated = (y2 * (z * jax.nn.sigmoid(z))).astype(BF16)
    ot = jnp.dot(wo_ref[...], gated, preferred_element_type=F32)
    ms = jnp.mean(ot * ot, axis=0, keepdims=True)
    nt = (ot * lax.rsqrt(ms + EPS)).T
    obuf[slot] = x_ref[...] + (nt * gp_ref[...]).reshape(x_ref.shape)
    out_copy(slot, s, j).start()

    @pl.when(step == last)
    def _():
        out_copy(slot, s, j).wait()

    @pl.when(jnp.logical_and(step == last, last >= 1))
    def _():
        out_copy(1 - slot, (step - 1) // nj, (step - 1) % nj).wait()


def _ssm_out(YT, UT, wgt, bg, wot, gp, xp, *, nb=2):
    B, _, M, D = xp.shape
    E = YT.shape[1]
    out = pl.pallas_call(
        functools.partial(_ssm_out_kernel, nb=nb),
        grid=(N_PHASE, B // nb),
        in_specs=[
            pl.BlockSpec((None, E, nb * M), lambda s, j: (s, 0, j)),
            pl.BlockSpec((None, E, nb * M), lambda s, j: (s, 1, j)),
            pl.BlockSpec((E, E), lambda s, j: (0, 0)),
            pl.BlockSpec((E, 1), lambda s, j: (0, 0)),
            pl.BlockSpec((D, E), lambda s, j: (0, 0)),
            pl.BlockSpec((1, D), lambda s, j: (0, 0)),
            pl.BlockSpec((nb, None, M, D), lambda s, j: (j, s, 0, 0)),
        ],
        out_specs=pl.BlockSpec(memory_space=pl.ANY),
        out_shape=jax.ShapeDtypeStruct((B, M, N_PHASE, D), F32),
        scratch_shapes=[pltpu.VMEM((2, nb, M, D), F32), pltpu.SemaphoreType.DMA((2,))],
        compiler_params=_cparams(("arbitrary", "arbitrary")),
        name="ssm_out",
    )(YT, UT, wgt, bg.reshape(E, 1), wot, gp.reshape(1, D), xp)
    return out.reshape(B, M * N_PHASE, D)


def kernel(x, rel_bias, attn_pre_norm, attn_w_in, attn_w_out, attn_post_norm, ssm_pre_norm, ssm_w_in, ssm_a_re, ssm_a_im, ssm_log_dt, ssm_b_re, ssm_b_im, ssm_c_re, ssm_c_im, ssm_d, ssm_w_glu, ssm_b_glu, ssm_w_out, ssm_post_norm):
    B, S, D = x.shape
    n_chunk = S // N_PHASE

    n_q = 3 * HEADS * HEAD_DIM
    col_scale = jnp.where(jnp.arange(attn_w_in.shape[-1]) < n_q,
                          HEAD_DIM ** -0.5 * math.log2(math.e), 1.0).astype(F32)
    xp = jnp.transpose(x.reshape(B, n_chunk, N_PHASE, D), (0, 2, 1, 3))
    P = _norm_proj(xp, attn_pre_norm[0], (attn_w_in[0] * col_scale).astype(BF16))
    O = _attention(P, _bias_tables(rel_bias))
    xp = _out_proj(O, attn_w_out[0].astype(BF16), attn_post_norm[0], xp)

    prm = _ssm_params(ssm_a_re[0], ssm_a_im[0], ssm_log_dt[0], ssm_b_re[0], ssm_b_im[0],
                      ssm_c_re[0], ssm_c_im[0], ssm_d[0], n_chunk)
    UT = _ssm_in_proj(xp, ssm_pre_norm[0], ssm_w_in[0].T.astype(BF16))
    YT = _ssm_core(UT, prm, n_chunk)
    return _ssm_out(YT, UT, ssm_w_glu[0].T.astype(BF16), ssm_b_glu[0],
                    ssm_w_out[0].T.astype(BF16), ssm_post_norm[0], xp)
```

```python
import functools
import math

import numpy as np
import jax
import jax.numpy as jnp
from jax import lax
from jax.experimental import pallas as pl
from jax.experimental.pallas import tpu as pltpu

F32 = jnp.float32
BF16 = jnp.bfloat16

D_MODEL = 1024
HEAD_DIM = 64
HEADS = 16
N_PHASE = 16
BLK = 128
DILATIONS = (1, 4, 16)
REL_BUCKETS = 32
REL_MAX_DIST = 2048
SSM_GROUP = 16
SSM_N_GROUPS = 64
SSM_STATE = 64
EPS = 1e-6
NEG = -1e30
VMEM_LIMIT = 56 * 1024 * 1024


def _cparams(sem):
    return pltpu.CompilerParams(dimension_semantics=sem, vmem_limit_bytes=VMEM_LIMIT)


def _norm_proj_kernel(x_ref, g_ref, w_ref, o_ref, h_ref, *, mh):
    @pl.when(pl.program_id(2) == 0)
    def _():
        x = x_ref[...].reshape(N_PHASE * mh, x_ref.shape[-1])
        ms = jnp.mean(x * x, axis=-1, keepdims=True)
        h_ref[...] = (x * lax.rsqrt(ms + EPS) * g_ref[...]).astype(BF16)

    res = jnp.dot(h_ref[...], w_ref[...], preferred_element_type=F32)
    o_ref[...] = res.reshape(N_PHASE, mh, res.shape[-1]).astype(BF16)


def _norm_proj(xp, g, w, *, mh=64, tn=1024):
    B, _, M, D = xp.shape
    N = w.shape[1]
    return pl.pallas_call(
        functools.partial(_norm_proj_kernel, mh=mh),
        grid=(B, M // mh, N // tn),
        in_specs=[
            pl.BlockSpec((None, N_PHASE, mh, D), lambda b, m, j: (b, 0, m, 0)),
            pl.BlockSpec((1, D), lambda b, m, j: (0, 0)),
            pl.BlockSpec((D, tn), lambda b, m, j: (0, j)),
        ],
        out_specs=pl.BlockSpec((None, N_PHASE, mh, tn), lambda b, m, j: (b, 0, m, j)),
        out_shape=jax.ShapeDtypeStruct((B, N_PHASE, M, N), BF16),
        scratch_shapes=[pltpu.VMEM((N_PHASE * mh, D), BF16)],
        compiler_params=_cparams(("parallel", "parallel", "arbitrary")),
        name="attn_norm_proj",
    )(xp, g.reshape(1, D), w)


def _attn_kernel(q0_ref, q1_ref, q2_ref, k0_ref, k1_ref, k2_ref, v0_ref, v1_ref, v2_ref,
                 z_ref, bm_ref, o_ref, qf_ref, kf_ref, vf_ref, acc_ref, l_ref, m_ref):
    W = 2 * HEAD_DIM
    lane = lax.broadcasted_iota(jnp.int32, (BLK, W), 1)
    first_head = lane < HEAD_DIM

    def logits(q, k, g, cur_only):
        zq = jnp.zeros_like(q)
        qs = jnp.concatenate([jnp.where(first_head, q, zq), jnp.where(first_head, zq, q)], axis=0)
        s = lax.dot_general(qs, k, (((1,), (1,)), ((), ())), preferred_element_type=F32)
        return s + (bm_ref[g, :, BLK:2 * BLK] if cur_only else bm_ref[g])

    def finish(s, v):
        m = jnp.max(s, axis=-1, keepdims=True)
        p = jnp.exp2(s - m).astype(BF16)
        va = jnp.concatenate([v, jnp.ones((v.shape[0], W), BF16)], axis=1)
        pv = jnp.dot(p, va, preferred_element_type=F32)
        acc = jnp.where(first_head, pv[:BLK, :W], pv[BLK:, :W])
        l = jnp.where(first_head, pv[:BLK, W:], pv[BLK:, W:])
        mm = jnp.where(first_head, m[:BLK], m[BLK:])
        return acc, l, mm

    def rows_of(ref, pieces):
        return jnp.concatenate([ref[ph, lo:lo + n, :] for ph, lo, n in pieces], axis=0)

    def keys_of(ref, prev, cur):
        return rows_of(ref, cur) if prev is None else rows_of(ref, prev + cur)

    def store(g, pieces, vals):
        at = 0
        for ph, lo, n in pieces:
            for ref, val in zip((acc_ref, l_ref, m_ref), vals):
                ref[g, ph, lo:lo + n, :] = val[at:at + n]
            at += n

    M = q0_ref.shape[1]
    g2_blocks = [[(r, 0, BLK)] for r in range(N_PHASE)]
    g1_blocks = {(r4, n): [(4 * q4 + r4, 32 * n, 32) for q4 in range(4)]
                 for r4 in range(4) for n in range(M // 32)}
    g0_blocks = [[(r, 8 * n, 8) for r in range(N_PHASE)] for n in range(M // 8)]

    for r in range(N_PHASE):
        qf_ref[r] = q0_ref[r].astype(F32)
        kf_ref[r] = k0_ref[r].astype(F32)
        vf_ref[r] = v0_ref[r].astype(F32)

    work = []
    for pieces in g2_blocks:
        work.append((2, pieces, None, (q2_ref, k2_ref, v2_ref), False))
    for (r4, n), pieces in g1_blocks.items():
        work.append((1, pieces, g1_blocks[(r4, n - 1)] if n else None, (q1_ref, k1_ref, v1_ref), False))
    for n, pieces in enumerate(g0_blocks):
        work.append((0, pieces, g0_blocks[n - 1] if n else None, (qf_ref, kf_ref, vf_ref), True))

    pending = None
    for item in work + [None]:
        if item is not None:
            g, pieces, prev, (q_r, k_r, _), cast = item
            q, k = rows_of(q_r, pieces), keys_of(k_r, prev, pieces)
            if cast:
                q, k = q.astype(BF16), k.astype(BF16)
            s = logits(q, k, g, prev is None)
        if pending is not None:
            (g_p, pieces_p, prev_p, (_, _, v_r), cast_p), s_p = pending
            v = keys_of(v_r, prev_p, pieces_p)
            store(g_p, pieces_p, finish(s_p, v.astype(BF16) if cast_p else v))
        pending = (item, s) if item is not None else None

    def combine(r, carry):
        m0, m1, m2 = m_ref[0, r], m_ref[1, r], m_ref[2, r]
        mx = jnp.maximum(jnp.maximum(m0, m1), m2)
        w0, w1, w2 = jnp.exp2(m0 - mx), jnp.exp2(m1 - mx), jnp.exp2(m2 - mx)
        num = w0 * acc_ref[0, r] + w1 * acc_ref[1, r] + w2 * acc_ref[2, r]
        den = w0 * l_ref[0, r] + w1 * l_ref[1, r] + w2 * l_ref[2, r]
        z = z_ref[r].astype(F32)
        o_ref[r] = (num / den * (z * jax.nn.sigmoid(z))).astype(BF16)
        return carry

    lax.fori_loop(0, N_PHASE, combine, 0, unroll=2)


def _attention(P, bm):
    B, _, M, _ = P.shape
    HP = HEADS // 2
    W = 2 * HEAD_DIM

    def spec(kind, g):
        base = (kind * 3 + g) * HP
        return pl.BlockSpec((None, N_PHASE, M, W), lambda b, hp, base=base: (b, 0, 0, base + hp))

    in_specs = [spec(kind, g) for kind in range(3) for g in range(3)]
    in_specs.append(pl.BlockSpec((None, N_PHASE, M, W), lambda b, hp: (b, 0, 0, 9 * HP + hp)))
    in_specs.append(pl.BlockSpec((3, None, 2 * BLK, 2 * BLK), lambda b, hp: (0, hp, 0, 0)))
    return pl.pallas_call(
        _attn_kernel,
        grid=(B, HP),
        in_specs=in_specs,
        out_specs=pl.BlockSpec((None, N_PHASE, M, W), lambda b, hp: (b, 0, 0, hp)),
        out_shape=jax.ShapeDtypeStruct((B, N_PHASE, M, HEADS * HEAD_DIM), BF16),
        scratch_shapes=[pltpu.VMEM((N_PHASE, M, W), F32) for _ in range(3)]
        + [pltpu.VMEM((3, N_PHASE, M, W), F32) for _ in range(3)],
        compiler_params=_cparams(("parallel", "parallel")),
        name="dilated_attention",
    )(*([P] * 10), bm)


def _t5_bucket(dist):
    max_exact = REL_BUCKETS // 2
    n = jnp.maximum(dist, 1).astype(F32)
    large = max_exact + (jnp.log(n / max_exact) / math.log(REL_MAX_DIST / max_exact)
                         * (REL_BUCKETS - max_exact)).astype(jnp.int32)
    large = jnp.minimum(large, REL_BUCKETS - 1)
    return jnp.where(dist < max_exact, dist, large)


def _bias_tables(rel_bias):
    a = np.arange(BLK)
    pos = (16 * (a % 8) + a // 8, 4 * (a % 32) + a // 32, a)
    back = np.stack([np.concatenate([BLK + p[:, None] - p[None, :], p[:, None] - p[None, :]], axis=1)
                     for p in pos])
    valid = (back >= 0) & (back <= BLK)
    dist = np.clip(back, 0, BLK) * np.asarray(DILATIONS)[:, None, None]
    bucket = jnp.where(jnp.asarray(valid), _t5_bucket(jnp.asarray(dist, jnp.int32)), REL_BUCKETS)
    onehot = (bucket[..., None] == jnp.arange(REL_BUCKETS + 1)).astype(F32)
    ext = jnp.concatenate([rel_bias.astype(F32), jnp.full((1, HEADS), NEG, F32)], axis=0)
    t = jnp.einsum("gijc,ch->ghij", onehot, ext * math.log2(math.e),
                   precision=lax.Precision.HIGHEST)
    return t.reshape(3, HEADS // 2, 2 * BLK, 2 * BLK)


def _out_proj_kernel(o_ref, w_ref, g_ref, x_ref, y_ref, *, mh):
    o = o_ref[...].reshape(N_PHASE * mh, o_ref.shape[-1])
    h = jnp.dot(o, w_ref[...], preferred_element_type=F32)
    ms = jnp.mean(h * h, axis=-1, keepdims=True)
    y = h * lax.rsqrt(ms + EPS) * g_ref[...]
    y_ref[...] = x_ref[...] + y.reshape(y_ref.shape)


def _out_proj(O, w, g, xp, *, mh=64):
    B, _, M, D = xp.shape
    return pl.pallas_call(
        functools.partial(_out_proj_kernel, mh=mh),
        grid=(B, M // mh),
        in_specs=[
            pl.BlockSpec((None, N_PHASE, mh, O.shape[-1]), lambda b, m: (b, 0, m, 0)),
            pl.BlockSpec(w.shape, lambda b, m: (0, 0)),
            pl.BlockSpec((1, D), lambda b, m: (0, 0)),
            pl.BlockSpec((None, N_PHASE, mh, D), lambda b, m: (b, 0, m, 0)),
        ],
        out_specs=pl.BlockSpec((None, N_PHASE, mh, D), lambda b, m: (b, 0, m, 0)),
        out_shape=jax.ShapeDtypeStruct(xp.shape, F32),
        compiler_params=_cparams(("parallel", "parallel")),
        name="attn_out_proj",
    )(O, w, g.reshape(1, D), xp)


def _ssm_in_proj_kernel(x_ref, g_ref, wt_ref, o_ref, h_ref):
    M = x_ref.shape[1]

    @pl.when(pl.program_id(1) == 0)
    def _():
        g = g_ref[...]
        for s in range(N_PHASE):
            xs = x_ref[s]
            ms = jnp.mean(xs * xs, axis=-1, keepdims=True)
            h_ref[s * M:(s + 1) * M, :] = (xs * lax.rsqrt(ms + EPS) * g).astype(BF16)

    res = lax.dot_general(wt_ref[...], h_ref[...], (((1,), (1,)), ((), ())),
                          preferred_element_type=F32)
    for s in range(N_PHASE):
        o_ref[s] = res[:, s * M:(s + 1) * M].astype(BF16)


def _ssm_in_proj(xp, g, wt, *, tn=512):
    B, _, M, D = xp.shape
    S = N_PHASE * M
    N = wt.shape[0]
    return pl.pallas_call(
        _ssm_in_proj_kernel,
        grid=(B, N // tn),
        in_specs=[
            pl.BlockSpec((None, N_PHASE, M, D), lambda b, j: (b, 0, 0, 0)),
            pl.BlockSpec((1, D), lambda b, j: (0, 0)),
            pl.BlockSpec((tn, D), lambda b, j: (j, 0)),
        ],
        out_specs=pl.BlockSpec((N_PHASE, tn, M), lambda b, j: (0, j, b)),
        out_shape=jax.ShapeDtypeStruct((N_PHASE, N, B * M), BF16),
        scratch_shapes=[pltpu.VMEM((S, D), BF16)],
        compiler_params=_cparams(("parallel", "arbitrary")),
        name="ssm_in_proj",
    )(xp, g.reshape(1, D), wt)


def _ssm_core_kernel(u_ref, m_ref, bre_ref, bim_ref, cre_ref, cim_ref, pwre_ref, pwim_ref,
                     d_ref, y_ref, *, n_chunk):
    C = SSM_GROUP
    QC = N_PHASE * C
    P = SSM_STATE
    u = u_ref[...]
    N = u.shape[-1]
    us = [u[:, i * C:(i + 1) * C, :].reshape(QC, N) for i in range(2)]

    xre = jnp.concatenate([jnp.dot(bre_ref[i], us[i], preferred_element_type=F32)
                           for i in range(2)], axis=0)
    xim = jnp.concatenate([jnp.dot(bim_ref[i], us[i], preferred_element_type=F32)
                           for i in range(2)], axis=0)

    c_idx = lax.broadcasted_iota(jnp.int32, xre.shape, 1) % n_chunk
    k = 0
    while (1 << k) < n_chunk:
        sh = 1 << k
        rre = pltpu.roll(xre, sh, axis=1)
        rim = pltpu.roll(xim, sh, axis=1)
        ar = pwre_ref[k]
        ai = pwim_ref[k]
        ok = c_idx >= sh
        xre, xim = (xre + jnp.where(ok, ar * rre - ai * rim, 0.0),
                    xim + jnp.where(ok, ar * rim + ai * rre, 0.0))
        k += 1
    ok = c_idx >= 1
    sre = jnp.where(ok, pltpu.roll(xre, 1, axis=1), 0.0).astype(BF16)
    sim = jnp.where(ok, pltpu.roll(xim, 1, axis=1), 0.0).astype(BF16)

    for i in range(2):
        y = jnp.dot(m_ref[i], us[i], preferred_element_type=F32)
        y = y + jnp.dot(cre_ref[i], sre[i * P:(i + 1) * P], preferred_element_type=F32)
        y = y + jnp.dot(cim_ref[i], sim[i * P:(i + 1) * P], preferred_element_type=F32)
        y = y.reshape(N_PHASE, C, N)
        y = y + d_ref[i * C:(i + 1) * C] * u[:, i * C:(i + 1) * C, :].astype(F32)
        y_ref[:, i * C:(i + 1) * C, :] = y.astype(BF16)


def _ssm_core(UT, prm, n_chunk):
    _, _, N = UT.shape
    G2 = SSM_N_GROUPS // 2
    C2 = 2 * SSM_GROUP
    QC = N_PHASE * SSM_GROUP
    P = SSM_STATE
    n_steps = prm["pw_re"].shape[1]
    return pl.pallas_call(
        functools.partial(_ssm_core_kernel, n_chunk=n_chunk),
        grid=(G2,),
        in_specs=[
            pl.BlockSpec((N_PHASE, C2, N), lambda g: (0, g, 0)),
            pl.BlockSpec((2, QC, QC), lambda g: (g, 0, 0)),
            pl.BlockSpec((2, P, QC), lambda g: (g, 0, 0)),
            pl.BlockSpec((2, P, QC), lambda g: (g, 0, 0)),
            pl.BlockSpec((2, QC, P), lambda g: (g, 0, 0)),
            pl.BlockSpec((2, QC, P), lambda g: (g, 0, 0)),
            pl.BlockSpec((None, n_steps, 2 * P, 1), lambda g: (g, 0, 0, 0)),
            pl.BlockSpec((None, n_steps, 2 * P, 1), lambda g: (g, 0, 0, 0)),
            pl.BlockSpec((None, C2, 1), lambda g: (g, 0, 0)),
        ],
        out_specs=pl.BlockSpec((N_PHASE, C2, N), lambda g: (0, g, 0)),
        out_shape=jax.ShapeDtypeStruct((N_PHASE, SSM_N_GROUPS * SSM_GROUP, N), BF16),
        compiler_params=_cparams(("parallel",)),
        name="ssm_core",
    )(UT, prm["m"], prm["b_re"], prm["b_im"], prm["c_re"], prm["c_im"],
      prm["pw_re"], prm["pw_im"], prm["d"])


def _ssm_params(a_re, a_im, log_dt, b_re, b_im, c_re, c_im, d_skip, n_chunk):
    G, P, C, Q = SSM_N_GROUPS, SSM_STATE, SSM_GROUP, N_PHASE
    hi = lax.Precision.HIGHEST
    dt = jnp.exp(log_dt.astype(F32))[:, None]
    a_re = a_re.astype(F32)
    a_im = a_im.astype(F32)
    lre, lim = a_re * dt, a_im * dt
    n = jnp.arange(Q + 1, dtype=F32)[None, :, None]
    mag = jnp.exp(lre[:, None, :] * n)
    pre, pim = mag * jnp.cos(lim[:, None, :] * n), mag * jnp.sin(lim[:, None, :] * n)
    abr, abi = pre[:, 1], pim[:, 1]
    den = a_re * a_re + a_im * a_im
    cfr = ((abr - 1.0) * a_re + abi * a_im) / den
    cfi = (abi * a_re - (abr - 1.0) * a_im) / den
    b_re = b_re.astype(F32)
    b_im = b_im.astype(F32)
    bbr = cfr[..., None] * b_re - cfi[..., None] * b_im
    bbi = cfr[..., None] * b_im + cfi[..., None] * b_re
    c_re = c_re.astype(F32)
    c_im = c_im.astype(F32)

    wr = pre[:, :Q, :, None] * bbr[:, None] - pim[:, :Q, :, None] * bbi[:, None]
    wi = pre[:, :Q, :, None] * bbi[:, None] + pim[:, :Q, :, None] * bbr[:, None]
    K = (jnp.einsum("gop,gdpi->gdoi", c_re, wr, precision=hi)
         - jnp.einsum("gop,gdpi->gdoi", c_im, wi, precision=hi))
    lag = np.arange(Q)[:, None] - np.arange(Q)[None, :]
    sel = (lag[:, :, None] == np.arange(Q)[None, None, :]).astype(np.float32)
    m = jnp.einsum("tsd,gdoi->gtosi", jnp.asarray(sel), K, precision=hi)
    m = m.reshape(G, Q * C, Q * C)

    er, ei = pre[:, Q - 1::-1], pim[:, Q - 1::-1]
    bst_re = er[..., None] * bbr[:, None] - ei[..., None] * bbi[:, None]
    bst_im = er[..., None] * bbi[:, None] + ei[..., None] * bbr[:, None]
    bst_re = jnp.transpose(bst_re, (0, 2, 1, 3)).reshape(G, P, Q * C)
    bst_im = jnp.transpose(bst_im, (0, 2, 1, 3)).reshape(G, P, Q * C)

    fr, fi = pre[:, 1:], pim[:, 1:]
    cst_re = c_re[:, None] * fr[:, :, None, :] - c_im[:, None] * fi[:, :, None, :]
    cst_im = -(c_re[:, None] * fi[:, :, None, :] + c_im[:, None] * fr[:, :, None, :])
    cst_re = cst_re.reshape(G, Q * C, P)
    cst_im = cst_im.reshape(G, Q * C, P)

    sr, si = [pre[:, Q]], [pim[:, Q]]
    k = 1
    while (1 << k) < n_chunk:
        r0, i0 = sr[-1], si[-1]
        sr.append(r0 * r0 - i0 * i0)
        si.append(2.0 * r0 * i0)
        k += 1

    def cols(lst):
        a = jnp.stack(lst, axis=1)
        a = a.reshape(G // 2, 2, len(lst), P)
        return jnp.transpose(a, (0, 2, 1, 3)).reshape(G // 2, len(lst), 2 * P, 1)

    return {
        "m": m.astype(BF16), "b_re": bst_re.astype(BF16), "b_im": bst_im.astype(BF16),
        "c_re": cst_re.astype(BF16), "c_im": cst_im.astype(BF16),
        "pw_re": cols(sr), "pw_im": cols(si),
        "d": d_skip.astype(F32).reshape(G // 2, 2 * C, 1),
    }


def _ssm_out_kernel(y_ref, z_ref, wg_ref, bg_ref, wo_ref, gp_ref, x_ref, o_hbm, obuf, sem, *, nb):
    s, j = pl.program_id(0), pl.program_id(1)
    nj = pl.num_programs(1)
    step = s * nj + j
    last = pl.num_programs(0) * nj - 1
    slot = step % 2

    def out_copy(slot_, s_, j_):
        return pltpu.make_async_copy(obuf.at[slot_], o_hbm.at[pl.ds(j_ * nb, nb), :, s_, :],
                                     sem.at[slot_])

    @pl.when(step >= 2)
    def _():
        out_copy(slot, (step - 2) // nj, (step - 2) % nj).wait()

    y = y_ref[...].astype(F32)
    g = jax.nn.gelu(y)
    gl = jnp.dot(wg_ref[...], g.astype(BF16), preferred_element_type=F32) + bg_ref[...]
    y2 = g * jax.nn.sigmoid(gl)
    z = z_ref[...].astype(F32)
    gated = (y2 * (z * jax.nn.sigmoid(z))).astype(BF16)
    ot = jnp.dot(wo_ref[...], gated, preferred_element_type=F32)
    ms = jnp.mean(ot * ot, axis=0, keepdims=True)
    nt = (ot * lax.rsqrt(ms + EPS)).T
    obuf[slot] = x_ref[...] + (nt * gp_ref[...]).reshape(x_ref.shape)
    out_copy(slot, s, j).start()

    @pl.when(step == last)
    def _():
        out_copy(slot, s, j).wait()

    @pl.when(jnp.logical_and(step == last, last >= 1))
    def _():
        out_copy(1 - slot, (step - 1) // nj, (step - 1) % nj).wait()


def _ssm_out(YT, UT, wgt, bg, wot, gp, xp, *, nb=2):
    B, _, M, D = xp.shape
    E = YT.shape[1]
    out = pl.pallas_call(
        functools.partial(_ssm_out_kernel, nb=nb),
        grid=(N_PHASE, B // nb),
        in_specs=[
            pl.BlockSpec((None, E, nb * M), lambda s, j: (s, 0, j)),
            pl.BlockSpec((None, E, nb * M), lambda s, j: (s, 1, j)),
            pl.BlockSpec((E, E), lambda s, j: (0, 0)),
            pl.BlockSpec((E, 1), lambda s, j: (0, 0)),
            pl.BlockSpec((D, E), lambda s, j: (0, 0)),
            pl.BlockSpec((1, D), lambda s, j: (0, 0)),
            pl.BlockSpec((nb, None, M, D), lambda s, j: (j, s, 0, 0)),
        ],
        out_specs=pl.BlockSpec(memory_space=pl.ANY),
        out_shape=jax.ShapeDtypeStruct((B, M, N_PHASE, D), F32),
        scratch_shapes=[pltpu.VMEM((2, nb, M, D), F32), pltpu.SemaphoreType.DMA((2,))],
        compiler_params=_cparams(("arbitrary", "arbitrary")),
        name="ssm_out",
    )(YT, UT, wgt, bg.reshape(E, 1), wot, gp.reshape(1, D), xp)
    return out.reshape(B, M * N_PHASE, D)


def kernel(x, rel_bias, attn_pre_norm, attn_w_in, attn_w_out, attn_post_norm, ssm_pre_norm, ssm_w_in, ssm_a_re, ssm_a_im, ssm_log_dt, ssm_b_re, ssm_b_im, ssm_c_re, ssm_c_im, ssm_d, ssm_w_glu, ssm_b_glu, ssm_w_out, ssm_post_norm):
    B, S, D = x.shape
    n_chunk = S // N_PHASE

    n_q = 3 * HEADS * HEAD_DIM
    col_scale = jnp.where(jnp.arange(attn_w_in.shape[-1]) < n_q,
                          HEAD_DIM ** -0.5 * math.log2(math.e), 1.0).astype(F32)
    xp = jnp.transpose(x.reshape(B, n_chunk, N_PHASE, D), (0, 2, 1, 3))
    P = _norm_proj(xp, attn_pre_norm[0], (attn_w_in[0] * col_scale).astype(BF16))
    O = _attention(P, _bias_tables(rel_bias))
    xp = _out_proj(O, attn_w_out[0].astype(BF16), attn_post_norm[0], xp)

    prm = _ssm_params(ssm_a_re[0], ssm_a_im[0], ssm_log_dt[0], ssm_b_re[0], ssm_b_im[0],
                      ssm_c_re[0], ssm_c_im[0], ssm_d[0], n_chunk)
    UT = _ssm_in_proj(xp, ssm_pre_norm[0], ssm_w_in[0].T.astype(BF16))
    YT = _ssm_core(UT, prm, n_chunk)
    return _ssm_out(YT, UT, ssm_w_glu[0].T.astype(BF16), ssm_b_glu[0],
                    ssm_w_out[0].T.astype(BF16), ssm_post_norm[0], xp)
```

```python
import functools
import math

import numpy as np
import jax
import jax.numpy as jnp
from jax import lax
from jax.experimental import pallas as pl
from jax.experimental.pallas import tpu as pltpu

F32 = jnp.float32
BF16 = jnp.bfloat16

D_MODEL = 1024
HEAD_DIM = 64
HEADS = 16
N_PHASE = 16
BLK = 128
DILATIONS = (1, 4, 16)
REL_BUCKETS = 32
REL_MAX_DIST = 2048
SSM_GROUP = 16
SSM_N_GROUPS = 64
SSM_STATE = 64
EPS = 1e-6
NEG = -1e30
VMEM_LIMIT = 56 * 1024 * 1024


def _cparams(sem):
    return pltpu.CompilerParams(dimension_semantics=sem, vmem_limit_bytes=VMEM_LIMIT)


LOG2E = math.log2(math.e)


def _sigmoid(v):
    return 1.0 / (1.0 + jnp.exp2(v * (-LOG2E)))


def _gelu_tanh(x):
    k0 = -2.0 * math.sqrt(2.0 / math.pi) * LOG2E
    return x / (1.0 + jnp.exp2(x * (k0 + (k0 * 0.044715) * (x * x))))


def _norm_proj_kernel(x_ref, g_ref, w_ref, o_ref, h_ref, *, mh):
    @pl.when(pl.program_id(2) == 0)
    def _():
        x = x_ref[...].reshape(N_PHASE * mh, x_ref.shape[-1])
        ms = jnp.mean(x * x, axis=-1, keepdims=True)
        h_ref[...] = (x * lax.rsqrt(ms + EPS) * g_ref[...]).astype(BF16)

    res = jnp.dot(h_ref[...], w_ref[...], preferred_element_type=F32)
    o_ref[...] = res.reshape(N_PHASE, mh, res.shape[-1]).astype(BF16)


def _norm_proj(xp, g, w, *, mh=64, tn=1024):
    B, _, M, D = xp.shape
    N = w.shape[1]
    return pl.pallas_call(
        functools.partial(_norm_proj_kernel, mh=mh),
        grid=(B, M // mh, N // tn),
        in_specs=[
            pl.BlockSpec((None, N_PHASE, mh, D), lambda b, m, j: (b, 0, m, 0)),
            pl.BlockSpec((1, D), lambda b, m, j: (0, 0)),
            pl.BlockSpec((D, tn), lambda b, m, j: (0, j)),
        ],
        out_specs=pl.BlockSpec((None, N_PHASE, mh, tn), lambda b, m, j: (b, 0, m, j)),
        out_shape=jax.ShapeDtypeStruct((B, N_PHASE, M, N), BF16),
        scratch_shapes=[pltpu.VMEM((N_PHASE * mh, D), BF16)],
        compiler_params=_cparams(("parallel", "parallel", "arbitrary")),
        name="attn_norm_proj",
    )(xp, g.reshape(1, D), w)


def _attn_kernel(q0_ref, q1_ref, q2_ref, k0_ref, k1_ref, k2_ref, v0_ref, v1_ref, v2_ref,
                 z_ref, bm_ref, o_ref, qf_ref, kf_ref, vf_ref, acc_ref, l_ref, m_ref):
    W = 2 * HEAD_DIM
    lane = lax.broadcasted_iota(jnp.int32, (BLK, W), 1)
    first_head = lane < HEAD_DIM

    def logits(q, k, g, cur_only):
        zq = jnp.zeros_like(q)
        qs = jnp.concatenate([jnp.where(first_head, q, zq), jnp.where(first_head, zq, q)], axis=0)
        s = lax.dot_general(qs, k, (((1,), (1,)), ((), ())), preferred_element_type=F32)
        return s + (bm_ref[g, :, BLK:2 * BLK] if cur_only else bm_ref[g])

    def finish(s, v):
        m = jnp.max(s, axis=-1, keepdims=True)
        p = jnp.exp2(s - m).astype(BF16)
        va = jnp.concatenate([v, jnp.ones((v.shape[0], W), BF16)], axis=1)
        pv = jnp.dot(p, va, preferred_element_type=F32)
        acc = jnp.where(first_head, pv[:BLK, :W], pv[BLK:, :W])
        l = jnp.where(first_head, pv[:BLK, W:], pv[BLK:, W:])
        mm = jnp.where(first_head, m[:BLK], m[BLK:])
        return acc, l, mm

    def rows_of(ref, pieces):
        return jnp.concatenate([ref[ph, lo:lo + n, :] for ph, lo, n in pieces], axis=0)

    def keys_of(ref, prev, cur):
        return rows_of(ref, cur) if prev is None else rows_of(ref, prev + cur)

    def store(g, pieces, vals):
        at = 0
        for ph, lo, n in pieces:
            for ref, val in zip((acc_ref, l_ref, m_ref), vals):
                ref[g, ph, lo:lo + n, :] = val[at:at + n]
            at += n

    M = q0_ref.shape[1]
    g2_blocks = [[(r, 0, BLK)] for r in range(N_PHASE)]
    g1_blocks = {(r4, n): [(4 * q4 + r4, 32 * n, 32) for q4 in range(4)]
                 for r4 in range(4) for n in range(M // 32)}
    g0_blocks = [[(r, 8 * n, 8) for r in range(N_PHASE)] for n in range(M // 8)]

    for r in range(N_PHASE):
        qf_ref[r] = q0_ref[r].astype(F32)
        kf_ref[r] = k0_ref[r].astype(F32)
        vf_ref[r] = v0_ref[r].astype(F32)

    work = []
    for pieces in g2_blocks:
        work.append((2, pieces, None, (q2_ref, k2_ref, v2_ref), False))
    for (r4, n), pieces in g1_blocks.items():
        work.append((1, pieces, g1_blocks[(r4, n - 1)] if n else None, (q1_ref, k1_ref, v1_ref), False))
    for n, pieces in enumerate(g0_blocks):
        work.append((0, pieces, g0_blocks[n - 1] if n else None, (qf_ref, kf_ref, vf_ref), True))

    pending = None
    for item in work + [None]:
        if item is not None:
            g, pieces, prev, (q_r, k_r, _), cast = item
            q, k = rows_of(q_r, pieces), keys_of(k_r, prev, pieces)
            if cast:
                q, k = q.astype(BF16), k.astype(BF16)
            s = logits(q, k, g, prev is None)
        if pending is not None:
            (g_p, pieces_p, prev_p, (_, _, v_r), cast_p), s_p = pending
            v = keys_of(v_r, prev_p, pieces_p)
            store(g_p, pieces_p, finish(s_p, v.astype(BF16) if cast_p else v))
        pending = (item, s) if item is not None else None

    def combine(r, carry):
        m0, m1, m2 = m_ref[0, r], m_ref[1, r], m_ref[2, r]
        mx = jnp.maximum(jnp.maximum(m0, m1), m2)
        w0, w1, w2 = jnp.exp2(m0 - mx), jnp.exp2(m1 - mx), jnp.exp2(m2 - mx)
        num = w0 * acc_ref[0, r] + w1 * acc_ref[1, r] + w2 * acc_ref[2, r]
        den = w0 * l_ref[0, r] + w1 * l_ref[1, r] + w2 * l_ref[2, r]
        z = z_ref[r].astype(F32)
        o_ref[r] = (num / den * (z * _sigmoid(z))).astype(BF16)
        return carry

    lax.fori_loop(0, N_PHASE, combine, 0, unroll=2)


def _attention(P, bm):
    B, _, M, _ = P.shape
    HP = HEADS // 2
    W = 2 * HEAD_DIM

    def spec(kind, g):
        base = (kind * 3 + g) * HP
        return pl.BlockSpec((None, N_PHASE, M, W), lambda b, hp, base=base: (b, 0, 0, base + hp))

    in_specs = [spec(kind, g) for kind in range(3) for g in range(3)]
    in_specs.append(pl.BlockSpec((None, N_PHASE, M, W), lambda b, hp: (b, 0, 0, 9 * HP + hp)))
    in_specs.append(pl.BlockSpec((3, None, 2 * BLK, 2 * BLK), lambda b, hp: (0, hp, 0, 0)))
    return pl.pallas_call(
        _attn_kernel,
        grid=(B, HP),
        in_specs=in_specs,
        out_specs=pl.BlockSpec((None, N_PHASE, M, W), lambda b, hp: (b, 0, 0, hp)),
        out_shape=jax.ShapeDtypeStruct((B, N_PHASE, M, HEADS * HEAD_DIM), BF16),
        scratch_shapes=[pltpu.VMEM((N_PHASE, M, W), F32) for _ in range(3)]
        + [pltpu.VMEM((3, N_PHASE, M, W), F32) for _ in range(3)],
        compiler_params=_cparams(("parallel", "parallel")),
        name="dilated_attention",
    )(*([P] * 10), bm)


def _t5_bucket(dist):
    max_exact = REL_BUCKETS // 2
    n = jnp.maximum(dist, 1).astype(F32)
    large = max_exact + (jnp.log(n / max_exact) / math.log(REL_MAX_DIST / max_exact)
                         * (REL_BUCKETS - max_exact)).astype(jnp.int32)
    large = jnp.minimum(large, REL_BUCKETS - 1)
    return jnp.where(dist < max_exact, dist, large)


def _bias_tables(rel_bias):
    a = np.arange(BLK)
    pos = (16 * (a % 8) + a // 8, 4 * (a % 32) + a // 32, a)
    back = np.stack([np.concatenate([BLK + p[:, None] - p[None, :], p[:, None] - p[None, :]], axis=1)
                     for p in pos])
    valid = (back >= 0) & (back <= BLK)
    dist = np.clip(back, 0, BLK) * np.asarray(DILATIONS)[:, None, None]
    bucket = jnp.where(jnp.asarray(valid), _t5_bucket(jnp.asarray(dist, jnp.int32)), REL_BUCKETS)
    onehot = (bucket[..., None] == jnp.arange(REL_BUCKETS + 1)).astype(F32)
    ext = jnp.concatenate([rel_bias.astype(F32), jnp.full((1, HEADS), NEG, F32)], axis=0)
    t = jnp.einsum("gijc,ch->ghij", onehot, ext * math.log2(math.e),
                   precision=lax.Precision.HIGHEST)
    return t.reshape(3, HEADS // 2, 2 * BLK, 2 * BLK)


def _out_proj_kernel(o_ref, w_ref, g_ref, x_ref, y_ref, *, mh):
    o = o_ref[...].reshape(N_PHASE * mh, o_ref.shape[-1])
    h = jnp.dot(o, w_ref[...], preferred_element_type=F32)
    ms = jnp.mean(h * h, axis=-1, keepdims=True)
    y = h * lax.rsqrt(ms + EPS) * g_ref[...]
    y_ref[...] = x_ref[...] + y.reshape(y_ref.shape)


def _out_proj(O, w, g, xp, *, mh=64):
    B, _, M, D = xp.shape
    return pl.pallas_call(
        functools.partial(_out_proj_kernel, mh=mh),
        grid=(B, M // mh),
        in_specs=[
            pl.BlockSpec((None, N_PHASE, mh, O.shape[-1]), lambda b, m: (b, 0, m, 0)),
            pl.BlockSpec(w.shape, lambda b, m: (0, 0)),
            pl.BlockSpec((1, D), lambda b, m: (0, 0)),
            pl.BlockSpec((None, N_PHASE, mh, D), lambda b, m: (b, 0, m, 0)),
        ],
        out_specs=pl.BlockSpec((None, N_PHASE, mh, D), lambda b, m: (b, 0, m, 0)),
        out_shape=jax.ShapeDtypeStruct(xp.shape, F32),
        compiler_params=_cparams(("parallel", "parallel")),
        name="attn_out_proj",
    )(O, w, g.reshape(1, D), xp)


def _ssm_in_proj_kernel(x_ref, g_ref, wt_ref, o_ref, h_ref):
    M = x_ref.shape[1]

    @pl.when(pl.program_id(1) == 0)
    def _():
        g = g_ref[...]
        for s in range(N_PHASE):
            xs = x_ref[s]
            ms = jnp.mean(xs * xs, axis=-1, keepdims=True)
            h_ref[s * M:(s + 1) * M, :] = (xs * lax.rsqrt(ms + EPS) * g).astype(BF16)

    res = lax.dot_general(wt_ref[...], h_ref[...], (((1,), (1,)), ((), ())),
                          preferred_element_type=F32)
    for s in range(N_PHASE):
        o_ref[s] = res[:, s * M:(s + 1) * M].astype(BF16)


def _ssm_in_proj(xp, g, wt, *, tn=512):
    B, _, M, D = xp.shape
    S = N_PHASE * M
    N = wt.shape[0]
    return pl.pallas_call(
        _ssm_in_proj_kernel,
        grid=(B, N // tn),
        in_specs=[
            pl.BlockSpec((None, N_PHASE, M, D), lambda b, j: (b, 0, 0, 0)),
            pl.BlockSpec((1, D), lambda b, j: (0, 0)),
            pl.BlockSpec((tn, D), lambda b, j: (j, 0)),
        ],
        out_specs=pl.BlockSpec((N_PHASE, tn, M), lambda b, j: (0, j, b)),
        out_shape=jax.ShapeDtypeStruct((N_PHASE, N, B * M), BF16),
        scratch_shapes=[pltpu.VMEM((S, D), BF16)],
        compiler_params=_cparams(("parallel", "arbitrary")),
        name="ssm_in_proj",
    )(xp, g.reshape(1, D), wt)


def _ssm_core_kernel(u_ref, m_ref, bre_ref, bim_ref, cre_ref, cim_ref, are_ref, aim_ref,
                     d_ref, y_ref, sre_ref, sim_ref, ym_ref, *, n_chunk):
    C = SSM_GROUP
    QC = N_PHASE * C
    u = u_ref[...]
    N = u.shape[-1]
    nb = N // n_chunk
    us = [u[:, i * C:(i + 1) * C, :].reshape(QC, N) for i in range(2)]

    sre_ref[...] = jnp.concatenate([jnp.dot(bre_ref[i], us[i], preferred_element_type=F32)
                                    for i in range(2)], axis=0).T
    sim_ref[...] = jnp.concatenate([jnp.dot(bim_ref[i], us[i], preferred_element_type=F32)
                                    for i in range(2)], axis=0).T

    for i in range(2):
        ym_ref[i] = jnp.dot(m_ref[i], us[i], preferred_element_type=F32)

    ar = jnp.broadcast_to(are_ref[...], (nb, are_ref.shape[-1]))
    ai = jnp.broadcast_to(aim_ref[...], (nb, aim_ref.shape[-1]))
    sr = jnp.zeros_like(ar)
    si = jnp.zeros_like(ar)
    for c in range(n_chunk):
        rows = pl.ds(c, nb, stride=n_chunk)
        xr, xi = sre_ref[rows, :], sim_ref[rows, :]
        sre_ref[rows, :] = sr
        sim_ref[rows, :] = si
        sr, si = ar * sr - ai * si + xr, ar * si + ai * sr + xi

    nt = (((1,), (1,)), ((), ()))
    inter = (lax.dot_general(cre_ref[...], sre_ref[...].astype(BF16), nt, preferred_element_type=F32)
             + lax.dot_general(cim_ref[...], sim_ref[...].astype(BF16), nt, preferred_element_type=F32))
    for i in range(2):
        y = (ym_ref[i] + inter[i * QC:(i + 1) * QC]).reshape(N_PHASE, C, N)
        y = y + d_ref[i * C:(i + 1) * C] * u[:, i * C:(i + 1) * C, :].astype(F32)
        y_ref[:, i * C:(i + 1) * C, :] = y.astype(BF16)


def _ssm_core(UT, prm, n_chunk):
    _, _, N = UT.shape
    G2 = SSM_N_GROUPS // 2
    C2 = 2 * SSM_GROUP
    QC = N_PHASE * SSM_GROUP
    P = SSM_STATE
    return pl.pallas_call(
        functools.partial(_ssm_core_kernel, n_chunk=n_chunk),
        grid=(G2,),
        in_specs=[
            pl.BlockSpec((N_PHASE, C2, N), lambda g: (0, g, 0)),
            pl.BlockSpec((2, QC, QC), lambda g: (g, 0, 0)),
            pl.BlockSpec((2, P, QC), lambda g: (g, 0, 0)),
            pl.BlockSpec((2, P, QC), lambda g: (g, 0, 0)),
            pl.BlockSpec((None, 2 * QC, 2 * P), lambda g: (g, 0, 0)),
            pl.BlockSpec((None, 2 * QC, 2 * P), lambda g: (g, 0, 0)),
            pl.BlockSpec((None, 1, 2 * P), lambda g: (g, 0, 0)),
            pl.BlockSpec((None, 1, 2 * P), lambda g: (g, 0, 0)),
            pl.BlockSpec((None, C2, 1), lambda g: (g, 0, 0)),
        ],
        out_specs=pl.BlockSpec((N_PHASE, C2, N), lambda g: (0, g, 0)),
        out_shape=jax.ShapeDtypeStruct((N_PHASE, SSM_N_GROUPS * SSM_GROUP, N), BF16),
        scratch_shapes=[pltpu.VMEM((N, 2 * P), F32), pltpu.VMEM((N, 2 * P), F32),
                        pltpu.VMEM((2, QC, N), F32)],
        compiler_params=_cparams(("parallel",)),
        name="ssm_core",
    )(UT, prm["m"], prm["b_re"], prm["b_im"], prm["c_re"], prm["c_im"],
      prm["a_re"], prm["a_im"], prm["d"])


def _ssm_params(a_re, a_im, log_dt, b_re, b_im, c_re, c_im, d_skip):
    G, P, C, Q = SSM_N_GROUPS, SSM_STATE, SSM_GROUP, N_PHASE
    hi = lax.Precision.HIGHEST
    dt = jnp.exp(log_dt.astype(F32))[:, None]
    a_re = a_re.astype(F32)
    a_im = a_im.astype(F32)
    lre, lim = a_re * dt, a_im * dt
    n = jnp.arange(Q + 1, dtype=F32)[None, :, None]
    mag = jnp.exp(lre[:, None, :] * n)
    pre, pim = mag * jnp.cos(lim[:, None, :] * n), mag * jnp.sin(lim[:, None, :] * n)
    abr, abi = pre[:, 1], pim[:, 1]
    den = a_re * a_re + a_im * a_im
    cfr = ((abr - 1.0) * a_re + abi * a_im) / den
    cfi = (abi * a_re - (abr - 1.0) * a_im) / den
    b_re = b_re.astype(F32)
    b_im = b_im.astype(F32)
    bbr = cfr[..., None] * b_re - cfi[..., None] * b_im
    bbi = cfr[..., None] * b_im + cfi[..., None] * b_re
    c_re = c_re.astype(F32)
    c_im = c_im.astype(F32)

    wr = pre[:, :Q, :, None] * bbr[:, None] - pim[:, :Q, :, None] * bbi[:, None]
    wi = pre[:, :Q, :, None] * bbi[:, None] + pim[:, :Q, :, None] * bbr[:, None]
    K = (jnp.einsum("gop,gdpi->gdoi", c_re, wr, precision=hi)
         - jnp.einsum("gop,gdpi->gdoi", c_im, wi, precision=hi))
    lag = np.arange(Q)[:, None] - np.arange(Q)[None, :]
    sel = (lag[:, :, None] == np.arange(Q)[None, None, :]).astype(np.float32)
    m = jnp.einsum("tsd,gdoi->gtosi", jnp.asarray(sel), K, precision=hi)
    m = m.reshape(G, Q * C, Q * C)

    er, ei = pre[:, Q - 1::-1], pim[:, Q - 1::-1]
    bst_re = er[..., None] * bbr[:, None] - ei[..., None] * bbi[:, None]
    bst_im = er[..., None] * bbi[:, None] + ei[..., None] * bbr[:, None]
    bst_re = jnp.transpose(bst_re, (0, 2, 1, 3)).reshape(G, P, Q * C)
    bst_im = jnp.transpose(bst_im, (0, 2, 1, 3)).reshape(G, P, Q * C)

    fr, fi = pre[:, 1:], pim[:, 1:]
    cst_re = c_re[:, None] * fr[:, :, None, :] - c_im[:, None] * fi[:, :, None, :]
    cst_im = -(c_re[:, None] * fi[:, :, None, :] + c_im[:, None] * fr[:, :, None, :])
    cst_re = cst_re.reshape(G, Q * C, P)
    cst_im = cst_im.reshape(G, Q * C, P)

    def pair_diag(c):
        z = jnp.zeros_like(c[0::2])
        return jnp.concatenate([jnp.concatenate([c[0::2], z], axis=2),
                                jnp.concatenate([z, c[1::2]], axis=2)], axis=1)

    return {
        "m": m.astype(BF16), "b_re": bst_re.astype(BF16), "b_im": bst_im.astype(BF16),
        "c_re": pair_diag(cst_re).astype(BF16), "c_im": pair_diag(cst_im).astype(BF16),
        "a_re": pre[:, Q].reshape(G // 2, 1, 2 * P), "a_im": pim[:, Q].reshape(G // 2, 1, 2 * P),
        "d": d_skip.astype(F32).reshape(G // 2, 2 * C, 1),
    }


def _ssm_out_kernel(y_ref, z_ref, wg_ref, bg_ref, wo_ref, gp_ref, x_ref, o_hbm, obuf, sem, *,
                    nb, n_sub):
    s, j = pl.program_id(0), pl.program_id(1)
    nj = pl.num_programs(1)
    step = s * nj + j
    last = pl.num_programs(0) * nj - 1
    slot = step % 2

    def out_copy(slot_, s_, j_):
        return pltpu.make_async_copy(obuf.at[slot_], o_hbm.at[pl.ds(j_ * nb, nb), :, s_, :],
                                     sem.at[slot_])

    @pl.when(step >= 2)
    def _():
        out_copy(slot, (step - 2) // nj, (step - 2) % nj).wait()

    w = y_ref.shape[-1] // n_sub
    bs = nb // n_sub

    def glu_in(k):
        g = _gelu_tanh(y_ref[:, k * w:(k + 1) * w].astype(F32))
        return g, jnp.dot(wg_ref[...], g.astype(BF16), preferred_element_type=F32)

    def proj(k, g, gl):
        y2 = g * _sigmoid(gl + bg_ref[...])
        z = z_ref[:, k * w:(k + 1) * w].astype(F32)
        gated = (y2 * (z * _sigmoid(z))).astype(BF16)
        return jnp.dot(wo_ref[...], gated, preferred_element_type=F32)

    def finish(k, ot):
        ms = jnp.mean(ot * ot, axis=0, keepdims=True)
        nt = (ot * lax.rsqrt(ms + EPS)).T
        rows = slice(k * bs, (k + 1) * bs)
        obuf[slot, rows] = x_ref[rows] + (nt * gp_ref[...]).reshape(bs, -1, nt.shape[-1])

    a = [glu_in(k) for k in range(n_sub)]
    o = [proj(k, *a[k]) for k in range(n_sub)]
    for k in range(n_sub):
        finish(k, o[k])
    out_copy(slot, s, j).start()

    @pl.when(step == last)
    def _():
        out_copy(slot, s, j).wait()

    @pl.when(jnp.logical_and(step == last, last >= 1))
    def _():
        out_copy(1 - slot, (step - 1) // nj, (step - 1) % nj).wait()


def _ssm_out(YT, UT, wgt, bg, wot, gp, xp, *, nb=8, n_sub=2):
    B, _, M, D = xp.shape
    E = YT.shape[1]
    out = pl.pallas_call(
        functools.partial(_ssm_out_kernel, nb=nb, n_sub=n_sub),
        grid=(N_PHASE, B // nb),
        in_specs=[
            pl.BlockSpec((None, E, nb * M), lambda s, j: (s, 0, j)),
            pl.BlockSpec((None, E, nb * M), lambda s, j: (s, 1, j)),
            pl.BlockSpec((E, E), lambda s, j: (0, 0)),
            pl.BlockSpec((E, 1), lambda s, j: (0, 0)),
            pl.BlockSpec((D, E), lambda s, j: (0, 0)),
            pl.BlockSpec((1, D), lambda s, j: (0, 0)),
            pl.BlockSpec((nb, None, M, D), lambda s, j: (j, s, 0, 0)),
        ],
        out_specs=pl.BlockSpec(memory_space=pl.ANY),
        out_shape=jax.ShapeDtypeStruct((B, M, N_PHASE, D), F32),
        scratch_shapes=[pltpu.VMEM((2, nb, M, D), F32), pltpu.SemaphoreType.DMA((2,))],
        compiler_params=_cparams(("arbitrary", "arbitrary")),
        name="ssm_out",
    )(YT, UT, wgt, bg.reshape(E, 1), wot, gp.reshape(1, D), xp)
    return out.reshape(B, M * N_PHASE, D)


def kernel(x, rel_bias, attn_pre_norm, attn_w_in, attn_w_out, attn_post_norm, ssm_pre_norm, ssm_w_in, ssm_a_re, ssm_a_im, ssm_log_dt, ssm_b_re, ssm_b_im, ssm_c_re, ssm_c_im, ssm_d, ssm_w_glu, ssm_b_glu, ssm_w_out, ssm_post_norm):
    B, S, D = x.shape
    n_chunk = S // N_PHASE

    n_q = 3 * HEADS * HEAD_DIM
    col_scale = jnp.where(jnp.arange(attn_w_in.shape[-1]) < n_q,
                          HEAD_DIM ** -0.5 * math.log2(math.e), 1.0).astype(F32)
    xp = jnp.transpose(x.reshape(B, n_chunk, N_PHASE, D), (0, 2, 1, 3))
    P = _norm_proj(xp, attn_pre_norm[0], (attn_w_in[0] * col_scale).astype(BF16))
    O = _attention(P, _bias_tables(rel_bias))
    xp = _out_proj(O, attn_w_out[0].astype(BF16), attn_post_norm[0], xp)

    prm = _ssm_params(ssm_a_re[0], ssm_a_im[0], ssm_log_dt[0], ssm_b_re[0], ssm_b_im[0],
                      ssm_c_re[0], ssm_c_im[0], ssm_d[0])
    UT = _ssm_in_proj(xp, ssm_pre_norm[0], ssm_w_in[0].T.astype(BF16))
    YT = _ssm_core(UT, prm, n_chunk)
    return _ssm_out(YT, UT, ssm_w_glu[0].T.astype(BF16), ssm_b_glu[0],
                    ssm_w_out[0].T.astype(BF16), ssm_post_norm[0], xp)
```

```python
import functools
import math

import numpy as np
import jax
import jax.numpy as jnp
from jax import lax
from jax.experimental import pallas as pl
from jax.experimental.pallas import tpu as pltpu

F32 = jnp.float32
BF16 = jnp.bfloat16

D_MODEL = 1024
HEAD_DIM = 64
HEADS = 16
N_PHASE = 16
BLK = 128
DILATIONS = (1, 4, 16)
REL_BUCKETS = 32
REL_MAX_DIST = 2048
SSM_GROUP = 16
SSM_N_GROUPS = 64
SSM_STATE = 64
EPS = 1e-6
NEG = -1e30
VMEM_LIMIT = 56 * 1024 * 1024


def _cparams(sem):
    return pltpu.CompilerParams(dimension_semantics=sem, vmem_limit_bytes=VMEM_LIMIT)


LOG2E = math.log2(math.e)


def _sigmoid(v):
    return 1.0 / (1.0 + jnp.exp2(v * (-LOG2E)))


def _gelu_tanh(x):
    k0 = -2.0 * math.sqrt(2.0 / math.pi) * LOG2E
    return x / (1.0 + jnp.exp2(x * (k0 + (k0 * 0.044715) * (x * x))))


def _norm_proj_kernel(x_ref, g_ref, w_ref, o_ref, h_ref, *, mh, scaled_tiles, scale):
    j = pl.program_id(2)

    @pl.when(j == 0)
    def _():
        x = x_ref[...].reshape(N_PHASE * mh, x_ref.shape[-1])
        ms = jnp.mean(x * x, axis=-1, keepdims=True)
        h_ref[...] = (x * lax.rsqrt(ms + EPS) * g_ref[...]).astype(BF16)

    w = (w_ref[...] * jnp.where(j < scaled_tiles, scale, 1.0).astype(F32)).astype(BF16)
    res = jnp.dot(h_ref[...], w, preferred_element_type=F32)
    o_ref[...] = res.reshape(N_PHASE, mh, res.shape[-1]).astype(BF16)


def _norm_proj(xp, g, w, *, scaled_cols, scale, mh=128, tn=1024):
    B, _, M, D = xp.shape
    N = w.shape[1]
    assert scaled_cols % tn == 0
    return pl.pallas_call(
        functools.partial(_norm_proj_kernel, mh=mh, scaled_tiles=scaled_cols // tn, scale=scale),
        grid=(B, M // mh, N // tn),
        in_specs=[
            pl.BlockSpec((None, N_PHASE, mh, D), lambda b, m, j: (b, 0, m, 0)),
            pl.BlockSpec((1, D), lambda b, m, j: (0, 0)),
            pl.BlockSpec((D, tn), lambda b, m, j: (0, j)),
        ],
        out_specs=pl.BlockSpec((None, N_PHASE, mh, tn), lambda b, m, j: (b, 0, m, j)),
        out_shape=jax.ShapeDtypeStruct((B, N_PHASE, M, N), BF16),
        scratch_shapes=[pltpu.VMEM((N_PHASE * mh, D), BF16)],
        compiler_params=_cparams(("parallel", "parallel", "arbitrary")),
        name="attn_norm_proj",
    )(xp, g.reshape(1, D), w)


def _attn_kernel(q0_ref, q1_ref, q2_ref, k0_ref, k1_ref, k2_ref, v0_ref, v1_ref, v2_ref,
                 z_ref, bm_ref, o_ref, qf_ref, kf_ref, vf_ref, acc_ref, l_ref, m_ref):
    W = 2 * HEAD_DIM
    lane = lax.broadcasted_iota(jnp.int32, (BLK, W), 1)
    first_head = lane < HEAD_DIM

    def logits(q, k, g, cur_only):
        zq = jnp.zeros_like(q)
        qs = jnp.concatenate([jnp.where(first_head, q, zq), jnp.where(first_head, zq, q)], axis=0)
        s = lax.dot_general(qs, k, (((1,), (1,)), ((), ())), preferred_element_type=F32)
        return s + (bm_ref[g, :, BLK:2 * BLK] if cur_only else bm_ref[g])

    def finish(s, v):
        m = jnp.max(s, axis=-1, keepdims=True)
        p = jnp.exp2(s - m).astype(BF16)
        va = jnp.concatenate([v, jnp.ones((v.shape[0], W), BF16)], axis=1)
        pv = jnp.dot(p, va, preferred_element_type=F32)
        acc = jnp.where(first_head, pv[:BLK, :W], pv[BLK:, :W])
        l = jnp.where(first_head, pv[:BLK, W:], pv[BLK:, W:])
        mm = jnp.where(first_head, m[:BLK], m[BLK:])
        return acc, l, mm

    def rows_of(ref, pieces):
        return jnp.concatenate([ref[ph, lo:lo + n, :] for ph, lo, n in pieces], axis=0)

    def keys_of(ref, prev, cur):
        return rows_of(ref, cur) if prev is None else rows_of(ref, prev + cur)

    def store(g, pieces, vals):
        at = 0
        for ph, lo, n in pieces:
            for ref, val in zip((acc_ref, l_ref, m_ref), vals):
                ref[g, ph, lo:lo + n, :] = val[at:at + n]
            at += n

    M = q0_ref.shape[1]
    g2_blocks = [[(r, 0, BLK)] for r in range(N_PHASE)]
    g1_blocks = {(r4, n): [(4 * q4 + r4, 32 * n, 32) for q4 in range(4)]
                 for r4 in range(4) for n in range(M // 32)}
    g0_blocks = [[(r, 8 * n, 8) for r in range(N_PHASE)] for n in range(M // 8)]

    for r in range(N_PHASE):
        qf_ref[r] = q0_ref[r].astype(F32)
        kf_ref[r] = k0_ref[r].astype(F32)
        vf_ref[r] = v0_ref[r].astype(F32)

    work = []
    for pieces in g2_blocks:
        work.append((2, pieces, None, (q2_ref, k2_ref, v2_ref), False))
    for (r4, n), pieces in g1_blocks.items():
        work.append((1, pieces, g1_blocks[(r4, n - 1)] if n else None, (q1_ref, k1_ref, v1_ref), False))
    for n, pieces in enumerate(g0_blocks):
        work.append((0, pieces, g0_blocks[n - 1] if n else None, (qf_ref, kf_ref, vf_ref), True))

    pending = None
    for item in work + [None]:
        if item is not None:
            g, pieces, prev, (q_r, k_r, _), cast = item
            q, k = rows_of(q_r, pieces), keys_of(k_r, prev, pieces)
            if cast:
                q, k = q.astype(BF16), k.astype(BF16)
            s = logits(q, k, g, prev is None)
        if pending is not None:
            (g_p, pieces_p, prev_p, (_, _, v_r), cast_p), s_p = pending
            v = keys_of(v_r, prev_p, pieces_p)
            store(g_p, pieces_p, finish(s_p, v.astype(BF16) if cast_p else v))
        pending = (item, s) if item is not None else None

    def combine(r, carry):
        m0, m1, m2 = m_ref[0, r], m_ref[1, r], m_ref[2, r]
        mx = jnp.maximum(jnp.maximum(m0, m1), m2)
        w0, w1, w2 = jnp.exp2(m0 - mx), jnp.exp2(m1 - mx), jnp.exp2(m2 - mx)
        num = w0 * acc_ref[0, r] + w1 * acc_ref[1, r] + w2 * acc_ref[2, r]
        den = w0 * l_ref[0, r] + w1 * l_ref[1, r] + w2 * l_ref[2, r]
        z = z_ref[r].astype(F32)
        o_ref[r] = (num / den * (z * _sigmoid(z))).astype(BF16)
        return carry

    lax.fori_loop(0, N_PHASE, combine, 0, unroll=2)


def _attention(P, bm):
    B, _, M, _ = P.shape
    HP = HEADS // 2
    W = 2 * HEAD_DIM

    def spec(kind, g):
        base = (kind * 3 + g) * HP
        return pl.BlockSpec((None, N_PHASE, M, W), lambda b, hp, base=base: (b, 0, 0, base + hp))

    in_specs = [spec(kind, g) for kind in range(3) for g in range(3)]
    in_specs.append(pl.BlockSpec((None, N_PHASE, M, W), lambda b, hp: (b, 0, 0, 9 * HP + hp)))
    in_specs.append(pl.BlockSpec((3, None, 2 * BLK, 2 * BLK), lambda b, hp: (0, hp, 0, 0)))
    return pl.pallas_call(
        _attn_kernel,
        grid=(B, HP),
        in_specs=in_specs,
        out_specs=pl.BlockSpec((None, N_PHASE, M, W), lambda b, hp: (b, 0, 0, hp)),
        out_shape=jax.ShapeDtypeStruct((B, N_PHASE, M, HEADS * HEAD_DIM), BF16),
        scratch_shapes=[pltpu.VMEM((N_PHASE, M, W), F32) for _ in range(3)]
        + [pltpu.VMEM((3, N_PHASE, M, W), F32) for _ in range(3)],
        compiler_params=_cparams(("parallel", "parallel")),
        name="dilated_attention",
    )(*([P] * 10), bm)


def _t5_bucket(dist):
    max_exact = REL_BUCKETS // 2
    n = jnp.maximum(dist, 1).astype(F32)
    large = max_exact + (jnp.log(n / max_exact) / math.log(REL_MAX_DIST / max_exact)
                         * (REL_BUCKETS - max_exact)).astype(jnp.int32)
    large = jnp.minimum(large, REL_BUCKETS - 1)
    return jnp.where(dist < max_exact, dist, large)


def _bias_tables(rel_bias):
    a = np.arange(BLK)
    pos = (16 * (a % 8) + a // 8, 4 * (a % 32) + a // 32, a)
    back = np.stack([np.concatenate([BLK + p[:, None] - p[None, :], p[:, None] - p[None, :]], axis=1)
                     for p in pos])
    valid = (back >= 0) & (back <= BLK)
    dist = np.clip(back, 0, BLK) * np.asarray(DILATIONS)[:, None, None]
    bucket = jnp.where(jnp.asarray(valid), _t5_bucket(jnp.asarray(dist, jnp.int32)), REL_BUCKETS)
    onehot = (bucket[..., None] == jnp.arange(REL_BUCKETS + 1)).astype(F32)
    ext = jnp.concatenate([rel_bias.astype(F32), jnp.full((1, HEADS), NEG, F32)], axis=0)
    t = jnp.einsum("gijc,ch->ghij", onehot, ext * math.log2(math.e),
                   precision=lax.Precision.HIGHEST)
    return t.reshape(3, HEADS // 2, 2 * BLK, 2 * BLK)


def _out_proj_kernel(o_ref, w_ref, g_ref, x_ref, y_ref, *, mh):
    o = o_ref[...].reshape(N_PHASE * mh, o_ref.shape[-1])
    h = jnp.dot(o, w_ref[...], preferred_element_type=F32)
    ms = jnp.mean(h * h, axis=-1, keepdims=True)
    y = h * lax.rsqrt(ms + EPS) * g_ref[...]
    y_ref[...] = x_ref[...] + y.reshape(y_ref.shape)


def _out_proj(O, w, g, xp, *, mh=64):
    B, _, M, D = xp.shape
    return pl.pallas_call(
        functools.partial(_out_proj_kernel, mh=mh),
        grid=(B, M // mh),
        in_specs=[
            pl.BlockSpec((None, N_PHASE, mh, O.shape[-1]), lambda b, m: (b, 0, m, 0)),
            pl.BlockSpec(w.shape, lambda b, m: (0, 0)),
            pl.BlockSpec((1, D), lambda b, m: (0, 0)),
            pl.BlockSpec((None, N_PHASE, mh, D), lambda b, m: (b, 0, m, 0)),
        ],
        out_specs=pl.BlockSpec((None, N_PHASE, mh, D), lambda b, m: (b, 0, m, 0)),
        out_shape=jax.ShapeDtypeStruct(xp.shape, F32),
        compiler_params=_cparams(("parallel", "parallel")),
        name="attn_out_proj",
    )(O, w, g.reshape(1, D), xp)


def _ssm_in_proj_kernel(x_ref, g_ref, wt_ref, o_ref, h_ref):
    M = x_ref.shape[1]

    @pl.when(pl.program_id(1) == 0)
    def _():
        g = g_ref[...]
        for s in range(N_PHASE):
            xs = x_ref[s]
            ms = jnp.mean(xs * xs, axis=-1, keepdims=True)
            h_ref[s * M:(s + 1) * M, :] = (xs * lax.rsqrt(ms + EPS) * g).astype(BF16)

    res = lax.dot_general(wt_ref[...], h_ref[...], (((1,), (1,)), ((), ())),
                          preferred_element_type=F32)
    for s in range(N_PHASE):
        o_ref[s] = res[:, s * M:(s + 1) * M].astype(BF16)


def _ssm_in_proj(xp, g, wt, *, tn=512):
    B, _, M, D = xp.shape
    S = N_PHASE * M
    N = wt.shape[0]
    return pl.pallas_call(
        _ssm_in_proj_kernel,
        grid=(B, N // tn),
        in_specs=[
            pl.BlockSpec((None, N_PHASE, M, D), lambda b, j: (b, 0, 0, 0)),
            pl.BlockSpec((1, D), lambda b, j: (0, 0)),
            pl.BlockSpec((tn, D), lambda b, j: (j, 0)),
        ],
        out_specs=pl.BlockSpec((N_PHASE, tn, M), lambda b, j: (0, j, b)),
        out_shape=jax.ShapeDtypeStruct((N_PHASE, N, B * M), BF16),
        scratch_shapes=[pltpu.VMEM((S, D), BF16)],
        compiler_params=_cparams(("parallel", "arbitrary")),
        name="ssm_in_proj",
    )(xp, g.reshape(1, D), wt)


def _ssm_core_kernel(u_ref, m_ref, bre_ref, bim_ref, cre_ref, cim_ref, are_ref, aim_ref,
                     d_ref, y_ref, sre_ref, sim_ref, ym_ref, *, n_chunk):
    C = SSM_GROUP
    QC = N_PHASE * C
    u = u_ref[...]
    N = u.shape[-1]
    nb = N // n_chunk
    us = [u[:, i * C:(i + 1) * C, :].reshape(QC, N) for i in range(2)]

    sre_ref[...] = jnp.concatenate([jnp.dot(bre_ref[i], us[i], preferred_element_type=F32)
                                    for i in range(2)], axis=0).T
    sim_ref[...] = jnp.concatenate([jnp.dot(bim_ref[i], us[i], preferred_element_type=F32)
                                    for i in range(2)], axis=0).T

    for i in range(2):
        ym_ref[i] = jnp.dot(m_ref[i], us[i], preferred_element_type=F32)

    ar = jnp.broadcast_to(are_ref[...], (nb, are_ref.shape[-1]))
    ai = jnp.broadcast_to(aim_ref[...], (nb, aim_ref.shape[-1]))
    sr = jnp.zeros_like(ar)
    si = jnp.zeros_like(ar)
    for c in range(n_chunk):
        rows = pl.ds(c, nb, stride=n_chunk)
        xr, xi = sre_ref[rows, :], sim_ref[rows, :]
        sre_ref[rows, :] = sr
        sim_ref[rows, :] = si
        sr, si = ar * sr - ai * si + xr, ar * si + ai * sr + xi

    nt = (((1,), (1,)), ((), ()))
    inter = (lax.dot_general(cre_ref[...], sre_ref[...].astype(BF16), nt, preferred_element_type=F32)
             + lax.dot_general(cim_ref[...], sim_ref[...].astype(BF16), nt, preferred_element_type=F32))
    for i in range(2):
        y = (ym_ref[i] + inter[i * QC:(i + 1) * QC]).reshape(N_PHASE, C, N)
        y = y + d_ref[i * C:(i + 1) * C] * u[:, i * C:(i + 1) * C, :].astype(F32)
        y_ref[:, i * C:(i + 1) * C, :] = y.astype(BF16)


def _ssm_core(UT, prm, n_chunk):
    _, _, N = UT.shape
    G2 = SSM_N_GROUPS // 2
    C2 = 2 * SSM_GROUP
    QC = N_PHASE * SSM_GROUP
    P = SSM_STATE
    return pl.pallas_call(
        functools.partial(_ssm_core_kernel, n_chunk=n_chunk),
        grid=(G2,),
        in_specs=[
            pl.BlockSpec((N_PHASE, C2, N), lambda g: (0, g, 0)),
            pl.BlockSpec((2, QC, QC), lambda g: (g, 0, 0)),
            pl.BlockSpec((2, P, QC), lambda g: (g, 0, 0)),
            pl.BlockSpec((2, P, QC), lambda g: (g, 0, 0)),
            pl.BlockSpec((None, 2 * QC, 2 * P), lambda g: (g, 0, 0)),
            pl.BlockSpec((None, 2 * QC, 2 * P), lambda g: (g, 0, 0)),
            pl.BlockSpec((None, 1, 2 * P), lambda g: (g, 0, 0)),
            pl.BlockSpec((None, 1, 2 * P), lambda g: (g, 0, 0)),
            pl.BlockSpec((None, C2, 1), lambda g: (g, 0, 0)),
        ],
        out_specs=pl.BlockSpec((N_PHASE, C2, N), lambda g: (0, g, 0)),
        out_shape=jax.ShapeDtypeStruct((N_PHASE, SSM_N_GROUPS * SSM_GROUP, N), BF16),
        scratch_shapes=[pltpu.VMEM((N, 2 * P), F32), pltpu.VMEM((N, 2 * P), F32),
                        pltpu.VMEM((2, QC, N), F32)],
        compiler_params=_cparams(("parallel",)),
        name="ssm_core",
    )(UT, prm["m"], prm["b_re"], prm["b_im"], prm["c_re"], prm["c_im"],
      prm["a_re"], prm["a_im"], prm["d"])


def _ssm_param_kernel(arc_ref, aic_ref, arr_ref, air_ref, ldr_ref, bre_ref, bim_ref,
                      cre_ref, cim_ref, cpr_ref, cpi_ref,
                      m_ref, br_ref, bi_ref, cr_ref, ci_ref, ar_ref, ai_ref):
    Q, C, P = N_PHASE, SSM_GROUP, SSM_STATE
    QC = Q * C
    hi = lax.Precision.HIGHEST
    tile = (lax.broadcasted_iota(jnp.int32, (C, QC), 1) % C
            == lax.broadcasted_iota(jnp.int32, (C, QC), 0)).astype(F32)
    lane_s = lax.broadcasted_iota(jnp.int32, (P, QC), 1) // C

    dt = jnp.exp(ldr_ref[...])
    mag = jnp.exp(arr_ref[...] * dt)
    a1r, a1i = mag * jnp.cos(air_ref[...] * dt), mag * jnp.sin(air_ref[...] * dt)
    a1r_c = jnp.broadcast_to(a1r, (8, 2 * P)).T[:, :1]
    a1i_c = jnp.broadcast_to(a1i, (8, 2 * P)).T[:, :1]

    for i in range(2):
        a_re, a_im = arc_ref[i], aic_ref[i]
        abr, abi = a1r_c[i * P:(i + 1) * P], a1i_c[i * P:(i + 1) * P]
        den = a_re * a_re + a_im * a_im
        cfr = ((abr - 1.0) * a_re + abi * a_im) / den
        cfi = (abi * a_re - (abr - 1.0) * a_im) / den
        bbr = cfr * bre_ref[i] - cfi * bim_ref[i]
        bbi = cfr * bim_ref[i] + cfi * bre_ref[i]

        pr, pi = jnp.ones_like(abr), jnp.zeros_like(abr)
        pw_r = jnp.zeros((P, QC), F32)
        pw_i = jnp.zeros((P, QC), F32)
        for n in range(Q):
            sel = lane_s == Q - 1 - n
            pw_r = jnp.where(sel, pr, pw_r)
            pw_i = jnp.where(sel, pi, pw_i)
            pr, pi = pr * abr - pi * abi, pr * abi + pi * abr
        bbr_t = jnp.dot(bbr, tile, precision=hi, preferred_element_type=F32)
        bbi_t = jnp.dot(bbi, tile, precision=hi, preferred_element_type=F32)
        bst_r = pw_r * bbr_t - pw_i * bbi_t
        bst_i = pw_r * bbi_t + pw_i * bbr_t
        br_ref[i] = bst_r.astype(BF16)
        bi_ref[i] = bst_i.astype(BF16)

        kl = (jnp.dot(cre_ref[i], bst_r, precision=hi, preferred_element_type=F32)
              - jnp.dot(cim_ref[i], bst_i, precision=hi, preferred_element_type=F32))
        klz = jnp.concatenate([kl, jnp.zeros_like(kl)], axis=1)
        for t in range(Q):
            sh = (2 * QC - C * (Q - 1 - t)) % (2 * QC)
            row = klz if sh == 0 else pltpu.roll(klz, sh, axis=1)
            m_ref[i, t * C:(t + 1) * C, :] = row[:, :QC].astype(BF16)

    qr, qi = a1r, a1i
    for t in range(Q):
        for i in range(2):
            rows = slice(i * QC + t * C, i * QC + (t + 1) * C)
            cr_ref[rows, :] = (cpr_ref[i] * qr - cpi_ref[i] * qi).astype(BF16)
            ci_ref[rows, :] = (-(cpr_ref[i] * qi + cpi_ref[i] * qr)).astype(BF16)
        if t < Q - 1:
            qr, qi = qr * a1r - qi * a1i, qr * a1i + qi * a1r
    ar_ref[...] = qr
    ai_ref[...] = qi


def _ssm_params(a_re, a_im, log_dt, b_re, b_im, c_re, c_im, d_skip):
    G, P, C, Q = SSM_N_GROUPS, SSM_STATE, SSM_GROUP, N_PHASE
    QC = Q * C
    f = lambda t: t.astype(F32)
    a_re, a_im, log_dt, b_re, b_im, c_re, c_im = map(f, (a_re, a_im, log_dt, b_re, b_im, c_re, c_im))
    even = (jnp.arange(G) % 2 == 0)[:, None, None]

    def lane_half(c):
        return jnp.concatenate([jnp.where(even, c, 0.0), jnp.where(even, 0.0, c)], axis=-1)

    pair = lambda blk: pl.BlockSpec((2,) + blk, lambda g: (g,) + (0,) * len(blk))
    one = lambda blk: pl.BlockSpec((None,) + blk, lambda g: (g,) + (0,) * len(blk))
    outs = pl.pallas_call(
        _ssm_param_kernel,
        grid=(G // 2,),
        in_specs=[pair((P, 1)), pair((P, 1)), one((1, 2 * P)), one((1, 2 * P)),
                  one((1, 2 * P)), pair((P, C)), pair((P, C)), pair((C, P)), pair((C, P)),
                  pair((C, 2 * P)), pair((C, 2 * P))],
        out_specs=[pair((QC, QC)), pair((P, QC)), pair((P, QC)), one((2 * QC, 2 * P)),
                   one((2 * QC, 2 * P)), one((1, 2 * P)), one((1, 2 * P))],
        out_shape=[jax.ShapeDtypeStruct((G, QC, QC), BF16), jax.ShapeDtypeStruct((G, P, QC), BF16),
                   jax.ShapeDtypeStruct((G, P, QC), BF16),
                   jax.ShapeDtypeStruct((G // 2, 2 * QC, 2 * P), BF16),
                   jax.ShapeDtypeStruct((G // 2, 2 * QC, 2 * P), BF16),
                   jax.ShapeDtypeStruct((G // 2, 1, 2 * P), F32),
                   jax.ShapeDtypeStruct((G // 2, 1, 2 * P), F32)],
        compiler_params=_cparams(("parallel",)),
        name="ssm_params",
    )(a_re.reshape(G, P, 1), a_im.reshape(G, P, 1),
      a_re.reshape(G // 2, 1, 2 * P), a_im.reshape(G // 2, 1, 2 * P),
      jnp.repeat(log_dt, P).reshape(G // 2, 1, 2 * P), b_re, b_im, c_re, c_im,
      lane_half(c_re), lane_half(c_im))
    keys = ("m", "b_re", "b_im", "c_re", "c_im", "a_re", "a_im")
    prm = dict(zip(keys, outs))
    prm["d"] = d_skip.astype(F32).reshape(G // 2, 2 * C, 1)
    return prm


def _ssm_out_kernel(y_ref, z_ref, wg_ref, bg_ref, wo_ref, gp_ref, x_ref, o_hbm, obuf, sem, *,
                    nb, n_sub):
    s, j = pl.program_id(0), pl.program_id(1)
    nj = pl.num_programs(1)
    step = s * nj + j
    last = pl.num_programs(0) * nj - 1
    slot = step % 2

    def out_copy(slot_, s_, j_):
        return pltpu.make_async_copy(obuf.at[slot_], o_hbm.at[pl.ds(j_ * nb, nb), :, s_, :],
                                     sem.at[slot_])

    @pl.when(step >= 2)
    def _():
        out_copy(slot, (step - 2) // nj, (step - 2) % nj).wait()

    w = y_ref.shape[-1] // n_sub
    bs = nb // n_sub

    def glu_in(k):
        g = _gelu_tanh(y_ref[:, k * w:(k + 1) * w].astype(F32))
        return g, jnp.dot(wg_ref[...], g.astype(BF16), preferred_element_type=F32)

    def proj(k, g, gl):
        y2 = g * _sigmoid(gl + bg_ref[...])
        z = z_ref[:, k * w:(k + 1) * w].astype(F32)
        gated = (y2 * (z * _sigmoid(z))).astype(BF16)
        return jnp.dot(wo_ref[...], gated, preferred_element_type=F32)

    def finish(k, ot):
        ms = jnp.mean(ot * ot, axis=0, keepdims=True)
        nt = (ot * lax.rsqrt(ms + EPS)).T
        rows = slice(k * bs, (k + 1) * bs)
        obuf[slot, rows] = x_ref[rows] + (nt * gp_ref[...]).reshape(bs, -1, nt.shape[-1])

    a = [glu_in(k) for k in range(n_sub)]
    o = [proj(k, *a[k]) for k in range(n_sub)]
    for k in range(n_sub):
        finish(k, o[k])
    out_copy(slot, s, j).start()

    @pl.when(step == last)
    def _():
        out_copy(slot, s, j).wait()

    @pl.when(jnp.logical_and(step == last, last >= 1))
    def _():
        out_copy(1 - slot, (step - 1) // nj, (step - 1) % nj).wait()


def _ssm_out(YT, UT, wgt, bg, wot, gp, xp, *, nb=8, n_sub=2):
    B, _, M, D = xp.shape
    E = YT.shape[1]
    out = pl.pallas_call(
        functools.partial(_ssm_out_kernel, nb=nb, n_sub=n_sub),
        grid=(N_PHASE, B // nb),
        in_specs=[
            pl.BlockSpec((None, E, nb * M), lambda s, j: (s, 0, j)),
            pl.BlockSpec((None, E, nb * M), lambda s, j: (s, 1, j)),
            pl.BlockSpec((E, E), lambda s, j: (0, 0)),
            pl.BlockSpec((E, 1), lambda s, j: (0, 0)),
            pl.BlockSpec((D, E), lambda s, j: (0, 0)),
            pl.BlockSpec((1, D), lambda s, j: (0, 0)),
            pl.BlockSpec((nb, None, M, D), lambda s, j: (j, s, 0, 0)),
        ],
        out_specs=pl.BlockSpec(memory_space=pl.ANY),
        out_shape=jax.ShapeDtypeStruct((B, M, N_PHASE, D), F32),
        scratch_shapes=[pltpu.VMEM((2, nb, M, D), F32), pltpu.SemaphoreType.DMA((2,))],
        compiler_params=_cparams(("arbitrary", "arbitrary")),
        name="ssm_out",
    )(YT, UT, wgt, bg.reshape(E, 1), wot, gp.reshape(1, D), xp)
    return out.reshape(B, M * N_PHASE, D)


def kernel(x, rel_bias, attn_pre_norm, attn_w_in, attn_w_out, attn_post_norm, ssm_pre_norm, ssm_w_in, ssm_a_re, ssm_a_im, ssm_log_dt, ssm_b_re, ssm_b_im, ssm_c_re, ssm_c_im, ssm_d, ssm_w_glu, ssm_b_glu, ssm_w_out, ssm_post_norm):
    B, S, D = x.shape
    n_chunk = S // N_PHASE

    xp = jnp.transpose(x.reshape(B, n_chunk, N_PHASE, D), (0, 2, 1, 3))
    P = _norm_proj(xp, attn_pre_norm[0], attn_w_in[0], scaled_cols=3 * HEADS * HEAD_DIM,
                   scale=HEAD_DIM ** -0.5 * LOG2E)
    O = _attention(P, _bias_tables(rel_bias))
    xp = _out_proj(O, attn_w_out[0].astype(BF16), attn_post_norm[0], xp)

    prm = _ssm_params(ssm_a_re[0], ssm_a_im[0], ssm_log_dt[0], ssm_b_re[0], ssm_b_im[0],
                      ssm_c_re[0], ssm_c_im[0], ssm_d[0])
    UT = _ssm_in_proj(xp, ssm_pre_norm[0], ssm_w_in[0].T.astype(BF16))
    YT = _ssm_core(UT, prm, n_chunk)
    return _ssm_out(YT, UT, ssm_w_glu[0].T.astype(BF16), ssm_b_glu[0],
                    ssm_w_out[0].T.astype(BF16), ssm_post_norm[0], xp)
```

```python
import functools
import math

import numpy as np
import jax
import jax.numpy as jnp
from jax import lax
from jax.experimental import pallas as pl
from jax.experimental.pallas import tpu as pltpu

F32 = jnp.float32
BF16 = jnp.bfloat16

D_MODEL = 1024
HEAD_DIM = 64
HEADS = 16
N_PHASE = 16
BLK = 128
DILATIONS = (1, 4, 16)
REL_BUCKETS = 32
REL_MAX_DIST = 2048
SSM_GROUP = 16
SSM_N_GROUPS = 64
SSM_STATE = 64
EPS = 1e-6
NEG = -1e30
VMEM_LIMIT = 56 * 1024 * 1024


def _cparams(sem):
    return pltpu.CompilerParams(dimension_semantics=sem, vmem_limit_bytes=VMEM_LIMIT)


LOG2E = math.log2(math.e)


def _sigmoid(v):
    return 1.0 / (1.0 + jnp.exp2(v * (-LOG2E)))


def _gelu_tanh(x):
    k0 = -2.0 * math.sqrt(2.0 / math.pi) * LOG2E
    return x / (1.0 + jnp.exp2(x * (k0 + (k0 * 0.044715) * (x * x))))


def _norm_proj_kernel(x_ref, g_ref, w_ref, o_ref, h_ref, *, mh, scaled_tiles, scale):
    j = pl.program_id(2)

    @pl.when(j == 0)
    def _():
        x = x_ref[...].reshape(N_PHASE * mh, x_ref.shape[-1])
        ms = jnp.mean(x * x, axis=-1, keepdims=True)
        h_ref[...] = (x * lax.rsqrt(ms + EPS) * g_ref[...]).astype(BF16)

    w = (w_ref[...] * jnp.where(j < scaled_tiles, scale, 1.0).astype(F32)).astype(BF16)
    res = jnp.dot(h_ref[...], w, preferred_element_type=F32)
    o_ref[...] = res.reshape(N_PHASE, mh, res.shape[-1]).astype(BF16)


def _norm_proj(xp, g, w, *, scaled_cols, scale, mh=128, tn=1024):
    B, _, M, D = xp.shape
    N = w.shape[1]
    assert scaled_cols % tn == 0
    return pl.pallas_call(
        functools.partial(_norm_proj_kernel, mh=mh, scaled_tiles=scaled_cols // tn, scale=scale),
        grid=(B, M // mh, N // tn),
        in_specs=[
            pl.BlockSpec((None, N_PHASE, mh, D), lambda b, m, j: (b, 0, m, 0)),
            pl.BlockSpec((1, D), lambda b, m, j: (0, 0)),
            pl.BlockSpec((D, tn), lambda b, m, j: (0, j)),
        ],
        out_specs=pl.BlockSpec((None, N_PHASE, mh, tn), lambda b, m, j: (b, 0, m, j)),
        out_shape=jax.ShapeDtypeStruct((B, N_PHASE, M, N), BF16),
        scratch_shapes=[pltpu.VMEM((N_PHASE * mh, D), BF16)],
        compiler_params=_cparams(("parallel", "parallel", "arbitrary")),
        name="attn_norm_proj",
    )(xp, g.reshape(1, D), w)


def _attn_kernel(q0_ref, q1_ref, q2_ref, k0_ref, k1_ref, k2_ref, v0_ref, v1_ref, v2_ref,
                 z_ref, bm_ref, o_ref, qf_ref, kf_ref, vf_ref, acc_ref, l_ref, m_ref):
    W = 2 * HEAD_DIM
    AHEAD = 3
    lane = lax.broadcasted_iota(jnp.int32, (BLK, W), 1)
    first_head = lane < HEAD_DIM

    def logits(q, k, g, cur_only):
        zq = jnp.zeros_like(q)
        qs = jnp.concatenate([jnp.where(first_head, q, zq), jnp.where(first_head, zq, q)], axis=0)
        s = lax.dot_general(qs, k, (((1,), (1,)), ((), ())), preferred_element_type=F32)
        return s + (bm_ref[g, :, BLK:2 * BLK] if cur_only else bm_ref[g])

    def finish(s, v):
        m = jnp.max(s, axis=-1, keepdims=True)
        p = jnp.exp2(s - m).astype(BF16)
        va = jnp.concatenate([v, jnp.ones((v.shape[0], W), BF16)], axis=1)
        pv = jnp.dot(p, va, preferred_element_type=F32)
        acc = jnp.where(first_head, pv[:BLK, :W], pv[BLK:, :W])
        l = jnp.where(first_head, pv[:BLK, W:], pv[BLK:, W:])
        mm = jnp.where(first_head, m[:BLK], m[BLK:])
        return acc, l, mm

    def rows_of(ref, pieces):
        return jnp.concatenate([ref[ph, lo:lo + n, :] for ph, lo, n in pieces], axis=0)

    def keys_of(ref, prev, cur):
        return rows_of(ref, cur) if prev is None else rows_of(ref, prev + cur)

    def store(g, pieces, vals):
        at = 0
        for ph, lo, n in pieces:
            for ref, val in zip((acc_ref, l_ref, m_ref), vals):
                ref[g, ph, lo:lo + n, :] = val[at:at + n]
            at += n

    M = q0_ref.shape[1]
    g2_blocks = [[(r, 0, BLK)] for r in range(N_PHASE)]
    g1_blocks = {(r4, n): [(4 * q4 + r4, 32 * n, 32) for q4 in range(4)]
                 for r4 in range(4) for n in range(M // 32)}
    g0_blocks = [[(r, 8 * n, 8) for r in range(N_PHASE)] for n in range(M // 8)]

    for r in range(N_PHASE):
        qf_ref[r] = q0_ref[r].astype(F32)
        kf_ref[r] = k0_ref[r].astype(F32)
        vf_ref[r] = v0_ref[r].astype(F32)

    work = []
    for n, pieces in enumerate(g0_blocks):
        work.append((0, pieces, g0_blocks[n - 1] if n else None, (qf_ref, kf_ref, vf_ref), True))
    for (r4, n), pieces in g1_blocks.items():
        work.append((1, pieces, g1_blocks[(r4, n - 1)] if n else None, (q1_ref, k1_ref, v1_ref), False))
    for pieces in g2_blocks:
        work.append((2, pieces, None, (q2_ref, k2_ref, v2_ref), False))

    def merge(r, acc2, l2, m2):
        m0, m1 = m_ref[0, r], m_ref[1, r]
        mx = jnp.maximum(jnp.maximum(m0, m1), m2)
        w0, w1, w2 = jnp.exp2(m0 - mx), jnp.exp2(m1 - mx), jnp.exp2(m2 - mx)
        num = w0 * acc_ref[0, r] + w1 * acc_ref[1, r] + w2 * acc2
        den = w0 * l_ref[0, r] + w1 * l_ref[1, r] + w2 * l2
        z = z_ref[r].astype(F32)
        o_ref[r] = (num * z / (den * (1.0 + jnp.exp2(z * (-LOG2E))))).astype(BF16)

    pending = []
    for item in work + [None] * AHEAD:
        if item is not None:
            g, pieces, prev, (q_r, k_r, _), cast = item
            q, k = rows_of(q_r, pieces), keys_of(k_r, prev, pieces)
            if cast:
                q, k = q.astype(BF16), k.astype(BF16)
            pending.append((item, logits(q, k, g, prev is None)))
        if item is None or len(pending) > AHEAD:
            (g_p, pieces_p, prev_p, (_, _, v_r), cast_p), s_p = pending.pop(0)
            v = keys_of(v_r, prev_p, pieces_p)
            vals = finish(s_p, v.astype(BF16) if cast_p else v)
            if g_p == 2:
                merge(pieces_p[0][0], *vals)
            else:
                store(g_p, pieces_p, vals)


def _attention(P, bm):
    B, _, M, _ = P.shape
    HP = HEADS // 2
    W = 2 * HEAD_DIM

    def spec(kind, g):
        base = (kind * 3 + g) * HP
        return pl.BlockSpec((None, N_PHASE, M, W), lambda b, hp, base=base: (b, 0, 0, base + hp))

    in_specs = [spec(kind, g) for kind in range(3) for g in range(3)]
    in_specs.append(pl.BlockSpec((None, N_PHASE, M, W), lambda b, hp: (b, 0, 0, 9 * HP + hp)))
    in_specs.append(pl.BlockSpec((3, None, 2 * BLK, 2 * BLK), lambda b, hp: (0, hp, 0, 0)))
    return pl.pallas_call(
        _attn_kernel,
        grid=(B, HP),
        in_specs=in_specs,
        out_specs=pl.BlockSpec((None, N_PHASE, M, W), lambda b, hp: (b, 0, 0, hp)),
        out_shape=jax.ShapeDtypeStruct((B, N_PHASE, M, HEADS * HEAD_DIM), BF16),
        scratch_shapes=[pltpu.VMEM((N_PHASE, M, W), F32) for _ in range(3)]
        + [pltpu.VMEM((2, N_PHASE, M, W), F32) for _ in range(3)],
        compiler_params=_cparams(("parallel", "parallel")),
        name="dilated_attention",
    )(*([P] * 10), bm)


def _t5_bucket(dist):
    max_exact = REL_BUCKETS // 2
    n = jnp.maximum(dist, 1).astype(F32)
    large = max_exact + (jnp.log(n / max_exact) / math.log(REL_MAX_DIST / max_exact)
                         * (REL_BUCKETS - max_exact)).astype(jnp.int32)
    large = jnp.minimum(large, REL_BUCKETS - 1)
    return jnp.where(dist < max_exact, dist, large)


def _bias_tables(rel_bias):
    a = np.arange(BLK)
    pos = (16 * (a % 8) + a // 8, 4 * (a % 32) + a // 32, a)
    back = np.stack([np.concatenate([BLK + p[:, None] - p[None, :], p[:, None] - p[None, :]], axis=1)
                     for p in pos])
    valid = (back >= 0) & (back <= BLK)
    dist = np.clip(back, 0, BLK) * np.asarray(DILATIONS)[:, None, None]
    bucket = jnp.where(jnp.asarray(valid), _t5_bucket(jnp.asarray(dist, jnp.int32)), REL_BUCKETS)
    onehot = (bucket[..., None] == jnp.arange(REL_BUCKETS + 1)).astype(F32)
    ext = jnp.concatenate([rel_bias.astype(F32), jnp.full((1, HEADS), NEG, F32)], axis=0)
    t = jnp.einsum("gijc,ch->ghij", onehot, ext * math.log2(math.e),
                   precision=lax.Precision.HIGHEST)
    return t.reshape(3, HEADS // 2, 2 * BLK, 2 * BLK)


def _out_proj_kernel(o_ref, w_ref, g_ref, x_ref, y_ref, *, mh):
    o = o_ref[...].reshape(N_PHASE * mh, o_ref.shape[-1])
    h = jnp.dot(o, w_ref[...], preferred_element_type=F32)
    ms = jnp.mean(h * h, axis=-1, keepdims=True)
    y = h * lax.rsqrt(ms + EPS) * g_ref[...]
    y_ref[...] = x_ref[...] + y.reshape(y_ref.shape)


def _out_proj(O, w, g, xp, *, mh=64):
    B, _, M, D = xp.shape
    return pl.pallas_call(
        functools.partial(_out_proj_kernel, mh=mh),
        grid=(B, M // mh),
        in_specs=[
            pl.BlockSpec((None, N_PHASE, mh, O.shape[-1]), lambda b, m: (b, 0, m, 0)),
            pl.BlockSpec(w.shape, lambda b, m: (0, 0)),
            pl.BlockSpec((1, D), lambda b, m: (0, 0)),
            pl.BlockSpec((None, N_PHASE, mh, D), lambda b, m: (b, 0, m, 0)),
        ],
        out_specs=pl.BlockSpec((None, N_PHASE, mh, D), lambda b, m: (b, 0, m, 0)),
        out_shape=jax.ShapeDtypeStruct(xp.shape, F32),
        compiler_params=_cparams(("parallel", "parallel")),
        name="attn_out_proj",
    )(O, w, g.reshape(1, D), xp)


def _ssm_in_proj_kernel(x_ref, g_ref, wt_ref, o_ref, h_ref):
    M = x_ref.shape[1]

    @pl.when(pl.program_id(1) == 0)
    def _():
        g = g_ref[...]
        for s in range(N_PHASE):
            xs = x_ref[s]
            ms = jnp.mean(xs * xs, axis=-1, keepdims=True)
            h_ref[s * M:(s + 1) * M, :] = (xs * lax.rsqrt(ms + EPS) * g).astype(BF16)

    res = lax.dot_general(wt_ref[...], h_ref[...], (((1,), (1,)), ((), ())),
                          preferred_element_type=F32)
    for s in range(N_PHASE):
        o_ref[s] = res[:, s * M:(s + 1) * M].astype(BF16)


def _ssm_in_proj(xp, g, wt, *, tn=512):
    B, _, M, D = xp.shape
    S = N_PHASE * M
    N = wt.shape[0]
    return pl.pallas_call(
        _ssm_in_proj_kernel,
        grid=(B, N // tn),
        in_specs=[
            pl.BlockSpec((None, N_PHASE, M, D), lambda b, j: (b, 0, 0, 0)),
            pl.BlockSpec((1, D), lambda b, j: (0, 0)),
            pl.BlockSpec((tn, D), lambda b, j: (j, 0)),
        ],
        out_specs=pl.BlockSpec((N_PHASE, tn, M), lambda b, j: (0, j, b)),
        out_shape=jax.ShapeDtypeStruct((N_PHASE, N, B * M), BF16),
        scratch_shapes=[pltpu.VMEM((S, D), BF16)],
        compiler_params=_cparams(("parallel", "arbitrary")),
        name="ssm_in_proj",
    )(xp, g.reshape(1, D), wt)


def _ssm_core_kernel(u_ref, m_ref, bre_ref, bim_ref, cre_ref, cim_ref, are_ref, aim_ref,
                     d_ref, y_ref, sre_ref, sim_ref, ym_ref, *, n_chunk, n_pair):
    C = SSM_GROUP
    QC = N_PHASE * C
    u = u_ref[...]
    N = u.shape[-1]
    nb = N // n_chunk
    n_grp = 2 * n_pair
    us = [u[:, i * C:(i + 1) * C, :].reshape(QC, N) for i in range(n_grp)]

    for k in range(n_pair):
        sre_ref[k] = jnp.concatenate([jnp.dot(bre_ref[i], us[i], preferred_element_type=F32)
                                      for i in (2 * k, 2 * k + 1)], axis=0).T
        sim_ref[k] = jnp.concatenate([jnp.dot(bim_ref[i], us[i], preferred_element_type=F32)
                                      for i in (2 * k, 2 * k + 1)], axis=0).T

    for i in range(n_grp):
        ym_ref[i] = jnp.dot(m_ref[i], us[i], preferred_element_type=F32)

    ar = [jnp.broadcast_to(are_ref[k], (nb, are_ref.shape[-1])) for k in range(n_pair)]
    ai = [jnp.broadcast_to(aim_ref[k], (nb, aim_ref.shape[-1])) for k in range(n_pair)]
    sr = [jnp.zeros_like(ar[0])] * n_pair
    si = [jnp.zeros_like(ar[0])] * n_pair
    for c in range(n_chunk):
        rows = pl.ds(c, nb, stride=n_chunk)
        for k in range(n_pair):
            xr, xi = sre_ref[k, rows, :], sim_ref[k, rows, :]
            sre_ref[k, rows, :] = sr[k]
            sim_ref[k, rows, :] = si[k]
            sr[k], si[k] = (ar[k] * sr[k] - ai[k] * si[k] + xr, ar[k] * si[k] + ai[k] * sr[k] + xi)

    nt = (((1,), (1,)), ((), ()))
    for k in range(n_pair):
        inter = (lax.dot_general(cre_ref[k], sre_ref[k].astype(BF16), nt, preferred_element_type=F32)
                 + lax.dot_general(cim_ref[k], sim_ref[k].astype(BF16), nt,
                                   preferred_element_type=F32))
        for i in (2 * k, 2 * k + 1):
            y = (ym_ref[i] + inter[(i % 2) * QC:(i % 2 + 1) * QC]).reshape(N_PHASE, C, N)
            y = y + d_ref[i * C:(i + 1) * C] * u[:, i * C:(i + 1) * C, :].astype(F32)
            y_ref[:, i * C:(i + 1) * C, :] = y.astype(BF16)


def _ssm_core(UT, prm, n_chunk, *, n_pair=2):
    _, _, N = UT.shape
    G2 = SSM_N_GROUPS // 2
    C2 = 2 * SSM_GROUP
    QC = N_PHASE * SSM_GROUP
    P = SSM_STATE
    blk = lambda *shape: pl.BlockSpec(shape, lambda g: (g,) + (0,) * (len(shape) - 1))
    return pl.pallas_call(
        functools.partial(_ssm_core_kernel, n_chunk=n_chunk, n_pair=n_pair),
        grid=(G2 // n_pair,),
        in_specs=[
            pl.BlockSpec((N_PHASE, n_pair * C2, N), lambda g: (0, g, 0)),
            blk(2 * n_pair, QC, QC), blk(2 * n_pair, P, QC), blk(2 * n_pair, P, QC),
            blk(n_pair, 2 * QC, 2 * P), blk(n_pair, 2 * QC, 2 * P),
            blk(n_pair, 1, 2 * P), blk(n_pair, 1, 2 * P),
            blk(n_pair * C2, 1),
        ],
        out_specs=pl.BlockSpec((N_PHASE, n_pair * C2, N), lambda g: (0, g, 0)),
        out_shape=jax.ShapeDtypeStruct((N_PHASE, SSM_N_GROUPS * SSM_GROUP, N), BF16),
        scratch_shapes=[pltpu.VMEM((n_pair, N, 2 * P), F32), pltpu.VMEM((n_pair, N, 2 * P), F32),
                        pltpu.VMEM((2 * n_pair, QC, N), F32)],
        compiler_params=_cparams(("parallel",)),
        name="ssm_core",
    )(UT, prm["m"], prm["b_re"], prm["b_im"], prm["c_re"], prm["c_im"],
      prm["a_re"], prm["a_im"], prm["d"])


def _ssm_param_kernel(arc_ref, aic_ref, arr_ref, air_ref, ldr_ref, bre_ref, bim_ref,
                      cre_ref, cim_ref, cpr_ref, cpi_ref,
                      m_ref, br_ref, bi_ref, cr_ref, ci_ref, ar_ref, ai_ref):
    Q, C, P = N_PHASE, SSM_GROUP, SSM_STATE
    QC = Q * C
    hi = lax.Precision.HIGHEST
    tile = (lax.broadcasted_iota(jnp.int32, (C, QC), 1) % C
            == lax.broadcasted_iota(jnp.int32, (C, QC), 0)).astype(F32)
    lane_s = lax.broadcasted_iota(jnp.int32, (P, QC), 1) // C

    dt = jnp.exp(ldr_ref[...])
    mag = jnp.exp(arr_ref[...] * dt)
    a1r, a1i = mag * jnp.cos(air_ref[...] * dt), mag * jnp.sin(air_ref[...] * dt)
    a1r_c = jnp.broadcast_to(a1r, (8, 2 * P)).T[:, :1]
    a1i_c = jnp.broadcast_to(a1i, (8, 2 * P)).T[:, :1]

    for i in range(2):
        a_re, a_im = arc_ref[i], aic_ref[i]
        abr, abi = a1r_c[i * P:(i + 1) * P], a1i_c[i * P:(i + 1) * P]
        den = a_re * a_re + a_im * a_im
        cfr = ((abr - 1.0) * a_re + abi * a_im) / den
        cfi = (abi * a_re - (abr - 1.0) * a_im) / den
        bbr = cfr * bre_ref[i] - cfi * bim_ref[i]
        bbi = cfr * bim_ref[i] + cfi * bre_ref[i]

        pr, pi = jnp.ones_like(abr), jnp.zeros_like(abr)
        pw_r = jnp.zeros((P, QC), F32)
        pw_i = jnp.zeros((P, QC), F32)
        for n in range(Q):
            sel = lane_s == Q - 1 - n
            pw_r = jnp.where(sel, pr, pw_r)
            pw_i = jnp.where(sel, pi, pw_i)
            pr, pi = pr * abr - pi * abi, pr * abi + pi * abr
        bbr_t = jnp.dot(bbr, tile, precision=hi, preferred_element_type=F32)
        bbi_t = jnp.dot(bbi, tile, precision=hi, preferred_element_type=F32)
        bst_r = pw_r * bbr_t - pw_i * bbi_t
        bst_i = pw_r * bbi_t + pw_i * bbr_t
        br_ref[i] = bst_r.astype(BF16)
        bi_ref[i] = bst_i.astype(BF16)

        kl = (jnp.dot(cre_ref[i], bst_r, precision=hi, preferred_element_type=F32)
              - jnp.dot(cim_ref[i], bst_i, precision=hi, preferred_element_type=F32))
        klz = jnp.concatenate([kl, jnp.zeros_like(kl)], axis=1)
        for t in range(Q):
            sh = (2 * QC - C * (Q - 1 - t)) % (2 * QC)
            row = klz if sh == 0 else pltpu.roll(klz, sh, axis=1)
            m_ref[i, t * C:(t + 1) * C, :] = row[:, :QC].astype(BF16)

    qr, qi = a1r, a1i
    for t in range(Q):
        for i in range(2):
            rows = slice(i * QC + t * C, i * QC + (t + 1) * C)
            cr_ref[rows, :] = (cpr_ref[i] * qr - cpi_ref[i] * qi).astype(BF16)
            ci_ref[rows, :] = (-(cpr_ref[i] * qi + cpi_ref[i] * qr)).astype(BF16)
        if t < Q - 1:
            qr, qi = qr * a1r - qi * a1i, qr * a1i + qi * a1r
    ar_ref[...] = qr
    ai_ref[...] = qi


def _ssm_params(a_re, a_im, log_dt, b_re, b_im, c_re, c_im, d_skip):
    G, P, C, Q = SSM_N_GROUPS, SSM_STATE, SSM_GROUP, N_PHASE
    QC = Q * C
    f = lambda t: t.astype(F32)
    a_re, a_im, log_dt, b_re, b_im, c_re, c_im = map(f, (a_re, a_im, log_dt, b_re, b_im, c_re, c_im))
    even = (jnp.arange(G) % 2 == 0)[:, None, None]

    def lane_half(c):
        return jnp.concatenate([jnp.where(even, c, 0.0), jnp.where(even, 0.0, c)], axis=-1)

    pair = lambda blk: pl.BlockSpec((2,) + blk, lambda g: (g,) + (0,) * len(blk))
    one = lambda blk: pl.BlockSpec((None,) + blk, lambda g: (g,) + (0,) * len(blk))
    outs = pl.pallas_call(
        _ssm_param_kernel,
        grid=(G // 2,),
        in_specs=[pair((P, 1)), pair((P, 1)), one((1, 2 * P)), one((1, 2 * P)),
                  one((1, 2 * P)), pair((P, C)), pair((P, C)), pair((C, P)), pair((C, P)),
                  pair((C, 2 * P)), pair((C, 2 * P))],
        out_specs=[pair((QC, QC)), pair((P, QC)), pair((P, QC)), one((2 * QC, 2 * P)),
                   one((2 * QC, 2 * P)), one((1, 2 * P)), one((1, 2 * P))],
        out_shape=[jax.ShapeDtypeStruct((G, QC, QC), BF16), jax.ShapeDtypeStruct((G, P, QC), BF16),
                   jax.ShapeDtypeStruct((G, P, QC), BF16),
                   jax.ShapeDtypeStruct((G // 2, 2 * QC, 2 * P), BF16),
                   jax.ShapeDtypeStruct((G // 2, 2 * QC, 2 * P), BF16),
                   jax.ShapeDtypeStruct((G // 2, 1, 2 * P), F32),
                   jax.ShapeDtypeStruct((G // 2, 1, 2 * P), F32)],
        compiler_params=_cparams(("parallel",)),
        name="ssm_params",
    )(a_re.reshape(G, P, 1), a_im.reshape(G, P, 1),
      a_re.reshape(G // 2, 1, 2 * P), a_im.reshape(G // 2, 1, 2 * P),
      jnp.repeat(log_dt, P).reshape(G // 2, 1, 2 * P), b_re, b_im, c_re, c_im,
      lane_half(c_re), lane_half(c_im))
    keys = ("m", "b_re", "b_im", "c_re", "c_im", "a_re", "a_im")
    prm = dict(zip(keys, outs))
    prm["d"] = d_skip.astype(F32).reshape(G * C, 1)
    return prm


def _ssm_out_kernel(y_ref, z_ref, wg_ref, bg_ref, wo_ref, gp_ref, x_ref, o_hbm, obuf, sem, *,
                    nb, n_sub):
    s, j = pl.program_id(0), pl.program_id(1)
    nj = pl.num_programs(1)
    step = s * nj + j
    last = pl.num_programs(0) * nj - 1
    slot = step % 2

    def out_copy(slot_, s_, j_):
        return pltpu.make_async_copy(obuf.at[slot_], o_hbm.at[pl.ds(j_ * nb, nb), :, s_, :],
                                     sem.at[slot_])

    @pl.when(step >= 2)
    def _():
        out_copy(slot, (step - 2) // nj, (step - 2) % nj).wait()

    w = y_ref.shape[-1] // n_sub
    bs = nb // n_sub

    def glu_in(k):
        g = _gelu_tanh(y_ref[:, k * w:(k + 1) * w].astype(F32))
        return g, jnp.dot(wg_ref[...], g.astype(BF16), preferred_element_type=F32)

    def proj(k, g, gl):
        y2 = g * _sigmoid(gl + bg_ref[...])
        z = z_ref[:, k * w:(k + 1) * w].astype(F32)
        gated = (y2 * (z * _sigmoid(z))).astype(BF16)
        return jnp.dot(wo_ref[...], gated, preferred_element_type=F32)

    def finish(k, ot):
        ms = jnp.mean(ot * ot, axis=0, keepdims=True)
        nt = (ot * lax.rsqrt(ms + EPS)).T
        rows = slice(k * bs, (k + 1) * bs)
        obuf[slot, rows] = x_ref[rows] + (nt * gp_ref[...]).reshape(bs, -1, nt.shape[-1])

    a = [glu_in(k) for k in range(n_sub)]
    o = [proj(k, *a[k]) for k in range(n_sub)]
    for k in range(n_sub):
        finish(k, o[k])
    out_copy(slot, s, j).start()

    @pl.when(step == last)
    def _():
        out_copy(slot, s, j).wait()

    @pl.when(jnp.logical_and(step == last, last >= 1))
    def _():
        out_copy(1 - slot, (step - 1) // nj, (step - 1) % nj).wait()


def _ssm_out(YT, UT, wgt, bg, wot, gp, xp, *, nb=8, n_sub=2):
    B, _, M, D = xp.shape
    E = YT.shape[1]
    out = pl.pallas_call(
        functools.partial(_ssm_out_kernel, nb=nb, n_sub=n_sub),
        grid=(N_PHASE, B // nb),
        in_specs=[
            pl.BlockSpec((None, E, nb * M), lambda s, j: (s, 0, j)),
            pl.BlockSpec((None, E, nb * M), lambda s, j: (s, 1, j)),
            pl.BlockSpec((E, E), lambda s, j: (0, 0)),
            pl.BlockSpec((E, 1), lambda s, j: (0, 0)),
            pl.BlockSpec((D, E), lambda s, j: (0, 0)),
            pl.BlockSpec((1, D), lambda s, j: (0, 0)),
            pl.BlockSpec((nb, None, M, D), lambda s, j: (j, s, 0, 0)),
        ],
        out_specs=pl.BlockSpec(memory_space=pl.ANY),
        out_shape=jax.ShapeDtypeStruct((B, M, N_PHASE, D), F32),
        scratch_shapes=[pltpu.VMEM((2, nb, M, D), F32), pltpu.SemaphoreType.DMA((2,))],
        compiler_params=_cparams(("arbitrary", "arbitrary")),
        name="ssm_out",
    )(YT, UT, wgt, bg.reshape(E, 1), wot, gp.reshape(1, D), xp)
    return out.reshape(B, M * N_PHASE, D)


def kernel(x, rel_bias, attn_pre_norm, attn_w_in, attn_w_out, attn_post_norm, ssm_pre_norm, ssm_w_in, ssm_a_re, ssm_a_im, ssm_log_dt, ssm_b_re, ssm_b_im, ssm_c_re, ssm_c_im, ssm_d, ssm_w_glu, ssm_b_glu, ssm_w_out, ssm_post_norm):
    B, S, D = x.shape
    n_chunk = S // N_PHASE

    xp = jnp.transpose(x.reshape(B, n_chunk, N_PHASE, D), (0, 2, 1, 3))
    P = _norm_proj(xp, attn_pre_norm[0], attn_w_in[0], scaled_cols=3 * HEADS * HEAD_DIM,
                   scale=HEAD_DIM ** -0.5 * LOG2E)
    O = _attention(P, _bias_tables(rel_bias))
    xp = _out_proj(O, attn_w_out[0].astype(BF16), attn_post_norm[0], xp)

    prm = _ssm_params(ssm_a_re[0], ssm_a_im[0], ssm_log_dt[0], ssm_b_re[0], ssm_b_im[0],
                      ssm_c_re[0], ssm_c_im[0], ssm_d[0])
    UT = _ssm_in_proj(xp, ssm_pre_norm[0], ssm_w_in[0].T.astype(BF16))
    YT = _ssm_core(UT, prm, n_chunk)
    return _ssm_out(YT, UT, ssm_w_glu[0].T.astype(BF16), ssm_b_glu[0],
                    ssm_w_out[0].T.astype(BF16), ssm_post_norm[0], xp)
```

```python
import functools
import math

import numpy as np
import jax
import jax.numpy as jnp
from jax import lax
from jax.experimental import pallas as pl
from jax.experimental.pallas import tpu as pltpu

F32 = jnp.float32
BF16 = jnp.bfloat16

D_MODEL = 1024
HEAD_DIM = 64
HEADS = 16
N_PHASE = 16
BLK = 128
DILATIONS = (1, 4, 16)
REL_BUCKETS = 32
REL_MAX_DIST = 2048
SSM_GROUP = 16
SSM_N_GROUPS = 64
SSM_STATE = 64
EPS = 1e-6
NEG = -1e30
VMEM_LIMIT = 56 * 1024 * 1024


def _cparams(sem):
    return pltpu.CompilerParams(dimension_semantics=sem, vmem_limit_bytes=VMEM_LIMIT)


LOG2E = math.log2(math.e)


def _sigmoid(v):
    return 1.0 / (1.0 + jnp.exp2(v * (-LOG2E)))


def _gelu_tanh(x):
    k0 = -2.0 * math.sqrt(2.0 / math.pi) * LOG2E
    return x / (1.0 + jnp.exp2(x * (k0 + (k0 * 0.044715) * (x * x))))


def _norm_proj_kernel(x_hbm, g_ref, w_ref, o_ref, xp_hbm, xbuf, h_ref, gsem, osem, *,
                      scaled_tiles, scale):
    b, j = pl.program_id(0), pl.program_id(1)
    n_b, n_j = pl.num_programs(0), pl.num_programs(1)
    slot = b % 2

    def phase_copy(bb, sl, r):
        return pltpu.make_async_copy(x_hbm.at[bb, :, r, :], xbuf.at[sl, r], gsem.at[sl])

    def gather_start(bb, sl):
        lax.fori_loop(0, N_PHASE, lambda r, c: (phase_copy(bb, sl, r).start(), c)[1], 0)

    def gather_wait(bb, sl):
        lax.fori_loop(0, N_PHASE, lambda r, c: (phase_copy(bb, sl, r).wait(), c)[1], 0)

    def write_out(bb, sl):
        return pltpu.make_async_copy(xbuf.at[sl], xp_hbm.at[bb], osem.at[sl])

    @pl.when(j == 0)
    def _():
        @pl.when(b == 0)
        def _():
            gather_start(0, 0)

        gather_wait(b, slot)

        @pl.when(b >= 1)
        def _():
            write_out(b - 1, 1 - slot).wait()

        @pl.when(b + 1 < n_b)
        def _():
            gather_start(b + 1, 1 - slot)

        write_out(b, slot).start()
        x = xbuf[slot].reshape(h_ref.shape)
        ms = jnp.mean(x * x, axis=-1, keepdims=True)
        h_ref[...] = (x * lax.rsqrt(ms + EPS) * g_ref[...]).astype(BF16)

    w = (w_ref[...] * jnp.where(j < scaled_tiles, scale, 1.0).astype(F32)).astype(BF16)
    res = jnp.dot(h_ref[...], w, preferred_element_type=F32)
    o_ref[...] = res.reshape(o_ref.shape).astype(BF16)

    @pl.when(jnp.logical_and(b == n_b - 1, j == n_j - 1))
    def _():
        write_out(b, slot).wait()


def _norm_proj(x, g, w, *, scaled_cols, scale, tn=1024):
    B, S, D = x.shape
    M = S // N_PHASE
    N = w.shape[1]
    assert scaled_cols % tn == 0
    return pl.pallas_call(
        functools.partial(_norm_proj_kernel, scaled_tiles=scaled_cols // tn, scale=scale),
        grid=(B, N // tn),
        in_specs=[
            pl.BlockSpec(memory_space=pl.ANY),
            pl.BlockSpec((1, D), lambda b, j: (0, 0)),
            pl.BlockSpec((D, tn), lambda b, j: (0, j)),
        ],
        out_specs=[pl.BlockSpec((None, N_PHASE, M, tn), lambda b, j: (b, 0, 0, j)),
                   pl.BlockSpec(memory_space=pl.ANY)],
        out_shape=[jax.ShapeDtypeStruct((B, N_PHASE, M, N), BF16),
                   jax.ShapeDtypeStruct((B, N_PHASE, M, D), F32)],
        scratch_shapes=[pltpu.VMEM((2, N_PHASE, M, D), F32), pltpu.VMEM((S, D), BF16),
                        pltpu.SemaphoreType.DMA((2,)), pltpu.SemaphoreType.DMA((2,))],
        compiler_params=_cparams(("arbitrary", "arbitrary")),
        name="attn_norm_proj",
    )(x.reshape(B, M, N_PHASE, D), g.reshape(1, D), w)


def _attn_kernel(q0_ref, q1_ref, q2_ref, k0_ref, k1_ref, k2_ref, v0_ref, v1_ref, v2_ref,
                 z_ref, bm_ref, o_ref, qf_ref, kf_ref, vf_ref, acc_ref, l_ref, m_ref):
    W = 2 * HEAD_DIM
    AHEAD = 3
    lane = lax.broadcasted_iota(jnp.int32, (BLK, W), 1)
    first_head = lane < HEAD_DIM

    def logits(q, k, g, cur_only):
        zq = jnp.zeros_like(q)
        qs = jnp.concatenate([jnp.where(first_head, q, zq), jnp.where(first_head, zq, q)], axis=0)
        s = lax.dot_general(qs, k, (((1,), (1,)), ((), ())), preferred_element_type=F32)
        return s + (bm_ref[g, :, BLK:2 * BLK] if cur_only else bm_ref[g])

    def finish(s, v):
        m = jnp.max(s, axis=-1, keepdims=True)
        p = jnp.exp2(s - m).astype(BF16)
        va = jnp.concatenate([v, jnp.ones((v.shape[0], W), BF16)], axis=1)
        pv = jnp.dot(p, va, preferred_element_type=F32)
        acc = jnp.where(first_head, pv[:BLK, :W], pv[BLK:, :W])
        l = jnp.where(first_head, pv[:BLK, W:], pv[BLK:, W:])
        mm = jnp.where(first_head, m[:BLK], m[BLK:])
        return acc, l, mm

    def rows_of(ref, pieces):
        return jnp.concatenate([ref[ph, lo:lo + n, :] for ph, lo, n in pieces], axis=0)

    def keys_of(ref, prev, cur):
        return rows_of(ref, cur) if prev is None else rows_of(ref, prev + cur)

    def store(g, pieces, vals):
        at = 0
        for ph, lo, n in pieces:
            for ref, val in zip((acc_ref, l_ref, m_ref), vals):
                ref[g, ph, lo:lo + n, :] = val[at:at + n]
            at += n

    M = q0_ref.shape[1]
    g2_blocks = [[(r, 0, BLK)] for r in range(N_PHASE)]
    g1_blocks = {(r4, n): [(4 * q4 + r4, 32 * n, 32) for q4 in range(4)]
                 for r4 in range(4) for n in range(M // 32)}
    g0_blocks = [[(r, 8 * n, 8) for r in range(N_PHASE)] for n in range(M // 8)]

    for r in range(N_PHASE):
        qf_ref[r] = q0_ref[r].astype(F32)
        kf_ref[r] = k0_ref[r].astype(F32)
        vf_ref[r] = v0_ref[r].astype(F32)

    work = []
    for n, pieces in enumerate(g0_blocks):
        work.append((0, pieces, g0_blocks[n - 1] if n else None, (qf_ref, kf_ref, vf_ref), True))
    for (r4, n), pieces in g1_blocks.items():
        work.append((1, pieces, g1_blocks[(r4, n - 1)] if n else None, (q1_ref, k1_ref, v1_ref), False))
    for pieces in g2_blocks:
        work.append((2, pieces, None, (q2_ref, k2_ref, v2_ref), False))

    def merge(r, acc2, l2, m2):
        m0, m1 = m_ref[0, r], m_ref[1, r]
        mx = jnp.maximum(jnp.maximum(m0, m1), m2)
        w0, w1, w2 = jnp.exp2(m0 - mx), jnp.exp2(m1 - mx), jnp.exp2(m2 - mx)
        num = w0 * acc_ref[0, r] + w1 * acc_ref[1, r] + w2 * acc2
        den = w0 * l_ref[0, r] + w1 * l_ref[1, r] + w2 * l2
        z = z_ref[r].astype(F32)
        o_ref[r] = (num * z / (den * (1.0 + jnp.exp2(z * (-LOG2E))))).astype(BF16)

    pending = []
    for item in work + [None] * AHEAD:
        if item is not None:
            g, pieces, prev, (q_r, k_r, _), cast = item
            q, k = rows_of(q_r, pieces), keys_of(k_r, prev, pieces)
            if cast:
                q, k = q.astype(BF16), k.astype(BF16)
            pending.append((item, logits(q, k, g, prev is None)))
        if item is None or len(pending) > AHEAD:
            (g_p, pieces_p, prev_p, (_, _, v_r), cast_p), s_p = pending.pop(0)
            v = keys_of(v_r, prev_p, pieces_p)
            vals = finish(s_p, v.astype(BF16) if cast_p else v)
            if g_p == 2:
                merge(pieces_p[0][0], *vals)
            else:
                store(g_p, pieces_p, vals)


def _attention(P, bm):
    B, _, M, _ = P.shape
    HP = HEADS // 2
    W = 2 * HEAD_DIM

    def spec(kind, g):
        base = (kind * 3 + g) * HP
        return pl.BlockSpec((None, N_PHASE, M, W), lambda b, hp, base=base: (b, 0, 0, base + hp))

    in_specs = [spec(kind, g) for kind in range(3) for g in range(3)]
    in_specs.append(pl.BlockSpec((None, N_PHASE, M, W), lambda b, hp: (b, 0, 0, 9 * HP + hp)))
    in_specs.append(pl.BlockSpec((3, None, 2 * BLK, 2 * BLK), lambda b, hp: (0, hp, 0, 0)))
    return pl.pallas_call(
        _attn_kernel,
        grid=(B, HP),
        in_specs=in_specs,
        out_specs=pl.BlockSpec((None, N_PHASE, M, W), lambda b, hp: (b, 0, 0, hp)),
        out_shape=jax.ShapeDtypeStruct((B, N_PHASE, M, HEADS * HEAD_DIM), BF16),
        scratch_shapes=[pltpu.VMEM((N_PHASE, M, W), F32) for _ in range(3)]
        + [pltpu.VMEM((2, N_PHASE, M, W), F32) for _ in range(3)],
        compiler_params=_cparams(("parallel", "parallel")),
        name="dilated_attention",
    )(*([P] * 10), bm)


def _t5_bucket(dist):
    max_exact = REL_BUCKETS // 2
    n = jnp.maximum(dist, 1).astype(F32)
    large = max_exact + (jnp.log(n / max_exact) / math.log(REL_MAX_DIST / max_exact)
                         * (REL_BUCKETS - max_exact)).astype(jnp.int32)
    large = jnp.minimum(large, REL_BUCKETS - 1)
    return jnp.where(dist < max_exact, dist, large)


def _bias_tables(rel_bias):
    a = np.arange(BLK)
    pos = (16 * (a % 8) + a // 8, 4 * (a % 32) + a // 32, a)
    back = np.stack([np.concatenate([BLK + p[:, None] - p[None, :], p[:, None] - p[None, :]], axis=1)
                     for p in pos])
    valid = (back >= 0) & (back <= BLK)
    dist = np.clip(back, 0, BLK) * np.asarray(DILATIONS)[:, None, None]
    bucket = jnp.where(jnp.asarray(valid), _t5_bucket(jnp.asarray(dist, jnp.int32)), REL_BUCKETS)
    onehot = (bucket[..., None] == jnp.arange(REL_BUCKETS + 1)).astype(F32)
    ext = jnp.concatenate([rel_bias.astype(F32), jnp.full((1, HEADS), NEG, F32)], axis=0)
    t = jnp.einsum("gijc,ch->ghij", onehot, ext * math.log2(math.e),
                   precision=lax.Precision.HIGHEST)
    return t.reshape(3, HEADS // 2, 2 * BLK, 2 * BLK)


def _out_proj_kernel(o_ref, w_ref, g_ref, x_ref, y_ref, *, mh):
    o = o_ref[...].reshape(N_PHASE * mh, o_ref.shape[-1])
    h = jnp.dot(o, w_ref[...], preferred_element_type=F32)
    ms = jnp.mean(h * h, axis=-1, keepdims=True)
    y = h * lax.rsqrt(ms + EPS) * g_ref[...]
    y_ref[...] = x_ref[...] + y.reshape(y_ref.shape)


def _out_proj(O, w, g, xp, *, mh=64):
    B, _, M, D = xp.shape
    return pl.pallas_call(
        functools.partial(_out_proj_kernel, mh=mh),
        grid=(B, M // mh),
        in_specs=[
            pl.BlockSpec((None, N_PHASE, mh, O.shape[-1]), lambda b, m: (b, 0, m, 0)),
            pl.BlockSpec(w.shape, lambda b, m: (0, 0)),
            pl.BlockSpec((1, D), lambda b, m: (0, 0)),
            pl.BlockSpec((None, N_PHASE, mh, D), lambda b, m: (b, 0, m, 0)),
        ],
        out_specs=pl.BlockSpec((None, N_PHASE, mh, D), lambda b, m: (b, 0, m, 0)),
        out_shape=jax.ShapeDtypeStruct(xp.shape, F32),
        compiler_params=_cparams(("parallel", "parallel")),
        name="attn_out_proj",
    )(O, w, g.reshape(1, D), xp)


def _ssm_in_proj_kernel(x_ref, g_ref, wt_ref, o_ref, h_ref):
    M = x_ref.shape[1]

    @pl.when(pl.program_id(1) == 0)
    def _():
        g = g_ref[...]
        for s in range(N_PHASE):
            xs = x_ref[s]
            ms = jnp.mean(xs * xs, axis=-1, keepdims=True)
            h_ref[s * M:(s + 1) * M, :] = (xs * lax.rsqrt(ms + EPS) * g).astype(BF16)

    res = lax.dot_general(wt_ref[...], h_ref[...], (((1,), (1,)), ((), ())),
                          preferred_element_type=F32)
    for s in range(N_PHASE):
        o_ref[s] = res[:, s * M:(s + 1) * M].astype(BF16)


def _ssm_in_proj(xp, g, wt, *, tn=512):
    B, _, M, D = xp.shape
    S = N_PHASE * M
    N = wt.shape[0]
    return pl.pallas_call(
        _ssm_in_proj_kernel,
        grid=(B, N // tn),
        in_specs=[
            pl.BlockSpec((None, N_PHASE, M, D), lambda b, j: (b, 0, 0, 0)),
            pl.BlockSpec((1, D), lambda b, j: (0, 0)),
            pl.BlockSpec((tn, D), lambda b, j: (j, 0)),
        ],
        out_specs=pl.BlockSpec((N_PHASE, tn, M), lambda b, j: (0, j, b)),
        out_shape=jax.ShapeDtypeStruct((N_PHASE, N, B * M), BF16),
        scratch_shapes=[pltpu.VMEM((S, D), BF16)],
        compiler_params=_cparams(("parallel", "arbitrary")),
        name="ssm_in_proj",
    )(xp, g.reshape(1, D), wt)


def _ssm_core_kernel(u_ref, m_ref, bre_ref, bim_ref, cre_ref, cim_ref, are_ref, aim_ref,
                     d_ref, y_ref, sre_ref, sim_ref, ym_ref, *, n_chunk, n_pair):
    C = SSM_GROUP
    QC = N_PHASE * C
    u = u_ref[...]
    N = u.shape[-1]
    nb = N // n_chunk
    n_grp = 2 * n_pair
    us = [u[:, i * C:(i + 1) * C, :].reshape(QC, N) for i in range(n_grp)]

    for k in range(n_pair):
        sre_ref[k] = jnp.concatenate([jnp.dot(bre_ref[i], us[i], preferred_element_type=F32)
                                      for i in (2 * k, 2 * k + 1)], axis=0).T
        sim_ref[k] = jnp.concatenate([jnp.dot(bim_ref[i], us[i], preferred_element_type=F32)
                                      for i in (2 * k, 2 * k + 1)], axis=0).T

    for i in range(n_grp):
        ym_ref[i] = jnp.dot(m_ref[i], us[i], preferred_element_type=F32)

    ar = [jnp.broadcast_to(are_ref[k], (nb, are_ref.shape[-1])) for k in range(n_pair)]
    ai = [jnp.broadcast_to(aim_ref[k], (nb, aim_ref.shape[-1])) for k in range(n_pair)]
    sr = [jnp.zeros_like(ar[0])] * n_pair
    si = [jnp.zeros_like(ar[0])] * n_pair
    for c in range(n_chunk):
        rows = pl.ds(c, nb, stride=n_chunk)
        for k in range(n_pair):
            xr, xi = sre_ref[k, rows, :], sim_ref[k, rows, :]
            sre_ref[k, rows, :] = sr[k]
            sim_ref[k, rows, :] = si[k]
            sr[k], si[k] = (ar[k] * sr[k] - ai[k] * si[k] + xr, ar[k] * si[k] + ai[k] * sr[k] + xi)

    nt = (((1,), (1,)), ((), ()))
    for k in range(n_pair):
        inter = (lax.dot_general(cre_ref[k], sre_ref[k].astype(BF16), nt, preferred_element_type=F32)
                 + lax.dot_general(cim_ref[k], sim_ref[k].astype(BF16), nt,
                                   preferred_element_type=F32))
        for i in (2 * k, 2 * k + 1):
            y = (ym_ref[i] + inter[(i % 2) * QC:(i % 2 + 1) * QC]).reshape(N_PHASE, C, N)
            y = y + d_ref[i * C:(i + 1) * C] * u[:, i * C:(i + 1) * C, :].astype(F32)
            y_ref[:, i * C:(i + 1) * C, :] = y.astype(BF16)


def _ssm_core(UT, prm, n_chunk, *, n_pair=2):
    _, _, N = UT.shape
    G2 = SSM_N_GROUPS // 2
    C2 = 2 * SSM_GROUP
    QC = N_PHASE * SSM_GROUP
    P = SSM_STATE
    blk = lambda *shape: pl.BlockSpec(shape, lambda g: (g,) + (0,) * (len(shape) - 1))
    return pl.pallas_call(
        functools.partial(_ssm_core_kernel, n_chunk=n_chunk, n_pair=n_pair),
        grid=(G2 // n_pair,),
        in_specs=[
            pl.BlockSpec((N_PHASE, n_pair * C2, N), lambda g: (0, g, 0)),
            blk(2 * n_pair, QC, QC), blk(2 * n_pair, P, QC), blk(2 * n_pair, P, QC),
            blk(n_pair, 2 * QC, 2 * P), blk(n_pair, 2 * QC, 2 * P),
            blk(n_pair, 1, 2 * P), blk(n_pair, 1, 2 * P),
            blk(n_pair * C2, 1),
        ],
        out_specs=pl.BlockSpec((N_PHASE, n_pair * C2, N), lambda g: (0, g, 0)),
        out_shape=jax.ShapeDtypeStruct((N_PHASE, SSM_N_GROUPS * SSM_GROUP, N), BF16),
        scratch_shapes=[pltpu.VMEM((n_pair, N, 2 * P), F32), pltpu.VMEM((n_pair, N, 2 * P), F32),
                        pltpu.VMEM((2 * n_pair, QC, N), F32)],
        compiler_params=_cparams(("parallel",)),
        name="ssm_core",
    )(UT, prm["m"], prm["b_re"], prm["b_im"], prm["c_re"], prm["c_im"],
      prm["a_re"], prm["a_im"], prm["d"])


def _ssm_param_kernel(arc_ref, aic_ref, arr_ref, air_ref, ldr_ref, bre_ref, bim_ref,
                      cre_ref, cim_ref, cpr_ref, cpi_ref,
                      m_ref, br_ref, bi_ref, cr_ref, ci_ref, ar_ref, ai_ref):
    Q, C, P = N_PHASE, SSM_GROUP, SSM_STATE
    QC = Q * C
    hi = lax.Precision.HIGHEST
    tile = (lax.broadcasted_iota(jnp.int32, (C, QC), 1) % C
            == lax.broadcasted_iota(jnp.int32, (C, QC), 0)).astype(F32)
    lane_s = lax.broadcasted_iota(jnp.int32, (P, QC), 1) // C

    dt = jnp.exp(ldr_ref[...])
    mag = jnp.exp(arr_ref[...] * dt)
    a1r, a1i = mag * jnp.cos(air_ref[...] * dt), mag * jnp.sin(air_ref[...] * dt)
    a1r_c = jnp.broadcast_to(a1r, (8, 2 * P)).T[:, :1]
    a1i_c = jnp.broadcast_to(a1i, (8, 2 * P)).T[:, :1]

    for i in range(2):
        a_re, a_im = arc_ref[i], aic_ref[i]
        abr, abi = a1r_c[i * P:(i + 1) * P], a1i_c[i * P:(i + 1) * P]
        den = a_re * a_re + a_im * a_im
        cfr = ((abr - 1.0) * a_re + abi * a_im) / den
        cfi = (abi * a_re - (abr - 1.0) * a_im) / den
        bbr = cfr * bre_ref[i] - cfi * bim_ref[i]
        bbi = cfr * bim_ref[i] + cfi * bre_ref[i]

        pr, pi = jnp.ones_like(abr), jnp.zeros_like(abr)
        pw_r = jnp.zeros((P, QC), F32)
        pw_i = jnp.zeros((P, QC), F32)
        for n in range(Q):
            sel = lane_s == Q - 1 - n
            pw_r = jnp.where(sel, pr, pw_r)
            pw_i = jnp.where(sel, pi, pw_i)
            pr, pi = pr * abr - pi * abi, pr * abi + pi * abr
        bbr_t = jnp.dot(bbr, tile, precision=hi, preferred_element_type=F32)
        bbi_t = jnp.dot(bbi, tile, precision=hi, preferred_element_type=F32)
        bst_r = pw_r * bbr_t - pw_i * bbi_t
        bst_i = pw_r * bbi_t + pw_i * bbr_t
        br_ref[i] = bst_r.astype(BF16)
        bi_ref[i] = bst_i.astype(BF16)

        kl = (jnp.dot(cre_ref[i], bst_r, precision=hi, preferred_element_type=F32)
              - jnp.dot(cim_ref[i], bst_i, precision=hi, preferred_element_type=F32))
        klz = jnp.concatenate([kl, jnp.zeros_like(kl)], axis=1)
        for t in range(Q):
            sh = (2 * QC - C * (Q - 1 - t)) % (2 * QC)
            row = klz if sh == 0 else pltpu.roll(klz, sh, axis=1)
            m_ref[i, t * C:(t + 1) * C, :] = row[:, :QC].astype(BF16)

    qr, qi = a1r, a1i
    for t in range(Q):
        for i in range(2):
            rows = slice(i * QC + t * C, i * QC + (t + 1) * C)
            cr_ref[rows, :] = (cpr_ref[i] * qr - cpi_ref[i] * qi).astype(BF16)
            ci_ref[rows, :] = (-(cpr_ref[i] * qi + cpi_ref[i] * qr)).astype(BF16)
        if t < Q - 1:
            qr, qi = qr * a1r - qi * a1i, qr * a1i + qi * a1r
    ar_ref[...] = qr
    ai_ref[...] = qi


def _ssm_params(a_re, a_im, log_dt, b_re, b_im, c_re, c_im, d_skip):
    G, P, C, Q = SSM_N_GROUPS, SSM_STATE, SSM_GROUP, N_PHASE
    QC = Q * C
    f = lambda t: t.astype(F32)
    a_re, a_im, log_dt, b_re, b_im, c_re, c_im = map(f, (a_re, a_im, log_dt, b_re, b_im, c_re, c_im))
    even = (jnp.arange(G) % 2 == 0)[:, None, None]

    def lane_half(c):
        return jnp.concatenate([jnp.where(even, c, 0.0), jnp.where(even, 0.0, c)], axis=-1)

    pair = lambda blk: pl.BlockSpec((2,) + blk, lambda g: (g,) + (0,) * len(blk))
    one = lambda blk: pl.BlockSpec((None,) + blk, lambda g: (g,) + (0,) * len(blk))
    outs = pl.pallas_call(
        _ssm_param_kernel,
        grid=(G // 2,),
        in_specs=[pair((P, 1)), pair((P, 1)), one((1, 2 * P)), one((1, 2 * P)),
                  one((1, 2 * P)), pair((P, C)), pair((P, C)), pair((C, P)), pair((C, P)),
                  pair((C, 2 * P)), pair((C, 2 * P))],
        out_specs=[pair((QC, QC)), pair((P, QC)), pair((P, QC)), one((2 * QC, 2 * P)),
                   one((2 * QC, 2 * P)), one((1, 2 * P)), one((1, 2 * P))],
        out_shape=[jax.ShapeDtypeStruct((G, QC, QC), BF16), jax.ShapeDtypeStruct((G, P, QC), BF16),
                   jax.ShapeDtypeStruct((G, P, QC), BF16),
                   jax.ShapeDtypeStruct((G // 2, 2 * QC, 2 * P), BF16),
                   jax.ShapeDtypeStruct((G // 2, 2 * QC, 2 * P), BF16),
                   jax.ShapeDtypeStruct((G // 2, 1, 2 * P), F32),
                   jax.ShapeDtypeStruct((G // 2, 1, 2 * P), F32)],
        compiler_params=_cparams(("parallel",)),
        name="ssm_params",
    )(a_re.reshape(G, P, 1), a_im.reshape(G, P, 1),
      a_re.reshape(G // 2, 1, 2 * P), a_im.reshape(G // 2, 1, 2 * P),
      jnp.repeat(log_dt, P).reshape(G // 2, 1, 2 * P), b_re, b_im, c_re, c_im,
      lane_half(c_re), lane_half(c_im))
    keys = ("m", "b_re", "b_im", "c_re", "c_im", "a_re", "a_im")
    prm = dict(zip(keys, outs))
    prm["d"] = d_skip.astype(F32).reshape(G * C, 1)
    return prm


def _ssm_out_kernel(y_ref, z_ref, wg_ref, bg_ref, wo_ref, gp_ref, x_ref, o_hbm, obuf, sem, *,
                    nb, n_sub):
    s, j = pl.program_id(0), pl.program_id(1)
    nj = pl.num_programs(1)
    step = s * nj + j
    last = pl.num_programs(0) * nj - 1
    slot = step % 2

    def out_copy(slot_, s_, j_):
        return pltpu.make_async_copy(obuf.at[slot_], o_hbm.at[pl.ds(j_ * nb, nb), :, s_, :],
                                     sem.at[slot_])

    @pl.when(step >= 2)
    def _():
        out_copy(slot, (step - 2) // nj, (step - 2) % nj).wait()

    w = y_ref.shape[-1] // n_sub
    bs = nb // n_sub

    def glu_in(k):
        g = _gelu_tanh(y_ref[:, k * w:(k + 1) * w].astype(F32))
        return g, jnp.dot(wg_ref[...], g.astype(BF16), preferred_element_type=F32)

    def proj(k, g, gl):
        y2 = g * _sigmoid(gl + bg_ref[...])
        z = z_ref[:, k * w:(k + 1) * w].astype(F32)
        gated = (y2 * (z * _sigmoid(z))).astype(BF16)
        return jnp.dot(wo_ref[...], gated, preferred_element_type=F32)

    def finish(k, ot):
        ms = jnp.mean(ot * ot, axis=0, keepdims=True)
        nt = (ot * lax.rsqrt(ms + EPS)).T
        rows = slice(k * bs, (k + 1) * bs)
        obuf[slot, rows] = x_ref[rows] + (nt * gp_ref[...]).reshape(bs, -1, nt.shape[-1])

    a = [glu_in(k) for k in range(n_sub)]
    o = [proj(k, *a[k]) for k in range(n_sub)]
    for k in range(n_sub):
        finish(k, o[k])
    out_copy(slot, s, j).start()

    @pl.when(step == last)
    def _():
        out_copy(slot, s, j).wait()

    @pl.when(jnp.logical_and(step == last, last >= 1))
    def _():
        out_copy(1 - slot, (step - 1) // nj, (step - 1) % nj).wait()


def _ssm_out(YT, UT, wgt, bg, wot, gp, xp, *, nb=8, n_sub=2):
    B, _, M, D = xp.shape
    E = YT.shape[1]
    out = pl.pallas_call(
        functools.partial(_ssm_out_kernel, nb=nb, n_sub=n_sub),
        grid=(N_PHASE, B // nb),
        in_specs=[
            pl.BlockSpec((None, E, nb * M), lambda s, j: (s, 0, j)),
            pl.BlockSpec((None, E, nb * M), lambda s, j: (s, 1, j)),
            pl.BlockSpec((E, E), lambda s, j: (0, 0)),
            pl.BlockSpec((E, 1), lambda s, j: (0, 0)),
            pl.BlockSpec((D, E), lambda s, j: (0, 0)),
            pl.BlockSpec((1, D), lambda s, j: (0, 0)),
            pl.BlockSpec((nb, None, M, D), lambda s, j: (j, s, 0, 0)),
        ],
        out_specs=pl.BlockSpec(memory_space=pl.ANY),
        out_shape=jax.ShapeDtypeStruct((B, M, N_PHASE, D), F32),
        scratch_shapes=[pltpu.VMEM((2, nb, M, D), F32), pltpu.SemaphoreType.DMA((2,))],
        compiler_params=_cparams(("arbitrary", "arbitrary")),
        name="ssm_out",
    )(YT, UT, wgt, bg.reshape(E, 1), wot, gp.reshape(1, D), xp)
    return out.reshape(B, M * N_PHASE, D)


def kernel(x, rel_bias, attn_pre_norm, attn_w_in, attn_w_out, attn_post_norm, ssm_pre_norm, ssm_w_in, ssm_a_re, ssm_a_im, ssm_log_dt, ssm_b_re, ssm_b_im, ssm_c_re, ssm_c_im, ssm_d, ssm_w_glu, ssm_b_glu, ssm_w_out, ssm_post_norm):
    B, S, D = x.shape
    n_chunk = S // N_PHASE

    P, xp = _norm_proj(x, attn_pre_norm[0], attn_w_in[0], scaled_cols=3 * HEADS * HEAD_DIM,
                       scale=HEAD_DIM ** -0.5 * LOG2E)
    O = _attention(P, _bias_tables(rel_bias))
    xp = _out_proj(O, attn_w_out[0].astype(BF16), attn_post_norm[0], xp)

    prm = _ssm_params(ssm_a_re[0], ssm_a_im[0], ssm_log_dt[0], ssm_b_re[0], ssm_b_im[0],
                      ssm_c_re[0], ssm_c_im[0], ssm_d[0])
    UT = _ssm_in_proj(xp, ssm_pre_norm[0], ssm_w_in[0].T.astype(BF16))
    YT = _ssm_core(UT, prm, n_chunk)
    return _ssm_out(YT, UT, ssm_w_glu[0].T.astype(BF16), ssm_b_glu[0],
                    ssm_w_out[0].T.astype(BF16), ssm_post_norm[0], xp)
```

```python
import functools
import math

import numpy as np
import jax
import jax.numpy as jnp
from jax import lax
from jax.experimental import pallas as pl
from jax.experimental.pallas import tpu as pltpu

F32 = jnp.float32
BF16 = jnp.bfloat16

D_MODEL = 1024
HEAD_DIM = 64
HEADS = 16
N_PHASE = 16
BLK = 128
DILATIONS = (1, 4, 16)
REL_BUCKETS = 32
REL_MAX_DIST = 2048
SSM_GROUP = 16
SSM_N_GROUPS = 64
SSM_STATE = 64
EPS = 1e-6
NEG = -1e30
VMEM_LIMIT = 56 * 1024 * 1024


def _cparams(sem):
    return pltpu.CompilerParams(dimension_semantics=sem, vmem_limit_bytes=VMEM_LIMIT)


LOG2E = math.log2(math.e)


def _sigmoid(v):
    return 1.0 / (1.0 + jnp.exp2(v * (-LOG2E)))


def _gelu_tanh(x):
    k0 = -2.0 * math.sqrt(2.0 / math.pi) * LOG2E
    return x / (1.0 + jnp.exp2(x * (k0 + (k0 * 0.044715) * (x * x))))


def _norm_proj_kernel(x_hbm, g_ref, w_ref, o_ref, xp_hbm, xbuf, h_ref, gsem, osem, *,
                      scaled_tiles, scale):
    b, j = pl.program_id(0), pl.program_id(1)
    n_b, n_j = pl.num_programs(0), pl.num_programs(1)
    slot = b % 2

    def phase_copy(bb, sl, r):
        return pltpu.make_async_copy(x_hbm.at[bb, :, r, :], xbuf.at[sl, r], gsem.at[sl])

    def gather_start(bb, sl):
        lax.fori_loop(0, N_PHASE, lambda r, c: (phase_copy(bb, sl, r).start(), c)[1], 0)

    def gather_wait(bb, sl):
        lax.fori_loop(0, N_PHASE, lambda r, c: (phase_copy(bb, sl, r).wait(), c)[1], 0)

    def write_out(bb, sl):
        return pltpu.make_async_copy(xbuf.at[sl], xp_hbm.at[bb], osem.at[sl])

    @pl.when(j == 0)
    def _():
        @pl.when(b == 0)
        def _():
            gather_start(0, 0)

        gather_wait(b, slot)

        @pl.when(b >= 1)
        def _():
            write_out(b - 1, 1 - slot).wait()

        @pl.when(b + 1 < n_b)
        def _():
            gather_start(b + 1, 1 - slot)

        write_out(b, slot).start()
        x = xbuf[slot].reshape(h_ref.shape)
        ms = jnp.mean(x * x, axis=-1, keepdims=True)
        h_ref[...] = (x * lax.rsqrt(ms + EPS) * g_ref[...]).astype(BF16)

    w = (w_ref[...] * jnp.where(j < scaled_tiles, scale, 1.0).astype(F32)).astype(BF16)
    res = jnp.dot(h_ref[...], w, preferred_element_type=F32)
    o_ref[...] = res.reshape(o_ref.shape).astype(BF16)

    @pl.when(jnp.logical_and(b == n_b - 1, j == n_j - 1))
    def _():
        write_out(b, slot).wait()


def _norm_proj(x, g, w, *, scaled_cols, scale, tn=1024):
    B, S, D = x.shape
    M = S // N_PHASE
    N = w.shape[1]
    assert scaled_cols % tn == 0
    return pl.pallas_call(
        functools.partial(_norm_proj_kernel, scaled_tiles=scaled_cols // tn, scale=scale),
        grid=(B, N // tn),
        in_specs=[
            pl.BlockSpec(memory_space=pl.ANY),
            pl.BlockSpec((1, D), lambda b, j: (0, 0)),
            pl.BlockSpec((D, tn), lambda b, j: (0, j)),
        ],
        out_specs=[pl.BlockSpec((None, N_PHASE, M, tn), lambda b, j: (b, 0, 0, j)),
                   pl.BlockSpec(memory_space=pl.ANY)],
        out_shape=[jax.ShapeDtypeStruct((B, N_PHASE, M, N), BF16),
                   jax.ShapeDtypeStruct((B, N_PHASE, M, D), F32)],
        scratch_shapes=[pltpu.VMEM((2, N_PHASE, M, D), F32), pltpu.VMEM((S, D), BF16),
                        pltpu.SemaphoreType.DMA((2,)), pltpu.SemaphoreType.DMA((2,))],
        compiler_params=_cparams(("arbitrary", "arbitrary")),
        name="attn_norm_proj",
    )(x.reshape(B, M, N_PHASE, D), g.reshape(1, D), w)


def _attn_kernel(q0_ref, q1_ref, q2_ref, k0_ref, k1_ref, k2_ref, v0_ref, v1_ref, v2_ref,
                 z_ref, bm_ref, o_ref, qf_ref, kf_ref, vf_ref, acc_ref, l_ref, m_ref):
    W = 2 * HEAD_DIM
    AHEAD = 3
    lane = lax.broadcasted_iota(jnp.int32, (BLK, W), 1)
    first_head = lane < HEAD_DIM

    def logits(q, k, g, cur_only):
        zq = jnp.zeros_like(q)
        qs = jnp.concatenate([jnp.where(first_head, q, zq), jnp.where(first_head, zq, q)], axis=0)
        s = lax.dot_general(qs, k, (((1,), (1,)), ((), ())), preferred_element_type=F32)
        return s + (bm_ref[g, :, BLK:2 * BLK] if cur_only else bm_ref[g])

    def finish(s, v):
        m = jnp.max(s, axis=-1, keepdims=True)
        p = jnp.exp2(s - m).astype(BF16)
        va = jnp.concatenate([v, jnp.ones((v.shape[0], W), BF16)], axis=1)
        pv = jnp.dot(p, va, preferred_element_type=F32)
        acc = jnp.where(first_head, pv[:BLK, :W], pv[BLK:, :W])
        l = jnp.where(first_head, pv[:BLK, W:], pv[BLK:, W:])
        mm = jnp.where(first_head, m[:BLK], m[BLK:])
        return acc, l, mm

    def rows_of(ref, pieces):
        return jnp.concatenate([ref[ph, lo:lo + n, :] for ph, lo, n in pieces], axis=0)

    def keys_of(ref, prev, cur):
        return rows_of(ref, cur) if prev is None else rows_of(ref, prev + cur)

    def store(g, pieces, vals):
        at = 0
        for ph, lo, n in pieces:
            for ref, val in zip((acc_ref, l_ref, m_ref), vals):
                ref[g, ph, lo:lo + n, :] = val[at:at + n]
            at += n

    M = q0_ref.shape[1]
    g2_blocks = [[(r, 0, BLK)] for r in range(N_PHASE)]
    g1_blocks = {(r4, n): [(4 * q4 + r4, 32 * n, 32) for q4 in range(4)]
                 for r4 in range(4) for n in range(M // 32)}
    g0_blocks = [[(r, 8 * n, 8) for r in range(N_PHASE)] for n in range(M // 8)]

    for r in range(N_PHASE):
        qf_ref[r] = q0_ref[r].astype(F32)
        kf_ref[r] = k0_ref[r].astype(F32)
        vf_ref[r] = v0_ref[r].astype(F32)

    work = []
    for n, pieces in enumerate(g0_blocks):
        work.append((0, pieces, g0_blocks[n - 1] if n else None, (qf_ref, kf_ref, vf_ref), True))
    for (r4, n), pieces in g1_blocks.items():
        work.append((1, pieces, g1_blocks[(r4, n - 1)] if n else None, (q1_ref, k1_ref, v1_ref), False))
    for pieces in g2_blocks:
        work.append((2, pieces, None, (q2_ref, k2_ref, v2_ref), False))

    def merge(r, acc2, l2, m2):
        m0, m1 = m_ref[0, r], m_ref[1, r]
        mx = jnp.maximum(jnp.maximum(m0, m1), m2)
        w0, w1, w2 = jnp.exp2(m0 - mx), jnp.exp2(m1 - mx), jnp.exp2(m2 - mx)
        num = w0 * acc_ref[0, r] + w1 * acc_ref[1, r] + w2 * acc2
        den = w0 * l_ref[0, r] + w1 * l_ref[1, r] + w2 * l2
        z = z_ref[r].astype(F32)
        o_ref[r] = (num * z / (den * (1.0 + jnp.exp2(z * (-LOG2E))))).astype(BF16)

    pending = []
    for item in work + [None] * AHEAD:
        if item is not None:
            g, pieces, prev, (q_r, k_r, _), cast = item
            q, k = rows_of(q_r, pieces), keys_of(k_r, prev, pieces)
            if cast:
                q, k = q.astype(BF16), k.astype(BF16)
            pending.append((item, logits(q, k, g, prev is None)))
        if item is None or len(pending) > AHEAD:
            (g_p, pieces_p, prev_p, (_, _, v_r), cast_p), s_p = pending.pop(0)
            v = keys_of(v_r, prev_p, pieces_p)
            vals = finish(s_p, v.astype(BF16) if cast_p else v)
            if g_p == 2:
                merge(pieces_p[0][0], *vals)
            else:
                store(g_p, pieces_p, vals)


def _attention(P, bm):
    B, _, M, _ = P.shape
    HP = HEADS // 2
    W = 2 * HEAD_DIM

    def spec(kind, g):
        base = (kind * 3 + g) * HP
        return pl.BlockSpec((None, N_PHASE, M, W), lambda b, hp, base=base: (b, 0, 0, base + hp))

    in_specs = [spec(kind, g) for kind in range(3) for g in range(3)]
    in_specs.append(pl.BlockSpec((None, N_PHASE, M, W), lambda b, hp: (b, 0, 0, 9 * HP + hp)))
    in_specs.append(pl.BlockSpec((3, None, 2 * BLK, 2 * BLK), lambda b, hp: (0, hp, 0, 0)))
    return pl.pallas_call(
        _attn_kernel,
        grid=(B, HP),
        in_specs=in_specs,
        out_specs=pl.BlockSpec((None, N_PHASE, M, W), lambda b, hp: (b, 0, 0, hp)),
        out_shape=jax.ShapeDtypeStruct((B, N_PHASE, M, HEADS * HEAD_DIM), BF16),
        scratch_shapes=[pltpu.VMEM((N_PHASE, M, W), F32) for _ in range(3)]
        + [pltpu.VMEM((2, N_PHASE, M, W), F32) for _ in range(3)],
        compiler_params=_cparams(("parallel", "parallel")),
        name="dilated_attention",
    )(*([P] * 10), bm)


def _t5_bucket(dist):
    max_exact = REL_BUCKETS // 2
    n = jnp.maximum(dist, 1).astype(F32)
    large = max_exact + (jnp.log(n / max_exact) / math.log(REL_MAX_DIST / max_exact)
                         * (REL_BUCKETS - max_exact)).astype(jnp.int32)
    large = jnp.minimum(large, REL_BUCKETS - 1)
    return jnp.where(dist < max_exact, dist, large)


def _bias_tables(rel_bias):
    a = np.arange(BLK)
    pos = (16 * (a % 8) + a // 8, 4 * (a % 32) + a // 32, a)
    back = np.stack([np.concatenate([BLK + p[:, None] - p[None, :], p[:, None] - p[None, :]], axis=1)
                     for p in pos])
    valid = (back >= 0) & (back <= BLK)
    dist = np.clip(back, 0, BLK) * np.asarray(DILATIONS)[:, None, None]
    bucket = jnp.where(jnp.asarray(valid), _t5_bucket(jnp.asarray(dist, jnp.int32)), REL_BUCKETS)
    onehot = (bucket[..., None] == jnp.arange(REL_BUCKETS + 1)).astype(F32)
    ext = jnp.concatenate([rel_bias.astype(F32), jnp.full((1, HEADS), NEG, F32)], axis=0)
    t = jnp.einsum("gijc,ch->ghij", onehot, ext * math.log2(math.e),
                   precision=lax.Precision.HIGHEST)
    return t.reshape(3, HEADS // 2, 2 * BLK, 2 * BLK)


def _out_proj_kernel(o_ref, w_ref, g_ref, x_ref, y_ref, *, mh):
    o = o_ref[...].reshape(N_PHASE * mh, o_ref.shape[-1])
    h = jnp.dot(o, w_ref[...], preferred_element_type=F32)
    ms = jnp.mean(h * h, axis=-1, keepdims=True)
    y = h * lax.rsqrt(ms + EPS) * g_ref[...]
    y_ref[...] = x_ref[...] + y.reshape(y_ref.shape)


def _out_proj(O, w, g, xp, *, mh=64):
    B, _, M, D = xp.shape
    return pl.pallas_call(
        functools.partial(_out_proj_kernel, mh=mh),
        grid=(B, M // mh),
        in_specs=[
            pl.BlockSpec((None, N_PHASE, mh, O.shape[-1]), lambda b, m: (b, 0, m, 0)),
            pl.BlockSpec(w.shape, lambda b, m: (0, 0)),
            pl.BlockSpec((1, D), lambda b, m: (0, 0)),
            pl.BlockSpec((None, N_PHASE, mh, D), lambda b, m: (b, 0, m, 0)),
        ],
        out_specs=pl.BlockSpec((None, N_PHASE, mh, D), lambda b, m: (b, 0, m, 0)),
        out_shape=jax.ShapeDtypeStruct(xp.shape, F32),
        compiler_params=_cparams(("parallel", "parallel")),
        name="attn_out_proj",
    )(O, w, g.reshape(1, D), xp)


def _ssm_in_proj_kernel(x_ref, g_ref, wt_ref, o_ref, h_ref):
    M = x_ref.shape[1]

    @pl.when(pl.program_id(1) == 0)
    def _():
        g = g_ref[...]
        for s in range(N_PHASE):
            xs = x_ref[s]
            ms = jnp.mean(xs * xs, axis=-1, keepdims=True)
            h_ref[s * M:(s + 1) * M, :] = (xs * lax.rsqrt(ms + EPS) * g).astype(BF16)

    res = lax.dot_general(wt_ref[...], h_ref[...], (((1,), (1,)), ((), ())),
                          preferred_element_type=F32)
    for s in range(N_PHASE):
        o_ref[s] = res[:, s * M:(s + 1) * M].astype(BF16)


def _ssm_in_proj(xp, g, wt, *, tn=512):
    B, _, M, D = xp.shape
    S = N_PHASE * M
    N = wt.shape[0]
    return pl.pallas_call(
        _ssm_in_proj_kernel,
        grid=(B, N // tn),
        in_specs=[
            pl.BlockSpec((None, N_PHASE, M, D), lambda b, j: (b, 0, 0, 0)),
            pl.BlockSpec((1, D), lambda b, j: (0, 0)),
            pl.BlockSpec((tn, D), lambda b, j: (j, 0)),
        ],
        out_specs=pl.BlockSpec((None, N_PHASE, tn, M), lambda b, j: (b, 0, j, 0)),
        out_shape=jax.ShapeDtypeStruct((B, N_PHASE, N, M), BF16),
        scratch_shapes=[pltpu.VMEM((S, D), BF16)],
        compiler_params=_cparams(("parallel", "arbitrary")),
        name="ssm_in_proj",
    )(xp, g.reshape(1, D), wt)


def _ssm_core_kernel(u_ref, m_ref, bre_ref, bim_ref, cre_ref, cim_ref, are_ref, aim_ref,
                     d_ref, y_ref, sre_ref, sim_ref, ym_ref, *, n_chunk, n_pair):
    C = SSM_GROUP
    QC = N_PHASE * C
    nb = u_ref.shape[0]
    N = nb * n_chunk
    n_grp = 2 * n_pair

    def grp(i, b):
        return u_ref[b, :, i * C:(i + 1) * C, :]

    us = [jnp.concatenate([grp(i, b).reshape(QC, n_chunk) for b in range(nb)], axis=1)
          for i in range(n_grp)]

    for k in range(n_pair):
        sre_ref[k] = jnp.concatenate([jnp.dot(bre_ref[i], us[i], preferred_element_type=F32)
                                      for i in (2 * k, 2 * k + 1)], axis=0).T
        sim_ref[k] = jnp.concatenate([jnp.dot(bim_ref[i], us[i], preferred_element_type=F32)
                                      for i in (2 * k, 2 * k + 1)], axis=0).T

    for i in range(n_grp):
        ym_ref[i] = jnp.dot(m_ref[i], us[i], preferred_element_type=F32)

    ar = [jnp.broadcast_to(are_ref[k], (nb, are_ref.shape[-1])) for k in range(n_pair)]
    ai = [jnp.broadcast_to(aim_ref[k], (nb, aim_ref.shape[-1])) for k in range(n_pair)]
    sr = [jnp.zeros_like(ar[0])] * n_pair
    si = [jnp.zeros_like(ar[0])] * n_pair
    for c in range(n_chunk):
        rows = pl.ds(c, nb, stride=n_chunk)
        for k in range(n_pair):
            xr, xi = sre_ref[k, rows, :], sim_ref[k, rows, :]
            sre_ref[k, rows, :] = sr[k]
            sim_ref[k, rows, :] = si[k]
            sr[k], si[k] = (ar[k] * sr[k] - ai[k] * si[k] + xr, ar[k] * si[k] + ai[k] * sr[k] + xi)

    nt = (((1,), (1,)), ((), ()))
    for k in range(n_pair):
        inter = (lax.dot_general(cre_ref[k], sre_ref[k].astype(BF16), nt, preferred_element_type=F32)
                 + lax.dot_general(cim_ref[k], sim_ref[k].astype(BF16), nt,
                                   preferred_element_type=F32))
        for i in (2 * k, 2 * k + 1):
            y = ym_ref[i] + inter[(i % 2) * QC:(i % 2 + 1) * QC]
            for b in range(nb):
                yb = y[:, b * n_chunk:(b + 1) * n_chunk].reshape(N_PHASE, C, n_chunk)
                yb = yb + d_ref[i * C:(i + 1) * C] * grp(i, b).astype(F32)
                y_ref[b, :, i * C:(i + 1) * C, :] = yb.astype(BF16)


def _ssm_core(UT, prm, *, n_pair=2):
    B, _, _, n_chunk = UT.shape
    N = B * n_chunk
    G2 = SSM_N_GROUPS // 2
    C2 = 2 * SSM_GROUP
    QC = N_PHASE * SSM_GROUP
    P = SSM_STATE
    blk = lambda *shape: pl.BlockSpec(shape, lambda g: (g,) + (0,) * (len(shape) - 1))
    return pl.pallas_call(
        functools.partial(_ssm_core_kernel, n_chunk=n_chunk, n_pair=n_pair),
        grid=(G2 // n_pair,),
        in_specs=[
            pl.BlockSpec((B, N_PHASE, n_pair * C2, n_chunk), lambda g: (0, 0, g, 0)),
            blk(2 * n_pair, QC, QC), blk(2 * n_pair, P, QC), blk(2 * n_pair, P, QC),
            blk(n_pair, 2 * QC, 2 * P), blk(n_pair, 2 * QC, 2 * P),
            blk(n_pair, 1, 2 * P), blk(n_pair, 1, 2 * P),
            blk(n_pair * C2, 1),
        ],
        out_specs=pl.BlockSpec((B, N_PHASE, n_pair * C2, n_chunk), lambda g: (0, 0, g, 0)),
        out_shape=jax.ShapeDtypeStruct((B, N_PHASE, SSM_N_GROUPS * SSM_GROUP, n_chunk), BF16),
        scratch_shapes=[pltpu.VMEM((n_pair, N, 2 * P), F32), pltpu.VMEM((n_pair, N, 2 * P), F32),
                        pltpu.VMEM((2 * n_pair, QC, N), F32)],
        compiler_params=_cparams(("parallel",)),
        name="ssm_core",
    )(UT, prm["m"], prm["b_re"], prm["b_im"], prm["c_re"], prm["c_im"],
      prm["a_re"], prm["a_im"], prm["d"])


def _ssm_param_kernel(arc_ref, aic_ref, arr_ref, air_ref, ldr_ref, bre_ref, bim_ref,
                      cre_ref, cim_ref, cpr_ref, cpi_ref,
                      m_ref, br_ref, bi_ref, cr_ref, ci_ref, ar_ref, ai_ref):
    Q, C, P = N_PHASE, SSM_GROUP, SSM_STATE
    QC = Q * C
    hi = lax.Precision.HIGHEST
    tile = (lax.broadcasted_iota(jnp.int32, (C, QC), 1) % C
            == lax.broadcasted_iota(jnp.int32, (C, QC), 0)).astype(F32)
    lane_s = lax.broadcasted_iota(jnp.int32, (P, QC), 1) // C

    dt = jnp.exp(ldr_ref[...])
    mag = jnp.exp(arr_ref[...] * dt)
    a1r, a1i = mag * jnp.cos(air_ref[...] * dt), mag * jnp.sin(air_ref[...] * dt)
    a1r_c = jnp.broadcast_to(a1r, (8, 2 * P)).T[:, :1]
    a1i_c = jnp.broadcast_to(a1i, (8, 2 * P)).T[:, :1]

    for i in range(2):
        a_re, a_im = arc_ref[i], aic_ref[i]
        abr, abi = a1r_c[i * P:(i + 1) * P], a1i_c[i * P:(i + 1) * P]
        den = a_re * a_re + a_im * a_im
        cfr = ((abr - 1.0) * a_re + abi * a_im) / den
        cfi = (abi * a_re - (abr - 1.0) * a_im) / den
        bbr = cfr * bre_ref[i] - cfi * bim_ref[i]
        bbi = cfr * bim_ref[i] + cfi * bre_ref[i]

        pr, pi = jnp.ones_like(abr), jnp.zeros_like(abr)
        pw_r = jnp.zeros((P, QC), F32)
        pw_i = jnp.zeros((P, QC), F32)
        for n in range(Q):
            sel = lane_s == Q - 1 - n
            pw_r = jnp.where(sel, pr, pw_r)
            pw_i = jnp.where(sel, pi, pw_i)
            pr, pi = pr * abr - pi * abi, pr * abi + pi * abr
        bbr_t = jnp.dot(bbr, tile, precision=hi, preferred_element_type=F32)
        bbi_t = jnp.dot(bbi, tile, precision=hi, preferred_element_type=F32)
        bst_r = pw_r * bbr_t - pw_i * bbi_t
        bst_i = pw_r * bbi_t + pw_i * bbr_t
        br_ref[i] = bst_r.astype(BF16)
        bi_ref[i] = bst_i.astype(BF16)

        kl = (jnp.dot(cre_ref[i], bst_r, precision=hi, preferred_element_type=F32)
              - jnp.dot(cim_ref[i], bst_i, precision=hi, preferred_element_type=F32))
        klz = jnp.concatenate([kl, jnp.zeros_like(kl)], axis=1)
        for t in range(Q):
            sh = (2 * QC - C * (Q - 1 - t)) % (2 * QC)
            row = klz if sh == 0 else pltpu.roll(klz, sh, axis=1)
            m_ref[i, t * C:(t + 1) * C, :] = row[:, :QC].astype(BF16)

    qr, qi = a1r, a1i
    for t in range(Q):
        for i in range(2):
            rows = slice(i * QC + t * C, i * QC + (t + 1) * C)
            cr_ref[rows, :] = (cpr_ref[i] * qr - cpi_ref[i] * qi).astype(BF16)
            ci_ref[rows, :] = (-(cpr_ref[i] * qi + cpi_ref[i] * qr)).astype(BF16)
        if t < Q - 1:
            qr, qi = qr * a1r - qi * a1i, qr * a1i + qi * a1r
    ar_ref[...] = qr
    ai_ref[...] = qi


def _ssm_params(a_re, a_im, log_dt, b_re, b_im, c_re, c_im, d_skip):
    G, P, C, Q = SSM_N_GROUPS, SSM_STATE, SSM_GROUP, N_PHASE
    QC = Q * C
    f = lambda t: t.astype(F32)
    a_re, a_im, log_dt, b_re, b_im, c_re, c_im = map(f, (a_re, a_im, log_dt, b_re, b_im, c_re, c_im))
    even = (jnp.arange(G) % 2 == 0)[:, None, None]

    def lane_half(c):
        return jnp.concatenate([jnp.where(even, c, 0.0), jnp.where(even, 0.0, c)], axis=-1)

    pair = lambda blk: pl.BlockSpec((2,) + blk, lambda g: (g,) + (0,) * len(blk))
    one = lambda blk: pl.BlockSpec((None,) + blk, lambda g: (g,) + (0,) * len(blk))
    outs = pl.pallas_call(
        _ssm_param_kernel,
        grid=(G // 2,),
        in_specs=[pair((P, 1)), pair((P, 1)), one((1, 2 * P)), one((1, 2 * P)),
                  one((1, 2 * P)), pair((P, C)), pair((P, C)), pair((C, P)), pair((C, P)),
                  pair((C, 2 * P)), pair((C, 2 * P))],
        out_specs=[pair((QC, QC)), pair((P, QC)), pair((P, QC)), one((2 * QC, 2 * P)),
                   one((2 * QC, 2 * P)), one((1, 2 * P)), one((1, 2 * P))],
        out_shape=[jax.ShapeDtypeStruct((G, QC, QC), BF16), jax.ShapeDtypeStruct((G, P, QC), BF16),
                   jax.ShapeDtypeStruct((G, P, QC), BF16),
                   jax.ShapeDtypeStruct((G // 2, 2 * QC, 2 * P), BF16),
                   jax.ShapeDtypeStruct((G // 2, 2 * QC, 2 * P), BF16),
                   jax.ShapeDtypeStruct((G // 2, 1, 2 * P), F32),
                   jax.ShapeDtypeStruct((G // 2, 1, 2 * P), F32)],
        compiler_params=_cparams(("parallel",)),
        name="ssm_params",
    )(a_re.reshape(G, P, 1), a_im.reshape(G, P, 1),
      a_re.reshape(G // 2, 1, 2 * P), a_im.reshape(G // 2, 1, 2 * P),
      jnp.repeat(log_dt, P).reshape(G // 2, 1, 2 * P), b_re, b_im, c_re, c_im,
      lane_half(c_re), lane_half(c_im))
    keys = ("m", "b_re", "b_im", "c_re", "c_im", "a_re", "a_im")
    prm = dict(zip(keys, outs))
    prm["d"] = d_skip.astype(F32).reshape(G * C, 1)
    return prm


def _ssm_out_kernel(y_ref, z_ref, wg_ref, bg_ref, wo_ref, gp_ref, x_ref, o_hbm, obuf, sem, *,
                    nb, n_sub):
    s, j = pl.program_id(0), pl.program_id(1)
    nj = pl.num_programs(1)
    step = s * nj + j
    last = pl.num_programs(0) * nj - 1
    slot = step % 2

    def out_copy(slot_, s_, j_):
        return pltpu.make_async_copy(obuf.at[slot_], o_hbm.at[pl.ds(j_ * nb, nb), :, s_, :],
                                     sem.at[slot_])

    @pl.when(step >= 2)
    def _():
        out_copy(slot, (step - 2) // nj, (step - 2) % nj).wait()

    bs = nb // n_sub

    def lanes(ref, k):
        return jnp.concatenate([ref[b] for b in range(k * bs, (k + 1) * bs)], axis=1).astype(F32)

    def glu_in(k):
        g = _gelu_tanh(lanes(y_ref, k))
        return g, jnp.dot(wg_ref[...], g.astype(BF16), preferred_element_type=F32)

    def proj(k, g, gl):
        y2 = g * _sigmoid(gl + bg_ref[...])
        z = lanes(z_ref, k)
        gated = (y2 * (z * _sigmoid(z))).astype(BF16)
        return jnp.dot(wo_ref[...], gated, preferred_element_type=F32)

    def finish(k, ot):
        ms = jnp.mean(ot * ot, axis=0, keepdims=True)
        nt = (ot * lax.rsqrt(ms + EPS)).T
        rows = slice(k * bs, (k + 1) * bs)
        obuf[slot, rows] = x_ref[rows] + (nt * gp_ref[...]).reshape(bs, -1, nt.shape[-1])

    a = [glu_in(k) for k in range(n_sub)]
    o = [proj(k, *a[k]) for k in range(n_sub)]
    for k in range(n_sub):
        finish(k, o[k])
    out_copy(slot, s, j).start()

    @pl.when(step == last)
    def _():
        out_copy(slot, s, j).wait()

    @pl.when(jnp.logical_and(step == last, last >= 1))
    def _():
        out_copy(1 - slot, (step - 1) // nj, (step - 1) % nj).wait()


def _ssm_out(YT, UT, wgt, bg, wot, gp, xp, *, nb=8, n_sub=2):
    B, _, M, D = xp.shape
    E = YT.shape[2]
    out = pl.pallas_call(
        functools.partial(_ssm_out_kernel, nb=nb, n_sub=n_sub),
        grid=(N_PHASE, B // nb),
        in_specs=[
            pl.BlockSpec((nb, None, E, M), lambda s, j: (j, s, 0, 0)),
            pl.BlockSpec((nb, None, E, M), lambda s, j: (j, s, 1, 0)),
            pl.BlockSpec((E, E), lambda s, j: (0, 0)),
            pl.BlockSpec((E, 1), lambda s, j: (0, 0)),
            pl.BlockSpec((D, E), lambda s, j: (0, 0)),
            pl.BlockSpec((1, D), lambda s, j: (0, 0)),
            pl.BlockSpec((nb, None, M, D), lambda s, j: (j, s, 0, 0)),
        ],
        out_specs=pl.BlockSpec(memory_space=pl.ANY),
        out_shape=jax.ShapeDtypeStruct((B, M, N_PHASE, D), F32),
        scratch_shapes=[pltpu.VMEM((2, nb, M, D), F32), pltpu.SemaphoreType.DMA((2,))],
        compiler_params=_cparams(("arbitrary", "arbitrary")),
        name="ssm_out",
    )(YT, UT, wgt, bg.reshape(E, 1), wot, gp.reshape(1, D), xp)
    return out.reshape(B, M * N_PHASE, D)


def kernel(x, rel_bias, attn_pre_norm, attn_w_in, attn_w_out, attn_post_norm, ssm_pre_norm, ssm_w_in, ssm_a_re, ssm_a_im, ssm_log_dt, ssm_b_re, ssm_b_im, ssm_c_re, ssm_c_im, ssm_d, ssm_w_glu, ssm_b_glu, ssm_w_out, ssm_post_norm):
    B, S, D = x.shape
    n_chunk = S // N_PHASE

    P, xp = _norm_proj(x, attn_pre_norm[0], attn_w_in[0], scaled_cols=3 * HEADS * HEAD_DIM,
                       scale=HEAD_DIM ** -0.5 * LOG2E)
    O = _attention(P, _bias_tables(rel_bias))
    xp = _out_proj(O, attn_w_out[0].astype(BF16), attn_post_norm[0], xp)

    prm = _ssm_params(ssm_a_re[0], ssm_a_im[0], ssm_log_dt[0], ssm_b_re[0], ssm_b_im[0],
                      ssm_c_re[0], ssm_c_im[0], ssm_d[0])
    UT = _ssm_in_proj(xp, ssm_pre_norm[0], ssm_w_in[0].T.astype(BF16))
    YT = _ssm_core(UT, prm)
    return _ssm_out(YT, UT, ssm_w_glu[0].T.astype(BF16), ssm_b_glu[0],
                    ssm_w_out[0].T.astype(BF16), ssm_post_norm[0], xp)
```

```python
import functools
import math

import numpy as np
import jax
import jax.numpy as jnp
from jax import lax
from jax.experimental import pallas as pl
from jax.experimental.pallas import tpu as pltpu

F32 = jnp.float32
BF16 = jnp.bfloat16

D_MODEL = 1024
HEAD_DIM = 64
HEADS = 16
N_PHASE = 16
BLK = 128
DILATIONS = (1, 4, 16)
REL_BUCKETS = 32
REL_MAX_DIST = 2048
SSM_GROUP = 16
SSM_N_GROUPS = 64
SSM_STATE = 64
EPS = 1e-6
NEG = -1e30
VMEM_LIMIT = 56 * 1024 * 1024


def _cparams(sem):
    return pltpu.CompilerParams(dimension_semantics=sem, vmem_limit_bytes=VMEM_LIMIT)


LOG2E = math.log2(math.e)


def _sigmoid(v):
    return 1.0 / (1.0 + jnp.exp2(v * (-LOG2E)))


def _gelu_tanh(x):
    k0 = -2.0 * math.sqrt(2.0 / math.pi) * LOG2E
    return x / (1.0 + jnp.exp2(x * (k0 + (k0 * 0.044715) * (x * x))))


def _norm_proj_kernel(x_hbm, g_ref, w_ref, o_ref, xp_hbm, xbuf, h_ref, gsem, osem, *,
                      scaled_tiles, scale):
    b, j = pl.program_id(0), pl.program_id(1)
    n_b, n_j = pl.num_programs(0), pl.num_programs(1)
    slot = b % 2

    def phase_copy(bb, sl, r):
        return pltpu.make_async_copy(x_hbm.at[bb, :, r, :], xbuf.at[sl, r], gsem.at[sl])

    def gather_start(bb, sl):
        lax.fori_loop(0, N_PHASE, lambda r, c: (phase_copy(bb, sl, r).start(), c)[1], 0)

    def gather_wait(bb, sl):
        lax.fori_loop(0, N_PHASE, lambda r, c: (phase_copy(bb, sl, r).wait(), c)[1], 0)

    def write_out(bb, sl):
        return pltpu.make_async_copy(xbuf.at[sl], xp_hbm.at[bb], osem.at[sl])

    @pl.when(j == 0)
    def _():
        @pl.when(b == 0)
        def _():
            gather_start(0, 0)

        gather_wait(b, slot)

        @pl.when(b >= 1)
        def _():
            write_out(b - 1, 1 - slot).wait()

        @pl.when(b + 1 < n_b)
        def _():
            gather_start(b + 1, 1 - slot)

        write_out(b, slot).start()
        x = xbuf[slot].reshape(h_ref.shape)
        ms = jnp.mean(x * x, axis=-1, keepdims=True)
        h_ref[...] = (x * lax.rsqrt(ms + EPS) * g_ref[...]).astype(BF16)

    w = (w_ref[...] * jnp.where(j < scaled_tiles, scale, 1.0).astype(F32)).astype(BF16)
    res = jnp.dot(h_ref[...], w, preferred_element_type=F32)
    o_ref[...] = res.reshape(o_ref.shape).astype(BF16)

    @pl.when(jnp.logical_and(b == n_b - 1, j == n_j - 1))
    def _():
        write_out(b, slot).wait()


def _norm_proj(x, g, w, *, scaled_cols, scale, tn=1024):
    B, S, D = x.shape
    M = S // N_PHASE
    N = w.shape[1]
    assert scaled_cols % tn == 0
    return pl.pallas_call(
        functools.partial(_norm_proj_kernel, scaled_tiles=scaled_cols // tn, scale=scale),
        grid=(B, N // tn),
        in_specs=[
            pl.BlockSpec(memory_space=pl.ANY),
            pl.BlockSpec((1, D), lambda b, j: (0, 0)),
            pl.BlockSpec((D, tn), lambda b, j: (0, j)),
        ],
        out_specs=[pl.BlockSpec((None, N_PHASE, M, tn), lambda b, j: (b, 0, 0, j)),
                   pl.BlockSpec(memory_space=pl.ANY)],
        out_shape=[jax.ShapeDtypeStruct((B, N_PHASE, M, N), BF16),
                   jax.ShapeDtypeStruct((B, N_PHASE, M, D), F32)],
        scratch_shapes=[pltpu.VMEM((2, N_PHASE, M, D), F32), pltpu.VMEM((S, D), BF16),
                        pltpu.SemaphoreType.DMA((2,)), pltpu.SemaphoreType.DMA((2,))],
        compiler_params=_cparams(("arbitrary", "arbitrary")),
        name="attn_norm_proj",
    )(x.reshape(B, M, N_PHASE, D), g.reshape(1, D), w)


def _attn_kernel(q0_ref, q1_ref, q2_ref, k0_ref, k1_ref, k2_ref, v0_ref, v1_ref, v2_ref,
                 z_ref, bm_ref, o_ref, qf_ref, kf_ref, vf_ref, acc_ref, l_ref, m_ref):
    W = 2 * HEAD_DIM
    AHEAD = 3
    lane = lax.broadcasted_iota(jnp.int32, (BLK, W), 1)
    first_head = lane < HEAD_DIM

    def logits(q, k, g, cur_only):
        zq = jnp.zeros_like(q)
        qs = jnp.concatenate([jnp.where(first_head, q, zq), jnp.where(first_head, zq, q)], axis=0)
        s = lax.dot_general(qs, k, (((1,), (1,)), ((), ())), preferred_element_type=F32)
        return s + (bm_ref[g, :, BLK:2 * BLK] if cur_only else bm_ref[g])

    def finish(s, v):
        m = jnp.max(s, axis=-1, keepdims=True)
        p = jnp.exp2(s - m).astype(BF16)
        va = jnp.concatenate([v, jnp.ones((v.shape[0], W), BF16)], axis=1)
        pv = jnp.dot(p, va, preferred_element_type=F32)
        acc = jnp.where(first_head, pv[:BLK, :W], pv[BLK:, :W])
        l = jnp.where(first_head, pv[:BLK, W:], pv[BLK:, W:])
        mm = jnp.where(first_head, m[:BLK], m[BLK:])
        return acc, l, mm

    def rows_of(ref, pieces):
        return jnp.concatenate([ref[ph, lo:lo + n, :] for ph, lo, n in pieces], axis=0)

    def keys_of(ref, prev, cur):
        return rows_of(ref, cur) if prev is None else rows_of(ref, prev + cur)

    def store(g, pieces, vals):
        at = 0
        for ph, lo, n in pieces:
            for ref, val in zip((acc_ref, l_ref, m_ref), vals):
                ref[g, ph, lo:lo + n, :] = val[at:at + n]
            at += n

    M = q0_ref.shape[1]
    g2_blocks = [[(r, 0, BLK)] for r in range(N_PHASE)]
    g1_blocks = {(r4, n): [(4 * q4 + r4, 32 * n, 32) for q4 in range(4)]
                 for r4 in range(4) for n in range(M // 32)}
    g0_blocks = [[(r, 8 * n, 8) for r in range(N_PHASE)] for n in range(M // 8)]

    for r in range(N_PHASE):
        qf_ref[r] = q0_ref[r].astype(F32)
        kf_ref[r] = k0_ref[r].astype(F32)
        vf_ref[r] = v0_ref[r].astype(F32)

    work = []
    for n, pieces in enumerate(g0_blocks):
        work.append((0, pieces, g0_blocks[n - 1] if n else None, (qf_ref, kf_ref, vf_ref), True))
    for (r4, n), pieces in g1_blocks.items():
        work.append((1, pieces, g1_blocks[(r4, n - 1)] if n else None, (q1_ref, k1_ref, v1_ref), False))
    for pieces in g2_blocks:
        work.append((2, pieces, None, (q2_ref, k2_ref, v2_ref), False))

    def merge(r, acc2, l2, m2):
        m0, m1 = m_ref[0, r], m_ref[1, r]
        mx = jnp.maximum(jnp.maximum(m0, m1), m2)
        w0, w1, w2 = jnp.exp2(m0 - mx), jnp.exp2(m1 - mx), jnp.exp2(m2 - mx)
        num = w0 * acc_ref[0, r] + w1 * acc_ref[1, r] + w2 * acc2
        den = w0 * l_ref[0, r] + w1 * l_ref[1, r] + w2 * l2
        z = z_ref[r].astype(F32)
        o_ref[r] = (num * z / (den * (1.0 + jnp.exp2(z * (-LOG2E))))).astype(BF16)

    pending = []
    for item in work + [None] * AHEAD:
        if item is not None:
            g, pieces, prev, (q_r, k_r, _), cast = item
            q, k = rows_of(q_r, pieces), keys_of(k_r, prev, pieces)
            if cast:
                q, k = q.astype(BF16), k.astype(BF16)
            pending.append((item, logits(q, k, g, prev is None)))
        if item is None or len(pending) > AHEAD:
            (g_p, pieces_p, prev_p, (_, _, v_r), cast_p), s_p = pending.pop(0)
            v = keys_of(v_r, prev_p, pieces_p)
            vals = finish(s_p, v.astype(BF16) if cast_p else v)
            if g_p == 2:
                merge(pieces_p[0][0], *vals)
            else:
                store(g_p, pieces_p, vals)


def _attention(P, bm):
    B, _, M, _ = P.shape
    HP = HEADS // 2
    W = 2 * HEAD_DIM

    def spec(kind, g):
        base = (kind * 3 + g) * HP
        return pl.BlockSpec((None, N_PHASE, M, W), lambda b, hp, base=base: (b, 0, 0, base + hp))

    in_specs = [spec(kind, g) for kind in range(3) for g in range(3)]
    in_specs.append(pl.BlockSpec((None, N_PHASE, M, W), lambda b, hp: (b, 0, 0, 9 * HP + hp)))
    in_specs.append(pl.BlockSpec((3, None, 2 * BLK, 2 * BLK), lambda b, hp: (0, hp, 0, 0)))
    return pl.pallas_call(
        _attn_kernel,
        grid=(B, HP),
        in_specs=in_specs,
        out_specs=pl.BlockSpec((None, N_PHASE, M, W), lambda b, hp: (b, 0, 0, hp)),
        out_shape=jax.ShapeDtypeStruct((B, N_PHASE, M, HEADS * HEAD_DIM), BF16),
        scratch_shapes=[pltpu.VMEM((N_PHASE, M, W), F32) for _ in range(3)]
        + [pltpu.VMEM((2, N_PHASE, M, W), F32) for _ in range(3)],
        compiler_params=_cparams(("parallel", "parallel")),
        name="dilated_attention",
    )(*([P] * 10), bm)


def _t5_bucket(dist):
    max_exact = REL_BUCKETS // 2
    n = jnp.maximum(dist, 1).astype(F32)
    large = max_exact + (jnp.log(n / max_exact) / math.log(REL_MAX_DIST / max_exact)
                         * (REL_BUCKETS - max_exact)).astype(jnp.int32)
    large = jnp.minimum(large, REL_BUCKETS - 1)
    return jnp.where(dist < max_exact, dist, large)


def _bias_tables(rel_bias):
    a = np.arange(BLK)
    pos = (16 * (a % 8) + a // 8, 4 * (a % 32) + a // 32, a)
    back = np.stack([np.concatenate([BLK + p[:, None] - p[None, :], p[:, None] - p[None, :]], axis=1)
                     for p in pos])
    valid = (back >= 0) & (back <= BLK)
    dist = np.clip(back, 0, BLK) * np.asarray(DILATIONS)[:, None, None]
    bucket = jnp.where(jnp.asarray(valid), _t5_bucket(jnp.asarray(dist, jnp.int32)), REL_BUCKETS)
    onehot = (bucket[..., None] == jnp.arange(REL_BUCKETS + 1)).astype(F32)
    ext = jnp.concatenate([rel_bias.astype(F32), jnp.full((1, HEADS), NEG, F32)], axis=0)
    t = jnp.einsum("gijc,ch->ghij", onehot, ext * math.log2(math.e),
                   precision=lax.Precision.HIGHEST)
    return t.reshape(3, HEADS // 2, 2 * BLK, 2 * BLK)


def _out_proj_kernel(o_ref, w_ref, g_ref, x_ref, y_ref, *, mh):
    o = o_ref[...].reshape(N_PHASE * mh, o_ref.shape[-1])
    h = jnp.dot(o, w_ref[...].astype(BF16), preferred_element_type=F32)
    ms = jnp.mean(h * h, axis=-1, keepdims=True)
    y = h * lax.rsqrt(ms + EPS) * g_ref[...]
    y_ref[...] = x_ref[...] + y.reshape(y_ref.shape)


def _out_proj(O, w, g, xp, *, mh=64):
    B, _, M, D = xp.shape
    return pl.pallas_call(
        functools.partial(_out_proj_kernel, mh=mh),
        grid=(B, M // mh),
        in_specs=[
            pl.BlockSpec((None, N_PHASE, mh, O.shape[-1]), lambda b, m: (b, 0, m, 0)),
            pl.BlockSpec(w.shape, lambda b, m: (0, 0)),
            pl.BlockSpec((1, D), lambda b, m: (0, 0)),
            pl.BlockSpec((None, N_PHASE, mh, D), lambda b, m: (b, 0, m, 0)),
        ],
        out_specs=pl.BlockSpec((None, N_PHASE, mh, D), lambda b, m: (b, 0, m, 0)),
        out_shape=jax.ShapeDtypeStruct(xp.shape, F32),
        compiler_params=_cparams(("parallel", "parallel")),
        name="attn_out_proj",
    )(O, w, g.reshape(1, D), xp)


def _ssm_in_proj_kernel(x_ref, g_ref, w_ref, o_ref, h_ref):
    M = x_ref.shape[1]

    @pl.when(pl.program_id(1) == 0)
    def _():
        g = g_ref[...]
        for s in range(N_PHASE):
            xs = x_ref[s]
            ms = jnp.mean(xs * xs, axis=-1, keepdims=True)
            h_ref[s * M:(s + 1) * M, :] = (xs * lax.rsqrt(ms + EPS) * g).astype(BF16)

    wt = w_ref[...].T.astype(BF16)
    res = lax.dot_general(wt, h_ref[...], (((1,), (1,)), ((), ())), preferred_element_type=F32)
    for s in range(N_PHASE):
        o_ref[s] = res[:, s * M:(s + 1) * M].astype(BF16)


def _ssm_in_proj(xp, g, w, *, tn=512):
    B, _, M, D = xp.shape
    S = N_PHASE * M
    N = w.shape[1]
    return pl.pallas_call(
        _ssm_in_proj_kernel,
        grid=(B, N // tn),
        in_specs=[
            pl.BlockSpec((None, N_PHASE, M, D), lambda b, j: (b, 0, 0, 0)),
            pl.BlockSpec((1, D), lambda b, j: (0, 0)),
            pl.BlockSpec((D, tn), lambda b, j: (0, j)),
        ],
        out_specs=pl.BlockSpec((None, N_PHASE, tn, M), lambda b, j: (b, 0, j, 0)),
        out_shape=jax.ShapeDtypeStruct((B, N_PHASE, N, M), BF16),
        scratch_shapes=[pltpu.VMEM((S, D), BF16)],
        compiler_params=_cparams(("parallel", "arbitrary")),
        name="ssm_in_proj",
    )(xp, g.reshape(1, D), w)


def _ssm_core_kernel(u_ref, m_ref, bre_ref, bim_ref, cre_ref, cim_ref, are_ref, aim_ref,
                     d_ref, y_ref, sre_ref, sim_ref, ym_ref, *, n_chunk, n_pair):
    C = SSM_GROUP
    QC = N_PHASE * C
    nb = u_ref.shape[0]
    N = nb * n_chunk
    n_grp = 2 * n_pair

    def grp(i, b):
        return u_ref[b, :, i * C:(i + 1) * C, :]

    us = [jnp.concatenate([grp(i, b).reshape(QC, n_chunk) for b in range(nb)], axis=1)
          for i in range(n_grp)]

    for k in range(n_pair):
        sre_ref[k] = jnp.concatenate([jnp.dot(bre_ref[i], us[i], preferred_element_type=F32)
                                      for i in (2 * k, 2 * k + 1)], axis=0).T
        sim_ref[k] = jnp.concatenate([jnp.dot(bim_ref[i], us[i], preferred_element_type=F32)
                                      for i in (2 * k, 2 * k + 1)], axis=0).T

    for i in range(n_grp):
        ym_ref[i] = jnp.dot(m_ref[i], us[i], preferred_element_type=F32)

    ar = [jnp.broadcast_to(are_ref[k], (nb, are_ref.shape[-1])) for k in range(n_pair)]
    ai = [jnp.broadcast_to(aim_ref[k], (nb, aim_ref.shape[-1])) for k in range(n_pair)]
    sr = [jnp.zeros_like(ar[0])] * n_pair
    si = [jnp.zeros_like(ar[0])] * n_pair
    for c in range(n_chunk):
        rows = pl.ds(c, nb, stride=n_chunk)
        for k in range(n_pair):
            xr, xi = sre_ref[k, rows, :], sim_ref[k, rows, :]
            sre_ref[k, rows, :] = sr[k]
            sim_ref[k, rows, :] = si[k]
            sr[k], si[k] = (ar[k] * sr[k] - ai[k] * si[k] + xr, ar[k] * si[k] + ai[k] * sr[k] + xi)

    nt = (((1,), (1,)), ((), ()))
    for k in range(n_pair):
        inter = (lax.dot_general(cre_ref[k], sre_ref[k].astype(BF16), nt, preferred_element_type=F32)
                 + lax.dot_general(cim_ref[k], sim_ref[k].astype(BF16), nt,
                                   preferred_element_type=F32))
        for i in (2 * k, 2 * k + 1):
            y = ym_ref[i] + inter[(i % 2) * QC:(i % 2 + 1) * QC]
            for b in range(nb):
                yb = y[:, b * n_chunk:(b + 1) * n_chunk].reshape(N_PHASE, C, n_chunk)
                yb = yb + d_ref[i * C:(i + 1) * C] * grp(i, b).astype(F32)
                y_ref[b, :, i * C:(i + 1) * C, :] = yb.astype(BF16)


def _ssm_core(UT, prm, *, n_pair=2):
    B, _, _, n_chunk = UT.shape
    N = B * n_chunk
    G2 = SSM_N_GROUPS // 2
    C2 = 2 * SSM_GROUP
    QC = N_PHASE * SSM_GROUP
    P = SSM_STATE
    blk = lambda *shape: pl.BlockSpec(shape, lambda g: (g,) + (0,) * (len(shape) - 1))
    return pl.pallas_call(
        functools.partial(_ssm_core_kernel, n_chunk=n_chunk, n_pair=n_pair),
        grid=(G2 // n_pair,),
        in_specs=[
            pl.BlockSpec((B, N_PHASE, n_pair * C2, n_chunk), lambda g: (0, 0, g, 0)),
            blk(2 * n_pair, QC, QC), blk(2 * n_pair, P, QC), blk(2 * n_pair, P, QC),
            blk(n_pair, 2 * QC, 2 * P), blk(n_pair, 2 * QC, 2 * P),
            blk(n_pair, 1, 2 * P), blk(n_pair, 1, 2 * P),
            blk(n_pair * C2, 1),
        ],
        out_specs=pl.BlockSpec((B, N_PHASE, n_pair * C2, n_chunk), lambda g: (0, 0, g, 0)),
        out_shape=jax.ShapeDtypeStruct((B, N_PHASE, SSM_N_GROUPS * SSM_GROUP, n_chunk), BF16),
        scratch_shapes=[pltpu.VMEM((n_pair, N, 2 * P), F32), pltpu.VMEM((n_pair, N, 2 * P), F32),
                        pltpu.VMEM((2 * n_pair, QC, N), F32)],
        compiler_params=_cparams(("parallel",)),
        name="ssm_core",
    )(UT, prm["m"], prm["b_re"], prm["b_im"], prm["c_re"], prm["c_im"],
      prm["a_re"], prm["a_im"], prm["d"])


def _ssm_param_kernel(arc_ref, aic_ref, arr_ref, air_ref, ldr_ref, bre_ref, bim_ref,
                      cre_ref, cim_ref, cpr_ref, cpi_ref,
                      m_ref, br_ref, bi_ref, cr_ref, ci_ref, ar_ref, ai_ref):
    Q, C, P = N_PHASE, SSM_GROUP, SSM_STATE
    QC = Q * C
    hi = lax.Precision.HIGHEST
    tile = (lax.broadcasted_iota(jnp.int32, (C, QC), 1) % C
            == lax.broadcasted_iota(jnp.int32, (C, QC), 0)).astype(F32)
    lane_s = lax.broadcasted_iota(jnp.int32, (P, QC), 1) // C

    dt = jnp.exp(ldr_ref[...])
    mag = jnp.exp(arr_ref[...] * dt)
    a1r, a1i = mag * jnp.cos(air_ref[...] * dt), mag * jnp.sin(air_ref[...] * dt)
    a1r_c = jnp.broadcast_to(a1r, (8, 2 * P)).T[:, :1]
    a1i_c = jnp.broadcast_to(a1i, (8, 2 * P)).T[:, :1]

    for i in range(2):
        a_re, a_im = arc_ref[i], aic_ref[i]
        abr, abi = a1r_c[i * P:(i + 1) * P], a1i_c[i * P:(i + 1) * P]
        den = a_re * a_re + a_im * a_im
        cfr = ((abr - 1.0) * a_re + abi * a_im) / den
        cfi = (abi * a_re - (abr - 1.0) * a_im) / den
        bbr = cfr * bre_ref[i] - cfi * bim_ref[i]
        bbi = cfr * bim_ref[i] + cfi * bre_ref[i]

        pr, pi = jnp.ones_like(abr), jnp.zeros_like(abr)
        pw_r = jnp.zeros((P, QC), F32)
        pw_i = jnp.zeros((P, QC), F32)
        for n in range(Q):
            sel = lane_s == Q - 1 - n
            pw_r = jnp.where(sel, pr, pw_r)
            pw_i = jnp.where(sel, pi, pw_i)
            pr, pi = pr * abr - pi * abi, pr * abi + pi * abr
        bbr_t = jnp.dot(bbr, tile, precision=hi, preferred_element_type=F32)
        bbi_t = jnp.dot(bbi, tile, precision=hi, preferred_element_type=F32)
        bst_r = pw_r * bbr_t - pw_i * bbi_t
        bst_i = pw_r * bbi_t + pw_i * bbr_t
        br_ref[i] = bst_r.astype(BF16)
        bi_ref[i] = bst_i.astype(BF16)

        kl = (jnp.dot(cre_ref[i], bst_r, precision=hi, preferred_element_type=F32)
              - jnp.dot(cim_ref[i], bst_i, precision=hi, preferred_element_type=F32))
        klz = jnp.concatenate([kl, jnp.zeros_like(kl)], axis=1)
        for t in range(Q):
            sh = (2 * QC - C * (Q - 1 - t)) % (2 * QC)
            row = klz if sh == 0 else pltpu.roll(klz, sh, axis=1)
            m_ref[i, t * C:(t + 1) * C, :] = row[:, :QC].astype(BF16)

    qr, qi = a1r, a1i
    for t in range(Q):
        for i in range(2):
            rows = slice(i * QC + t * C, i * QC + (t + 1) * C)
            cr_ref[rows, :] = (cpr_ref[i] * qr - cpi_ref[i] * qi).astype(BF16)
            ci_ref[rows, :] = (-(cpr_ref[i] * qi + cpi_ref[i] * qr)).astype(BF16)
        if t < Q - 1:
            qr, qi = qr * a1r - qi * a1i, qr * a1i + qi * a1r
    ar_ref[...] = qr
    ai_ref[...] = qi


def _ssm_params(a_re, a_im, log_dt, b_re, b_im, c_re, c_im, d_skip):
    G, P, C, Q = SSM_N_GROUPS, SSM_STATE, SSM_GROUP, N_PHASE
    QC = Q * C
    f = lambda t: t.astype(F32)
    a_re, a_im, log_dt, b_re, b_im, c_re, c_im = map(f, (a_re, a_im, log_dt, b_re, b_im, c_re, c_im))
    even = (jnp.arange(G) % 2 == 0)[:, None, None]

    def lane_half(c):
        return jnp.concatenate([jnp.where(even, c, 0.0), jnp.where(even, 0.0, c)], axis=-1)

    pair = lambda blk: pl.BlockSpec((2,) + blk, lambda g: (g,) + (0,) * len(blk))
    one = lambda blk: pl.BlockSpec((None,) + blk, lambda g: (g,) + (0,) * len(blk))
    outs = pl.pallas_call(
        _ssm_param_kernel,
        grid=(G // 2,),
        in_specs=[pair((P, 1)), pair((P, 1)), one((1, 2 * P)), one((1, 2 * P)),
                  one((1, 2 * P)), pair((P, C)), pair((P, C)), pair((C, P)), pair((C, P)),
                  pair((C, 2 * P)), pair((C, 2 * P))],
        out_specs=[pair((QC, QC)), pair((P, QC)), pair((P, QC)), one((2 * QC, 2 * P)),
                   one((2 * QC, 2 * P)), one((1, 2 * P)), one((1, 2 * P))],
        out_shape=[jax.ShapeDtypeStruct((G, QC, QC), BF16), jax.ShapeDtypeStruct((G, P, QC), BF16),
                   jax.ShapeDtypeStruct((G, P, QC), BF16),
                   jax.ShapeDtypeStruct((G // 2, 2 * QC, 2 * P), BF16),
                   jax.ShapeDtypeStruct((G // 2, 2 * QC, 2 * P), BF16),
                   jax.ShapeDtypeStruct((G // 2, 1, 2 * P), F32),
                   jax.ShapeDtypeStruct((G // 2, 1, 2 * P), F32)],
        compiler_params=_cparams(("parallel",)),
        name="ssm_params",
    )(a_re.reshape(G, P, 1), a_im.reshape(G, P, 1),
      a_re.reshape(G // 2, 1, 2 * P), a_im.reshape(G // 2, 1, 2 * P),
      jnp.repeat(log_dt, P).reshape(G // 2, 1, 2 * P), b_re, b_im, c_re, c_im,
      lane_half(c_re), lane_half(c_im))
    keys = ("m", "b_re", "b_im", "c_re", "c_im", "a_re", "a_im")
    prm = dict(zip(keys, outs))
    prm["d"] = d_skip.astype(F32).reshape(G * C, 1)
    return prm


def _ssm_out_kernel(y_ref, z_ref, wg_ref, bg_ref, wo_ref, gp_ref, x_ref, o_hbm, obuf, sem,
                    wgt_ref, wot_ref, *, nb, n_sub):
    s, j = pl.program_id(0), pl.program_id(1)
    nj = pl.num_programs(1)
    step = s * nj + j
    last = pl.num_programs(0) * nj - 1
    slot = step % 2

    def out_copy(slot_, s_, j_):
        return pltpu.make_async_copy(obuf.at[slot_], o_hbm.at[pl.ds(j_ * nb, nb), :, s_, :],
                                     sem.at[slot_])

    @pl.when(step >= 2)
    def _():
        out_copy(slot, (step - 2) // nj, (step - 2) % nj).wait()

    @pl.when(step == 0)
    def _():
        wgt_ref[...] = wg_ref[...].T.astype(BF16)
        wot_ref[...] = wo_ref[...].T.astype(BF16)

    bs = nb // n_sub

    def lanes(ref, k):
        return jnp.concatenate([ref[b] for b in range(k * bs, (k + 1) * bs)], axis=1).astype(F32)

    def glu_in(k):
        g = _gelu_tanh(lanes(y_ref, k))
        return g, jnp.dot(wgt_ref[...], g.astype(BF16), preferred_element_type=F32)

    def proj(k, g, gl):
        y2 = g * _sigmoid(gl + bg_ref[...])
        z = lanes(z_ref, k)
        gated = (y2 * (z * _sigmoid(z))).astype(BF16)
        return jnp.dot(wot_ref[...], gated, preferred_element_type=F32)

    def finish(k, ot):
        ms = jnp.mean(ot * ot, axis=0, keepdims=True)
        nt = (ot * lax.rsqrt(ms + EPS)).T
        rows = slice(k * bs, (k + 1) * bs)
        obuf[slot, rows] = x_ref[rows] + (nt * gp_ref[...]).reshape(bs, -1, nt.shape[-1])

    a = [glu_in(k) for k in range(n_sub)]
    o = [proj(k, *a[k]) for k in range(n_sub)]
    for k in range(n_sub):
        finish(k, o[k])
    out_copy(slot, s, j).start()

    @pl.when(step == last)
    def _():
        out_copy(slot, s, j).wait()

    @pl.when(jnp.logical_and(step == last, last >= 1))
    def _():
        out_copy(1 - slot, (step - 1) // nj, (step - 1) % nj).wait()


def _ssm_out(YT, UT, wg, bg, wo, gp, xp, *, nb=8, n_sub=2):
    B, _, M, D = xp.shape
    E = YT.shape[2]
    out = pl.pallas_call(
        functools.partial(_ssm_out_kernel, nb=nb, n_sub=n_sub),
        grid=(N_PHASE, B // nb),
        in_specs=[
            pl.BlockSpec((nb, None, E, M), lambda s, j: (j, s, 0, 0)),
            pl.BlockSpec((nb, None, E, M), lambda s, j: (j, s, 1, 0)),
            pl.BlockSpec((E, E), lambda s, j: (0, 0)),
            pl.BlockSpec((E, 1), lambda s, j: (0, 0)),
            pl.BlockSpec((E, D), lambda s, j: (0, 0)),
            pl.BlockSpec((1, D), lambda s, j: (0, 0)),
            pl.BlockSpec((nb, None, M, D), lambda s, j: (j, s, 0, 0)),
        ],
        out_specs=pl.BlockSpec(memory_space=pl.ANY),
        out_shape=jax.ShapeDtypeStruct((B, M, N_PHASE, D), F32),
        scratch_shapes=[pltpu.VMEM((2, nb, M, D), F32), pltpu.SemaphoreType.DMA((2,)),
                        pltpu.VMEM((E, E), BF16), pltpu.VMEM((D, E), BF16)],
        compiler_params=_cparams(("arbitrary", "arbitrary")),
        name="ssm_out",
    )(YT, UT, wg, bg.reshape(E, 1), wo, gp.reshape(1, D), xp)
    return out.reshape(B, M * N_PHASE, D)


def kernel(x, rel_bias, attn_pre_norm, attn_w_in, attn_w_out, attn_post_norm, ssm_pre_norm, ssm_w_in, ssm_a_re, ssm_a_im, ssm_log_dt, ssm_b_re, ssm_b_im, ssm_c_re, ssm_c_im, ssm_d, ssm_w_glu, ssm_b_glu, ssm_w_out, ssm_post_norm):
    B, S, D = x.shape
    n_chunk = S // N_PHASE

    P, xp = _norm_proj(x, attn_pre_norm[0], attn_w_in[0], scaled_cols=3 * HEADS * HEAD_DIM,
                       scale=HEAD_DIM ** -0.5 * LOG2E)
    O = _attention(P, _bias_tables(rel_bias))
    xp = _out_proj(O, attn_w_out[0], attn_post_norm[0], xp)

    prm = _ssm_params(ssm_a_re[0], ssm_a_im[0], ssm_log_dt[0], ssm_b_re[0], ssm_b_im[0],
                      ssm_c_re[0], ssm_c_im[0], ssm_d[0])
    UT = _ssm_in_proj(xp, ssm_pre_norm[0], ssm_w_in[0])
    YT = _ssm_core(UT, prm)
    return _ssm_out(YT, UT, ssm_w_glu[0], ssm_b_glu[0], ssm_w_out[0], ssm_post_norm[0], xp)
```

```python
import functools
import math

import numpy as np
import jax
import jax.numpy as jnp
from jax import lax
from jax.experimental import pallas as pl
from jax.experimental.pallas import tpu as pltpu

F32 = jnp.float32
BF16 = jnp.bfloat16

D_MODEL = 1024
HEAD_DIM = 64
HEADS = 16
N_PHASE = 16
BLK = 128
DILATIONS = (1, 4, 16)
REL_BUCKETS = 32
REL_MAX_DIST = 2048
SSM_GROUP = 16
SSM_N_GROUPS = 64
SSM_STATE = 64
EPS = 1e-6
NEG = -1e30
VMEM_LIMIT = 56 * 1024 * 1024


def _cparams(sem):
    return pltpu.CompilerParams(dimension_semantics=sem, vmem_limit_bytes=VMEM_LIMIT)


LOG2E = math.log2(math.e)


def _sigmoid(v):
    return 1.0 / (1.0 + jnp.exp2(v * (-LOG2E)))


def _gelu_tanh(x):
    k0 = -2.0 * math.sqrt(2.0 / math.pi) * LOG2E
    return x / (1.0 + jnp.exp2(x * (k0 + (k0 * 0.044715) * (x * x))))


def _norm_proj_kernel(x_hbm, g_ref, w_ref, o_ref, xp_hbm, xbuf, h_ref, gsem, osem, *,
                      scaled_tiles, scale):
    b, j = pl.program_id(0), pl.program_id(1)
    n_b, n_j = pl.num_programs(0), pl.num_programs(1)
    slot = b % 2

    def phase_copy(bb, sl, r):
        return pltpu.make_async_copy(x_hbm.at[bb, :, r, :], xbuf.at[sl, r], gsem.at[sl])

    def gather_start(bb, sl):
        lax.fori_loop(0, N_PHASE, lambda r, c: (phase_copy(bb, sl, r).start(), c)[1], 0)

    def gather_wait(bb, sl):
        lax.fori_loop(0, N_PHASE, lambda r, c: (phase_copy(bb, sl, r).wait(), c)[1], 0)

    def write_out(bb, sl):
        return pltpu.make_async_copy(xbuf.at[sl], xp_hbm.at[bb], osem.at[sl])

    @pl.when(j == 0)
    def _():
        @pl.when(b == 0)
        def _():
            gather_start(0, 0)

        gather_wait(b, slot)

        @pl.when(b >= 1)
        def _():
            write_out(b - 1, 1 - slot).wait()

        @pl.when(b + 1 < n_b)
        def _():
            gather_start(b + 1, 1 - slot)

        write_out(b, slot).start()
        x = xbuf[slot].reshape(h_ref.shape)
        ms = jnp.mean(x * x, axis=-1, keepdims=True)
        h_ref[...] = (x * lax.rsqrt(ms + EPS) * g_ref[...]).astype(BF16)

    w = (w_ref[...] * jnp.where(j < scaled_tiles, scale, 1.0).astype(F32)).astype(BF16)
    res = jnp.dot(h_ref[...], w, preferred_element_type=F32)
    o_ref[...] = res.reshape(o_ref.shape).astype(BF16)

    @pl.when(jnp.logical_and(b == n_b - 1, j == n_j - 1))
    def _():
        write_out(b, slot).wait()


def _norm_proj(x, g, w, *, scaled_cols, scale, tn=1024):
    B, S, D = x.shape
    M = S // N_PHASE
    N = w.shape[1]
    assert scaled_cols % tn == 0
    return pl.pallas_call(
        functools.partial(_norm_proj_kernel, scaled_tiles=scaled_cols // tn, scale=scale),
        grid=(B, N // tn),
        in_specs=[
            pl.BlockSpec(memory_space=pl.ANY),
            pl.BlockSpec((1, D), lambda b, j: (0, 0)),
            pl.BlockSpec((D, tn), lambda b, j: (0, j)),
        ],
        out_specs=[pl.BlockSpec((None, N_PHASE, M, tn), lambda b, j: (b, 0, 0, j)),
                   pl.BlockSpec(memory_space=pl.ANY)],
        out_shape=[jax.ShapeDtypeStruct((B, N_PHASE, M, N), BF16),
                   jax.ShapeDtypeStruct((B, N_PHASE, M, D), F32)],
        scratch_shapes=[pltpu.VMEM((2, N_PHASE, M, D), F32), pltpu.VMEM((S, D), BF16),
                        pltpu.SemaphoreType.DMA((2,)), pltpu.SemaphoreType.DMA((2,))],
        compiler_params=_cparams(("arbitrary", "arbitrary")),
        name="attn_norm_proj",
    )(x.reshape(B, M, N_PHASE, D), g.reshape(1, D), w)


def _attn_kernel(q0_ref, q1_ref, q2_ref, k0_ref, k1_ref, k2_ref, v0_ref, v1_ref, v2_ref,
                 z_ref, bm_ref, o_ref, qf_ref, kf_ref, vf_ref, acc_ref, l_ref, m_ref):
    W = 2 * HEAD_DIM
    AHEAD = 3
    lane = lax.broadcasted_iota(jnp.int32, (BLK, W), 1)
    first_head = lane < HEAD_DIM

    def logits(q, k, g, cur_only):
        zq = jnp.zeros_like(q)
        qs = jnp.concatenate([jnp.where(first_head, q, zq), jnp.where(first_head, zq, q)], axis=0)
        s = lax.dot_general(qs, k, (((1,), (1,)), ((), ())), preferred_element_type=F32)
        return s + (bm_ref[g, :, BLK:2 * BLK] if cur_only else bm_ref[g])

    ones_a = jnp.where(first_head, 1.0, 0.0).astype(BF16)
    ones_b = jnp.where(first_head, 0.0, 1.0).astype(BF16)
    masked_v = {}

    def value_blocks(g, v_ref, pieces, cast):
        key = (g, tuple(pieces))
        if key not in masked_v:
            v = rows_of(v_ref, pieces)
            v = v.astype(BF16) if cast else v
            zv = jnp.zeros_like(v)
            masked_v[key] = (jnp.concatenate([jnp.where(first_head, v, zv), ones_a], axis=1),
                             jnp.concatenate([jnp.where(first_head, zv, v), ones_b], axis=1))
        return masked_v[key]

    def finish(s, vblocks):
        m = jnp.max(s, axis=-1, keepdims=True)
        p = jnp.exp2(s - m).astype(BF16)
        pcat = jnp.concatenate([p[:BLK], p[BLK:]], axis=1)
        rhs = jnp.concatenate([vb[0] for vb in vblocks] + [vb[1] for vb in vblocks], axis=0)
        pv = jnp.dot(pcat, rhs, preferred_element_type=F32)
        mm = jnp.where(first_head, m[:BLK], m[BLK:])
        return pv[:, :W], pv[:, W:], mm

    def rows_of(ref, pieces):
        return jnp.concatenate([ref[ph, lo:lo + n, :] for ph, lo, n in pieces], axis=0)

    def keys_of(ref, prev, cur):
        return rows_of(ref, cur) if prev is None else rows_of(ref, prev + cur)

    def store(g, pieces, vals):
        at = 0
        for ph, lo, n in pieces:
            for ref, val in zip((acc_ref, l_ref, m_ref), vals):
                ref[g, ph, lo:lo + n, :] = val[at:at + n]
            at += n

    M = q0_ref.shape[1]
    g2_blocks = [[(r, 0, BLK)] for r in range(N_PHASE)]
    g1_blocks = {(r4, n): [(4 * q4 + r4, 32 * n, 32) for q4 in range(4)]
                 for r4 in range(4) for n in range(M // 32)}
    g0_blocks = [[(r, 8 * n, 8) for r in range(N_PHASE)] for n in range(M // 8)]

    for r in range(N_PHASE):
        qf_ref[r] = q0_ref[r].astype(F32)
        kf_ref[r] = k0_ref[r].astype(F32)
        vf_ref[r] = v0_ref[r].astype(F32)

    work = []
    for n, pieces in enumerate(g0_blocks):
        work.append((0, pieces, g0_blocks[n - 1] if n else None, (qf_ref, kf_ref, vf_ref), True))
    for (r4, n), pieces in g1_blocks.items():
        work.append((1, pieces, g1_blocks[(r4, n - 1)] if n else None, (q1_ref, k1_ref, v1_ref), False))
    for pieces in g2_blocks:
        work.append((2, pieces, None, (q2_ref, k2_ref, v2_ref), False))

    def merge(r, acc2, l2, m2):
        m0, m1 = m_ref[0, r], m_ref[1, r]
        mx = jnp.maximum(jnp.maximum(m0, m1), m2)
        w0, w1, w2 = jnp.exp2(m0 - mx), jnp.exp2(m1 - mx), jnp.exp2(m2 - mx)
        num = w0 * acc_ref[0, r] + w1 * acc_ref[1, r] + w2 * acc2
        den = w0 * l_ref[0, r] + w1 * l_ref[1, r] + w2 * l2
        z = z_ref[r].astype(F32)
        o_ref[r] = (num * z / (den * (1.0 + jnp.exp2(z * (-LOG2E))))).astype(BF16)

    pending = []
    for item in work + [None] * AHEAD:
        if item is not None:
            g, pieces, prev, (q_r, k_r, _), cast = item
            q, k = rows_of(q_r, pieces), keys_of(k_r, prev, pieces)
            if cast:
                q, k = q.astype(BF16), k.astype(BF16)
            pending.append((item, logits(q, k, g, prev is None)))
        if item is None or len(pending) > AHEAD:
            (g_p, pieces_p, prev_p, (_, _, v_r), cast_p), s_p = pending.pop(0)
            key_blocks = ([] if prev_p is None else [prev_p]) + [pieces_p]
            vals = finish(s_p, [value_blocks(g_p, v_r, kb, cast_p) for kb in key_blocks])
            if g_p == 2:
                merge(pieces_p[0][0], *vals)
            else:
                store(g_p, pieces_p, vals)


def _attention(P, bm):
    B, _, M, _ = P.shape
    HP = HEADS // 2
    W = 2 * HEAD_DIM

    def spec(kind, g):
        base = (kind * 3 + g) * HP
        return pl.BlockSpec((None, N_PHASE, M, W), lambda b, hp, base=base: (b, 0, 0, base + hp))

    in_specs = [spec(kind, g) for kind in range(3) for g in range(3)]
    in_specs.append(pl.BlockSpec((None, N_PHASE, M, W), lambda b, hp: (b, 0, 0, 9 * HP + hp)))
    in_specs.append(pl.BlockSpec((3, None, 2 * BLK, 2 * BLK), lambda b, hp: (0, hp, 0, 0)))
    return pl.pallas_call(
        _attn_kernel,
        grid=(B, HP),
        in_specs=in_specs,
        out_specs=pl.BlockSpec((None, N_PHASE, M, W), lambda b, hp: (b, 0, 0, hp)),
        out_shape=jax.ShapeDtypeStruct((B, N_PHASE, M, HEADS * HEAD_DIM), BF16),
        scratch_shapes=[pltpu.VMEM((N_PHASE, M, W), F32) for _ in range(3)]
        + [pltpu.VMEM((2, N_PHASE, M, W), F32) for _ in range(3)],
        compiler_params=_cparams(("parallel", "parallel")),
        name="dilated_attention",
    )(*([P] * 10), bm)


def _t5_bucket(dist):
    max_exact = REL_BUCKETS // 2
    n = jnp.maximum(dist, 1).astype(F32)
    large = max_exact + (jnp.log(n / max_exact) / math.log(REL_MAX_DIST / max_exact)
                         * (REL_BUCKETS - max_exact)).astype(jnp.int32)
    large = jnp.minimum(large, REL_BUCKETS - 1)
    return jnp.where(dist < max_exact, dist, large)


def _bias_tables(rel_bias):
    a = np.arange(BLK)
    pos = (16 * (a % 8) + a // 8, 4 * (a % 32) + a // 32, a)
    back = np.stack([np.concatenate([BLK + p[:, None] - p[None, :], p[:, None] - p[None, :]], axis=1)
                     for p in pos])
    valid = (back >= 0) & (back <= BLK)
    dist = np.clip(back, 0, BLK) * np.asarray(DILATIONS)[:, None, None]
    bucket = jnp.where(jnp.asarray(valid), _t5_bucket(jnp.asarray(dist, jnp.int32)), REL_BUCKETS)
    onehot = (bucket[..., None] == jnp.arange(REL_BUCKETS + 1)).astype(F32)
    ext = jnp.concatenate([rel_bias.astype(F32), jnp.full((1, HEADS), NEG, F32)], axis=0)
    t = jnp.einsum("gijc,ch->ghij", onehot, ext * math.log2(math.e),
                   precision=lax.Precision.HIGHEST)
    return t.reshape(3, HEADS // 2, 2 * BLK, 2 * BLK)


def _out_proj_kernel(o_ref, w_ref, g_ref, x_ref, y_ref, *, mh):
    o = o_ref[...].reshape(N_PHASE * mh, o_ref.shape[-1])
    h = jnp.dot(o, w_ref[...].astype(BF16), preferred_element_type=F32)
    ms = jnp.mean(h * h, axis=-1, keepdims=True)
    y = h * lax.rsqrt(ms + EPS) * g_ref[...]
    y_ref[...] = x_ref[...] + y.reshape(y_ref.shape)


def _out_proj(O, w, g, xp, *, mh=64):
    B, _, M, D = xp.shape
    return pl.pallas_call(
        functools.partial(_out_proj_kernel, mh=mh),
        grid=(B, M // mh),
        in_specs=[
            pl.BlockSpec((None, N_PHASE, mh, O.shape[-1]), lambda b, m: (b, 0, m, 0)),
            pl.BlockSpec(w.shape, lambda b, m: (0, 0)),
            pl.BlockSpec((1, D), lambda b, m: (0, 0)),
            pl.BlockSpec((None, N_PHASE, mh, D), lambda b, m: (b, 0, m, 0)),
        ],
        out_specs=pl.BlockSpec((None, N_PHASE, mh, D), lambda b, m: (b, 0, m, 0)),
        out_shape=jax.ShapeDtypeStruct(xp.shape, F32),
        compiler_params=_cparams(("parallel", "parallel")),
        name="attn_out_proj",
    )(O, w, g.reshape(1, D), xp)


def _ssm_in_proj_kernel(x_ref, g_ref, w_ref, *rest):
    prm_in, o_ref, prm_out, h_ref = rest[:11], rest[11], rest[12:19], rest[19]
    M = x_ref.shape[1]

    @pl.when(pl.program_id(1) == 0)
    def _():
        g = g_ref[...]
        for s in range(N_PHASE):
            xs = x_ref[s]
            ms = jnp.mean(xs * xs, axis=-1, keepdims=True)
            h_ref[s * M:(s + 1) * M, :] = (xs * lax.rsqrt(ms + EPS) * g).astype(BF16)

    wt = w_ref[...].T.astype(BF16)
    res = lax.dot_general(wt, h_ref[...], (((1,), (1,)), ((), ())), preferred_element_type=F32)
    _ssm_param_kernel(*prm_in, *prm_out)
    for s in range(N_PHASE):
        o_ref[s] = res[:, s * M:(s + 1) * M].astype(BF16)


def _ssm_in_proj(xp, g, w, a_re, a_im, log_dt, b_re, b_im, c_re, c_im, d_skip, *, tn=512):
    B, _, M, D = xp.shape
    S = N_PHASE * M
    N = w.shape[1]
    G, P, C, Q = SSM_N_GROUPS, SSM_STATE, SSM_GROUP, N_PHASE
    QC = Q * C
    nj = N // tn
    assert B * nj == G // 2, "one pair of state groups is discretised per grid step"
    f = lambda t: t.astype(F32)
    a_re, a_im, log_dt, b_re, b_im, c_re, c_im = map(f, (a_re, a_im, log_dt, b_re, b_im, c_re, c_im))
    even = (jnp.arange(G) % 2 == 0)[:, None, None]

    def lane_half(c):
        return jnp.concatenate([jnp.where(even, c, 0.0), jnp.where(even, 0.0, c)], axis=-1)

    pair = lambda blk: pl.BlockSpec((2,) + blk, lambda b, j: (b * nj + j,) + (0,) * len(blk))
    one = lambda blk: pl.BlockSpec((None,) + blk, lambda b, j: (b * nj + j,) + (0,) * len(blk))
    outs = pl.pallas_call(
        _ssm_in_proj_kernel,
        grid=(B, nj),
        in_specs=[
            pl.BlockSpec((None, N_PHASE, M, D), lambda b, j: (b, 0, 0, 0)),
            pl.BlockSpec((1, D), lambda b, j: (0, 0)),
            pl.BlockSpec((D, tn), lambda b, j: (0, j)),
            pair((P, 1)), pair((P, 1)), one((1, 2 * P)), one((1, 2 * P)), one((1, 2 * P)),
            pair((P, C)), pair((P, C)), pair((C, P)), pair((C, P)), pair((C, 2 * P)), pair((C, 2 * P)),
        ],
        out_specs=[pl.BlockSpec((None, N_PHASE, tn, M), lambda b, j: (b, 0, j, 0)),
                   pair((QC, QC)), pair((P, QC)), pair((P, QC)), one((2 * QC, 2 * P)),
                   one((2 * QC, 2 * P)), one((1, 2 * P)), one((1, 2 * P))],
        out_shape=[jax.ShapeDtypeStruct((B, N_PHASE, N, M), BF16),
                   jax.ShapeDtypeStruct((G, QC, QC), BF16), jax.ShapeDtypeStruct((G, P, QC), BF16),
                   jax.ShapeDtypeStruct((G, P, QC), BF16),
                   jax.ShapeDtypeStruct((G // 2, 2 * QC, 2 * P), BF16),
                   jax.ShapeDtypeStruct((G // 2, 2 * QC, 2 * P), BF16),
                   jax.ShapeDtypeStruct((G // 2, 1, 2 * P), F32),
                   jax.ShapeDtypeStruct((G // 2, 1, 2 * P), F32)],
        scratch_shapes=[pltpu.VMEM((S, D), BF16)],
        compiler_params=_cparams(("parallel", "arbitrary")),
        name="ssm_in_proj",
    )(xp, g.reshape(1, D), w,
      a_re.reshape(G, P, 1), a_im.reshape(G, P, 1),
      a_re.reshape(G // 2, 1, 2 * P), a_im.reshape(G // 2, 1, 2 * P),
      jnp.repeat(log_dt, P).reshape(G // 2, 1, 2 * P), b_re, b_im, c_re, c_im,
      lane_half(c_re), lane_half(c_im))
    keys = ("m", "b_re", "b_im", "c_re", "c_im", "a_re", "a_im")
    prm = dict(zip(keys, outs[1:]))
    prm["d"] = d_skip.astype(F32).reshape(G * C, 1)
    return outs[0], prm


def _ssm_core_kernel(u_ref, m_ref, bre_ref, bim_ref, cre_ref, cim_ref, are_ref, aim_ref,
                     d_ref, y_ref, sre_ref, sim_ref, ym_ref, *, n_chunk, n_pair):
    C = SSM_GROUP
    QC = N_PHASE * C
    nb = u_ref.shape[0]
    N = nb * n_chunk
    n_grp = 2 * n_pair

    def grp(i, b):
        return u_ref[b, :, i * C:(i + 1) * C, :]

    us = [jnp.concatenate([grp(i, b).reshape(QC, n_chunk) for b in range(nb)], axis=1)
          for i in range(n_grp)]

    for k in range(n_pair):
        sre_ref[k] = jnp.concatenate([jnp.dot(bre_ref[i], us[i], preferred_element_type=F32)
                                      for i in (2 * k, 2 * k + 1)], axis=0).T
        sim_ref[k] = jnp.concatenate([jnp.dot(bim_ref[i], us[i], preferred_element_type=F32)
                                      for i in (2 * k, 2 * k + 1)], axis=0).T

    for i in range(n_grp):
        ym_ref[i] = jnp.dot(m_ref[i], us[i], preferred_element_type=F32)

    ar = [jnp.broadcast_to(are_ref[k], (nb, are_ref.shape[-1])) for k in range(n_pair)]
    ai = [jnp.broadcast_to(aim_ref[k], (nb, aim_ref.shape[-1])) for k in range(n_pair)]
    sr = [jnp.zeros_like(ar[0])] * n_pair
    si = [jnp.zeros_like(ar[0])] * n_pair
    for c in range(n_chunk):
        rows = pl.ds(c, nb, stride=n_chunk)
        for k in range(n_pair):
            xr, xi = sre_ref[k, rows, :], sim_ref[k, rows, :]
            sre_ref[k, rows, :] = sr[k]
            sim_ref[k, rows, :] = si[k]
            sr[k], si[k] = (ar[k] * sr[k] - ai[k] * si[k] + xr, ar[k] * si[k] + ai[k] * sr[k] + xi)

    nt = (((1,), (1,)), ((), ()))
    for k in range(n_pair):
        inter = (lax.dot_general(cre_ref[k], sre_ref[k].astype(BF16), nt, preferred_element_type=F32)
                 + lax.dot_general(cim_ref[k], sim_ref[k].astype(BF16), nt,
                                   preferred_element_type=F32))
        for i in (2 * k, 2 * k + 1):
            y = ym_ref[i] + inter[(i % 2) * QC:(i % 2 + 1) * QC]
            for b in range(nb):
                yb = y[:, b * n_chunk:(b + 1) * n_chunk].reshape(N_PHASE, C, n_chunk)
                yb = yb + d_ref[i * C:(i + 1) * C] * grp(i, b).astype(F32)
                y_ref[b, :, i * C:(i + 1) * C, :] = yb.astype(BF16)


def _ssm_core(UT, prm, *, n_pair=2):
    B, _, _, n_chunk = UT.shape
    N = B * n_chunk
    G2 = SSM_N_GROUPS // 2
    C2 = 2 * SSM_GROUP
    QC = N_PHASE * SSM_GROUP
    P = SSM_STATE
    blk = lambda *shape: pl.BlockSpec(shape, lambda g: (g,) + (0,) * (len(shape) - 1))
    return pl.pallas_call(
        functools.partial(_ssm_core_kernel, n_chunk=n_chunk, n_pair=n_pair),
        grid=(G2 // n_pair,),
        in_specs=[
            pl.BlockSpec((B, N_PHASE, n_pair * C2, n_chunk), lambda g: (0, 0, g, 0)),
            blk(2 * n_pair, QC, QC), blk(2 * n_pair, P, QC), blk(2 * n_pair, P, QC),
            blk(n_pair, 2 * QC, 2 * P), blk(n_pair, 2 * QC, 2 * P),
            blk(n_pair, 1, 2 * P), blk(n_pair, 1, 2 * P),
            blk(n_pair * C2, 1),
        ],
        out_specs=pl.BlockSpec((B, N_PHASE, n_pair * C2, n_chunk), lambda g: (0, 0, g, 0)),
        out_shape=jax.ShapeDtypeStruct((B, N_PHASE, SSM_N_GROUPS * SSM_GROUP, n_chunk), BF16),
        scratch_shapes=[pltpu.VMEM((n_pair, N, 2 * P), F32), pltpu.VMEM((n_pair, N, 2 * P), F32),
                        pltpu.VMEM((2 * n_pair, QC, N), F32)],
        compiler_params=_cparams(("parallel",)),
        name="ssm_core",
    )(UT, prm["m"], prm["b_re"], prm["b_im"], prm["c_re"], prm["c_im"],
      prm["a_re"], prm["a_im"], prm["d"])


def _ssm_param_kernel(arc_ref, aic_ref, arr_ref, air_ref, ldr_ref, bre_ref, bim_ref,
                      cre_ref, cim_ref, cpr_ref, cpi_ref,
                      m_ref, br_ref, bi_ref, cr_ref, ci_ref, ar_ref, ai_ref):
    Q, C, P = N_PHASE, SSM_GROUP, SSM_STATE
    QC = Q * C
    hi = lax.Precision.HIGHEST
    tile = (lax.broadcasted_iota(jnp.int32, (C, QC), 1) % C
            == lax.broadcasted_iota(jnp.int32, (C, QC), 0)).astype(F32)
    lane_s = lax.broadcasted_iota(jnp.int32, (P, QC), 1) // C

    dt = jnp.exp(ldr_ref[...])
    mag = jnp.exp(arr_ref[...] * dt)
    a1r, a1i = mag * jnp.cos(air_ref[...] * dt), mag * jnp.sin(air_ref[...] * dt)
    a1r_c = jnp.broadcast_to(a1r, (8, 2 * P)).T[:, :1]
    a1i_c = jnp.broadcast_to(a1i, (8, 2 * P)).T[:, :1]

    for i in range(2):
        a_re, a_im = arc_ref[i], aic_ref[i]
        abr, abi = a1r_c[i * P:(i + 1) * P], a1i_c[i * P:(i + 1) * P]
        den = a_re * a_re + a_im * a_im
        cfr = ((abr - 1.0) * a_re + abi * a_im) / den
        cfi = (abi * a_re - (abr - 1.0) * a_im) / den
        bbr = cfr * bre_ref[i] - cfi * bim_ref[i]
        bbi = cfr * bim_ref[i] + cfi * bre_ref[i]

        pr, pi = jnp.ones_like(abr), jnp.zeros_like(abr)
        pw_r = jnp.zeros((P, QC), F32)
        pw_i = jnp.zeros((P, QC), F32)
        for n in range(Q):
            sel = lane_s == Q - 1 - n
            pw_r = jnp.where(sel, pr, pw_r)
            pw_i = jnp.where(sel, pi, pw_i)
            pr, pi = pr * abr - pi * abi, pr * abi + pi * abr
        bbr_t = jnp.dot(bbr, tile, precision=hi, preferred_element_type=F32)
        bbi_t = jnp.dot(bbi, tile, precision=hi, preferred_element_type=F32)
        bst_r = pw_r * bbr_t - pw_i * bbi_t
        bst_i = pw_r * bbi_t + pw_i * bbr_t
        br_ref[i] = bst_r.astype(BF16)
        bi_ref[i] = bst_i.astype(BF16)

        kl = (jnp.dot(cre_ref[i], bst_r, precision=hi, preferred_element_type=F32)
              - jnp.dot(cim_ref[i], bst_i, precision=hi, preferred_element_type=F32))
        klz = jnp.concatenate([kl, jnp.zeros_like(kl)], axis=1)
        for t in range(Q):
            sh = (2 * QC - C * (Q - 1 - t)) % (2 * QC)
            row = klz if sh == 0 else pltpu.roll(klz, sh, axis=1)
            m_ref[i, t * C:(t + 1) * C, :] = row[:, :QC].astype(BF16)

    qr, qi = a1r, a1i
    for t in range(Q):
        for i in range(2):
            rows = slice(i * QC + t * C, i * QC + (t + 1) * C)
            cr_ref[rows, :] = (cpr_ref[i] * qr - cpi_ref[i] * qi).astype(BF16)
            ci_ref[rows, :] = (-(cpr_ref[i] * qi + cpi_ref[i] * qr)).astype(BF16)
        if t < Q - 1:
            qr, qi = qr * a1r - qi * a1i, qr * a1i + qi * a1r
    ar_ref[...] = qr
    ai_ref[...] = qi


def _ssm_out_kernel(y_ref, z_ref, wg_ref, bg_ref, wo_ref, gp_ref, x_ref, o_hbm, obuf, sem,
                    wgt_ref, wot_ref, *, nb, n_sub):
    s, j = pl.program_id(0), pl.program_id(1)
    nj = pl.num_programs(1)
    step = s * nj + j
    last = pl.num_programs(0) * nj - 1
    slot = step % 2

    def out_copy(slot_, s_, j_):
        return pltpu.make_async_copy(obuf.at[slot_], o_hbm.at[pl.ds(j_ * nb, nb), :, s_, :],
                                     sem.at[slot_])

    @pl.when(step >= 2)
    def _():
        out_copy(slot, (step - 2) // nj, (step - 2) % nj).wait()

    @pl.when(step == 0)
    def _():
        wgt_ref[...] = wg_ref[...].T.astype(BF16)
        wot_ref[...] = wo_ref[...].T.astype(BF16)

    bs = nb // n_sub

    def lanes(ref, k):
        return jnp.concatenate([ref[b] for b in range(k * bs, (k + 1) * bs)], axis=1).astype(F32)

    def glu_in(k):
        g = _gelu_tanh(lanes(y_ref, k))
        return g, jnp.dot(wgt_ref[...], g.astype(BF16), preferred_element_type=F32)

    def proj(k, g, gl):
        y2 = g * _sigmoid(gl + bg_ref[...])
        z = lanes(z_ref, k)
        gated = (y2 * (z * _sigmoid(z))).astype(BF16)
        return jnp.dot(wot_ref[...], gated, preferred_element_type=F32)

    def finish(k, ot):
        ms = jnp.mean(ot * ot, axis=0, keepdims=True)
        nt = (ot * lax.rsqrt(ms + EPS)).T
        rows = slice(k * bs, (k + 1) * bs)
        obuf[slot, rows] = x_ref[rows] + (nt * gp_ref[...]).reshape(bs, -1, nt.shape[-1])

    a = [glu_in(k) for k in range(n_sub)]
    o = [proj(k, *a[k]) for k in range(n_sub)]
    for k in range(n_sub):
        finish(k, o[k])
    out_copy(slot, s, j).start()

    @pl.when(step == last)
    def _():
        out_copy(slot, s, j).wait()

    @pl.when(jnp.logical_and(step == last, last >= 1))
    def _():
        out_copy(1 - slot, (step - 1) // nj, (step - 1) % nj).wait()


def _ssm_out(YT, UT, wg, bg, wo, gp, xp, *, nb=8, n_sub=2):
    B, _, M, D = xp.shape
    E = YT.shape[2]
    out = pl.pallas_call(
        functools.partial(_ssm_out_kernel, nb=nb, n_sub=n_sub),
        grid=(N_PHASE, B // nb),
        in_specs=[
            pl.BlockSpec((nb, None, E, M), lambda s, j: (j, s, 0, 0)),
            pl.BlockSpec((nb, None, E, M), lambda s, j: (j, s, 1, 0)),
            pl.BlockSpec((E, E), lambda s, j: (0, 0)),
            pl.BlockSpec((E, 1), lambda s, j: (0, 0)),
            pl.BlockSpec((E, D), lambda s, j: (0, 0)),
            pl.BlockSpec((1, D), lambda s, j: (0, 0)),
            pl.BlockSpec((nb, None, M, D), lambda s, j: (j, s, 0, 0)),
        ],
        out_specs=pl.BlockSpec(memory_space=pl.ANY),
        out_shape=jax.ShapeDtypeStruct((B, M, N_PHASE, D), F32),
        scratch_shapes=[pltpu.VMEM((2, nb, M, D), F32), pltpu.SemaphoreType.DMA((2,)),
                        pltpu.VMEM((E, E), BF16), pltpu.VMEM((D, E), BF16)],
        compiler_params=_cparams(("arbitrary", "arbitrary")),
        name="ssm_out",
    )(YT, UT, wg, bg.reshape(E, 1), wo, gp.reshape(1, D), xp)
    return out.reshape(B, M * N_PHASE, D)


def kernel(x, rel_bias, attn_pre_norm, attn_w_in, attn_w_out, attn_post_norm, ssm_pre_norm, ssm_w_in, ssm_a_re, ssm_a_im, ssm_log_dt, ssm_b_re, ssm_b_im, ssm_c_re, ssm_c_im, ssm_d, ssm_w_glu, ssm_b_glu, ssm_w_out, ssm_post_norm):
    B, S, D = x.shape
    n_chunk = S // N_PHASE

    P, xp = _norm_proj(x, attn_pre_norm[0], attn_w_in[0], scaled_cols=3 * HEADS * HEAD_DIM,
                       scale=HEAD_DIM ** -0.5 * LOG2E)
    O = _attention(P, _bias_tables(rel_bias))
    xp = _out_proj(O, attn_w_out[0], attn_post_norm[0], xp)

    UT, prm = _ssm_in_proj(xp, ssm_pre_norm[0], ssm_w_in[0], ssm_a_re[0], ssm_a_im[0], ssm_log_dt[0],
                           ssm_b_re[0], ssm_b_im[0], ssm_c_re[0], ssm_c_im[0], ssm_d[0])
    YT = _ssm_core(UT, prm)
    return _ssm_out(YT, UT, ssm_w_glu[0], ssm_b_glu[0], ssm_w_out[0], ssm_post_norm[0], xp)
```

```python
import functools
import math

import numpy as np
import jax
import jax.numpy as jnp
from jax import lax
from jax.experimental import pallas as pl
from jax.experimental.pallas import tpu as pltpu

F32 = jnp.float32
BF16 = jnp.bfloat16

D_MODEL = 1024
HEAD_DIM = 64
HEADS = 16
N_PHASE = 16
BLK = 128
DILATIONS = (1, 4, 16)
REL_BUCKETS = 32
REL_MAX_DIST = 2048
SSM_GROUP = 16
SSM_N_GROUPS = 64
SSM_STATE = 64
EPS = 1e-6
NEG = -1e30
VMEM_LIMIT = 56 * 1024 * 1024


def _cparams(sem):
    return pltpu.CompilerParams(dimension_semantics=sem, vmem_limit_bytes=VMEM_LIMIT)


LOG2E = math.log2(math.e)


def _sigmoid(v):
    return 1.0 / (1.0 + jnp.exp2(v * (-LOG2E)))


def _gelu_tanh(x):
    k0 = -2.0 * math.sqrt(2.0 / math.pi) * LOG2E
    return x / (1.0 + jnp.exp2(x * (k0 + (k0 * 0.044715) * (x * x))))


def _norm_proj_kernel(x_hbm, g_ref, w_ref, o_ref, xp_hbm, xbuf, h_ref, gsem, osem, *,
                      scaled_tiles, scale):
    b, j = pl.program_id(0), pl.program_id(1)
    n_b, n_j = pl.num_programs(0), pl.num_programs(1)
    slot = b % 2

    def phase_copy(bb, sl, r):
        return pltpu.make_async_copy(x_hbm.at[bb, :, r, :], xbuf.at[sl, r], gsem.at[sl])

    def gather_start(bb, sl):
        lax.fori_loop(0, N_PHASE, lambda r, c: (phase_copy(bb, sl, r).start(), c)[1], 0)

    def gather_wait(bb, sl):
        lax.fori_loop(0, N_PHASE, lambda r, c: (phase_copy(bb, sl, r).wait(), c)[1], 0)

    def write_out(bb, sl):
        return pltpu.make_async_copy(xbuf.at[sl], xp_hbm.at[bb], osem.at[sl])

    @pl.when(j == 0)
    def _():
        @pl.when(b == 0)
        def _():
            gather_start(0, 0)

        gather_wait(b, slot)

        @pl.when(b >= 1)
        def _():
            write_out(b - 1, 1 - slot).wait()

        @pl.when(b + 1 < n_b)
        def _():
            gather_start(b + 1, 1 - slot)

        write_out(b, slot).start()
        x = xbuf[slot].reshape(h_ref.shape)
        ms = jnp.mean(x * x, axis=-1, keepdims=True)
        h_ref[...] = (x * lax.rsqrt(ms + EPS) * g_ref[...]).astype(BF16)

    w = (w_ref[...] * jnp.where(j < scaled_tiles, scale, 1.0).astype(F32)).astype(BF16)
    res = jnp.dot(h_ref[...], w, preferred_element_type=F32)
    o_ref[...] = res.reshape(o_ref.shape).astype(BF16)

    @pl.when(jnp.logical_and(b == n_b - 1, j == n_j - 1))
    def _():
        write_out(b, slot).wait()


def _norm_proj(x, g, w, *, scaled_cols, scale, tn=1024):
    B, S, D = x.shape
    M = S // N_PHASE
    N = w.shape[1]
    assert scaled_cols % tn == 0
    return pl.pallas_call(
        functools.partial(_norm_proj_kernel, scaled_tiles=scaled_cols // tn, scale=scale),
        grid=(B, N // tn),
        in_specs=[
            pl.BlockSpec(memory_space=pl.ANY),
            pl.BlockSpec((1, D), lambda b, j: (0, 0)),
            pl.BlockSpec((D, tn), lambda b, j: (0, j)),
        ],
        out_specs=[pl.BlockSpec((None, N_PHASE, M, tn), lambda b, j: (b, 0, 0, j)),
                   pl.BlockSpec(memory_space=pl.ANY)],
        out_shape=[jax.ShapeDtypeStruct((B, N_PHASE, M, N), BF16),
                   jax.ShapeDtypeStruct((B, N_PHASE, M, D), F32)],
        scratch_shapes=[pltpu.VMEM((2, N_PHASE, M, D), F32), pltpu.VMEM((S, D), BF16),
                        pltpu.SemaphoreType.DMA((2,)), pltpu.SemaphoreType.DMA((2,))],
        compiler_params=_cparams(("arbitrary", "arbitrary")),
        name="attn_norm_proj",
    )(x.reshape(B, M, N_PHASE, D), g.reshape(1, D), w)


def _attn_kernel(q0_ref, q1_ref, q2_ref, k0_ref, k1_ref, k2_ref, v0_ref, v1_ref, v2_ref,
                 z_ref, bm_ref, o_ref, qf_ref, kf_ref, vf_ref, acc_ref, l_ref, m_ref):
    W = 2 * HEAD_DIM
    AHEAD = 3
    lane = lax.broadcasted_iota(jnp.int32, (BLK, W), 1)
    first_head = lane < HEAD_DIM

    def logits(q, k, g, cur_only):
        zq = jnp.zeros_like(q)
        qs = jnp.concatenate([jnp.where(first_head, q, zq), jnp.where(first_head, zq, q)], axis=0)
        s = lax.dot_general(qs, k, (((1,), (1,)), ((), ())), preferred_element_type=F32)
        return s + (bm_ref[g, :, BLK:2 * BLK] if cur_only else bm_ref[g])

    ones_a = jnp.where(first_head, 1.0, 0.0).astype(BF16)
    ones_b = jnp.where(first_head, 0.0, 1.0).astype(BF16)
    masked_v = {}

    def value_blocks(g, v_ref, pieces, cast):
        key = (g, tuple(pieces))
        if key not in masked_v:
            v = rows_of(v_ref, pieces)
            v = v.astype(BF16) if cast else v
            zv = jnp.zeros_like(v)
            masked_v[key] = (jnp.concatenate([jnp.where(first_head, v, zv), ones_a], axis=1),
                             jnp.concatenate([jnp.where(first_head, zv, v), ones_b], axis=1))
        return masked_v[key]

    def finish(s, vblocks):
        m = jnp.max(s, axis=-1, keepdims=True)
        p = jnp.exp2(s - m).astype(BF16)
        pcat = jnp.concatenate([p[:BLK], p[BLK:]], axis=1)
        rhs = jnp.concatenate([vb[0] for vb in vblocks] + [vb[1] for vb in vblocks], axis=0)
        pv = jnp.dot(pcat, rhs, preferred_element_type=F32)
        mm = jnp.where(first_head, m[:BLK], m[BLK:])
        return pv[:, :W], pv[:, W:], mm

    def rows_of(ref, pieces):
        return jnp.concatenate([ref[ph, lo:lo + n, :] for ph, lo, n in pieces], axis=0)

    def keys_of(ref, prev, cur):
        return rows_of(ref, cur) if prev is None else rows_of(ref, prev + cur)

    def store(g, pieces, vals):
        at = 0
        for ph, lo, n in pieces:
            for ref, val in zip((acc_ref, l_ref, m_ref), vals):
                ref[g, ph, lo:lo + n, :] = val[at:at + n]
            at += n

    M = q0_ref.shape[1]
    g2_blocks = [[(r, 0, BLK)] for r in range(N_PHASE)]
    g1_blocks = {(r4, n): [(4 * q4 + r4, 32 * n, 32) for q4 in range(4)]
                 for r4 in range(4) for n in range(M // 32)}
    g0_blocks = [[(r, 8 * n, 8) for r in range(N_PHASE)] for n in range(M // 8)]

    for r in range(N_PHASE):
        qf_ref[r] = q0_ref[r].astype(F32)
        kf_ref[r] = k0_ref[r].astype(F32)
        vf_ref[r] = v0_ref[r].astype(F32)

    work = []
    for n, pieces in enumerate(g0_blocks):
        work.append((0, pieces, g0_blocks[n - 1] if n else None, (qf_ref, kf_ref, vf_ref), True))
    for (r4, n), pieces in g1_blocks.items():
        work.append((1, pieces, g1_blocks[(r4, n - 1)] if n else None, (q1_ref, k1_ref, v1_ref), False))
    for pieces in g2_blocks:
        work.append((2, pieces, None, (q2_ref, k2_ref, v2_ref), False))

    def merge(r, acc2, l2, m2):
        m0, m1 = m_ref[0, r], m_ref[1, r]
        mx = jnp.maximum(jnp.maximum(m0, m1), m2)
        w0, w1, w2 = jnp.exp2(m0 - mx), jnp.exp2(m1 - mx), jnp.exp2(m2 - mx)
        num = w0 * acc_ref[0, r] + w1 * acc_ref[1, r] + w2 * acc2
        den = w0 * l_ref[0, r] + w1 * l_ref[1, r] + w2 * l2
        z = z_ref[r].astype(F32)
        o_ref[r] = (num * z / (den * (1.0 + jnp.exp2(z * (-LOG2E))))).astype(BF16)

    pending = []
    for item in work + [None] * AHEAD:
        if item is not None:
            g, pieces, prev, (q_r, k_r, _), cast = item
            q, k = rows_of(q_r, pieces), keys_of(k_r, prev, pieces)
            if cast:
                q, k = q.astype(BF16), k.astype(BF16)
            pending.append((item, logits(q, k, g, prev is None)))
        if item is None or len(pending) > AHEAD:
            (g_p, pieces_p, prev_p, (_, _, v_r), cast_p), s_p = pending.pop(0)
            key_blocks = ([] if prev_p is None else [prev_p]) + [pieces_p]
            vals = finish(s_p, [value_blocks(g_p, v_r, kb, cast_p) for kb in key_blocks])
            if g_p == 2:
                merge(pieces_p[0][0], *vals)
            else:
                store(g_p, pieces_p, vals)


def _attention(P, bm):
    B, _, M, _ = P.shape
    HP = HEADS // 2
    W = 2 * HEAD_DIM

    def spec(kind, g):
        base = (kind * 3 + g) * HP
        return pl.BlockSpec((None, N_PHASE, M, W), lambda b, hp, base=base: (b, 0, 0, base + hp))

    in_specs = [spec(kind, g) for kind in range(3) for g in range(3)]
    in_specs.append(pl.BlockSpec((None, N_PHASE, M, W), lambda b, hp: (b, 0, 0, 9 * HP + hp)))
    in_specs.append(pl.BlockSpec((3, None, 2 * BLK, 2 * BLK), lambda b, hp: (0, hp, 0, 0)))
    return pl.pallas_call(
        _attn_kernel,
        grid=(B, HP),
        in_specs=in_specs,
        out_specs=pl.BlockSpec((None, N_PHASE, M, W), lambda b, hp: (b, 0, 0, hp)),
        out_shape=jax.ShapeDtypeStruct((B, N_PHASE, M, HEADS * HEAD_DIM), BF16),
        scratch_shapes=[pltpu.VMEM((N_PHASE, M, W), F32) for _ in range(3)]
        + [pltpu.VMEM((2, N_PHASE, M, W), F32) for _ in range(3)],
        compiler_params=_cparams(("parallel", "parallel")),
        name="dilated_attention",
    )(*([P] * 10), bm)


def _t5_bucket(dist):
    max_exact = REL_BUCKETS // 2
    n = jnp.maximum(dist, 1).astype(F32)
    large = max_exact + (jnp.log(n / max_exact) / math.log(REL_MAX_DIST / max_exact)
                         * (REL_BUCKETS - max_exact)).astype(jnp.int32)
    large = jnp.minimum(large, REL_BUCKETS - 1)
    return jnp.where(dist < max_exact, dist, large)


def _bias_tables(rel_bias):
    a = np.arange(BLK)
    pos = (16 * (a % 8) + a // 8, 4 * (a % 32) + a // 32, a)
    back = np.stack([np.concatenate([BLK + p[:, None] - p[None, :], p[:, None] - p[None, :]], axis=1)
                     for p in pos])
    valid = (back >= 0) & (back <= BLK)
    dist = np.clip(back, 0, BLK) * np.asarray(DILATIONS)[:, None, None]
    bucket = jnp.where(jnp.asarray(valid), _t5_bucket(jnp.asarray(dist, jnp.int32)), REL_BUCKETS)
    onehot = (bucket[..., None] == jnp.arange(REL_BUCKETS + 1)).astype(F32)
    ext = jnp.concatenate([rel_bias.astype(F32), jnp.full((1, HEADS), NEG, F32)], axis=0)
    t = jnp.einsum("gijc,ch->ghij", onehot, ext * math.log2(math.e),
                   precision=lax.Precision.HIGHEST)
    return t.reshape(3, HEADS // 2, 2 * BLK, 2 * BLK)


def _out_proj_kernel(o_ref, w_ref, g_ref, x_ref, y_ref, *, mh):
    o = o_ref[...].reshape(N_PHASE * mh, o_ref.shape[-1])
    h = jnp.dot(o, w_ref[...].astype(BF16), preferred_element_type=F32)
    ms = jnp.mean(h * h, axis=-1, keepdims=True)
    y = h * lax.rsqrt(ms + EPS) * g_ref[...]
    y_ref[...] = x_ref[...] + y.reshape(y_ref.shape)


def _out_proj(O, w, g, xp, *, mh=64):
    B, _, M, D = xp.shape
    return pl.pallas_call(
        functools.partial(_out_proj_kernel, mh=mh),
        grid=(B, M // mh),
        in_specs=[
            pl.BlockSpec((None, N_PHASE, mh, O.shape[-1]), lambda b, m: (b, 0, m, 0)),
            pl.BlockSpec(w.shape, lambda b, m: (0, 0)),
            pl.BlockSpec((1, D), lambda b, m: (0, 0)),
            pl.BlockSpec((None, N_PHASE, mh, D), lambda b, m: (b, 0, m, 0)),
        ],
        out_specs=pl.BlockSpec((None, N_PHASE, mh, D), lambda b, m: (b, 0, m, 0)),
        out_shape=jax.ShapeDtypeStruct(xp.shape, F32),
        compiler_params=_cparams(("parallel", "parallel")),
        name="attn_out_proj",
    )(O, w, g.reshape(1, D), xp)


def _ssm_in_proj_kernel(x_ref, g_ref, w_ref, *rest):
    prm_in, o_ref, prm_out, h_ref = rest[:11], rest[11], rest[12:19], rest[19]
    M = x_ref.shape[1]
    n_pair = prm_out[-1].shape[0]

    @pl.when(pl.program_id(1) == 0)
    def _():
        g = g_ref[...]
        for s in range(N_PHASE):
            xs = x_ref[s]
            ms = jnp.mean(xs * xs, axis=-1, keepdims=True)
            h_ref[s * M:(s + 1) * M, :] = (xs * lax.rsqrt(ms + EPS) * g).astype(BF16)

    wt = w_ref[...].T.astype(BF16)
    res = lax.dot_general(wt, h_ref[...], (((1,), (1,)), ((), ())), preferred_element_type=F32)
    for k in range(n_pair):
        _ssm_param_kernel(*[r.at[2 * k:2 * k + 2] if r.shape[0] == 2 * n_pair else r.at[k]
                            for r in prm_in + prm_out])
    for s in range(N_PHASE):
        o_ref[s] = res[:, s * M:(s + 1) * M].astype(BF16)


def _ssm_in_proj(xp, g, w, a_re, a_im, log_dt, b_re, b_im, c_re, c_im, d_skip, *, tn=1024):
    B, _, M, D = xp.shape
    S = N_PHASE * M
    N = w.shape[1]
    G, P, C, Q = SSM_N_GROUPS, SSM_STATE, SSM_GROUP, N_PHASE
    QC = Q * C
    nj = N // tn
    n_pair = (G // 2) // (B * nj)
    assert B * nj * n_pair == G // 2
    f = lambda t: t.astype(F32)
    a_re, a_im, log_dt, b_re, b_im, c_re, c_im = map(f, (a_re, a_im, log_dt, b_re, b_im, c_re, c_im))
    even = (jnp.arange(G) % 2 == 0)[:, None, None]

    def lane_half(c):
        return jnp.concatenate([jnp.where(even, c, 0.0), jnp.where(even, 0.0, c)], axis=-1)

    pair = lambda blk: pl.BlockSpec((2 * n_pair,) + blk, lambda b, j: (b * nj + j,) + (0,) * len(blk))
    one = lambda blk: pl.BlockSpec((n_pair,) + blk, lambda b, j: (b * nj + j,) + (0,) * len(blk))
    outs = pl.pallas_call(
        _ssm_in_proj_kernel,
        grid=(B, nj),
        in_specs=[
            pl.BlockSpec((None, N_PHASE, M, D), lambda b, j: (b, 0, 0, 0)),
            pl.BlockSpec((1, D), lambda b, j: (0, 0)),
            pl.BlockSpec((D, tn), lambda b, j: (0, j)),
            pair((P, 1)), pair((P, 1)), one((1, 2 * P)), one((1, 2 * P)), one((1, 2 * P)),
            pair((P, C)), pair((P, C)), pair((C, P)), pair((C, P)), pair((C, 2 * P)), pair((C, 2 * P)),
        ],
        out_specs=[pl.BlockSpec((None, N_PHASE, tn, M), lambda b, j: (b, 0, j, 0)),
                   pair((QC, QC)), pair((P, QC)), pair((P, QC)), one((2 * QC, 2 * P)),
                   one((2 * QC, 2 * P)), one((1, 2 * P)), one((1, 2 * P))],
        out_shape=[jax.ShapeDtypeStruct((B, N_PHASE, N, M), BF16),
                   jax.ShapeDtypeStruct((G, QC, QC), BF16), jax.ShapeDtypeStruct((G, P, QC), BF16),
                   jax.ShapeDtypeStruct((G, P, QC), BF16),
                   jax.ShapeDtypeStruct((G // 2, 2 * QC, 2 * P), BF16),
                   jax.ShapeDtypeStruct((G // 2, 2 * QC, 2 * P), BF16),
                   jax.ShapeDtypeStruct((G // 2, 1, 2 * P), F32),
                   jax.ShapeDtypeStruct((G // 2, 1, 2 * P), F32)],
        scratch_shapes=[pltpu.VMEM((S, D), BF16)],
        compiler_params=_cparams(("parallel", "arbitrary")),
        name="ssm_in_proj",
    )(xp, g.reshape(1, D), w,
      a_re.reshape(G, P, 1), a_im.reshape(G, P, 1),
      a_re.reshape(G // 2, 1, 2 * P), a_im.reshape(G // 2, 1, 2 * P),
      jnp.repeat(log_dt, P).reshape(G // 2, 1, 2 * P), b_re, b_im, c_re, c_im,
      lane_half(c_re), lane_half(c_im))
    keys = ("m", "b_re", "b_im", "c_re", "c_im", "a_re", "a_im")
    prm = dict(zip(keys, outs[1:]))
    prm["d"] = d_skip.astype(F32).reshape(G * C, 1)
    return outs[0], prm


def _ssm_core_kernel(u_ref, m_ref, bre_ref, bim_ref, cre_ref, cim_ref, are_ref, aim_ref,
                     d_ref, y_ref, sre_ref, sim_ref, ym_ref, *, n_chunk, n_pair):
    C = SSM_GROUP
    QC = N_PHASE * C
    nb = u_ref.shape[0]
    N = nb * n_chunk
    n_grp = 2 * n_pair

    def grp(i, b):
        return u_ref[b, :, i * C:(i + 1) * C, :]

    us = [jnp.concatenate([grp(i, b).reshape(QC, n_chunk) for b in range(nb)], axis=1)
          for i in range(n_grp)]

    P = bre_ref.shape[1]
    xre, xim = [], []
    for i in range(n_grp):
        lhs = jnp.concatenate([m_ref[i], bre_ref[i], bim_ref[i]], axis=0)
        r = jnp.dot(lhs, us[i], preferred_element_type=F32)
        ym_ref[i] = r[:QC]
        xre.append(r[QC:QC + P])
        xim.append(r[QC + P:])
    for k in range(n_pair):
        sre_ref[k] = jnp.concatenate(xre[2 * k:2 * k + 2], axis=0).T
        sim_ref[k] = jnp.concatenate(xim[2 * k:2 * k + 2], axis=0).T

    ar = [jnp.broadcast_to(are_ref[k], (nb, are_ref.shape[-1])) for k in range(n_pair)]
    ai = [jnp.broadcast_to(aim_ref[k], (nb, aim_ref.shape[-1])) for k in range(n_pair)]
    sr = [jnp.zeros_like(ar[0])] * n_pair
    si = [jnp.zeros_like(ar[0])] * n_pair
    for c in range(n_chunk):
        rows = pl.ds(c, nb, stride=n_chunk)
        for k in range(n_pair):
            xr, xi = sre_ref[k, rows, :], sim_ref[k, rows, :]
            sre_ref[k, rows, :] = sr[k]
            sim_ref[k, rows, :] = si[k]
            sr[k], si[k] = (ar[k] * sr[k] - ai[k] * si[k] + xr, ar[k] * si[k] + ai[k] * sr[k] + xi)

    nt = (((1,), (1,)), ((), ()))
    for k in range(n_pair):
        inter = (lax.dot_general(cre_ref[k], sre_ref[k].astype(BF16), nt, preferred_element_type=F32)
                 + lax.dot_general(cim_ref[k], sim_ref[k].astype(BF16), nt,
                                   preferred_element_type=F32))
        for i in (2 * k, 2 * k + 1):
            y = ym_ref[i] + inter[(i % 2) * QC:(i % 2 + 1) * QC]
            for b in range(nb):
                yb = y[:, b * n_chunk:(b + 1) * n_chunk].reshape(N_PHASE, C, n_chunk)
                yb = yb + d_ref[i * C:(i + 1) * C] * grp(i, b).astype(F32)
                y_ref[b, :, i * C:(i + 1) * C, :] = yb.astype(BF16)


def _ssm_core(UT, prm, *, n_pair=2):
    B, _, _, n_chunk = UT.shape
    N = B * n_chunk
    G2 = SSM_N_GROUPS // 2
    C2 = 2 * SSM_GROUP
    QC = N_PHASE * SSM_GROUP
    P = SSM_STATE
    blk = lambda *shape: pl.BlockSpec(shape, lambda g: (g,) + (0,) * (len(shape) - 1))
    return pl.pallas_call(
        functools.partial(_ssm_core_kernel, n_chunk=n_chunk, n_pair=n_pair),
        grid=(G2 // n_pair,),
        in_specs=[
            pl.BlockSpec((B, N_PHASE, n_pair * C2, n_chunk), lambda g: (0, 0, g, 0)),
            blk(2 * n_pair, QC, QC), blk(2 * n_pair, P, QC), blk(2 * n_pair, P, QC),
            blk(n_pair, 2 * QC, 2 * P), blk(n_pair, 2 * QC, 2 * P),
            blk(n_pair, 1, 2 * P), blk(n_pair, 1, 2 * P),
            blk(n_pair * C2, 1),
        ],
        out_specs=pl.BlockSpec((B, N_PHASE, n_pair * C2, n_chunk), lambda g: (0, 0, g, 0)),
        out_shape=jax.ShapeDtypeStruct((B, N_PHASE, SSM_N_GROUPS * SSM_GROUP, n_chunk), BF16),
        scratch_shapes=[pltpu.VMEM((n_pair, N, 2 * P), F32), pltpu.VMEM((n_pair, N, 2 * P), F32),
                        pltpu.VMEM((2 * n_pair, QC, N), F32)],
        compiler_params=_cparams(("parallel",)),
        name="ssm_core",
    )(UT, prm["m"], prm["b_re"], prm["b_im"], prm["c_re"], prm["c_im"],
      prm["a_re"], prm["a_im"], prm["d"])


def _ssm_param_kernel(arc_ref, aic_ref, arr_ref, air_ref, ldr_ref, bre_ref, bim_ref,
                      cre_ref, cim_ref, cpr_ref, cpi_ref,
                      m_ref, br_ref, bi_ref, cr_ref, ci_ref, ar_ref, ai_ref):
    Q, C, P = N_PHASE, SSM_GROUP, SSM_STATE
    QC = Q * C
    hi = lax.Precision.HIGHEST
    tile = (lax.broadcasted_iota(jnp.int32, (C, QC), 1) % C
            == lax.broadcasted_iota(jnp.int32, (C, QC), 0)).astype(F32)
    lane_s = lax.broadcasted_iota(jnp.int32, (P, QC), 1) // C

    dt = jnp.exp(ldr_ref[...])
    mag = jnp.exp(arr_ref[...] * dt)
    a1r, a1i = mag * jnp.cos(air_ref[...] * dt), mag * jnp.sin(air_ref[...] * dt)
    a1r_c = jnp.broadcast_to(a1r, (8, 2 * P)).T[:, :1]
    a1i_c = jnp.broadcast_to(a1i, (8, 2 * P)).T[:, :1]

    for i in range(2):
        a_re, a_im = arc_ref[i], aic_ref[i]
        abr, abi = a1r_c[i * P:(i + 1) * P], a1i_c[i * P:(i + 1) * P]
        den = a_re * a_re + a_im * a_im
        cfr = ((abr - 1.0) * a_re + abi * a_im) / den
        cfi = (abi * a_re - (abr - 1.0) * a_im) / den
        bbr = cfr * bre_ref[i] - cfi * bim_ref[i]
        bbi = cfr * bim_ref[i] + cfi * bre_ref[i]

        pr, pi = jnp.ones_like(abr), jnp.zeros_like(abr)
        pw_r = jnp.zeros((P, QC), F32)
        pw_i = jnp.zeros((P, QC), F32)
        for n in range(Q):
            sel = lane_s == Q - 1 - n
            pw_r = jnp.where(sel, pr, pw_r)
            pw_i = jnp.where(sel, pi, pw_i)
            pr, pi = pr * abr - pi * abi, pr * abi + pi * abr
        bbr_t = jnp.dot(bbr, tile, precision=hi, preferred_element_type=F32)
        bbi_t = jnp.dot(bbi, tile, precision=hi, preferred_element_type=F32)
        bst_r = pw_r * bbr_t - pw_i * bbi_t
        bst_i = pw_r * bbi_t + pw_i * bbr_t
        br_ref[i] = bst_r.astype(BF16)
        bi_ref[i] = bst_i.astype(BF16)

        kl = (jnp.dot(cre_ref[i], bst_r, precision=hi, preferred_element_type=F32)
              - jnp.dot(cim_ref[i], bst_i, precision=hi, preferred_element_type=F32))
        klz = jnp.concatenate([kl, jnp.zeros_like(kl)], axis=1)
        for t in range(Q):
            sh = (2 * QC - C * (Q - 1 - t)) % (2 * QC)
            row = klz if sh == 0 else pltpu.roll(klz, sh, axis=1)
            m_ref[i, t * C:(t + 1) * C, :] = row[:, :QC].astype(BF16)

    qr, qi = a1r, a1i
    for t in range(Q):
        for i in range(2):
            rows = slice(i * QC + t * C, i * QC + (t + 1) * C)
            cr_ref[rows, :] = (cpr_ref[i] * qr - cpi_ref[i] * qi).astype(BF16)
            ci_ref[rows, :] = (-(cpr_ref[i] * qi + cpi_ref[i] * qr)).astype(BF16)
        if t < Q - 1:
            qr, qi = qr * a1r - qi * a1i, qr * a1i + qi * a1r
    ar_ref[...] = qr
    ai_ref[...] = qi


def _ssm_out_kernel(y_ref, z_ref, wg_ref, bg_ref, wo_ref, gp_ref, x_ref, o_hbm, obuf, sem,
                    wgt_ref, wot_ref, *, nb, n_sub):
    s, j = pl.program_id(0), pl.program_id(1)
    nj = pl.num_programs(1)
    step = s * nj + j
    last = pl.num_programs(0) * nj - 1
    slot = step % 2

    def out_copy(slot_, s_, j_):
        return pltpu.make_async_copy(obuf.at[slot_], o_hbm.at[pl.ds(j_ * nb, nb), :, s_, :],
                                     sem.at[slot_])

    @pl.when(step >= 2)
    def _():
        out_copy(slot, (step - 2) // nj, (step - 2) % nj).wait()

    @pl.when(step == 0)
    def _():
        wgt_ref[...] = wg_ref[...].T.astype(BF16)
        wot_ref[...] = wo_ref[...].T.astype(BF16)

    bs = nb // n_sub

    def lanes(ref, k):
        return jnp.concatenate([ref[b] for b in range(k * bs, (k + 1) * bs)], axis=1).astype(F32)

    def glu_in(k):
        g = _gelu_tanh(lanes(y_ref, k))
        return g, jnp.dot(wgt_ref[...], g.astype(BF16), preferred_element_type=F32)

    def proj(k, g, gl):
        y2 = g * _sigmoid(gl + bg_ref[...])
        z = lanes(z_ref, k)
        gated = (y2 * (z * _sigmoid(z))).astype(BF16)
        return jnp.dot(wot_ref[...], gated, preferred_element_type=F32)

    def finish(k, ot):
        ms = jnp.mean(ot * ot, axis=0, keepdims=True)
        nt = (ot * lax.rsqrt(ms + EPS)).T
        rows = slice(k * bs, (k + 1) * bs)
        obuf[slot, rows] = x_ref[rows] + (nt * gp_ref[...]).reshape(bs, -1, nt.shape[-1])

    a = [glu_in(k) for k in range(n_sub)]
    o = [proj(k, *a[k]) for k in range(n_sub)]
    for k in range(n_sub):
        finish(k, o[k])
    out_copy(slot, s, j).start()

    @pl.when(step == last)
    def _():
        out_copy(slot, s, j).wait()

    @pl.when(jnp.logical_and(step == last, last >= 1))
    def _():
        out_copy(1 - slot, (step - 1) // nj, (step - 1) % nj).wait()


def _ssm_out(YT, UT, wg, bg, wo, gp, xp, *, nb=8, n_sub=2):
    B, _, M, D = xp.shape
    E = YT.shape[2]
    out = pl.pallas_call(
        functools.partial(_ssm_out_kernel, nb=nb, n_sub=n_sub),
        grid=(N_PHASE, B // nb),
        in_specs=[
            pl.BlockSpec((nb, None, E, M), lambda s, j: (j, s, 0, 0)),
            pl.BlockSpec((nb, None, E, M), lambda s, j: (j, s, 1, 0)),
            pl.BlockSpec((E, E), lambda s, j: (0, 0)),
            pl.BlockSpec((E, 1), lambda s, j: (0, 0)),
            pl.BlockSpec((E, D), lambda s, j: (0, 0)),
            pl.BlockSpec((1, D), lambda s, j: (0, 0)),
            pl.BlockSpec((nb, None, M, D), lambda s, j: (j, s, 0, 0)),
        ],
        out_specs=pl.BlockSpec(memory_space=pl.ANY),
        out_shape=jax.ShapeDtypeStruct((B, M, N_PHASE, D), F32),
        scratch_shapes=[pltpu.VMEM((2, nb, M, D), F32), pltpu.SemaphoreType.DMA((2,)),
                        pltpu.VMEM((E, E), BF16), pltpu.VMEM((D, E), BF16)],
        compiler_params=_cparams(("arbitrary", "arbitrary")),
        name="ssm_out",
    )(YT, UT, wg, bg.reshape(E, 1), wo, gp.reshape(1, D), xp)
    return out.reshape(B, M * N_PHASE, D)


def kernel(x, rel_bias, attn_pre_norm, attn_w_in, attn_w_out, attn_post_norm, ssm_pre_norm, ssm_w_in, ssm_a_re, ssm_a_im, ssm_log_dt, ssm_b_re, ssm_b_im, ssm_c_re, ssm_c_im, ssm_d, ssm_w_glu, ssm_b_glu, ssm_w_out, ssm_post_norm):
    B, S, D = x.shape
    n_chunk = S // N_PHASE

    P, xp = _norm_proj(x, attn_pre_norm[0], attn_w_in[0], scaled_cols=3 * HEADS * HEAD_DIM,
                       scale=HEAD_DIM ** -0.5 * LOG2E)
    O = _attention(P, _bias_tables(rel_bias))
    xp = _out_proj(O, attn_w_out[0], attn_post_norm[0], xp)

    UT, prm = _ssm_in_proj(xp, ssm_pre_norm[0], ssm_w_in[0], ssm_a_re[0], ssm_a_im[0], ssm_log_dt[0],
                           ssm_b_re[0], ssm_b_im[0], ssm_c_re[0], ssm_c_im[0], ssm_d[0])
    YT = _ssm_core(UT, prm)
    return _ssm_out(YT, UT, ssm_w_glu[0], ssm_b_glu[0], ssm_w_out[0], ssm_post_norm[0], xp)
```

```python
import functools
import math

import numpy as np
import jax
import jax.numpy as jnp
from jax import lax
from jax.experimental import pallas as pl
from jax.experimental.pallas import tpu as pltpu

F32 = jnp.float32
BF16 = jnp.bfloat16

D_MODEL = 1024
HEAD_DIM = 64
HEADS = 16
N_PHASE = 16
BLK = 128
DILATIONS = (1, 4, 16)
REL_BUCKETS = 32
REL_MAX_DIST = 2048
SSM_GROUP = 16
SSM_N_GROUPS = 64
SSM_STATE = 64
EPS = 1e-6
NEG = -1e30
VMEM_LIMIT = 56 * 1024 * 1024


def _cparams(sem):
    return pltpu.CompilerParams(dimension_semantics=sem, vmem_limit_bytes=VMEM_LIMIT)


LOG2E = math.log2(math.e)


def _sigmoid(v):
    return 1.0 / (1.0 + jnp.exp2(v * (-LOG2E)))


def _gelu_tanh(x):
    k0 = -2.0 * math.sqrt(2.0 / math.pi) * LOG2E
    return x / (1.0 + jnp.exp2(x * (k0 + (k0 * 0.044715) * (x * x))))


def _norm_proj_kernel(x_hbm, g_ref, w_ref, o_ref, xp_hbm, xbuf, h_ref, gsem, osem, *,
                      scaled_tiles, scale):
    b, j = pl.program_id(0), pl.program_id(1)
    n_b, n_j = pl.num_programs(0), pl.num_programs(1)
    slot = b % 2

    def phase_copy(bb, sl, r):
        return pltpu.make_async_copy(x_hbm.at[bb, :, r, :], xbuf.at[sl, r], gsem.at[sl])

    def gather_start(bb, sl):
        lax.fori_loop(0, N_PHASE, lambda r, c: (phase_copy(bb, sl, r).start(), c)[1], 0)

    def gather_wait(bb, sl):
        lax.fori_loop(0, N_PHASE, lambda r, c: (phase_copy(bb, sl, r).wait(), c)[1], 0)

    def write_out(bb, sl):
        return pltpu.make_async_copy(xbuf.at[sl], xp_hbm.at[bb], osem.at[sl])

    @pl.when(j == 0)
    def _():
        @pl.when(b == 0)
        def _():
            gather_start(0, 0)

        gather_wait(b, slot)

        @pl.when(b >= 1)
        def _():
            write_out(b - 1, 1 - slot).wait()

        @pl.when(b + 1 < n_b)
        def _():
            gather_start(b + 1, 1 - slot)

        write_out(b, slot).start()
        x = xbuf[slot].reshape(h_ref.shape)
        ms = jnp.mean(x * x, axis=-1, keepdims=True)
        h_ref[...] = (x * lax.rsqrt(ms + EPS) * g_ref[...]).astype(BF16)

    w = (w_ref[...] * jnp.where(j < scaled_tiles, scale, 1.0).astype(F32)).astype(BF16)
    res = jnp.dot(h_ref[...], w, preferred_element_type=F32)
    o_ref[...] = res.reshape(o_ref.shape).astype(BF16)

    @pl.when(jnp.logical_and(b == n_b - 1, j == n_j - 1))
    def _():
        write_out(b, slot).wait()


def _norm_proj(x, g, w, *, scaled_cols, scale, tn=1024):
    B, S, D = x.shape
    M = S // N_PHASE
    N = w.shape[1]
    assert scaled_cols % tn == 0
    return pl.pallas_call(
        functools.partial(_norm_proj_kernel, scaled_tiles=scaled_cols // tn, scale=scale),
        grid=(B, N // tn),
        in_specs=[
            pl.BlockSpec(memory_space=pl.ANY),
            pl.BlockSpec((1, D), lambda b, j: (0, 0)),
            pl.BlockSpec((D, tn), lambda b, j: (0, j)),
        ],
        out_specs=[pl.BlockSpec((None, N_PHASE, M, tn), lambda b, j: (b, 0, 0, j)),
                   pl.BlockSpec(memory_space=pl.ANY)],
        out_shape=[jax.ShapeDtypeStruct((B, N_PHASE, M, N), BF16),
                   jax.ShapeDtypeStruct((B, N_PHASE, M, D), F32)],
        scratch_shapes=[pltpu.VMEM((2, N_PHASE, M, D), F32), pltpu.VMEM((S, D), BF16),
                        pltpu.SemaphoreType.DMA((2,)), pltpu.SemaphoreType.DMA((2,))],
        compiler_params=_cparams(("arbitrary", "arbitrary")),
        name="attn_norm_proj",
    )(x.reshape(B, M, N_PHASE, D), g.reshape(1, D), w)


def _attn_kernel(q0_ref, q1_ref, q2_ref, k0_ref, k1_ref, k2_ref, v0_ref, v1_ref, v2_ref,
                 z_ref, bm_ref, o_ref, qf_ref, kf_ref, vf_ref, acc_ref, l_ref, m_ref):
    W = 2 * HEAD_DIM
    AHEAD = 3
    lane = lax.broadcasted_iota(jnp.int32, (BLK, W), 1)
    first_head = lane < HEAD_DIM

    def logits(q, k, g, cur_only):
        zq = jnp.zeros_like(q)
        qs = jnp.concatenate([jnp.where(first_head, q, zq), jnp.where(first_head, zq, q)], axis=0)
        s = lax.dot_general(qs, k, (((1,), (1,)), ((), ())), preferred_element_type=F32)
        return s + (bm_ref[g, :, BLK:2 * BLK] if cur_only else bm_ref[g])

    ones_a = jnp.where(first_head, 1.0, 0.0).astype(BF16)
    ones_b = jnp.where(first_head, 0.0, 1.0).astype(BF16)
    masked_v = {}

    def value_blocks(g, v_ref, pieces, cast):
        key = (g, tuple(pieces))
        if key not in masked_v:
            v = rows_of(v_ref, pieces)
            v = v.astype(BF16) if cast else v
            zv = jnp.zeros_like(v)
            masked_v[key] = (jnp.concatenate([jnp.where(first_head, v, zv), ones_a], axis=1),
                             jnp.concatenate([jnp.where(first_head, zv, v), ones_b], axis=1))
        return masked_v[key]

    def finish(s, vblocks):
        m = jnp.max(s, axis=-1, keepdims=True)
        p = jnp.exp2(s - m).astype(BF16)
        pcat = jnp.concatenate([p[:BLK], p[BLK:]], axis=1)
        rhs = jnp.concatenate([vb[0] for vb in vblocks] + [vb[1] for vb in vblocks], axis=0)
        pv = jnp.dot(pcat, rhs, preferred_element_type=F32)
        mm = jnp.where(first_head, m[:BLK], m[BLK:])
        return pv[:, :W], pv[:, W:], mm

    def rows_of(ref, pieces):
        return jnp.concatenate([ref[ph, lo:lo + n, :] for ph, lo, n in pieces], axis=0)

    def keys_of(ref, prev, cur):
        return rows_of(ref, cur) if prev is None else rows_of(ref, prev + cur)

    def store(g, pieces, vals):
        at = 0
        for ph, lo, n in pieces:
            for ref, val in zip((acc_ref, l_ref, m_ref), vals):
                ref[g, ph, lo:lo + n, :] = val[at:at + n]
            at += n

    M = q0_ref.shape[1]
    g2_blocks = [[(r, 0, BLK)] for r in range(N_PHASE)]
    g1_blocks = {(r4, n): [(4 * q4 + r4, 32 * n, 32) for q4 in range(4)]
                 for r4 in range(4) for n in range(M // 32)}
    g0_blocks = [[(r, 8 * n, 8) for r in range(N_PHASE)] for n in range(M // 8)]

    for r in range(N_PHASE):
        qf_ref[r] = q0_ref[r].astype(F32)
        kf_ref[r] = k0_ref[r].astype(F32)
        vf_ref[r] = v0_ref[r].astype(F32)

    work = []
    for n, pieces in enumerate(g0_blocks):
        work.append((0, pieces, g0_blocks[n - 1] if n else None, (qf_ref, kf_ref, vf_ref), True))
    for (r4, n), pieces in g1_blocks.items():
        work.append((1, pieces, g1_blocks[(r4, n - 1)] if n else None, (q1_ref, k1_ref, v1_ref), False))
    for pieces in g2_blocks:
        work.append((2, pieces, None, (q2_ref, k2_ref, v2_ref), False))

    def merge(r, acc2, l2, m2):
        m0, m1 = m_ref[0, r], m_ref[1, r]
        mx = jnp.maximum(jnp.maximum(m0, m1), m2)
        w0, w1, w2 = jnp.exp2(m0 - mx), jnp.exp2(m1 - mx), jnp.exp2(m2 - mx)
        num = w0 * acc_ref[0, r] + w1 * acc_ref[1, r] + w2 * acc2
        den = w0 * l_ref[0, r] + w1 * l_ref[1, r] + w2 * l2
        z = z_ref[r].astype(F32)
        o_ref[r] = (num * z / (den * (1.0 + jnp.exp2(z * (-LOG2E))))).astype(BF16)

    pending = []
    for item in work + [None] * AHEAD:
        if item is not None:
            g, pieces, prev, (q_r, k_r, _), cast = item
            q, k = rows_of(q_r, pieces), keys_of(k_r, prev, pieces)
            if cast:
                q, k = q.astype(BF16), k.astype(BF16)
            pending.append((item, logits(q, k, g, prev is None)))
        if item is None or len(pending) > AHEAD:
            (g_p, pieces_p, prev_p, (_, _, v_r), cast_p), s_p = pending.pop(0)
            key_blocks = ([] if prev_p is None else [prev_p]) + [pieces_p]
            vals = finish(s_p, [value_blocks(g_p, v_r, kb, cast_p) for kb in key_blocks])
            if g_p == 2:
                merge(pieces_p[0][0], *vals)
            else:
                store(g_p, pieces_p, vals)


def _attention(P, bm):
    B, _, M, _ = P.shape
    HP = HEADS // 2
    W = 2 * HEAD_DIM

    def spec(kind, g):
        base = (kind * 3 + g) * HP
        return pl.BlockSpec((None, N_PHASE, M, W), lambda b, hp, base=base: (b, 0, 0, base + hp))

    in_specs = [spec(kind, g) for kind in range(3) for g in range(3)]
    in_specs.append(pl.BlockSpec((None, N_PHASE, M, W), lambda b, hp: (b, 0, 0, 9 * HP + hp)))
    in_specs.append(pl.BlockSpec((3, None, 2 * BLK, 2 * BLK), lambda b, hp: (0, hp, 0, 0)))
    return pl.pallas_call(
        _attn_kernel,
        grid=(B, HP),
        in_specs=in_specs,
        out_specs=pl.BlockSpec((None, N_PHASE, M, W), lambda b, hp: (b, 0, 0, hp)),
        out_shape=jax.ShapeDtypeStruct((B, N_PHASE, M, HEADS * HEAD_DIM), BF16),
        scratch_shapes=[pltpu.VMEM((N_PHASE, M, W), F32) for _ in range(3)]
        + [pltpu.VMEM((2, N_PHASE, M, W), F32) for _ in range(3)],
        compiler_params=_cparams(("parallel", "parallel")),
        name="dilated_attention",
    )(*([P] * 10), bm)


def _t5_bucket(dist):
    max_exact = REL_BUCKETS // 2
    n = jnp.maximum(dist, 1).astype(F32)
    large = max_exact + (jnp.log(n / max_exact) / math.log(REL_MAX_DIST / max_exact)
                         * (REL_BUCKETS - max_exact)).astype(jnp.int32)
    large = jnp.minimum(large, REL_BUCKETS - 1)
    return jnp.where(dist < max_exact, dist, large)


def _bias_tables(rel_bias):
    a = np.arange(BLK)
    pos = (16 * (a % 8) + a // 8, 4 * (a % 32) + a // 32, a)
    back = np.stack([np.concatenate([BLK + p[:, None] - p[None, :], p[:, None] - p[None, :]], axis=1)
                     for p in pos])
    valid = (back >= 0) & (back <= BLK)
    dist = np.clip(back, 0, BLK) * np.asarray(DILATIONS)[:, None, None]
    bucket = jnp.where(jnp.asarray(valid), _t5_bucket(jnp.asarray(dist, jnp.int32)), REL_BUCKETS)
    onehot = (bucket[..., None] == jnp.arange(REL_BUCKETS + 1)).astype(F32)
    ext = jnp.concatenate([rel_bias.astype(F32), jnp.full((1, HEADS), NEG, F32)], axis=0)
    t = jnp.einsum("gijc,ch->ghij", onehot, ext * math.log2(math.e),
                   precision=lax.Precision.HIGHEST)
    return t.reshape(3, HEADS // 2, 2 * BLK, 2 * BLK)


def _out_proj_kernel(o_ref, w_ref, g_ref, x_ref, y_ref, *, mh):
    o = o_ref[...].reshape(N_PHASE * mh, o_ref.shape[-1])
    h = jnp.dot(o, w_ref[...].astype(BF16), preferred_element_type=F32)
    ms = jnp.mean(h * h, axis=-1, keepdims=True)
    y = h * lax.rsqrt(ms + EPS) * g_ref[...]
    y_ref[...] = x_ref[...] + y.reshape(y_ref.shape)


def _out_proj(O, w, g, xp, *, mh=64):
    B, _, M, D = xp.shape
    return pl.pallas_call(
        functools.partial(_out_proj_kernel, mh=mh),
        grid=(B, M // mh),
        in_specs=[
            pl.BlockSpec((None, N_PHASE, mh, O.shape[-1]), lambda b, m: (b, 0, m, 0)),
            pl.BlockSpec(w.shape, lambda b, m: (0, 0)),
            pl.BlockSpec((1, D), lambda b, m: (0, 0)),
            pl.BlockSpec((None, N_PHASE, mh, D), lambda b, m: (b, 0, m, 0)),
        ],
        out_specs=pl.BlockSpec((None, N_PHASE, mh, D), lambda b, m: (b, 0, m, 0)),
        out_shape=jax.ShapeDtypeStruct(xp.shape, F32),
        compiler_params=_cparams(("parallel", "parallel")),
        name="attn_out_proj",
    )(O, w, g.reshape(1, D), xp)


def _ssm_in_proj_kernel(x_ref, g_ref, w_ref, *rest):
    prm_in, o_ref, prm_out, h_ref = rest[:11], rest[11], rest[12:19], rest[19]
    M = x_ref.shape[1]
    n_pair = prm_out[-1].shape[0]

    @pl.when(pl.program_id(1) == 0)
    def _():
        g = g_ref[...]
        for s in range(N_PHASE):
            xs = x_ref[s]
            ms = jnp.mean(xs * xs, axis=-1, keepdims=True)
            h_ref[s * M:(s + 1) * M, :] = (xs * lax.rsqrt(ms + EPS) * g).astype(BF16)

    wt = w_ref[...].T.astype(BF16)
    res = lax.dot_general(wt, h_ref[...], (((1,), (1,)), ((), ())), preferred_element_type=F32)
    for k in range(n_pair):
        _ssm_param_kernel(*[r.at[2 * k:2 * k + 2] if r.shape[0] == 2 * n_pair else r.at[k]
                            for r in prm_in + prm_out])
    for s in range(N_PHASE):
        o_ref[s] = res[:, s * M:(s + 1) * M].astype(BF16)


def _ssm_in_proj(xp, g, w, a_re, a_im, log_dt, b_re, b_im, c_re, c_im, d_skip, *, tn=1024):
    B, _, M, D = xp.shape
    S = N_PHASE * M
    N = w.shape[1]
    G, P, C, Q = SSM_N_GROUPS, SSM_STATE, SSM_GROUP, N_PHASE
    QC = Q * C
    nj = N // tn
    n_pair = (G // 2) // (B * nj)
    assert B * nj * n_pair == G // 2
    f = lambda t: t.astype(F32)
    a_re, a_im, log_dt, b_re, b_im, c_re, c_im = map(f, (a_re, a_im, log_dt, b_re, b_im, c_re, c_im))
    even = (jnp.arange(G) % 2 == 0)[:, None, None]

    def lane_half(c):
        return jnp.concatenate([jnp.where(even, c, 0.0), jnp.where(even, 0.0, c)], axis=-1)

    pair = lambda blk: pl.BlockSpec((2 * n_pair,) + blk, lambda b, j: (b * nj + j,) + (0,) * len(blk))
    one = lambda blk: pl.BlockSpec((n_pair,) + blk, lambda b, j: (b * nj + j,) + (0,) * len(blk))
    outs = pl.pallas_call(
        _ssm_in_proj_kernel,
        grid=(B, nj),
        in_specs=[
            pl.BlockSpec((None, N_PHASE, M, D), lambda b, j: (b, 0, 0, 0)),
            pl.BlockSpec((1, D), lambda b, j: (0, 0)),
            pl.BlockSpec((D, tn), lambda b, j: (0, j)),
            pair((P, 1)), pair((P, 1)), one((1, 2 * P)), one((1, 2 * P)), one((1, 2 * P)),
            pair((P, C)), pair((P, C)), pair((C, P)), pair((C, P)), pair((C, 2 * P)), pair((C, 2 * P)),
        ],
        out_specs=[pl.BlockSpec((None, N_PHASE, tn, M), lambda b, j: (b, 0, j, 0)),
                   pair((QC, QC)), pair((P, QC)), pair((P, QC)), one((2 * QC, 2 * P)),
                   one((2 * QC, 2 * P)), one((1, 2 * P)), one((1, 2 * P))],
        out_shape=[jax.ShapeDtypeStruct((B, N_PHASE, N, M), BF16),
                   jax.ShapeDtypeStruct((G, QC, QC), BF16), jax.ShapeDtypeStruct((G, P, QC), BF16),
                   jax.ShapeDtypeStruct((G, P, QC), BF16),
                   jax.ShapeDtypeStruct((G // 2, 2 * QC, 2 * P), BF16),
                   jax.ShapeDtypeStruct((G // 2, 2 * QC, 2 * P), BF16),
                   jax.ShapeDtypeStruct((G // 2, 1, 2 * P), F32),
                   jax.ShapeDtypeStruct((G // 2, 1, 2 * P), F32)],
        scratch_shapes=[pltpu.VMEM((S, D), BF16)],
        compiler_params=_cparams(("parallel", "arbitrary")),
        name="ssm_in_proj",
    )(xp, g.reshape(1, D), w,
      a_re.reshape(G, P, 1), a_im.reshape(G, P, 1),
      a_re.reshape(G // 2, 1, 2 * P), a_im.reshape(G // 2, 1, 2 * P),
      jnp.repeat(log_dt, P).reshape(G // 2, 1, 2 * P), b_re, b_im, c_re, c_im,
      lane_half(c_re), lane_half(c_im))
    keys = ("m", "b_re", "b_im", "c_re", "c_im", "a_re", "a_im")
    prm = dict(zip(keys, outs[1:]))
    prm["d"] = d_skip.astype(F32).reshape(G * C, 1)
    return outs[0], prm


def _ssm_core_kernel(u_ref, m_ref, bre_ref, bim_ref, cre_ref, cim_ref, are_ref, aim_ref,
                     d_ref, y_ref, sre_ref, sim_ref, ym_ref, *, n_chunk, n_pair):
    C = SSM_GROUP
    QC = N_PHASE * C
    nb = u_ref.shape[0]
    N = nb * n_chunk
    n_grp = 2 * n_pair

    def grp(i, b):
        return u_ref[b, :, i * C:(i + 1) * C, :]

    us = [jnp.concatenate([grp(i, b).reshape(QC, n_chunk) for b in range(nb)], axis=1)
          for i in range(n_grp)]

    P = bre_ref.shape[1]
    xre, xim = [], []
    for i in range(n_grp):
        lhs = jnp.concatenate([m_ref[i], bre_ref[i], bim_ref[i]], axis=0)
        r = jnp.dot(lhs, us[i], preferred_element_type=F32)
        ym_ref[i] = r[:QC]
        xre.append(r[QC:QC + P])
        xim.append(r[QC + P:])
    for k in range(n_pair):
        sre_ref[k] = jnp.concatenate(xre[2 * k:2 * k + 2], axis=0).T
        sim_ref[k] = jnp.concatenate(xim[2 * k:2 * k + 2], axis=0).T

    sub = lax.broadcasted_iota(jnp.int32, (8, are_ref.shape[-1]), 0)

    def cmul(xr, xi, yr, yi):
        return xr * yr - xi * yi, xr * yi + xi * yr

    for k in range(n_pair):
        a1 = (are_ref[k], aim_ref[k])
        a2 = cmul(*a1, *a1)
        a4 = cmul(*a2, *a2)
        pows = [a1, a2, cmul(*a2, *a1), a4, cmul(*a4, *a1), cmul(*a4, *a2)]
        pows += [cmul(*pows[5], *a1), cmul(*a4, *a4)]
        apow_r = jnp.zeros(sub.shape, F32)
        apow_i = jnp.zeros(sub.shape, F32)
        for i, (pr, pi) in enumerate(pows):
            apow_r = jnp.where(sub == i, pr, apow_r)
            apow_i = jnp.where(sub == i, pi, apow_i)
        step = {d: (jnp.where(sub >= d, pows[d - 1][0], 0.0), jnp.where(sub >= d, pows[d - 1][1], 0.0))
                for d in (1, 2, 4)}
        carry = [(jnp.zeros(sub.shape, F32), jnp.zeros(sub.shape, F32)) for _ in range(nb)]
        for j in range(n_chunk // 8):
            for b in range(nb):
                rows = slice(b * n_chunk + 8 * j, b * n_chunk + 8 * j + 8)
                er, ei = sre_ref[k, rows, :], sim_ref[k, rows, :]
                for d in (1, 2, 4):
                    dr, di = cmul(*step[d], pltpu.roll(er, d, axis=0), pltpu.roll(ei, d, axis=0))
                    er, ei = er + dr, ei + di
                cr, ci = carry[b]
                dr, di = cmul(apow_r, apow_i, cr, ci)
                fr, fi = er + dr, ei + di
                sre_ref[k, rows, :] = jnp.where(sub == 0, cr, pltpu.roll(fr, 1, axis=0))
                sim_ref[k, rows, :] = jnp.where(sub == 0, ci, pltpu.roll(fi, 1, axis=0))
                carry[b] = (jnp.broadcast_to(fr[7:8], sub.shape), jnp.broadcast_to(fi[7:8], sub.shape))

    nt = (((1,), (1,)), ((), ()))
    for k in range(n_pair):
        inter = (lax.dot_general(cre_ref[k], sre_ref[k].astype(BF16), nt, preferred_element_type=F32)
                 + lax.dot_general(cim_ref[k], sim_ref[k].astype(BF16), nt,
                                   preferred_element_type=F32))
        for i in (2 * k, 2 * k + 1):
            y = ym_ref[i] + inter[(i % 2) * QC:(i % 2 + 1) * QC]
            for b in range(nb):
                yb = y[:, b * n_chunk:(b + 1) * n_chunk].reshape(N_PHASE, C, n_chunk)
                yb = yb + d_ref[i * C:(i + 1) * C] * grp(i, b).astype(F32)
                y_ref[b, :, i * C:(i + 1) * C, :] = yb.astype(BF16)


def _ssm_core(UT, prm, *, n_pair=2):
    B, _, _, n_chunk = UT.shape
    N = B * n_chunk
    G2 = SSM_N_GROUPS // 2
    C2 = 2 * SSM_GROUP
    QC = N_PHASE * SSM_GROUP
    P = SSM_STATE
    blk = lambda *shape: pl.BlockSpec(shape, lambda g: (g,) + (0,) * (len(shape) - 1))
    return pl.pallas_call(
        functools.partial(_ssm_core_kernel, n_chunk=n_chunk, n_pair=n_pair),
        grid=(G2 // n_pair,),
        in_specs=[
            pl.BlockSpec((B, N_PHASE, n_pair * C2, n_chunk), lambda g: (0, 0, g, 0)),
            blk(2 * n_pair, QC, QC), blk(2 * n_pair, P, QC), blk(2 * n_pair, P, QC),
            blk(n_pair, 2 * QC, 2 * P), blk(n_pair, 2 * QC, 2 * P),
            blk(n_pair, 1, 2 * P), blk(n_pair, 1, 2 * P),
            blk(n_pair * C2, 1),
        ],
        out_specs=pl.BlockSpec((B, N_PHASE, n_pair * C2, n_chunk), lambda g: (0, 0, g, 0)),
        out_shape=jax.ShapeDtypeStruct((B, N_PHASE, SSM_N_GROUPS * SSM_GROUP, n_chunk), BF16),
        scratch_shapes=[pltpu.VMEM((n_pair, N, 2 * P), F32), pltpu.VMEM((n_pair, N, 2 * P), F32),
                        pltpu.VMEM((2 * n_pair, QC, N), F32)],
        compiler_params=_cparams(("parallel",)),
        name="ssm_core",
    )(UT, prm["m"], prm["b_re"], prm["b_im"], prm["c_re"], prm["c_im"],
      prm["a_re"], prm["a_im"], prm["d"])


def _ssm_param_kernel(arc_ref, aic_ref, arr_ref, air_ref, ldr_ref, bre_ref, bim_ref,
                      cre_ref, cim_ref, cpr_ref, cpi_ref,
                      m_ref, br_ref, bi_ref, cr_ref, ci_ref, ar_ref, ai_ref):
    Q, C, P = N_PHASE, SSM_GROUP, SSM_STATE
    QC = Q * C
    hi = lax.Precision.HIGHEST
    tile = (lax.broadcasted_iota(jnp.int32, (C, QC), 1) % C
            == lax.broadcasted_iota(jnp.int32, (C, QC), 0)).astype(F32)
    lane_s = lax.broadcasted_iota(jnp.int32, (P, QC), 1) // C

    dt = jnp.exp(ldr_ref[...])
    mag = jnp.exp(arr_ref[...] * dt)
    a1r, a1i = mag * jnp.cos(air_ref[...] * dt), mag * jnp.sin(air_ref[...] * dt)
    a1r_c = jnp.broadcast_to(a1r, (8, 2 * P)).T[:, :1]
    a1i_c = jnp.broadcast_to(a1i, (8, 2 * P)).T[:, :1]

    for i in range(2):
        a_re, a_im = arc_ref[i], aic_ref[i]
        abr, abi = a1r_c[i * P:(i + 1) * P], a1i_c[i * P:(i + 1) * P]
        den = a_re * a_re + a_im * a_im
        cfr = ((abr - 1.0) * a_re + abi * a_im) / den
        cfi = (abi * a_re - (abr - 1.0) * a_im) / den
        bbr = cfr * bre_ref[i] - cfi * bim_ref[i]
        bbi = cfr * bim_ref[i] + cfi * bre_ref[i]

        pr, pi = jnp.ones_like(abr), jnp.zeros_like(abr)
        pw_r = jnp.zeros((P, QC), F32)
        pw_i = jnp.zeros((P, QC), F32)
        for n in range(Q):
            sel = lane_s == Q - 1 - n
            pw_r = jnp.where(sel, pr, pw_r)
            pw_i = jnp.where(sel, pi, pw_i)
            pr, pi = pr * abr - pi * abi, pr * abi + pi * abr
        bbr_t = jnp.dot(bbr, tile, precision=hi, preferred_element_type=F32)
        bbi_t = jnp.dot(bbi, tile, precision=hi, preferred_element_type=F32)
        bst_r = pw_r * bbr_t - pw_i * bbi_t
        bst_i = pw_r * bbi_t + pw_i * bbr_t
        br_ref[i] = bst_r.astype(BF16)
        bi_ref[i] = bst_i.astype(BF16)

        kl = (jnp.dot(cre_ref[i], bst_r, precision=hi, preferred_element_type=F32)
              - jnp.dot(cim_ref[i], bst_i, precision=hi, preferred_element_type=F32))
        klz = jnp.concatenate([kl, jnp.zeros_like(kl)], axis=1)
        for t in range(Q):
            sh = (2 * QC - C * (Q - 1 - t)) % (2 * QC)
            row = klz if sh == 0 else pltpu.roll(klz, sh, axis=1)
            m_ref[i, t * C:(t + 1) * C, :] = row[:, :QC].astype(BF16)

    qr, qi = a1r, a1i
    for t in range(Q):
        for i in range(2):
            rows = slice(i * QC + t * C, i * QC + (t + 1) * C)
            cr_ref[rows, :] = (cpr_ref[i] * qr - cpi_ref[i] * qi).astype(BF16)
            ci_ref[rows, :] = (-(cpr_ref[i] * qi + cpi_ref[i] * qr)).astype(BF16)
        if t < Q - 1:
            qr, qi = qr * a1r - qi * a1i, qr * a1i + qi * a1r
    ar_ref[...] = qr
    ai_ref[...] = qi


def _ssm_out_kernel(y_ref, z_ref, wg_ref, bg_ref, wo_ref, gp_ref, x_ref, o_hbm, obuf, sem,
                    wgt_ref, wot_ref, *, nb, n_sub):
    s, j = pl.program_id(0), pl.program_id(1)
    nj = pl.num_programs(1)
    step = s * nj + j
    last = pl.num_programs(0) * nj - 1
    slot = step % 2

    def out_copy(slot_, s_, j_):
        return pltpu.make_async_copy(obuf.at[slot_], o_hbm.at[pl.ds(j_ * nb, nb), :, s_, :],
                                     sem.at[slot_])

    @pl.when(step >= 2)
    def _():
        out_copy(slot, (step - 2) // nj, (step - 2) % nj).wait()

    @pl.when(step == 0)
    def _():
        wgt_ref[...] = wg_ref[...].T.astype(BF16)
        wot_ref[...] = wo_ref[...].T.astype(BF16)

    bs = nb // n_sub

    def lanes(ref, k):
        return jnp.concatenate([ref[b] for b in range(k * bs, (k + 1) * bs)], axis=1).astype(F32)

    def glu_in(k):
        g = _gelu_tanh(lanes(y_ref, k))
        return g, jnp.dot(wgt_ref[...], g.astype(BF16), preferred_element_type=F32)

    def proj(k, g, gl):
        y2 = g * _sigmoid(gl + bg_ref[...])
        z = lanes(z_ref, k)
        gated = (y2 * (z * _sigmoid(z))).astype(BF16)
        return jnp.dot(wot_ref[...], gated, preferred_element_type=F32)

    def finish(k, ot):
        ms = jnp.mean(ot * ot, axis=0, keepdims=True)
        nt = (ot * lax.rsqrt(ms + EPS)).T
        rows = slice(k * bs, (k + 1) * bs)
        obuf[slot, rows] = x_ref[rows] + (nt * gp_ref[...]).reshape(bs, -1, nt.shape[-1])

    a = [glu_in(k) for k in range(n_sub)]
    o = [proj(k, *a[k]) for k in range(n_sub)]
    for k in range(n_sub):
        finish(k, o[k])
    out_copy(slot, s, j).start()

    @pl.when(step == last)
    def _():
        out_copy(slot, s, j).wait()

    @pl.when(jnp.logical_and(step == last, last >= 1))
    def _():
        out_copy(1 - slot, (step - 1) // nj, (step - 1) % nj).wait()


def _ssm_out(YT, UT, wg, bg, wo, gp, xp, *, nb=8, n_sub=2):
    B, _, M, D = xp.shape
    E = YT.shape[2]
    out = pl.pallas_call(
        functools.partial(_ssm_out_kernel, nb=nb, n_sub=n_sub),
        grid=(N_PHASE, B // nb),
        in_specs=[
            pl.BlockSpec((nb, None, E, M), lambda s, j: (j, s, 0, 0)),
            pl.BlockSpec((nb, None, E, M), lambda s, j: (j, s, 1, 0)),
            pl.BlockSpec((E, E), lambda s, j: (0, 0)),
            pl.BlockSpec((E, 1), lambda s, j: (0, 0)),
            pl.BlockSpec((E, D), lambda s, j: (0, 0)),
            pl.BlockSpec((1, D), lambda s, j: (0, 0)),
            pl.BlockSpec((nb, None, M, D), lambda s, j: (j, s, 0, 0)),
        ],
        out_specs=pl.BlockSpec(memory_space=pl.ANY),
        out_shape=jax.ShapeDtypeStruct((B, M, N_PHASE, D), F32),
        scratch_shapes=[pltpu.VMEM((2, nb, M, D), F32), pltpu.SemaphoreType.DMA((2,)),
                        pltpu.VMEM((E, E), BF16), pltpu.VMEM((D, E), BF16)],
        compiler_params=_cparams(("arbitrary", "arbitrary")),
        name="ssm_out",
    )(YT, UT, wg, bg.reshape(E, 1), wo, gp.reshape(1, D), xp)
    return out.reshape(B, M * N_PHASE, D)


def kernel(x, rel_bias, attn_pre_norm, attn_w_in, attn_w_out, attn_post_norm, ssm_pre_norm, ssm_w_in, ssm_a_re, ssm_a_im, ssm_log_dt, ssm_b_re, ssm_b_im, ssm_c_re, ssm_c_im, ssm_d, ssm_w_glu, ssm_b_glu, ssm_w_out, ssm_post_norm):
    B, S, D = x.shape
    n_chunk = S // N_PHASE

    P, xp = _norm_proj(x, attn_pre_norm[0], attn_w_in[0], scaled_cols=3 * HEADS * HEAD_DIM,
                       scale=HEAD_DIM ** -0.5 * LOG2E)
    O = _attention(P, _bias_tables(rel_bias))
    xp = _out_proj(O, attn_w_out[0], attn_post_norm[0], xp)

    UT, prm = _ssm_in_proj(xp, ssm_pre_norm[0], ssm_w_in[0], ssm_a_re[0], ssm_a_im[0], ssm_log_dt[0],
                           ssm_b_re[0], ssm_b_im[0], ssm_c_re[0], ssm_c_im[0], ssm_d[0])
    YT = _ssm_core(UT, prm)
    return _ssm_out(YT, UT, ssm_w_glu[0], ssm_b_glu[0], ssm_w_out[0], ssm_post_norm[0], xp)
```

```python
import functools
import math

import numpy as np
import jax
import jax.numpy as jnp
from jax import lax
from jax.experimental import pallas as pl
from jax.experimental.pallas import tpu as pltpu

F32 = jnp.float32
BF16 = jnp.bfloat16

D_MODEL = 1024
HEAD_DIM = 64
HEADS = 16
N_PHASE = 16
BLK = 128
DILATIONS = (1, 4, 16)
REL_BUCKETS = 32
REL_MAX_DIST = 2048
SSM_GROUP = 16
SSM_N_GROUPS = 64
SSM_STATE = 64
EPS = 1e-6
NEG = -1e30
VMEM_LIMIT = 56 * 1024 * 1024


def _cparams(sem):
    return pltpu.CompilerParams(dimension_semantics=sem, vmem_limit_bytes=VMEM_LIMIT)


LOG2E = math.log2(math.e)


def _sigmoid(v):
    return 1.0 / (1.0 + jnp.exp2(v * (-LOG2E)))


def _gelu_tanh(x):
    k0 = -2.0 * math.sqrt(2.0 / math.pi) * LOG2E
    return x / (1.0 + jnp.exp2(x * (k0 + (k0 * 0.044715) * (x * x))))


def _norm_proj_kernel(x_hbm, g_ref, w_ref, o_ref, xp_hbm, xbuf, h_ref, gsem, osem, *,
                      scaled_tiles, scale):
    b, j = pl.program_id(0), pl.program_id(1)
    n_b, n_j = pl.num_programs(0), pl.num_programs(1)
    slot = b % 2

    def phase_copy(bb, sl, r):
        return pltpu.make_async_copy(x_hbm.at[bb, :, r, :], xbuf.at[sl, r], gsem.at[sl])

    def gather_start(bb, sl):
        lax.fori_loop(0, N_PHASE, lambda r, c: (phase_copy(bb, sl, r).start(), c)[1], 0)

    def gather_wait(bb, sl):
        lax.fori_loop(0, N_PHASE, lambda r, c: (phase_copy(bb, sl, r).wait(), c)[1], 0)

    def write_out(bb, sl):
        return pltpu.make_async_copy(xbuf.at[sl], xp_hbm.at[bb], osem.at[sl])

    @pl.when(j == 0)
    def _():
        @pl.when(b == 0)
        def _():
            gather_start(0, 0)

        gather_wait(b, slot)

        @pl.when(b >= 1)
        def _():
            write_out(b - 1, 1 - slot).wait()

        @pl.when(b + 1 < n_b)
        def _():
            gather_start(b + 1, 1 - slot)

        write_out(b, slot).start()
        x = xbuf[slot].reshape(h_ref.shape)
        ms = jnp.mean(x * x, axis=-1, keepdims=True)
        h_ref[...] = (x * lax.rsqrt(ms + EPS) * g_ref[...]).astype(BF16)

    w = (w_ref[...] * jnp.where(j < scaled_tiles, scale, 1.0).astype(F32)).astype(BF16)
    res = jnp.dot(h_ref[...], w, preferred_element_type=F32)
    o_ref[...] = res.reshape(o_ref.shape).astype(BF16)

    @pl.when(jnp.logical_and(b == n_b - 1, j == n_j - 1))
    def _():
        write_out(b, slot).wait()


def _norm_proj(x, g, w, *, scaled_cols, scale, tn=1024):
    B, S, D = x.shape
    M = S // N_PHASE
    N = w.shape[1]
    assert scaled_cols % tn == 0
    return pl.pallas_call(
        functools.partial(_norm_proj_kernel, scaled_tiles=scaled_cols // tn, scale=scale),
        grid=(B, N // tn),
        in_specs=[
            pl.BlockSpec(memory_space=pl.ANY),
            pl.BlockSpec((1, D), lambda b, j: (0, 0)),
            pl.BlockSpec((D, tn), lambda b, j: (0, j)),
        ],
        out_specs=[pl.BlockSpec((None, N_PHASE, M, tn), lambda b, j: (b, 0, 0, j)),
                   pl.BlockSpec(memory_space=pl.ANY)],
        out_shape=[jax.ShapeDtypeStruct((B, N_PHASE, M, N), BF16),
                   jax.ShapeDtypeStruct((B, N_PHASE, M, D), F32)],
        scratch_shapes=[pltpu.VMEM((2, N_PHASE, M, D), F32), pltpu.VMEM((S, D), BF16),
                        pltpu.SemaphoreType.DMA((2,)), pltpu.SemaphoreType.DMA((2,))],
        compiler_params=_cparams(("arbitrary", "arbitrary")),
        name="attn_norm_proj",
    )(x.reshape(B, M, N_PHASE, D), g.reshape(1, D), w)


def _attn_kernel(q0_ref, q1_ref, q2_ref, k0_ref, k1_ref, k2_ref, v0_ref, v1_ref, v2_ref,
                 z_ref, bm_ref, o_ref, qf_ref, kf_ref, vf_ref, acc_ref, l_ref, m_ref):
    W = 2 * HEAD_DIM
    AHEAD = 3
    lane = lax.broadcasted_iota(jnp.int32, (BLK, W), 1)
    first_head = lane < HEAD_DIM

    def logits(q, k, g, cur_only):
        zq = jnp.zeros_like(q)
        qs = jnp.concatenate([jnp.where(first_head, q, zq), jnp.where(first_head, zq, q)], axis=0)
        s = lax.dot_general(qs, k, (((1,), (1,)), ((), ())), preferred_element_type=F32)
        return s + (bm_ref[g, :, BLK:2 * BLK] if cur_only else bm_ref[g])

    ones_a = jnp.where(first_head, 1.0, 0.0).astype(BF16)
    ones_b = jnp.where(first_head, 0.0, 1.0).astype(BF16)
    masked_v = {}

    def value_blocks(g, v_ref, pieces, cast):
        key = (g, tuple(pieces))
        if key not in masked_v:
            v = rows_of(v_ref, pieces)
            v = v.astype(BF16) if cast else v
            zv = jnp.zeros_like(v)
            masked_v[key] = (jnp.concatenate([jnp.where(first_head, v, zv), ones_a], axis=1),
                             jnp.concatenate([jnp.where(first_head, zv, v), ones_b], axis=1))
        return masked_v[key]

    def finish(s, vblocks):
        m = jnp.max(s, axis=-1, keepdims=True)
        p = jnp.exp2(s - m).astype(BF16)
        pcat = jnp.concatenate([p[:BLK], p[BLK:]], axis=1)
        rhs = jnp.concatenate([vb[0] for vb in vblocks] + [vb[1] for vb in vblocks], axis=0)
        pv = jnp.dot(pcat, rhs, preferred_element_type=F32)
        mm = jnp.where(first_head, m[:BLK], m[BLK:])
        return pv[:, :W], pv[:, W:], mm

    def rows_of(ref, pieces):
        return jnp.concatenate([ref[ph, lo:lo + n, :] for ph, lo, n in pieces], axis=0)

    def keys_of(ref, prev, cur):
        return rows_of(ref, cur) if prev is None else rows_of(ref, prev + cur)

    def store(g, pieces, vals):
        at = 0
        for ph, lo, n in pieces:
            for ref, val in zip((acc_ref, l_ref, m_ref), vals):
                ref[g, ph, lo:lo + n, :] = val[at:at + n]
            at += n

    M = q0_ref.shape[1]
    g2_blocks = [[(r, 0, BLK)] for r in range(N_PHASE)]
    g1_blocks = {(r4, n): [(4 * q4 + r4, 32 * n, 32) for q4 in range(4)]
                 for r4 in range(4) for n in range(M // 32)}
    g0_blocks = [[(r, 8 * n, 8) for r in range(N_PHASE)] for n in range(M // 8)]

    for r in range(N_PHASE):
        qf_ref[r] = q0_ref[r].astype(F32)
        kf_ref[r] = k0_ref[r].astype(F32)
        vf_ref[r] = v0_ref[r].astype(F32)

    work = []
    for n, pieces in enumerate(g0_blocks):
        work.append((0, pieces, g0_blocks[n - 1] if n else None, (qf_ref, kf_ref, vf_ref), True))
    for (r4, n), pieces in g1_blocks.items():
        work.append((1, pieces, g1_blocks[(r4, n - 1)] if n else None, (q1_ref, k1_ref, v1_ref), False))
    for pieces in g2_blocks:
        work.append((2, pieces, None, (q2_ref, k2_ref, v2_ref), False))

    def merge(r, acc2, l2, m2):
        m0, m1 = m_ref[0, r], m_ref[1, r]
        mx = jnp.maximum(jnp.maximum(m0, m1), m2)
        w0, w1, w2 = jnp.exp2(m0 - mx), jnp.exp2(m1 - mx), jnp.exp2(m2 - mx)
        num = w0 * acc_ref[0, r] + w1 * acc_ref[1, r] + w2 * acc2
        den = w0 * l_ref[0, r] + w1 * l_ref[1, r] + w2 * l2
        z = z_ref[r].astype(F32)
        o_ref[r] = (num * z / (den * (1.0 + jnp.exp2(z * (-LOG2E))))).astype(BF16)

    pending = []
    for item in work + [None] * AHEAD:
        if item is not None:
            g, pieces, prev, (q_r, k_r, _), cast = item
            q, k = rows_of(q_r, pieces), keys_of(k_r, prev, pieces)
            if cast:
                q, k = q.astype(BF16), k.astype(BF16)
            pending.append((item, logits(q, k, g, prev is None)))
        if item is None or len(pending) > AHEAD:
            (g_p, pieces_p, prev_p, (_, _, v_r), cast_p), s_p = pending.pop(0)
            key_blocks = ([] if prev_p is None else [prev_p]) + [pieces_p]
            vals = finish(s_p, [value_blocks(g_p, v_r, kb, cast_p) for kb in key_blocks])
            if g_p == 2:
                merge(pieces_p[0][0], *vals)
            else:
                store(g_p, pieces_p, vals)


def _attention(P, bm):
    B, _, M, _ = P.shape
    HP = HEADS // 2
    W = 2 * HEAD_DIM

    def spec(kind, g):
        base = (kind * 3 + g) * HP
        return pl.BlockSpec((None, N_PHASE, M, W), lambda b, hp, base=base: (b, 0, 0, base + hp))

    in_specs = [spec(kind, g) for kind in range(3) for g in range(3)]
    in_specs.append(pl.BlockSpec((None, N_PHASE, M, W), lambda b, hp: (b, 0, 0, 9 * HP + hp)))
    in_specs.append(pl.BlockSpec((3, None, 2 * BLK, 2 * BLK), lambda b, hp: (0, hp, 0, 0)))
    return pl.pallas_call(
        _attn_kernel,
        grid=(B, HP),
        in_specs=in_specs,
        out_specs=pl.BlockSpec((None, N_PHASE, M, W), lambda b, hp: (b, 0, 0, hp)),
        out_shape=jax.ShapeDtypeStruct((B, N_PHASE, M, HEADS * HEAD_DIM), BF16),
        scratch_shapes=[pltpu.VMEM((N_PHASE, M, W), F32) for _ in range(3)]
        + [pltpu.VMEM((2, N_PHASE, M, W), F32) for _ in range(3)],
        compiler_params=_cparams(("parallel", "parallel")),
        name="dilated_attention",
    )(*([P] * 10), bm)


def _t5_bucket(dist):
    max_exact = REL_BUCKETS // 2
    n = jnp.maximum(dist, 1).astype(F32)
    large = max_exact + (jnp.log(n / max_exact) / math.log(REL_MAX_DIST / max_exact)
                         * (REL_BUCKETS - max_exact)).astype(jnp.int32)
    large = jnp.minimum(large, REL_BUCKETS - 1)
    return jnp.where(dist < max_exact, dist, large)


def _bias_tables(rel_bias):
    a = np.arange(BLK)
    pos = (16 * (a % 8) + a // 8, 4 * (a % 32) + a // 32, a)
    back = np.stack([np.concatenate([BLK + p[:, None] - p[None, :], p[:, None] - p[None, :]], axis=1)
                     for p in pos])
    valid = (back >= 0) & (back <= BLK)
    dist = np.clip(back, 0, BLK) * np.asarray(DILATIONS)[:, None, None]
    bucket = jnp.where(jnp.asarray(valid), _t5_bucket(jnp.asarray(dist, jnp.int32)), REL_BUCKETS)
    onehot = (bucket[..., None] == jnp.arange(REL_BUCKETS + 1)).astype(F32)
    ext = jnp.concatenate([rel_bias.astype(F32), jnp.full((1, HEADS), NEG, F32)], axis=0)
    t = jnp.einsum("gijc,ch->ghij", onehot, ext * math.log2(math.e),
                   precision=lax.Precision.HIGHEST)
    return t.reshape(3, HEADS // 2, 2 * BLK, 2 * BLK)


def _out_proj_kernel(o_ref, w_ref, g_ref, x_ref, *rest, mh):
    prm_in, y_ref, prm_out = rest[:11], rest[11], rest[12:]
    n_pair = prm_out[-1].shape[0]
    o = o_ref[...].reshape(N_PHASE * mh, o_ref.shape[-1])
    h = jnp.dot(o, w_ref[...].astype(BF16), preferred_element_type=F32)
    for k in range(n_pair):
        _ssm_param_kernel(*[r.at[2 * k:2 * k + 2] if r.shape[0] == 2 * n_pair else r.at[k]
                            for r in prm_in + prm_out])
    ms = jnp.mean(h * h, axis=-1, keepdims=True)
    y = h * lax.rsqrt(ms + EPS) * g_ref[...]
    y_ref[...] = x_ref[...] + y.reshape(y_ref.shape)


def _out_proj(O, w, g, xp, a_re, a_im, log_dt, b_re, b_im, c_re, c_im, d_skip, *, mh=64):
    B, _, M, D = xp.shape
    G, P, C, Q = SSM_N_GROUPS, SSM_STATE, SSM_GROUP, N_PHASE
    QC = Q * C
    nm = M // mh
    n_pair = (G // 2) // (B * nm)
    assert B * nm * n_pair == G // 2
    f = lambda t: t.astype(F32)
    a_re, a_im, log_dt, b_re, b_im, c_re, c_im = map(f, (a_re, a_im, log_dt, b_re, b_im, c_re, c_im))
    even = (jnp.arange(G) % 2 == 0)[:, None, None]

    def lane_half(c):
        return jnp.concatenate([jnp.where(even, c, 0.0), jnp.where(even, 0.0, c)], axis=-1)

    pair = lambda blk: pl.BlockSpec((2 * n_pair,) + blk, lambda b, m: (b * nm + m,) + (0,) * len(blk))
    one = lambda blk: pl.BlockSpec((n_pair,) + blk, lambda b, m: (b * nm + m,) + (0,) * len(blk))
    outs = pl.pallas_call(
        functools.partial(_out_proj_kernel, mh=mh),
        grid=(B, nm),
        in_specs=[
            pl.BlockSpec((None, N_PHASE, mh, O.shape[-1]), lambda b, m: (b, 0, m, 0)),
            pl.BlockSpec(w.shape, lambda b, m: (0, 0)),
            pl.BlockSpec((1, D), lambda b, m: (0, 0)),
            pl.BlockSpec((None, N_PHASE, mh, D), lambda b, m: (b, 0, m, 0)),
            pair((P, 1)), pair((P, 1)), one((1, 2 * P)), one((1, 2 * P)), one((1, 2 * P)),
            pair((P, C)), pair((P, C)), pair((C, P)), pair((C, P)), pair((C, 2 * P)), pair((C, 2 * P)),
        ],
        out_specs=[pl.BlockSpec((None, N_PHASE, mh, D), lambda b, m: (b, 0, m, 0)),
                   pair((QC, QC)), pair((P, QC)), pair((P, QC)), one((2 * QC, 2 * P)),
                   one((2 * QC, 2 * P)), one((1, 2 * P)), one((1, 2 * P))],
        out_shape=[jax.ShapeDtypeStruct(xp.shape, F32),
                   jax.ShapeDtypeStruct((G, QC, QC), BF16), jax.ShapeDtypeStruct((G, P, QC), BF16),
                   jax.ShapeDtypeStruct((G, P, QC), BF16),
                   jax.ShapeDtypeStruct((G // 2, 2 * QC, 2 * P), BF16),
                   jax.ShapeDtypeStruct((G // 2, 2 * QC, 2 * P), BF16),
                   jax.ShapeDtypeStruct((G // 2, 1, 2 * P), F32),
                   jax.ShapeDtypeStruct((G // 2, 1, 2 * P), F32)],
        compiler_params=_cparams(("parallel", "parallel")),
        name="attn_out_proj",
    )(O, w, g.reshape(1, D), xp,
      a_re.reshape(G, P, 1), a_im.reshape(G, P, 1),
      a_re.reshape(G // 2, 1, 2 * P), a_im.reshape(G // 2, 1, 2 * P),
      jnp.repeat(log_dt, P).reshape(G // 2, 1, 2 * P), b_re, b_im, c_re, c_im,
      lane_half(c_re), lane_half(c_im))
    keys = ("m", "b_re", "b_im", "c_re", "c_im", "a_re", "a_im")
    prm = dict(zip(keys, outs[1:]))
    prm["d"] = d_skip.astype(F32).reshape(G * C, 1)
    return outs[0], prm


def _ssm_in_proj_kernel(x_ref, g_ref, w_ref, o_ref, h_ref):
    M = x_ref.shape[1]

    @pl.when(pl.program_id(1) == 0)
    def _():
        g = g_ref[...]
        for s in range(N_PHASE):
            xs = x_ref[s]
            ms = jnp.mean(xs * xs, axis=-1, keepdims=True)
            h_ref[s * M:(s + 1) * M, :] = (xs * lax.rsqrt(ms + EPS) * g).astype(BF16)

    wt = w_ref[...].T.astype(BF16)
    res = lax.dot_general(wt, h_ref[...], (((1,), (1,)), ((), ())), preferred_element_type=F32)
    for s in range(N_PHASE):
        o_ref[s] = res[:, s * M:(s + 1) * M].astype(BF16)


def _ssm_in_proj(xp, g, w, *, tn=1024):
    B, _, M, D = xp.shape
    S = N_PHASE * M
    N = w.shape[1]
    return pl.pallas_call(
        _ssm_in_proj_kernel,
        grid=(B, N // tn),
        in_specs=[
            pl.BlockSpec((None, N_PHASE, M, D), lambda b, j: (b, 0, 0, 0)),
            pl.BlockSpec((1, D), lambda b, j: (0, 0)),
            pl.BlockSpec((D, tn), lambda b, j: (0, j)),
        ],
        out_specs=pl.BlockSpec((None, N_PHASE, tn, M), lambda b, j: (b, 0, j, 0)),
        out_shape=jax.ShapeDtypeStruct((B, N_PHASE, N, M), BF16),
        scratch_shapes=[pltpu.VMEM((S, D), BF16)],
        compiler_params=_cparams(("parallel", "arbitrary")),
        name="ssm_in_proj",
    )(xp, g.reshape(1, D), w)


def _ssm_core_kernel(u_ref, m_ref, bre_ref, bim_ref, cre_ref, cim_ref, are_ref, aim_ref,
                     d_ref, y_ref, sre_ref, sim_ref, ym_ref, *, n_chunk, n_pair):
    C = SSM_GROUP
    QC = N_PHASE * C
    nb = u_ref.shape[0]
    N = nb * n_chunk
    n_grp = 2 * n_pair

    def grp(i, b):
        return u_ref[b, :, i * C:(i + 1) * C, :]

    us = [jnp.concatenate([grp(i, b).reshape(QC, n_chunk) for b in range(nb)], axis=1)
          for i in range(n_grp)]

    P = bre_ref.shape[1]
    xre, xim = [], []
    for i in range(n_grp):
        lhs = jnp.concatenate([m_ref[i], bre_ref[i], bim_ref[i]], axis=0)
        r = jnp.dot(lhs, us[i], preferred_element_type=F32)
        ym_ref[i] = r[:QC]
        xre.append(r[QC:QC + P])
        xim.append(r[QC + P:])
    for k in range(n_pair):
        sre_ref[k] = jnp.concatenate(xre[2 * k:2 * k + 2], axis=0).T
        sim_ref[k] = jnp.concatenate(xim[2 * k:2 * k + 2], axis=0).T

    sub = lax.broadcasted_iota(jnp.int32, (8, are_ref.shape[-1]), 0)

    def cmul(xr, xi, yr, yi):
        return xr * yr - xi * yi, xr * yi + xi * yr

    for k in range(n_pair):
        a1 = (are_ref[k], aim_ref[k])
        a2 = cmul(*a1, *a1)
        a4 = cmul(*a2, *a2)
        pows = [a1, a2, cmul(*a2, *a1), a4, cmul(*a4, *a1), cmul(*a4, *a2)]
        pows += [cmul(*pows[5], *a1), cmul(*a4, *a4)]
        apow_r = jnp.zeros(sub.shape, F32)
        apow_i = jnp.zeros(sub.shape, F32)
        for i, (pr, pi) in enumerate(pows):
            apow_r = jnp.where(sub == i, pr, apow_r)
            apow_i = jnp.where(sub == i, pi, apow_i)
        step = {d: (jnp.where(sub >= d, pows[d - 1][0], 0.0), jnp.where(sub >= d, pows[d - 1][1], 0.0))
                for d in (1, 2, 4)}
        carry = [(jnp.zeros(sub.shape, F32), jnp.zeros(sub.shape, F32)) for _ in range(nb)]
        for j in range(n_chunk // 8):
            for b in range(nb):
                rows = slice(b * n_chunk + 8 * j, b * n_chunk + 8 * j + 8)
                er, ei = sre_ref[k, rows, :], sim_ref[k, rows, :]
                for d in (1, 2, 4):
                    dr, di = cmul(*step[d], pltpu.roll(er, d, axis=0), pltpu.roll(ei, d, axis=0))
                    er, ei = er + dr, ei + di
                cr, ci = carry[b]
                dr, di = cmul(apow_r, apow_i, cr, ci)
                fr, fi = er + dr, ei + di
                sre_ref[k, rows, :] = jnp.where(sub == 0, cr, pltpu.roll(fr, 1, axis=0))
                sim_ref[k, rows, :] = jnp.where(sub == 0, ci, pltpu.roll(fi, 1, axis=0))
                carry[b] = (jnp.broadcast_to(fr[7:8], sub.shape), jnp.broadcast_to(fi[7:8], sub.shape))

    nt = (((1,), (1,)), ((), ()))
    for k in range(n_pair):
        inter = (lax.dot_general(cre_ref[k], sre_ref[k].astype(BF16), nt, preferred_element_type=F32)
                 + lax.dot_general(cim_ref[k], sim_ref[k].astype(BF16), nt,
                                   preferred_element_type=F32))
        for i in (2 * k, 2 * k + 1):
            y = ym_ref[i] + inter[(i % 2) * QC:(i % 2 + 1) * QC]
            for b in range(nb):
                yb = y[:, b * n_chunk:(b + 1) * n_chunk].reshape(N_PHASE, C, n_chunk)
                yb = yb + d_ref[i * C:(i + 1) * C] * grp(i, b).astype(F32)
                y_ref[b, :, i * C:(i + 1) * C, :] = yb.astype(BF16)


def _ssm_core(UT, prm, *, n_pair=2):
    B, _, _, n_chunk = UT.shape
    N = B * n_chunk
    G2 = SSM_N_GROUPS // 2
    C2 = 2 * SSM_GROUP
    QC = N_PHASE * SSM_GROUP
    P = SSM_STATE
    blk = lambda *shape: pl.BlockSpec(shape, lambda g: (g,) + (0,) * (len(shape) - 1))
    return pl.pallas_call(
        functools.partial(_ssm_core_kernel, n_chunk=n_chunk, n_pair=n_pair),
        grid=(G2 // n_pair,),
        in_specs=[
            pl.BlockSpec((B, N_PHASE, n_pair * C2, n_chunk), lambda g: (0, 0, g, 0)),
            blk(2 * n_pair, QC, QC), blk(2 * n_pair, P, QC), blk(2 * n_pair, P, QC),
            blk(n_pair, 2 * QC, 2 * P), blk(n_pair, 2 * QC, 2 * P),
            blk(n_pair, 1, 2 * P), blk(n_pair, 1, 2 * P),
            blk(n_pair * C2, 1),
        ],
        out_specs=pl.BlockSpec((B, N_PHASE, n_pair * C2, n_chunk), lambda g: (0, 0, g, 0)),
        out_shape=jax.ShapeDtypeStruct((B, N_PHASE, SSM_N_GROUPS * SSM_GROUP, n_chunk), BF16),
        scratch_shapes=[pltpu.VMEM((n_pair, N, 2 * P), F32), pltpu.VMEM((n_pair, N, 2 * P), F32),
                        pltpu.VMEM((2 * n_pair, QC, N), F32)],
        compiler_params=_cparams(("parallel",)),
        name="ssm_core",
    )(UT, prm["m"], prm["b_re"], prm["b_im"], prm["c_re"], prm["c_im"],
      prm["a_re"], prm["a_im"], prm["d"])


def _ssm_param_kernel(arc_ref, aic_ref, arr_ref, air_ref, ldr_ref, bre_ref, bim_ref,
                      cre_ref, cim_ref, cpr_ref, cpi_ref,
                      m_ref, br_ref, bi_ref, cr_ref, ci_ref, ar_ref, ai_ref):
    Q, C, P = N_PHASE, SSM_GROUP, SSM_STATE
    QC = Q * C
    hi = lax.Precision.HIGHEST
    tile = (lax.broadcasted_iota(jnp.int32, (C, QC), 1) % C
            == lax.broadcasted_iota(jnp.int32, (C, QC), 0)).astype(F32)
    lane_s = lax.broadcasted_iota(jnp.int32, (P, QC), 1) // C

    dt = jnp.exp(ldr_ref[...])
    mag = jnp.exp(arr_ref[...] * dt)
    a1r, a1i = mag * jnp.cos(air_ref[...] * dt), mag * jnp.sin(air_ref[...] * dt)
    a1r_c = jnp.broadcast_to(a1r, (8, 2 * P)).T[:, :1]
    a1i_c = jnp.broadcast_to(a1i, (8, 2 * P)).T[:, :1]

    for i in range(2):
        a_re, a_im = arc_ref[i], aic_ref[i]
        abr, abi = a1r_c[i * P:(i + 1) * P], a1i_c[i * P:(i + 1) * P]
        den = a_re * a_re + a_im * a_im
        cfr = ((abr - 1.0) * a_re + abi * a_im) / den
        cfi = (abi * a_re - (abr - 1.0) * a_im) / den
        bbr = cfr * bre_ref[i] - cfi * bim_ref[i]
        bbi = cfr * bim_ref[i] + cfi * bre_ref[i]

        pr, pi = jnp.ones_like(abr), jnp.zeros_like(abr)
        pw_r = jnp.zeros((P, QC), F32)
        pw_i = jnp.zeros((P, QC), F32)
        for n in range(Q):
            sel = lane_s == Q - 1 - n
            pw_r = jnp.where(sel, pr, pw_r)
            pw_i = jnp.where(sel, pi, pw_i)
            pr, pi = pr * abr - pi * abi, pr * abi + pi * abr
        bbr_t = jnp.dot(bbr, tile, precision=hi, preferred_element_type=F32)
        bbi_t = jnp.dot(bbi, tile, precision=hi, preferred_element_type=F32)
        bst_r = pw_r * bbr_t - pw_i * bbi_t
        bst_i = pw_r * bbi_t + pw_i * bbr_t
        br_ref[i] = bst_r.astype(BF16)
        bi_ref[i] = bst_i.astype(BF16)

        kl = (jnp.dot(cre_ref[i], bst_r, precision=hi, preferred_element_type=F32)
              - jnp.dot(cim_ref[i], bst_i, precision=hi, preferred_element_type=F32))
        klz = jnp.concatenate([kl, jnp.zeros_like(kl)], axis=1)
        for t in range(Q):
            sh = (2 * QC - C * (Q - 1 - t)) % (2 * QC)
            row = klz if sh == 0 else pltpu.roll(klz, sh, axis=1)
            m_ref[i, t * C:(t + 1) * C, :] = row[:, :QC].astype(BF16)

    qr, qi = a1r, a1i
    for t in range(Q):
        for i in range(2):
            rows = slice(i * QC + t * C, i * QC + (t + 1) * C)
            cr_ref[rows, :] = (cpr_ref[i] * qr - cpi_ref[i] * qi).astype(BF16)
            ci_ref[rows, :] = (-(cpr_ref[i] * qi + cpi_ref[i] * qr)).astype(BF16)
        if t < Q - 1:
            qr, qi = qr * a1r - qi * a1i, qr * a1i + qi * a1r
    ar_ref[...] = qr
    ai_ref[...] = qi


def _ssm_out_kernel(y_ref, z_ref, wg_ref, bg_ref, wo_ref, gp_ref, x_ref, o_hbm, obuf, sem,
                    wgt_ref, wot_ref, *, nb, n_sub):
    s, j = pl.program_id(0), pl.program_id(1)
    nj = pl.num_programs(1)
    step = s * nj + j
    last = pl.num_programs(0) * nj - 1
    slot = step % 2

    def out_copy(slot_, s_, j_):
        return pltpu.make_async_copy(obuf.at[slot_], o_hbm.at[pl.ds(j_ * nb, nb), :, s_, :],
                                     sem.at[slot_])

    @pl.when(step >= 2)
    def _():
        out_copy(slot, (step - 2) // nj, (step - 2) % nj).wait()

    @pl.when(step == 0)
    def _():
        wgt_ref[...] = wg_ref[...].T.astype(BF16)
        wot_ref[...] = wo_ref[...].T.astype(BF16)

    bs = nb // n_sub

    def lanes(ref, k):
        return jnp.concatenate([ref[b] for b in range(k * bs, (k + 1) * bs)], axis=1).astype(F32)

    def glu_in(k):
        g = _gelu_tanh(lanes(y_ref, k))
        return g, jnp.dot(wgt_ref[...], g.astype(BF16), preferred_element_type=F32)

    def proj(k, g, gl):
        y2 = g * _sigmoid(gl + bg_ref[...])
        z = lanes(z_ref, k)
        gated = (y2 * (z * _sigmoid(z))).astype(BF16)
        return jnp.dot(wot_ref[...], gated, preferred_element_type=F32)

    def finish(k, ot):
        ms = jnp.mean(ot * ot, axis=0, keepdims=True)
        nt = (ot * lax.rsqrt(ms + EPS)).T
        rows = slice(k * bs, (k + 1) * bs)
        obuf[slot, rows] = x_ref[rows] + (nt * gp_ref[...]).reshape(bs, -1, nt.shape[-1])

    a = [glu_in(k) for k in range(n_sub)]
    o = [proj(k, *a[k]) for k in range(n_sub)]
    for k in range(n_sub):
        finish(k, o[k])
    out_copy(slot, s, j).start()

    @pl.when(step == last)
    def _():
        out_copy(slot, s, j).wait()

    @pl.when(jnp.logical_and(step == last, last >= 1))
    def _():
        out_copy(1 - slot, (step - 1) // nj, (step - 1) % nj).wait()


def _ssm_out(YT, UT, wg, bg, wo, gp, xp, *, nb=8, n_sub=2):
    B, _, M, D = xp.shape
    E = YT.shape[2]
    out = pl.pallas_call(
        functools.partial(_ssm_out_kernel, nb=nb, n_sub=n_sub),
        grid=(N_PHASE, B // nb),
        in_specs=[
            pl.BlockSpec((nb, None, E, M), lambda s, j: (j, s, 0, 0)),
            pl.BlockSpec((nb, None, E, M), lambda s, j: (j, s, 1, 0)),
            pl.BlockSpec((E, E), lambda s, j: (0, 0)),
            pl.BlockSpec((E, 1), lambda s, j: (0, 0)),
            pl.BlockSpec((E, D), lambda s, j: (0, 0)),
            pl.BlockSpec((1, D), lambda s, j: (0, 0)),
            pl.BlockSpec((nb, None, M, D), lambda s, j: (j, s, 0, 0)),
        ],
        out_specs=pl.BlockSpec(memory_space=pl.ANY),
        out_shape=jax.ShapeDtypeStruct((B, M, N_PHASE, D), F32),
        scratch_shapes=[pltpu.VMEM((2, nb, M, D), F32), pltpu.SemaphoreType.DMA((2,)),
                        pltpu.VMEM((E, E), BF16), pltpu.VMEM((D, E), BF16)],
        compiler_params=_cparams(("arbitrary", "arbitrary")),
        name="ssm_out",
    )(YT, UT, wg, bg.reshape(E, 1), wo, gp.reshape(1, D), xp)
    return out.reshape(B, M * N_PHASE, D)


def kernel(x, rel_bias, attn_pre_norm, attn_w_in, attn_w_out, attn_post_norm, ssm_pre_norm, ssm_w_in, ssm_a_re, ssm_a_im, ssm_log_dt, ssm_b_re, ssm_b_im, ssm_c_re, ssm_c_im, ssm_d, ssm_w_glu, ssm_b_glu, ssm_w_out, ssm_post_norm):
    B, S, D = x.shape
    n_chunk = S // N_PHASE

    P, xp = _norm_proj(x, attn_pre_norm[0], attn_w_in[0], scaled_cols=3 * HEADS * HEAD_DIM,
                       scale=HEAD_DIM ** -0.5 * LOG2E)
    O = _attention(P, _bias_tables(rel_bias))
    xp, prm = _out_proj(O, attn_w_out[0], attn_post_norm[0], xp, ssm_a_re[0], ssm_a_im[0],
                        ssm_log_dt[0], ssm_b_re[0], ssm_b_im[0], ssm_c_re[0], ssm_c_im[0], ssm_d[0])

    UT = _ssm_in_proj(xp, ssm_pre_norm[0], ssm_w_in[0])
    YT = _ssm_core(UT, prm)
    return _ssm_out(YT, UT, ssm_w_glu[0], ssm_b_glu[0], ssm_w_out[0], ssm_post_norm[0], xp)
```

```python
import functools
import math

import numpy as np
import jax
import jax.numpy as jnp
from jax import lax
from jax.experimental import pallas as pl
from jax.experimental.pallas import tpu as pltpu

F32 = jnp.float32
BF16 = jnp.bfloat16

D_MODEL = 1024
HEAD_DIM = 64
HEADS = 16
N_PHASE = 16
BLK = 128
DILATIONS = (1, 4, 16)
REL_BUCKETS = 32
REL_MAX_DIST = 2048
SSM_GROUP = 16
SSM_N_GROUPS = 64
SSM_STATE = 64
EPS = 1e-6
NEG = -1e30
VMEM_LIMIT = 56 * 1024 * 1024
NORM_CHUNK = 4


def _cparams(sem):
    return pltpu.CompilerParams(dimension_semantics=sem, vmem_limit_bytes=VMEM_LIMIT)


LOG2E = math.log2(math.e)


def _sigmoid(v):
    return 1.0 / (1.0 + jnp.exp2(v * (-LOG2E)))


def _gelu_tanh(x):
    k0 = -2.0 * math.sqrt(2.0 / math.pi) * LOG2E
    return x / (1.0 + jnp.exp2(x * (k0 + (k0 * 0.044715) * (x * x))))


def _norm_proj_kernel(x_hbm, g_ref, w_ref, o_ref, xp_hbm, xbuf, h_ref, gsem, osem, *,
                      scaled_tiles, scale):
    b, j = pl.program_id(0), pl.program_id(1)
    n_b, n_j = pl.num_programs(0), pl.num_programs(1)
    slot = b % 2

    def phase_copy(bb, sl, r):
        return pltpu.make_async_copy(x_hbm.at[bb, :, r, :], xbuf.at[sl, r], gsem.at[sl])

    def gather_start(bb, sl):
        lax.fori_loop(0, N_PHASE, lambda r, c: (phase_copy(bb, sl, r).start(), c)[1], 0)

    def gather_wait(bb, sl):
        lax.fori_loop(0, N_PHASE, lambda r, c: (phase_copy(bb, sl, r).wait(), c)[1], 0)

    def write_out(bb, sl):
        return pltpu.make_async_copy(xbuf.at[sl], xp_hbm.at[bb], osem.at[sl])

    M = xbuf.shape[2]
    w = (w_ref[...] * jnp.where(j < scaled_tiles, scale, 1.0).astype(F32)).astype(BF16)

    @pl.when(j == 0)
    def _():
        @pl.when(b == 0)
        def _():
            gather_start(0, 0)

        gather_wait(b, slot)

        @pl.when(b >= 1)
        def _():
            write_out(b - 1, 1 - slot).wait()

        @pl.when(b + 1 < n_b)
        def _():
            gather_start(b + 1, 1 - slot)

        write_out(b, slot).start()
        for c in range(0, N_PHASE, NORM_CHUNK):
            x = xbuf[slot, c:c + NORM_CHUNK].reshape(NORM_CHUNK * M, -1)
            ms = jnp.mean(x * x, axis=-1, keepdims=True)
            hc = (x * lax.rsqrt(ms + EPS) * g_ref[...]).astype(BF16)
            h_ref[c * M:(c + NORM_CHUNK) * M, :] = hc
            res = jnp.dot(hc, w, preferred_element_type=F32)
            o_ref[c:c + NORM_CHUNK] = res.reshape(NORM_CHUNK, M, -1).astype(BF16)

    @pl.when(j > 0)
    def _():
        res = jnp.dot(h_ref[...], w, preferred_element_type=F32)
        o_ref[...] = res.reshape(o_ref.shape).astype(BF16)

    @pl.when(jnp.logical_and(b == n_b - 1, j == n_j - 1))
    def _():
        write_out(b, slot).wait()


def _norm_proj(x, g, w, *, scaled_cols, scale, tn=1024):
    B, S, D = x.shape
    M = S // N_PHASE
    N = w.shape[1]
    assert scaled_cols % tn == 0
    return pl.pallas_call(
        functools.partial(_norm_proj_kernel, scaled_tiles=scaled_cols // tn, scale=scale),
        grid=(B, N // tn),
        in_specs=[
            pl.BlockSpec(memory_space=pl.ANY),
            pl.BlockSpec((1, D), lambda b, j: (0, 0)),
            pl.BlockSpec((D, tn), lambda b, j: (0, j)),
        ],
        out_specs=[pl.BlockSpec((None, N_PHASE, M, tn), lambda b, j: (b, 0, 0, j)),
                   pl.BlockSpec(memory_space=pl.ANY)],
        out_shape=[jax.ShapeDtypeStruct((B, N_PHASE, M, N), BF16),
                   jax.ShapeDtypeStruct((B, N_PHASE, M, D), F32)],
        scratch_shapes=[pltpu.VMEM((2, N_PHASE, M, D), F32), pltpu.VMEM((S, D), BF16),
                        pltpu.SemaphoreType.DMA((2,)), pltpu.SemaphoreType.DMA((2,))],
        compiler_params=_cparams(("arbitrary", "arbitrary")),
        name="attn_norm_proj",
    )(x.reshape(B, M, N_PHASE, D), g.reshape(1, D), w)


def _attn_kernel(q0_ref, q1_ref, q2_ref, k0_ref, k1_ref, k2_ref, v0_ref, v1_ref, v2_ref,
                 z_ref, bm_ref, o_ref, qf_ref, kf_ref, vf_ref, acc_ref, l_ref, m_ref):
    W = 2 * HEAD_DIM
    AHEAD = 3
    lane = lax.broadcasted_iota(jnp.int32, (BLK, W), 1)
    first_head = lane < HEAD_DIM

    def logits(q, k, g, cur_only):
        zq = jnp.zeros_like(q)
        qs = jnp.concatenate([jnp.where(first_head, q, zq), jnp.where(first_head, zq, q)], axis=0)
        s = lax.dot_general(qs, k, (((1,), (1,)), ((), ())), preferred_element_type=F32)
        return s + (bm_ref[g, :, BLK:2 * BLK] if cur_only else bm_ref[g])

    ones_a = jnp.where(first_head, 1.0, 0.0).astype(BF16)
    ones_b = jnp.where(first_head, 0.0, 1.0).astype(BF16)
    masked_v = {}

    def value_blocks(g, v_ref, pieces, cast):
        key = (g, tuple(pieces))
        if key not in masked_v:
            v = rows_of(v_ref, pieces)
            v = v.astype(BF16) if cast else v
            zv = jnp.zeros_like(v)
            masked_v[key] = (jnp.concatenate([jnp.where(first_head, v, zv), ones_a], axis=1),
                             jnp.concatenate([jnp.where(first_head, zv, v), ones_b], axis=1))
        return masked_v[key]

    def finish(s, vblocks):
        m = jnp.max(s, axis=-1, keepdims=True)
        p = jnp.exp2(s - m).astype(BF16)
        pcat = jnp.concatenate([p[:BLK], p[BLK:]], axis=1)
        rhs = jnp.concatenate([vb[0] for vb in vblocks] + [vb[1] for vb in vblocks], axis=0)
        pv = jnp.dot(pcat, rhs, preferred_element_type=F32)
        mm = jnp.where(first_head, m[:BLK], m[BLK:])
        return pv[:, :W], pv[:, W:], mm

    def rows_of(ref, pieces):
        return jnp.concatenate([ref[ph, lo:lo + n, :] for ph, lo, n in pieces], axis=0)

    def keys_of(ref, prev, cur):
        return rows_of(ref, cur) if prev is None else rows_of(ref, prev + cur)

    def store(g, pieces, vals):
        at = 0
        for ph, lo, n in pieces:
            for ref, val in zip((acc_ref, l_ref, m_ref), vals):
                ref[g, ph, lo:lo + n, :] = val[at:at + n]
            at += n

    M = q0_ref.shape[1]
    g2_blocks = [[(r, 0, BLK)] for r in range(N_PHASE)]
    g1_blocks = {(r4, n): [(4 * q4 + r4, 32 * n, 32) for q4 in range(4)]
                 for r4 in range(4) for n in range(M // 32)}
    g0_blocks = [[(r, 8 * n, 8) for r in range(N_PHASE)] for n in range(M // 8)]

    for r in range(N_PHASE):
        qf_ref[r] = q0_ref[r].astype(F32)
        kf_ref[r] = k0_ref[r].astype(F32)
        vf_ref[r] = v0_ref[r].astype(F32)

    work = []
    for n, pieces in enumerate(g0_blocks):
        work.append((0, pieces, g0_blocks[n - 1] if n else None, (qf_ref, kf_ref, vf_ref), True))
    for (r4, n), pieces in g1_blocks.items():
        work.append((1, pieces, g1_blocks[(r4, n - 1)] if n else None, (q1_ref, k1_ref, v1_ref), False))
    for pieces in g2_blocks:
        work.append((2, pieces, None, (q2_ref, k2_ref, v2_ref), False))

    def merge(r, acc2, l2, m2):
        m0, m1 = m_ref[0, r], m_ref[1, r]
        mx = jnp.maximum(jnp.maximum(m0, m1), m2)
        w0, w1, w2 = jnp.exp2(m0 - mx), jnp.exp2(m1 - mx), jnp.exp2(m2 - mx)
        num = w0 * acc_ref[0, r] + w1 * acc_ref[1, r] + w2 * acc2
        den = w0 * l_ref[0, r] + w1 * l_ref[1, r] + w2 * l2
        z = z_ref[r].astype(F32)
        o_ref[r] = (num * z / (den * (1.0 + jnp.exp2(z * (-LOG2E))))).astype(BF16)

    pending = []
    for item in work + [None] * AHEAD:
        if item is not None:
            g, pieces, prev, (q_r, k_r, _), cast = item
            q, k = rows_of(q_r, pieces), keys_of(k_r, prev, pieces)
            if cast:
                q, k = q.astype(BF16), k.astype(BF16)
            pending.append((item, logits(q, k, g, prev is None)))
        if item is None or len(pending) > AHEAD:
            (g_p, pieces_p, prev_p, (_, _, v_r), cast_p), s_p = pending.pop(0)
            key_blocks = ([] if prev_p is None else [prev_p]) + [pieces_p]
            vals = finish(s_p, [value_blocks(g_p, v_r, kb, cast_p) for kb in key_blocks])
            if g_p == 2:
                merge(pieces_p[0][0], *vals)
            else:
                store(g_p, pieces_p, vals)


def _attention(P, bm):
    B, _, M, _ = P.shape
    HP = HEADS // 2
    W = 2 * HEAD_DIM

    def spec(kind, g):
        base = (kind * 3 + g) * HP
        return pl.BlockSpec((None, N_PHASE, M, W), lambda b, hp, base=base: (b, 0, 0, base + hp))

    in_specs = [spec(kind, g) for kind in range(3) for g in range(3)]
    in_specs.append(pl.BlockSpec((None, N_PHASE, M, W), lambda b, hp: (b, 0, 0, 9 * HP + hp)))
    in_specs.append(pl.BlockSpec((3, None, 2 * BLK, 2 * BLK), lambda b, hp: (0, hp, 0, 0)))
    return pl.pallas_call(
        _attn_kernel,
        grid=(B, HP),
        in_specs=in_specs,
        out_specs=pl.BlockSpec((None, N_PHASE, M, W), lambda b, hp: (b, 0, 0, hp)),
        out_shape=jax.ShapeDtypeStruct((B, N_PHASE, M, HEADS * HEAD_DIM), BF16),
        scratch_shapes=[pltpu.VMEM((N_PHASE, M, W), F32) for _ in range(3)]
        + [pltpu.VMEM((2, N_PHASE, M, W), F32) for _ in range(3)],
        compiler_params=_cparams(("parallel", "parallel")),
        name="dilated_attention",
    )(*([P] * 10), bm)


def _t5_bucket(dist):
    max_exact = REL_BUCKETS // 2
    n = jnp.maximum(dist, 1).astype(F32)
    large = max_exact + (jnp.log(n / max_exact) / math.log(REL_MAX_DIST / max_exact)
                         * (REL_BUCKETS - max_exact)).astype(jnp.int32)
    large = jnp.minimum(large, REL_BUCKETS - 1)
    return jnp.where(dist < max_exact, dist, large)


def _bias_tables(rel_bias):
    a = np.arange(BLK)
    pos = (16 * (a % 8) + a // 8, 4 * (a % 32) + a // 32, a)
    back = np.stack([np.concatenate([BLK + p[:, None] - p[None, :], p[:, None] - p[None, :]], axis=1)
                     for p in pos])
    valid = (back >= 0) & (back <= BLK)
    dist = np.clip(back, 0, BLK) * np.asarray(DILATIONS)[:, None, None]
    bucket = jnp.where(jnp.asarray(valid), _t5_bucket(jnp.asarray(dist, jnp.int32)), REL_BUCKETS)
    onehot = (bucket[..., None] == jnp.arange(REL_BUCKETS + 1)).astype(F32)
    ext = jnp.concatenate([rel_bias.astype(F32), jnp.full((1, HEADS), NEG, F32)], axis=0)
    t = jnp.einsum("gijc,ch->ghij", onehot, ext * math.log2(math.e),
                   precision=lax.Precision.HIGHEST)
    return t.reshape(3, HEADS // 2, 2 * BLK, 2 * BLK)


def _out_proj_kernel(o_ref, w_ref, g_ref, x_ref, *rest, mh):
    prm_in, y_ref, prm_out = rest[:11], rest[11], rest[12:]
    n_pair = prm_out[-1].shape[0]
    o = o_ref[...].reshape(N_PHASE * mh, o_ref.shape[-1])
    h = jnp.dot(o, w_ref[...].astype(BF16), preferred_element_type=F32)
    for k in range(n_pair):
        _ssm_param_kernel(*[r.at[2 * k:2 * k + 2] if r.shape[0] == 2 * n_pair else r.at[k]
                            for r in prm_in + prm_out])
    ms = jnp.mean(h * h, axis=-1, keepdims=True)
    y = h * lax.rsqrt(ms + EPS) * g_ref[...]
    y_ref[...] = x_ref[...] + y.reshape(y_ref.shape)


def _out_proj(O, w, g, xp, a_re, a_im, log_dt, b_re, b_im, c_re, c_im, d_skip, *, mh=64):
    B, _, M, D = xp.shape
    G, P, C, Q = SSM_N_GROUPS, SSM_STATE, SSM_GROUP, N_PHASE
    QC = Q * C
    nm = M // mh
    n_pair = (G // 2) // (B * nm)
    assert B * nm * n_pair == G // 2
    f = lambda t: t.astype(F32)
    a_re, a_im, log_dt, b_re, b_im, c_re, c_im = map(f, (a_re, a_im, log_dt, b_re, b_im, c_re, c_im))
    even = (jnp.arange(G) % 2 == 0)[:, None, None]

    def lane_half(c):
        return jnp.concatenate([jnp.where(even, c, 0.0), jnp.where(even, 0.0, c)], axis=-1)

    pair = lambda blk: pl.BlockSpec((2 * n_pair,) + blk, lambda b, m: (b * nm + m,) + (0,) * len(blk))
    one = lambda blk: pl.BlockSpec((n_pair,) + blk, lambda b, m: (b * nm + m,) + (0,) * len(blk))
    outs = pl.pallas_call(
        functools.partial(_out_proj_kernel, mh=mh),
        grid=(B, nm),
        in_specs=[
            pl.BlockSpec((None, N_PHASE, mh, O.shape[-1]), lambda b, m: (b, 0, m, 0)),
            pl.BlockSpec(w.shape, lambda b, m: (0, 0)),
            pl.BlockSpec((1, D), lambda b, m: (0, 0)),
            pl.BlockSpec((None, N_PHASE, mh, D), lambda b, m: (b, 0, m, 0)),
            pair((P, 1)), pair((P, 1)), one((1, 2 * P)), one((1, 2 * P)), one((1, 2 * P)),
            pair((P, C)), pair((P, C)), pair((C, P)), pair((C, P)), pair((C, 2 * P)), pair((C, 2 * P)),
        ],
        out_specs=[pl.BlockSpec((None, N_PHASE, mh, D), lambda b, m: (b, 0, m, 0)),
                   pair((QC, QC)), pair((P, QC)), pair((P, QC)), one((2 * QC, 2 * P)),
                   one((2 * QC, 2 * P)), one((1, 2 * P)), one((1, 2 * P))],
        out_shape=[jax.ShapeDtypeStruct(xp.shape, F32),
                   jax.ShapeDtypeStruct((G, QC, QC), BF16), jax.ShapeDtypeStruct((G, P, QC), BF16),
                   jax.ShapeDtypeStruct((G, P, QC), BF16),
                   jax.ShapeDtypeStruct((G // 2, 2 * QC, 2 * P), BF16),
                   jax.ShapeDtypeStruct((G // 2, 2 * QC, 2 * P), BF16),
                   jax.ShapeDtypeStruct((G // 2, 1, 2 * P), F32),
                   jax.ShapeDtypeStruct((G // 2, 1, 2 * P), F32)],
        compiler_params=_cparams(("parallel", "parallel")),
        name="attn_out_proj",
    )(O, w, g.reshape(1, D), xp,
      a_re.reshape(G, P, 1), a_im.reshape(G, P, 1),
      a_re.reshape(G // 2, 1, 2 * P), a_im.reshape(G // 2, 1, 2 * P),
      jnp.repeat(log_dt, P).reshape(G // 2, 1, 2 * P), b_re, b_im, c_re, c_im,
      lane_half(c_re), lane_half(c_im))
    keys = ("m", "b_re", "b_im", "c_re", "c_im", "a_re", "a_im")
    prm = dict(zip(keys, outs[1:]))
    prm["d"] = d_skip.astype(F32).reshape(G * C, 1)
    return outs[0], prm


def _ssm_in_proj_kernel(x_ref, g_ref, w_ref, o_ref, h_ref):
    M = x_ref.shape[1]
    nt = (((1,), (1,)), ((), ()))
    wt = w_ref[...].T.astype(BF16)

    @pl.when(pl.program_id(1) == 0)
    def _():
        for c in range(0, N_PHASE, NORM_CHUNK):
            x = x_ref[c:c + NORM_CHUNK].reshape(NORM_CHUNK * M, -1)
            ms = jnp.mean(x * x, axis=-1, keepdims=True)
            hc = (x * lax.rsqrt(ms + EPS) * g_ref[...]).astype(BF16)
            h_ref[c * M:(c + NORM_CHUNK) * M, :] = hc
            res = lax.dot_general(wt, hc, nt, preferred_element_type=F32)
            for s in range(NORM_CHUNK):
                o_ref[c + s] = res[:, s * M:(s + 1) * M].astype(BF16)

    @pl.when(pl.program_id(1) > 0)
    def _():
        res = lax.dot_general(wt, h_ref[...], nt, preferred_element_type=F32)
        for s in range(N_PHASE):
            o_ref[s] = res[:, s * M:(s + 1) * M].astype(BF16)


def _ssm_in_proj(xp, g, w, *, tn=1024):
    B, _, M, D = xp.shape
    S = N_PHASE * M
    N = w.shape[1]
    return pl.pallas_call(
        _ssm_in_proj_kernel,
        grid=(B, N // tn),
        in_specs=[
            pl.BlockSpec((None, N_PHASE, M, D), lambda b, j: (b, 0, 0, 0)),
            pl.BlockSpec((1, D), lambda b, j: (0, 0)),
            pl.BlockSpec((D, tn), lambda b, j: (0, j)),
        ],
        out_specs=pl.BlockSpec((None, N_PHASE, tn, M), lambda b, j: (b, 0, j, 0)),
        out_shape=jax.ShapeDtypeStruct((B, N_PHASE, N, M), BF16),
        scratch_shapes=[pltpu.VMEM((S, D), BF16)],
        compiler_params=_cparams(("parallel", "arbitrary")),
        name="ssm_in_proj",
    )(xp, g.reshape(1, D), w)


def _ssm_core_kernel(u_ref, m_ref, bre_ref, bim_ref, cre_ref, cim_ref, are_ref, aim_ref,
                     d_ref, y_ref, sre_ref, sim_ref, ym_ref, *, n_chunk, n_pair):
    C = SSM_GROUP
    QC = N_PHASE * C
    nb = u_ref.shape[0]
    N = nb * n_chunk
    n_grp = 2 * n_pair

    def grp(i, b):
        return u_ref[b, :, i * C:(i + 1) * C, :]

    us = [jnp.concatenate([grp(i, b).reshape(QC, n_chunk) for b in range(nb)], axis=1)
          for i in range(n_grp)]

    P = bre_ref.shape[1]
    xre, xim = [], []
    for i in range(n_grp):
        lhs = jnp.concatenate([m_ref[i], bre_ref[i], bim_ref[i]], axis=0)
        r = jnp.dot(lhs, us[i], preferred_element_type=F32)
        ym_ref[i] = r[:QC]
        xre.append(r[QC:QC + P])
        xim.append(r[QC + P:])
    for k in range(n_pair):
        sre_ref[k] = jnp.concatenate(xre[2 * k:2 * k + 2], axis=0).T
        sim_ref[k] = jnp.concatenate(xim[2 * k:2 * k + 2], axis=0).T

    sub = lax.broadcasted_iota(jnp.int32, (8, are_ref.shape[-1]), 0)

    def cmul(xr, xi, yr, yi):
        return xr * yr - xi * yi, xr * yi + xi * yr

    for k in range(n_pair):
        a1 = (are_ref[k], aim_ref[k])
        a2 = cmul(*a1, *a1)
        a4 = cmul(*a2, *a2)
        pows = [a1, a2, cmul(*a2, *a1), a4, cmul(*a4, *a1), cmul(*a4, *a2)]
        pows += [cmul(*pows[5], *a1), cmul(*a4, *a4)]
        apow_r = jnp.zeros(sub.shape, F32)
        apow_i = jnp.zeros(sub.shape, F32)
        for i, (pr, pi) in enumerate(pows):
            apow_r = jnp.where(sub == i, pr, apow_r)
            apow_i = jnp.where(sub == i, pi, apow_i)
        step = {d: (jnp.where(sub >= d, pows[d - 1][0], 0.0), jnp.where(sub >= d, pows[d - 1][1], 0.0))
                for d in (1, 2, 4)}
        carry = [(jnp.zeros(sub.shape, F32), jnp.zeros(sub.shape, F32)) for _ in range(nb)]
        for j in range(n_chunk // 8):
            for b in range(nb):
                rows = slice(b * n_chunk + 8 * j, b * n_chunk + 8 * j + 8)
                er, ei = sre_ref[k, rows, :], sim_ref[k, rows, :]
                for d in (1, 2, 4):
                    dr, di = cmul(*step[d], pltpu.roll(er, d, axis=0), pltpu.roll(ei, d, axis=0))
                    er, ei = er + dr, ei + di
                cr, ci = carry[b]
                dr, di = cmul(apow_r, apow_i, cr, ci)
                fr, fi = er + dr, ei + di
                sre_ref[k, rows, :] = jnp.where(sub == 0, cr, pltpu.roll(fr, 1, axis=0))
                sim_ref[k, rows, :] = jnp.where(sub == 0, ci, pltpu.roll(fi, 1, axis=0))
                carry[b] = (jnp.broadcast_to(fr[7:8], sub.shape), jnp.broadcast_to(fi[7:8], sub.shape))

    nt = (((1,), (1,)), ((), ()))
    for k in range(n_pair):
        inter = (lax.dot_general(cre_ref[k], sre_ref[k].astype(BF16), nt, preferred_element_type=F32)
                 + lax.dot_general(cim_ref[k], sim_ref[k].astype(BF16), nt,
                                   preferred_element_type=F32))
        for i in (2 * k, 2 * k + 1):
            y = ym_ref[i] + inter[(i % 2) * QC:(i % 2 + 1) * QC]
            for b in range(nb):
                yb = y[:, b * n_chunk:(b + 1) * n_chunk].reshape(N_PHASE, C, n_chunk)
                yb = yb + d_ref[i * C:(i + 1) * C] * grp(i, b).astype(F32)
                y_ref[b, :, i * C:(i + 1) * C, :] = yb.astype(BF16)


def _ssm_core(UT, prm, *, n_pair=2):
    B, _, _, n_chunk = UT.shape
    N = B * n_chunk
    G2 = SSM_N_GROUPS // 2
    C2 = 2 * SSM_GROUP
    QC = N_PHASE * SSM_GROUP
    P = SSM_STATE
    blk = lambda *shape: pl.BlockSpec(shape, lambda g: (g,) + (0,) * (len(shape) - 1))
    return pl.pallas_call(
        functools.partial(_ssm_core_kernel, n_chunk=n_chunk, n_pair=n_pair),
        grid=(G2 // n_pair,),
        in_specs=[
            pl.BlockSpec((B, N_PHASE, n_pair * C2, n_chunk), lambda g: (0, 0, g, 0)),
            blk(2 * n_pair, QC, QC), blk(2 * n_pair, P, QC), blk(2 * n_pair, P, QC),
            blk(n_pair, 2 * QC, 2 * P), blk(n_pair, 2 * QC, 2 * P),
            blk(n_pair, 1, 2 * P), blk(n_pair, 1, 2 * P),
            blk(n_pair * C2, 1),
        ],
        out_specs=pl.BlockSpec((B, N_PHASE, n_pair * C2, n_chunk), lambda g: (0, 0, g, 0)),
        out_shape=jax.ShapeDtypeStruct((B, N_PHASE, SSM_N_GROUPS * SSM_GROUP, n_chunk), BF16),
        scratch_shapes=[pltpu.VMEM((n_pair, N, 2 * P), F32), pltpu.VMEM((n_pair, N, 2 * P), F32),
                        pltpu.VMEM((2 * n_pair, QC, N), F32)],
        compiler_params=_cparams(("parallel",)),
        name="ssm_core",
    )(UT, prm["m"], prm["b_re"], prm["b_im"], prm["c_re"], prm["c_im"],
      prm["a_re"], prm["a_im"], prm["d"])


def _ssm_param_kernel(arc_ref, aic_ref, arr_ref, air_ref, ldr_ref, bre_ref, bim_ref,
                      cre_ref, cim_ref, cpr_ref, cpi_ref,
                      m_ref, br_ref, bi_ref, cr_ref, ci_ref, ar_ref, ai_ref):
    Q, C, P = N_PHASE, SSM_GROUP, SSM_STATE
    QC = Q * C
    hi = lax.Precision.HIGHEST
    tile = (lax.broadcasted_iota(jnp.int32, (C, QC), 1) % C
            == lax.broadcasted_iota(jnp.int32, (C, QC), 0)).astype(F32)

    dt = jnp.exp(ldr_ref[...])
    mag = jnp.exp(arr_ref[...] * dt)
    a1r, a1i = mag * jnp.cos(air_ref[...] * dt), mag * jnp.sin(air_ref[...] * dt)
    a1r_c = jnp.broadcast_to(a1r, (8, 2 * P)).T[:, :1]
    a1i_c = jnp.broadcast_to(a1i, (8, 2 * P)).T[:, :1]

    pows = [(jnp.ones_like(a1r), jnp.zeros_like(a1r))]
    for _ in range(Q):
        qr, qi = pows[-1]
        pows.append((qr * a1r - qi * a1i, qr * a1i + qi * a1r))
    pw_r = jnp.concatenate([jnp.broadcast_to(pows[Q - 1 - s][0], (C, 2 * P)) for s in range(Q)], axis=0).T
    pw_i = jnp.concatenate([jnp.broadcast_to(pows[Q - 1 - s][1], (C, 2 * P)) for s in range(Q)], axis=0).T

    for i in range(2):
        a_re, a_im = arc_ref[i], aic_ref[i]
        abr, abi = a1r_c[i * P:(i + 1) * P], a1i_c[i * P:(i + 1) * P]
        den = a_re * a_re + a_im * a_im
        cfr = ((abr - 1.0) * a_re + abi * a_im) / den
        cfi = (abi * a_re - (abr - 1.0) * a_im) / den
        bbr = cfr * bre_ref[i] - cfi * bim_ref[i]
        bbi = cfr * bim_ref[i] + cfi * bre_ref[i]

        bbr_t = jnp.dot(bbr, tile, precision=hi, preferred_element_type=F32)
        bbi_t = jnp.dot(bbi, tile, precision=hi, preferred_element_type=F32)
        pr, pi = pw_r[i * P:(i + 1) * P], pw_i[i * P:(i + 1) * P]
        bst_r = pr * bbr_t - pi * bbi_t
        bst_i = pr * bbi_t + pi * bbr_t
        br_ref[i] = bst_r.astype(BF16)
        bi_ref[i] = bst_i.astype(BF16)

        kl = (jnp.dot(cre_ref[i], bst_r, precision=hi, preferred_element_type=F32)
              - jnp.dot(cim_ref[i], bst_i, precision=hi, preferred_element_type=F32))
        klz = jnp.concatenate([kl, jnp.zeros_like(kl)], axis=1)
        for t in range(Q):
            sh = (2 * QC - C * (Q - 1 - t)) % (2 * QC)
            row = klz if sh == 0 else pltpu.roll(klz, sh, axis=1)
            m_ref[i, t * C:(t + 1) * C, :] = row[:, :QC].astype(BF16)

    for t in range(Q):
        qr, qi = pows[t + 1]
        for i in range(2):
            rows = slice(i * QC + t * C, i * QC + (t + 1) * C)
            cr_ref[rows, :] = (cpr_ref[i] * qr - cpi_ref[i] * qi).astype(BF16)
            ci_ref[rows, :] = (-(cpr_ref[i] * qi + cpi_ref[i] * qr)).astype(BF16)
    ar_ref[...], ai_ref[...] = pows[Q]


def _ssm_out_kernel(y_ref, z_ref, wg_ref, bg_ref, wo_ref, gp_ref, x_ref, o_hbm, obuf, sem,
                    wgt_ref, wot_ref, *, nb, n_sub):
    s, j = pl.program_id(0), pl.program_id(1)
    nj = pl.num_programs(1)
    step = s * nj + j
    last = pl.num_programs(0) * nj - 1
    slot = step % 2

    def out_copy(slot_, s_, j_):
        return pltpu.make_async_copy(obuf.at[slot_], o_hbm.at[pl.ds(j_ * nb, nb), :, s_, :],
                                     sem.at[slot_])

    @pl.when(step >= 2)
    def _():
        out_copy(slot, (step - 2) // nj, (step - 2) % nj).wait()

    @pl.when(step == 0)
    def _():
        wgt_ref[...] = wg_ref[...].T.astype(BF16)
        wot_ref[...] = wo_ref[...].T.astype(BF16)

    bs = nb // n_sub

    def lanes(ref, k):
        return jnp.concatenate([ref[b] for b in range(k * bs, (k + 1) * bs)], axis=1).astype(F32)

    def glu_in(k):
        g = _gelu_tanh(lanes(y_ref, k))
        return g, jnp.dot(wgt_ref[...], g.astype(BF16), preferred_element_type=F32)

    def proj(k, g, gl):
        y2 = g * _sigmoid(gl + bg_ref[...])
        z = lanes(z_ref, k)
        gated = (y2 * (z * _sigmoid(z))).astype(BF16)
        return jnp.dot(wot_ref[...], gated, preferred_element_type=F32)

    def finish(k, ot):
        ms = jnp.mean(ot * ot, axis=0, keepdims=True)
        nt = (ot * lax.rsqrt(ms + EPS)).T
        rows = slice(k * bs, (k + 1) * bs)
        obuf[slot, rows] = x_ref[rows] + (nt * gp_ref[...]).reshape(bs, -1, nt.shape[-1])

    a = [glu_in(k) for k in range(n_sub)]
    o = [proj(k, *a[k]) for k in range(n_sub)]
    for k in range(n_sub):
        finish(k, o[k])
    out_copy(slot, s, j).start()

    @pl.when(step == last)
    def _():
        out_copy(slot, s, j).wait()

    @pl.when(jnp.logical_and(step == last, last >= 1))
    def _():
        out_copy(1 - slot, (step - 1) // nj, (step - 1) % nj).wait()


def _ssm_out(YT, UT, wg, bg, wo, gp, xp, *, nb=8, n_sub=2):
    B, _, M, D = xp.shape
    E = YT.shape[2]
    out = pl.pallas_call(
        functools.partial(_ssm_out_kernel, nb=nb, n_sub=n_sub),
        grid=(N_PHASE, B // nb),
        in_specs=[
            pl.BlockSpec((nb, None, E, M), lambda s, j: (j, s, 0, 0)),
            pl.BlockSpec((nb, None, E, M), lambda s, j: (j, s, 1, 0)),
            pl.BlockSpec((E, E), lambda s, j: (0, 0)),
            pl.BlockSpec((E, 1), lambda s, j: (0, 0)),
            pl.BlockSpec((E, D), lambda s, j: (0, 0)),
            pl.BlockSpec((1, D), lambda s, j: (0, 0)),
            pl.BlockSpec((nb, None, M, D), lambda s, j: (j, s, 0, 0)),
        ],
        out_specs=pl.BlockSpec(memory_space=pl.ANY),
        out_shape=jax.ShapeDtypeStruct((B, M, N_PHASE, D), F32),
        scratch_shapes=[pltpu.VMEM((2, nb, M, D), F32), pltpu.SemaphoreType.DMA((2,)),
                        pltpu.VMEM((E, E), BF16), pltpu.VMEM((D, E), BF16)],
        compiler_params=_cparams(("arbitrary", "arbitrary")),
        name="ssm_out",
    )(YT, UT, wg, bg.reshape(E, 1), wo, gp.reshape(1, D), xp)
    return out.reshape(B, M * N_PHASE, D)


def kernel(x, rel_bias, attn_pre_norm, attn_w_in, attn_w_out, attn_post_norm, ssm_pre_norm, ssm_w_in, ssm_a_re, ssm_a_im, ssm_log_dt, ssm_b_re, ssm_b_im, ssm_c_re, ssm_c_im, ssm_d, ssm_w_glu, ssm_b_glu, ssm_w_out, ssm_post_norm):
    B, S, D = x.shape
    n_chunk = S // N_PHASE

    P, xp = _norm_proj(x, attn_pre_norm[0], attn_w_in[0], scaled_cols=3 * HEADS * HEAD_DIM,
                       scale=HEAD_DIM ** -0.5 * LOG2E)
    O = _attention(P, _bias_tables(rel_bias))
    xp, prm = _out_proj(O, attn_w_out[0], attn_post_norm[0], xp, ssm_a_re[0], ssm_a_im[0],
                        ssm_log_dt[0], ssm_b_re[0], ssm_b_im[0], ssm_c_re[0], ssm_c_im[0], ssm_d[0])

    UT = _ssm_in_proj(xp, ssm_pre_norm[0], ssm_w_in[0])
    YT = _ssm_core(UT, prm)
    return _ssm_out(YT, UT, ssm_w_glu[0], ssm_b_glu[0], ssm_w_out[0], ssm_post_norm[0], xp)
```

```python
import functools
import math

import numpy as np
import jax
import jax.numpy as jnp
from jax import lax
from jax.experimental import pallas as pl
from jax.experimental.pallas import tpu as pltpu

F32 = jnp.float32
BF16 = jnp.bfloat16

D_MODEL = 1024
HEAD_DIM = 64
HEADS = 16
N_PHASE = 16
BLK = 128
DILATIONS = (1, 4, 16)
REL_BUCKETS = 32
REL_MAX_DIST = 2048
SSM_GROUP = 16
SSM_N_GROUPS = 64
SSM_STATE = 64
EPS = 1e-6
NEG = -1e30
VMEM_LIMIT = 56 * 1024 * 1024
NORM_CHUNK = 4


def _cparams(sem):
    return pltpu.CompilerParams(dimension_semantics=sem, vmem_limit_bytes=VMEM_LIMIT)


LOG2E = math.log2(math.e)


def _sigmoid(v):
    return 1.0 / (1.0 + jnp.exp2(v * (-LOG2E)))


def _gelu_tanh(x):
    k0 = -2.0 * math.sqrt(2.0 / math.pi) * LOG2E
    return x / (1.0 + jnp.exp2(x * (k0 + (k0 * 0.044715) * (x * x))))


def _norm_proj_kernel(x_hbm, g_ref, w_ref, o_ref, xp_hbm, xbuf, h_ref, gsem, osem, *,
                      scaled_tiles, scale):
    b, j = pl.program_id(0), pl.program_id(1)
    n_b, n_j = pl.num_programs(0), pl.num_programs(1)
    slot = b % 2

    def phase_copy(bb, sl, r):
        return pltpu.make_async_copy(x_hbm.at[bb, :, r, :], xbuf.at[sl, r], gsem.at[sl])

    def gather_start(bb, sl):
        lax.fori_loop(0, N_PHASE, lambda r, c: (phase_copy(bb, sl, r).start(), c)[1], 0)

    def gather_wait(bb, sl):
        lax.fori_loop(0, N_PHASE, lambda r, c: (phase_copy(bb, sl, r).wait(), c)[1], 0)

    def write_out(bb, sl):
        return pltpu.make_async_copy(xbuf.at[sl], xp_hbm.at[bb], osem.at[sl])

    M = xbuf.shape[2]

    def weights():
        return (w_ref[...] * jnp.where(j < scaled_tiles, scale, 1.0).astype(F32)).astype(BF16)

    @pl.when(j == 0)
    def _():
        @pl.when(b == 0)
        def _():
            gather_start(0, 0)

        gather_wait(b, slot)

        @pl.when(b >= 1)
        def _():
            write_out(b - 1, 1 - slot).wait()

        @pl.when(b + 1 < n_b)
        def _():
            gather_start(b + 1, 1 - slot)

        write_out(b, slot).start()
        w = weights()
        for c in range(0, N_PHASE, NORM_CHUNK):
            x = xbuf[slot, c:c + NORM_CHUNK].reshape(NORM_CHUNK * M, -1)
            ms = jnp.mean(x * x, axis=-1, keepdims=True)
            hc = (x * lax.rsqrt(ms + EPS) * g_ref[...]).astype(BF16)
            h_ref[c * M:(c + NORM_CHUNK) * M, :] = hc
            res = jnp.dot(hc, w, preferred_element_type=F32)
            o_ref[c:c + NORM_CHUNK] = res.reshape(NORM_CHUNK, M, -1).astype(BF16)

    @pl.when(j > 0)
    def _():
        res = jnp.dot(h_ref[...], weights(), preferred_element_type=F32)
        o_ref[...] = res.reshape(o_ref.shape).astype(BF16)

    @pl.when(jnp.logical_and(b == n_b - 1, j == n_j - 1))
    def _():
        write_out(b, slot).wait()


def _norm_proj(x, g, w, *, scaled_cols, scale, tn=1024):
    B, S, D = x.shape
    M = S // N_PHASE
    N = w.shape[1]
    assert scaled_cols % tn == 0
    return pl.pallas_call(
        functools.partial(_norm_proj_kernel, scaled_tiles=scaled_cols // tn, scale=scale),
        grid=(B, N // tn),
        in_specs=[
            pl.BlockSpec(memory_space=pl.ANY),
            pl.BlockSpec((1, D), lambda b, j: (0, 0)),
            pl.BlockSpec((D, tn), lambda b, j: (0, j)),
        ],
        out_specs=[pl.BlockSpec((None, N_PHASE, M, tn), lambda b, j: (b, 0, 0, j)),
                   pl.BlockSpec(memory_space=pl.ANY)],
        out_shape=[jax.ShapeDtypeStruct((B, N_PHASE, M, N), BF16),
                   jax.ShapeDtypeStruct((B, N_PHASE, M, D), F32)],
        scratch_shapes=[pltpu.VMEM((2, N_PHASE, M, D), F32), pltpu.VMEM((S, D), BF16),
                        pltpu.SemaphoreType.DMA((2,)), pltpu.SemaphoreType.DMA((2,))],
        compiler_params=_cparams(("arbitrary", "arbitrary")),
        name="attn_norm_proj",
    )(x.reshape(B, M, N_PHASE, D), g.reshape(1, D), w)


def _attn_kernel(q0_ref, q1_ref, q2_ref, k0_ref, k1_ref, k2_ref, v0_ref, v1_ref, v2_ref,
                 z_ref, bm_ref, o_ref, qf_ref, kf_ref, vf_ref, acc_ref, l_ref, m_ref):
    W = 2 * HEAD_DIM
    AHEAD = 3
    lane = lax.broadcasted_iota(jnp.int32, (BLK, W), 1)
    first_head = lane < HEAD_DIM

    def logits(q, k, g, cur_only):
        zq = jnp.zeros_like(q)
        qs = jnp.concatenate([jnp.where(first_head, q, zq), jnp.where(first_head, zq, q)], axis=0)
        s = lax.dot_general(qs, k, (((1,), (1,)), ((), ())), preferred_element_type=F32)
        return s + (bm_ref[g, :, BLK:2 * BLK] if cur_only else bm_ref[g])

    ones_a = jnp.where(first_head, 1.0, 0.0).astype(BF16)
    ones_b = jnp.where(first_head, 0.0, 1.0).astype(BF16)
    masked_v = {}

    def value_blocks(g, v_ref, pieces, cast):
        key = (g, tuple(pieces))
        if key not in masked_v:
            v = rows_of(v_ref, pieces)
            v = v.astype(BF16) if cast else v
            zv = jnp.zeros_like(v)
            masked_v[key] = (jnp.concatenate([jnp.where(first_head, v, zv), ones_a], axis=1),
                             jnp.concatenate([jnp.where(first_head, zv, v), ones_b], axis=1))
        return masked_v[key]

    def finish(s, vblocks):
        m = jnp.max(s, axis=-1, keepdims=True)
        p = jnp.exp2(s - m).astype(BF16)
        pcat = jnp.concatenate([p[:BLK], p[BLK:]], axis=1)
        rhs = jnp.concatenate([vb[0] for vb in vblocks] + [vb[1] for vb in vblocks], axis=0)
        pv = jnp.dot(pcat, rhs, preferred_element_type=F32)
        mm = jnp.where(first_head, m[:BLK], m[BLK:])
        return pv[:, :W], pv[:, W:], mm

    def rows_of(ref, pieces):
        return jnp.concatenate([ref[ph, lo:lo + n, :] for ph, lo, n in pieces], axis=0)

    def keys_of(ref, prev, cur):
        return rows_of(ref, cur) if prev is None else rows_of(ref, prev + cur)

    def store(g, pieces, vals):
        at = 0
        for ph, lo, n in pieces:
            for ref, val in zip((acc_ref, l_ref, m_ref), vals):
                ref[g, ph, lo:lo + n, :] = val[at:at + n]
            at += n

    M = q0_ref.shape[1]
    g2_blocks = [[(r, 0, BLK)] for r in range(N_PHASE)]
    g1_blocks = {(r4, n): [(4 * q4 + r4, 32 * n, 32) for q4 in range(4)]
                 for r4 in range(4) for n in range(M // 32)}
    g0_blocks = [[(r, 8 * n, 8) for r in range(N_PHASE)] for n in range(M // 8)]

    for r in range(N_PHASE):
        qf_ref[r] = q0_ref[r].astype(F32)
        kf_ref[r] = k0_ref[r].astype(F32)
        vf_ref[r] = v0_ref[r].astype(F32)

    work = []
    for n, pieces in enumerate(g0_blocks):
        work.append((0, pieces, g0_blocks[n - 1] if n else None, (qf_ref, kf_ref, vf_ref), True))
    for (r4, n), pieces in g1_blocks.items():
        work.append((1, pieces, g1_blocks[(r4, n - 1)] if n else None, (q1_ref, k1_ref, v1_ref), False))
    for pieces in g2_blocks:
        work.append((2, pieces, None, (q2_ref, k2_ref, v2_ref), False))

    def merge(r, acc2, l2, m2):
        m0, m1 = m_ref[0, r], m_ref[1, r]
        mx = jnp.maximum(jnp.maximum(m0, m1), m2)
        w0, w1, w2 = jnp.exp2(m0 - mx), jnp.exp2(m1 - mx), jnp.exp2(m2 - mx)
        num = w0 * acc_ref[0, r] + w1 * acc_ref[1, r] + w2 * acc2
        den = w0 * l_ref[0, r] + w1 * l_ref[1, r] + w2 * l2
        z = z_ref[r].astype(F32)
        o_ref[r] = (num * z / (den * (1.0 + jnp.exp2(z * (-LOG2E))))).astype(BF16)

    pending = []
    for item in work + [None] * AHEAD:
        if item is not None:
            g, pieces, prev, (q_r, k_r, _), cast = item
            q, k = rows_of(q_r, pieces), keys_of(k_r, prev, pieces)
            if cast:
                q, k = q.astype(BF16), k.astype(BF16)
            pending.append((item, logits(q, k, g, prev is None)))
        if item is None or len(pending) > AHEAD:
            (g_p, pieces_p, prev_p, (_, _, v_r), cast_p), s_p = pending.pop(0)
            key_blocks = ([] if prev_p is None else [prev_p]) + [pieces_p]
            vals = finish(s_p, [value_blocks(g_p, v_r, kb, cast_p) for kb in key_blocks])
            if g_p == 2:
                merge(pieces_p[0][0], *vals)
            else:
                store(g_p, pieces_p, vals)


def _attention(P, bm):
    B, _, M, _ = P.shape
    HP = HEADS // 2
    W = 2 * HEAD_DIM

    def spec(kind, g):
        base = (kind * 3 + g) * HP
        return pl.BlockSpec((None, N_PHASE, M, W), lambda b, hp, base=base: (b, 0, 0, base + hp))

    in_specs = [spec(kind, g) for kind in range(3) for g in range(3)]
    in_specs.append(pl.BlockSpec((None, N_PHASE, M, W), lambda b, hp: (b, 0, 0, 9 * HP + hp)))
    in_specs.append(pl.BlockSpec((3, None, 2 * BLK, 2 * BLK), lambda b, hp: (0, hp, 0, 0)))
    return pl.pallas_call(
        _attn_kernel,
        grid=(B, HP),
        in_specs=in_specs,
        out_specs=pl.BlockSpec((None, N_PHASE, M, W), lambda b, hp: (b, 0, 0, hp)),
        out_shape=jax.ShapeDtypeStruct((B, N_PHASE, M, HEADS * HEAD_DIM), BF16),
        scratch_shapes=[pltpu.VMEM((N_PHASE, M, W), F32) for _ in range(3)]
        + [pltpu.VMEM((2, N_PHASE, M, W), F32) for _ in range(3)],
        compiler_params=_cparams(("parallel", "parallel")),
        name="dilated_attention",
    )(*([P] * 10), bm)


def _t5_bucket(dist):
    max_exact = REL_BUCKETS // 2
    n = jnp.maximum(dist, 1).astype(F32)
    large = max_exact + (jnp.log(n / max_exact) / math.log(REL_MAX_DIST / max_exact)
                         * (REL_BUCKETS - max_exact)).astype(jnp.int32)
    large = jnp.minimum(large, REL_BUCKETS - 1)
    return jnp.where(dist < max_exact, dist, large)


def _bias_tables(rel_bias):
    a = np.arange(BLK)
    pos = (16 * (a % 8) + a // 8, 4 * (a % 32) + a // 32, a)
    back = np.stack([np.concatenate([BLK + p[:, None] - p[None, :], p[:, None] - p[None, :]], axis=1)
                     for p in pos])
    valid = (back >= 0) & (back <= BLK)
    dist = np.clip(back, 0, BLK) * np.asarray(DILATIONS)[:, None, None]
    bucket = jnp.where(jnp.asarray(valid), _t5_bucket(jnp.asarray(dist, jnp.int32)), REL_BUCKETS)
    onehot = (bucket[..., None] == jnp.arange(REL_BUCKETS + 1)).astype(F32)
    ext = jnp.concatenate([rel_bias.astype(F32), jnp.full((1, HEADS), NEG, F32)], axis=0)
    t = jnp.einsum("gijc,ch->ghij", onehot, ext * math.log2(math.e),
                   precision=lax.Precision.HIGHEST)
    return t.reshape(3, HEADS // 2, 2 * BLK, 2 * BLK)


def _out_proj_kernel(o_ref, w_ref, g_ref, x_ref, *rest, mh):
    prm_in, y_ref, prm_out = rest[:11], rest[11], rest[12:]
    n_pair = prm_out[-1].shape[0]
    o = o_ref[...].reshape(N_PHASE * mh, o_ref.shape[-1])
    h = jnp.dot(o, w_ref[...].astype(BF16), preferred_element_type=F32)
    for k in range(n_pair):
        _ssm_param_kernel(*[r.at[2 * k:2 * k + 2] if r.shape[0] == 2 * n_pair else r.at[k]
                            for r in prm_in + prm_out])
    ms = jnp.mean(h * h, axis=-1, keepdims=True)
    y = h * lax.rsqrt(ms + EPS) * g_ref[...]
    y_ref[...] = x_ref[...] + y.reshape(y_ref.shape)


def _out_proj(O, w, g, xp, a_re, a_im, log_dt, b_re, b_im, c_re, c_im, d_skip, *, mh=64):
    B, _, M, D = xp.shape
    G, P, C, Q = SSM_N_GROUPS, SSM_STATE, SSM_GROUP, N_PHASE
    QC = Q * C
    nm = M // mh
    n_pair = (G // 2) // (B * nm)
    assert B * nm * n_pair == G // 2
    f = lambda t: t.astype(F32)
    a_re, a_im, log_dt, b_re, b_im, c_re, c_im = map(f, (a_re, a_im, log_dt, b_re, b_im, c_re, c_im))
    even = (jnp.arange(G) % 2 == 0)[:, None, None]

    def lane_half(c):
        return jnp.concatenate([jnp.where(even, c, 0.0), jnp.where(even, 0.0, c)], axis=-1)

    pair = lambda blk: pl.BlockSpec((2 * n_pair,) + blk, lambda b, m: (b * nm + m,) + (0,) * len(blk))
    one = lambda blk: pl.BlockSpec((n_pair,) + blk, lambda b, m: (b * nm + m,) + (0,) * len(blk))
    outs = pl.pallas_call(
        functools.partial(_out_proj_kernel, mh=mh),
        grid=(B, nm),
        in_specs=[
            pl.BlockSpec((None, N_PHASE, mh, O.shape[-1]), lambda b, m: (b, 0, m, 0)),
            pl.BlockSpec(w.shape, lambda b, m: (0, 0)),
            pl.BlockSpec((1, D), lambda b, m: (0, 0)),
            pl.BlockSpec((None, N_PHASE, mh, D), lambda b, m: (b, 0, m, 0)),
            pair((P, 1)), pair((P, 1)), one((1, 2 * P)), one((1, 2 * P)), one((1, 2 * P)),
            pair((P, C)), pair((P, C)), pair((C, P)), pair((C, P)), pair((C, 2 * P)), pair((C, 2 * P)),
        ],
        out_specs=[pl.BlockSpec((None, N_PHASE, mh, D), lambda b, m: (b, 0, m, 0)),
                   pair((QC, QC)), pair((P, QC)), pair((P, QC)), one((2 * QC, 2 * P)),
                   one((2 * QC, 2 * P)), one((1, 2 * P)), one((1, 2 * P))],
        out_shape=[jax.ShapeDtypeStruct(xp.shape, F32),
                   jax.ShapeDtypeStruct((G, QC, QC), BF16), jax.ShapeDtypeStruct((G, P, QC), BF16),
                   jax.ShapeDtypeStruct((G, P, QC), BF16),
                   jax.ShapeDtypeStruct((G // 2, 2 * QC, 2 * P), BF16),
                   jax.ShapeDtypeStruct((G // 2, 2 * QC, 2 * P), BF16),
                   jax.ShapeDtypeStruct((G // 2, 1, 2 * P), F32),
                   jax.ShapeDtypeStruct((G // 2, 1, 2 * P), F32)],
        compiler_params=_cparams(("parallel", "parallel")),
        name="attn_out_proj",
    )(O, w, g.reshape(1, D), xp,
      a_re.reshape(G, P, 1), a_im.reshape(G, P, 1),
      a_re.reshape(G // 2, 1, 2 * P), a_im.reshape(G // 2, 1, 2 * P),
      jnp.repeat(log_dt, P).reshape(G // 2, 1, 2 * P), b_re, b_im, c_re, c_im,
      lane_half(c_re), lane_half(c_im))
    keys = ("m", "b_re", "b_im", "c_re", "c_im", "a_re", "a_im")
    prm = dict(zip(keys, outs[1:]))
    prm["d"] = d_skip.astype(F32).reshape(G * C, 1)
    return outs[0], prm


def _ssm_in_proj_kernel(x_ref, g_ref, w_ref, o_ref, h_ref):
    M = x_ref.shape[1]
    nt = (((1,), (1,)), ((), ()))

    def weights():
        return w_ref[...].T.astype(BF16)

    @pl.when(pl.program_id(1) == 0)
    def _():
        wt = weights()
        for c in range(0, N_PHASE, NORM_CHUNK):
            x = x_ref[c:c + NORM_CHUNK].reshape(NORM_CHUNK * M, -1)
            ms = jnp.mean(x * x, axis=-1, keepdims=True)
            hc = (x * lax.rsqrt(ms + EPS) * g_ref[...]).astype(BF16)
            h_ref[c * M:(c + NORM_CHUNK) * M, :] = hc
            res = lax.dot_general(wt, hc, nt, preferred_element_type=F32)
            for s in range(NORM_CHUNK):
                o_ref[c + s] = res[:, s * M:(s + 1) * M].astype(BF16)

    @pl.when(pl.program_id(1) > 0)
    def _():
        res = lax.dot_general(weights(), h_ref[...], nt, preferred_element_type=F32)
        for s in range(N_PHASE):
            o_ref[s] = res[:, s * M:(s + 1) * M].astype(BF16)


def _ssm_in_proj(xp, g, w, *, tn=1024):
    B, _, M, D = xp.shape
    S = N_PHASE * M
    N = w.shape[1]
    return pl.pallas_call(
        _ssm_in_proj_kernel,
        grid=(B, N // tn),
        in_specs=[
            pl.BlockSpec((None, N_PHASE, M, D), lambda b, j: (b, 0, 0, 0)),
            pl.BlockSpec((1, D), lambda b, j: (0, 0)),
            pl.BlockSpec((D, tn), lambda b, j: (0, j)),
        ],
        out_specs=pl.BlockSpec((None, N_PHASE, tn, M), lambda b, j: (b, 0, j, 0)),
        out_shape=jax.ShapeDtypeStruct((B, N_PHASE, N, M), BF16),
        scratch_shapes=[pltpu.VMEM((S, D), BF16)],
        compiler_params=_cparams(("parallel", "arbitrary")),
        name="ssm_in_proj",
    )(xp, g.reshape(1, D), w)


def _ssm_core_kernel(u_ref, m_ref, bre_ref, bim_ref, cre_ref, cim_ref, are_ref, aim_ref,
                     d_ref, y_ref, sre_ref, sim_ref, ym_ref, *, n_chunk, n_pair):
    C = SSM_GROUP
    QC = N_PHASE * C
    nb = u_ref.shape[0]
    N = nb * n_chunk
    n_grp = 2 * n_pair

    def grp(i, b):
        return u_ref[b, :, i * C:(i + 1) * C, :]

    us = [jnp.concatenate([grp(i, b).reshape(QC, n_chunk) for b in range(nb)], axis=1)
          for i in range(n_grp)]

    P = bre_ref.shape[1]
    xre, xim = [], []
    for i in range(n_grp):
        lhs = jnp.concatenate([m_ref[i], bre_ref[i], bim_ref[i]], axis=0)
        r = jnp.dot(lhs, us[i], preferred_element_type=F32)
        ym_ref[i] = r[:QC]
        xre.append(r[QC:QC + P])
        xim.append(r[QC + P:])
    for k in range(n_pair):
        sre_ref[k] = jnp.concatenate(xre[2 * k:2 * k + 2], axis=0).T
        sim_ref[k] = jnp.concatenate(xim[2 * k:2 * k + 2], axis=0).T

    sub = lax.broadcasted_iota(jnp.int32, (8, are_ref.shape[-1]), 0)

    def cmul(xr, xi, yr, yi):
        return xr * yr - xi * yi, xr * yi + xi * yr

    for k in range(n_pair):
        a1 = (are_ref[k], aim_ref[k])
        a2 = cmul(*a1, *a1)
        a4 = cmul(*a2, *a2)
        pows = [a1, a2, cmul(*a2, *a1), a4, cmul(*a4, *a1), cmul(*a4, *a2)]
        pows += [cmul(*pows[5], *a1), cmul(*a4, *a4)]
        apow_r = jnp.zeros(sub.shape, F32)
        apow_i = jnp.zeros(sub.shape, F32)
        for i, (pr, pi) in enumerate(pows):
            apow_r = jnp.where(sub == i, pr, apow_r)
            apow_i = jnp.where(sub == i, pi, apow_i)
        step = {d: (jnp.where(sub >= d, pows[d - 1][0], 0.0), jnp.where(sub >= d, pows[d - 1][1], 0.0))
                for d in (1, 2, 4)}
        carry = [(jnp.zeros(sub.shape, F32), jnp.zeros(sub.shape, F32)) for _ in range(nb)]
        for j in range(n_chunk // 8):
            for b in range(nb):
                rows = slice(b * n_chunk + 8 * j, b * n_chunk + 8 * j + 8)
                er, ei = sre_ref[k, rows, :], sim_ref[k, rows, :]
                for d in (1, 2, 4):
                    dr, di = cmul(*step[d], pltpu.roll(er, d, axis=0), pltpu.roll(ei, d, axis=0))
                    er, ei = er + dr, ei + di
                cr, ci = carry[b]
                dr, di = cmul(apow_r, apow_i, cr, ci)
                fr, fi = er + dr, ei + di
                sre_ref[k, rows, :] = jnp.where(sub == 0, cr, pltpu.roll(fr, 1, axis=0))
                sim_ref[k, rows, :] = jnp.where(sub == 0, ci, pltpu.roll(fi, 1, axis=0))
                carry[b] = (jnp.broadcast_to(fr[7:8], sub.shape), jnp.broadcast_to(fi[7:8], sub.shape))

    nt = (((1,), (1,)), ((), ()))
    for k in range(n_pair):
        inter = (lax.dot_general(cre_ref[k], sre_ref[k].astype(BF16), nt, preferred_element_type=F32)
                 + lax.dot_general(cim_ref[k], sim_ref[k].astype(BF16), nt,
                                   preferred_element_type=F32))
        for i in (2 * k, 2 * k + 1):
            y = ym_ref[i] + inter[(i % 2) * QC:(i % 2 + 1) * QC]
            for b in range(nb):
                yb = y[:, b * n_chunk:(b + 1) * n_chunk].reshape(N_PHASE, C, n_chunk)
                yb = yb + d_ref[i * C:(i + 1) * C] * grp(i, b).astype(F32)
                y_ref[b, :, i * C:(i + 1) * C, :] = yb.astype(BF16)


def _ssm_core(UT, prm, *, n_pair=2):
    B, _, _, n_chunk = UT.shape
    N = B * n_chunk
    G2 = SSM_N_GROUPS // 2
    C2 = 2 * SSM_GROUP
    QC = N_PHASE * SSM_GROUP
    P = SSM_STATE
    blk = lambda *shape: pl.BlockSpec(shape, lambda g: (g,) + (0,) * (len(shape) - 1))
    return pl.pallas_call(
        functools.partial(_ssm_core_kernel, n_chunk=n_chunk, n_pair=n_pair),
        grid=(G2 // n_pair,),
        in_specs=[
            pl.BlockSpec((B, N_PHASE, n_pair * C2, n_chunk), lambda g: (0, 0, g, 0)),
            blk(2 * n_pair, QC, QC), blk(2 * n_pair, P, QC), blk(2 * n_pair, P, QC),
            blk(n_pair, 2 * QC, 2 * P), blk(n_pair, 2 * QC, 2 * P),
            blk(n_pair, 1, 2 * P), blk(n_pair, 1, 2 * P),
            blk(n_pair * C2, 1),
        ],
        out_specs=pl.BlockSpec((B, N_PHASE, n_pair * C2, n_chunk), lambda g: (0, 0, g, 0)),
        out_shape=jax.ShapeDtypeStruct((B, N_PHASE, SSM_N_GROUPS * SSM_GROUP, n_chunk), BF16),
        scratch_shapes=[pltpu.VMEM((n_pair, N, 2 * P), F32), pltpu.VMEM((n_pair, N, 2 * P), F32),
                        pltpu.VMEM((2 * n_pair, QC, N), F32)],
        compiler_params=_cparams(("parallel",)),
        name="ssm_core",
    )(UT, prm["m"], prm["b_re"], prm["b_im"], prm["c_re"], prm["c_im"],
      prm["a_re"], prm["a_im"], prm["d"])


def _ssm_param_kernel(arc_ref, aic_ref, arr_ref, air_ref, ldr_ref, bre_ref, bim_ref,
                      cre_ref, cim_ref, cpr_ref, cpi_ref,
                      m_ref, br_ref, bi_ref, cr_ref, ci_ref, ar_ref, ai_ref):
    Q, C, P = N_PHASE, SSM_GROUP, SSM_STATE
    QC = Q * C
    hi = lax.Precision.HIGHEST
    tile = (lax.broadcasted_iota(jnp.int32, (C, QC), 1) % C
            == lax.broadcasted_iota(jnp.int32, (C, QC), 0)).astype(F32)

    dt = jnp.exp(ldr_ref[...])
    mag = jnp.exp(arr_ref[...] * dt)
    a1r, a1i = mag * jnp.cos(air_ref[...] * dt), mag * jnp.sin(air_ref[...] * dt)
    a1r_c = jnp.broadcast_to(a1r, (8, 2 * P)).T[:, :1]
    a1i_c = jnp.broadcast_to(a1i, (8, 2 * P)).T[:, :1]

    pows = [(jnp.ones_like(a1r), jnp.zeros_like(a1r))]
    for _ in range(Q):
        qr, qi = pows[-1]
        pows.append((qr * a1r - qi * a1i, qr * a1i + qi * a1r))
    pw_r = jnp.concatenate([jnp.broadcast_to(pows[Q - 1 - s][0], (C, 2 * P)) for s in range(Q)], axis=0).T
    pw_i = jnp.concatenate([jnp.broadcast_to(pows[Q - 1 - s][1], (C, 2 * P)) for s in range(Q)], axis=0).T

    for i in range(2):
        a_re, a_im = arc_ref[i], aic_ref[i]
        abr, abi = a1r_c[i * P:(i + 1) * P], a1i_c[i * P:(i + 1) * P]
        den = a_re * a_re + a_im * a_im
        cfr = ((abr - 1.0) * a_re + abi * a_im) / den
        cfi = (abi * a_re - (abr - 1.0) * a_im) / den
        bbr = cfr * bre_ref[i] - cfi * bim_ref[i]
        bbi = cfr * bim_ref[i] + cfi * bre_ref[i]

        bbr_t = jnp.dot(bbr, tile, precision=hi, preferred_element_type=F32)
        bbi_t = jnp.dot(bbi, tile, precision=hi, preferred_element_type=F32)
        pr, pi = pw_r[i * P:(i + 1) * P], pw_i[i * P:(i + 1) * P]
        bst_r = pr * bbr_t - pi * bbi_t
        bst_i = pr * bbi_t + pi * bbr_t
        br_ref[i] = bst_r.astype(BF16)
        bi_ref[i] = bst_i.astype(BF16)

        kl = (jnp.dot(cre_ref[i], bst_r, precision=hi, preferred_element_type=F32)
              - jnp.dot(cim_ref[i], bst_i, precision=hi, preferred_element_type=F32))
        klz = jnp.concatenate([kl, jnp.zeros_like(kl)], axis=1)
        for t in range(Q):
            sh = (2 * QC - C * (Q - 1 - t)) % (2 * QC)
            row = klz if sh == 0 else pltpu.roll(klz, sh, axis=1)
            m_ref[i, t * C:(t + 1) * C, :] = row[:, :QC].astype(BF16)

    for t in range(Q):
        qr, qi = pows[t + 1]
        for i in range(2):
            rows = slice(i * QC + t * C, i * QC + (t + 1) * C)
            cr_ref[rows, :] = (cpr_ref[i] * qr - cpi_ref[i] * qi).astype(BF16)
            ci_ref[rows, :] = (-(cpr_ref[i] * qi + cpi_ref[i] * qr)).astype(BF16)
    ar_ref[...], ai_ref[...] = pows[Q]


def _ssm_out_kernel(y_ref, z_ref, wg_ref, bg_ref, wo_ref, gp_ref, x_ref, o_hbm, obuf, sem,
                    wgt_ref, wot_ref, *, nb, n_sub):
    s, j = pl.program_id(0), pl.program_id(1)
    nj = pl.num_programs(1)
    step = s * nj + j
    last = pl.num_programs(0) * nj - 1
    slot = step % 2

    def out_copy(slot_, s_, j_):
        return pltpu.make_async_copy(obuf.at[slot_], o_hbm.at[pl.ds(j_ * nb, nb), :, s_, :],
                                     sem.at[slot_])

    @pl.when(step >= 2)
    def _():
        out_copy(slot, (step - 2) // nj, (step - 2) % nj).wait()

    @pl.when(step == 0)
    def _():
        wgt_ref[...] = wg_ref[...].T.astype(BF16)
        wot_ref[...] = wo_ref[...].T.astype(BF16)

    bs = nb // n_sub

    def lanes(ref, k):
        return jnp.concatenate([ref[b] for b in range(k * bs, (k + 1) * bs)], axis=1).astype(F32)

    def glu_in(k):
        g = _gelu_tanh(lanes(y_ref, k))
        return g, jnp.dot(wgt_ref[...], g.astype(BF16), preferred_element_type=F32)

    def proj(k, g, gl):
        y2 = g * _sigmoid(gl + bg_ref[...])
        z = lanes(z_ref, k)
        gated = (y2 * (z * _sigmoid(z))).astype(BF16)
        return jnp.dot(wot_ref[...], gated, preferred_element_type=F32)

    def finish(k, ot):
        ms = jnp.mean(ot * ot, axis=0, keepdims=True)
        nt = (ot * lax.rsqrt(ms + EPS)).T
        rows = slice(k * bs, (k + 1) * bs)
        obuf[slot, rows] = x_ref[rows] + (nt * gp_ref[...]).reshape(bs, -1, nt.shape[-1])

    a = [glu_in(k) for k in range(n_sub)]
    o = [proj(k, *a[k]) for k in range(n_sub)]
    for k in range(n_sub):
        finish(k, o[k])
    out_copy(slot, s, j).start()

    @pl.when(step == last)
    def _():
        out_copy(slot, s, j).wait()

    @pl.when(jnp.logical_and(step == last, last >= 1))
    def _():
        out_copy(1 - slot, (step - 1) // nj, (step - 1) % nj).wait()


def _ssm_out(YT, UT, wg, bg, wo, gp, xp, *, nb=8, n_sub=2):
    B, _, M, D = xp.shape
    E = YT.shape[2]
    out = pl.pallas_call(
        functools.partial(_ssm_out_kernel, nb=nb, n_sub=n_sub),
        grid=(N_PHASE, B // nb),
        in_specs=[
            pl.BlockSpec((nb, None, E, M), lambda s, j: (j, s, 0, 0)),
            pl.BlockSpec((nb, None, E, M), lambda s, j: (j, s, 1, 0)),
            pl.BlockSpec((E, E), lambda s, j: (0, 0)),
            pl.BlockSpec((E, 1), lambda s, j: (0, 0)),
            pl.BlockSpec((E, D), lambda s, j: (0, 0)),
            pl.BlockSpec((1, D), lambda s, j: (0, 0)),
            pl.BlockSpec((nb, None, M, D), lambda s, j: (j, s, 0, 0)),
        ],
        out_specs=pl.BlockSpec(memory_space=pl.ANY),
        out_shape=jax.ShapeDtypeStruct((B, M, N_PHASE, D), F32),
        scratch_shapes=[pltpu.VMEM((2, nb, M, D), F32), pltpu.SemaphoreType.DMA((2,)),
                        pltpu.VMEM((E, E), BF16), pltpu.VMEM((D, E), BF16)],
        compiler_params=_cparams(("arbitrary", "arbitrary")),
        name="ssm_out",
    )(YT, UT, wg, bg.reshape(E, 1), wo, gp.reshape(1, D), xp)
    return out.reshape(B, M * N_PHASE, D)


def kernel(x, rel_bias, attn_pre_norm, attn_w_in, attn_w_out, attn_post_norm, ssm_pre_norm, ssm_w_in, ssm_a_re, ssm_a_im, ssm_log_dt, ssm_b_re, ssm_b_im, ssm_c_re, ssm_c_im, ssm_d, ssm_w_glu, ssm_b_glu, ssm_w_out, ssm_post_norm):
    B, S, D = x.shape
    n_chunk = S // N_PHASE

    P, xp = _norm_proj(x, attn_pre_norm[0], attn_w_in[0], scaled_cols=3 * HEADS * HEAD_DIM,
                       scale=HEAD_DIM ** -0.5 * LOG2E)
    O = _attention(P, _bias_tables(rel_bias))
    xp, prm = _out_proj(O, attn_w_out[0], attn_post_norm[0], xp, ssm_a_re[0], ssm_a_im[0],
                        ssm_log_dt[0], ssm_b_re[0], ssm_b_im[0], ssm_c_re[0], ssm_c_im[0], ssm_d[0])

    UT = _ssm_in_proj(xp, ssm_pre_norm[0], ssm_w_in[0])
    YT = _ssm_core(UT, prm)
    return _ssm_out(YT, UT, ssm_w_glu[0], ssm_b_glu[0], ssm_w_out[0], ssm_post_norm[0], xp)
```

```python
import functools
import math

import numpy as np
import jax
import jax.numpy as jnp
from jax import lax
from jax.experimental import pallas as pl
from jax.experimental.pallas import tpu as pltpu

F32 = jnp.float32
BF16 = jnp.bfloat16

D_MODEL = 1024
HEAD_DIM = 64
HEADS = 16
N_PHASE = 16
BLK = 128
DILATIONS = (1, 4, 16)
REL_BUCKETS = 32
REL_MAX_DIST = 2048
SSM_GROUP = 16
SSM_N_GROUPS = 64
SSM_STATE = 64
EPS = 1e-6
NEG = -1e30
VMEM_LIMIT = 56 * 1024 * 1024
NORM_CHUNK = 4


def _cparams(sem):
    return pltpu.CompilerParams(dimension_semantics=sem, vmem_limit_bytes=VMEM_LIMIT)


LOG2E = math.log2(math.e)


def _sigmoid(v):
    return 1.0 / (1.0 + jnp.exp2(v * (-LOG2E)))


def _gelu_tanh(x):
    k0 = -2.0 * math.sqrt(2.0 / math.pi) * LOG2E
    return x / (1.0 + jnp.exp2(x * (k0 + (k0 * 0.044715) * (x * x))))


def _norm_proj_kernel(x_hbm, g_ref, w_ref, o_ref, xp_hbm, xbuf, h_ref, gsem, osem, *,
                      scaled_tiles, scale):
    b, j = pl.program_id(0), pl.program_id(1)
    n_b, n_j = pl.num_programs(0), pl.num_programs(1)
    slot = b % 2

    def phase_copy(bb, sl, r):
        return pltpu.make_async_copy(x_hbm.at[bb, :, r, :], xbuf.at[sl, r], gsem.at[sl])

    def gather_start(bb, sl):
        lax.fori_loop(0, N_PHASE, lambda r, c: (phase_copy(bb, sl, r).start(), c)[1], 0)

    def gather_wait(bb, sl):
        lax.fori_loop(0, N_PHASE, lambda r, c: (phase_copy(bb, sl, r).wait(), c)[1], 0)

    def write_out(bb, sl):
        return pltpu.make_async_copy(xbuf.at[sl], xp_hbm.at[bb], osem.at[sl])

    M = xbuf.shape[2]

    def weights():
        return (w_ref[...] * jnp.where(j < scaled_tiles, scale, 1.0).astype(F32)).astype(BF16)

    @pl.when(j == 0)
    def _():
        @pl.when(b == 0)
        def _():
            gather_start(0, 0)

        gather_wait(b, slot)

        @pl.when(b >= 1)
        def _():
            write_out(b - 1, 1 - slot).wait()

        @pl.when(b + 1 < n_b)
        def _():
            gather_start(b + 1, 1 - slot)

        write_out(b, slot).start()
        w = weights()
        for c in range(0, N_PHASE, NORM_CHUNK):
            x = xbuf[slot, c:c + NORM_CHUNK].reshape(NORM_CHUNK * M, -1)
            ms = jnp.mean(x * x, axis=-1, keepdims=True)
            hc = (x * lax.rsqrt(ms + EPS) * g_ref[...]).astype(BF16)
            h_ref[c * M:(c + NORM_CHUNK) * M, :] = hc
            res = jnp.dot(hc, w, preferred_element_type=F32)
            o_ref[c:c + NORM_CHUNK] = res.reshape(NORM_CHUNK, M, -1).astype(BF16)

    @pl.when(j > 0)
    def _():
        res = jnp.dot(h_ref[...], weights(), preferred_element_type=F32)
        o_ref[...] = res.reshape(o_ref.shape).astype(BF16)

    @pl.when(jnp.logical_and(b == n_b - 1, j == n_j - 1))
    def _():
        write_out(b, slot).wait()


def _norm_proj(x, g, w, *, scaled_cols, scale, tn=1024):
    B, S, D = x.shape
    M = S // N_PHASE
    N = w.shape[1]
    assert scaled_cols % tn == 0
    return pl.pallas_call(
        functools.partial(_norm_proj_kernel, scaled_tiles=scaled_cols // tn, scale=scale),
        grid=(B, N // tn),
        in_specs=[
            pl.BlockSpec(memory_space=pl.ANY),
            pl.BlockSpec((1, D), lambda b, j: (0, 0)),
            pl.BlockSpec((D, tn), lambda b, j: (0, j)),
        ],
        out_specs=[pl.BlockSpec((None, N_PHASE, M, tn), lambda b, j: (b, 0, 0, j)),
                   pl.BlockSpec(memory_space=pl.ANY)],
        out_shape=[jax.ShapeDtypeStruct((B, N_PHASE, M, N), BF16),
                   jax.ShapeDtypeStruct((B, N_PHASE, M, D), F32)],
        scratch_shapes=[pltpu.VMEM((2, N_PHASE, M, D), F32), pltpu.VMEM((S, D), BF16),
                        pltpu.SemaphoreType.DMA((2,)), pltpu.SemaphoreType.DMA((2,))],
        compiler_params=_cparams(("arbitrary", "arbitrary")),
        name="attn_norm_proj",
    )(x.reshape(B, M, N_PHASE, D), g.reshape(1, D), w)


def _attn_kernel(q0_ref, q1_ref, q2_ref, k0_ref, k1_ref, k2_ref, v0_ref, v1_ref, v2_ref,
                 z_ref, bm_ref, o_ref, qf_ref, kf_ref, vf_ref, acc_ref, l_ref, m_ref):
    W = 2 * HEAD_DIM
    AHEAD = 3
    lane = lax.broadcasted_iota(jnp.int32, (BLK, W), 1)
    first_head = lane < HEAD_DIM

    def logits(q, k, g, cur_only):
        zq = jnp.zeros_like(q)
        qs = jnp.concatenate([jnp.where(first_head, q, zq), jnp.where(first_head, zq, q)], axis=0)
        s = lax.dot_general(qs, k, (((1,), (1,)), ((), ())), preferred_element_type=F32)
        return s + (bm_ref[g, :, BLK:2 * BLK] if cur_only else bm_ref[g])

    ones_a = jnp.where(first_head, 1.0, 0.0).astype(BF16)
    ones_b = jnp.where(first_head, 0.0, 1.0).astype(BF16)
    masked_v = {}

    def value_blocks(g, v_ref, pieces, cast):
        key = (g, tuple(pieces))
        if key not in masked_v:
            v = rows_of(v_ref, pieces)
            v = v.astype(BF16) if cast else v
            zv = jnp.zeros_like(v)
            masked_v[key] = (jnp.concatenate([jnp.where(first_head, v, zv), ones_a], axis=1),
                             jnp.concatenate([jnp.where(first_head, zv, v), ones_b], axis=1))
        return masked_v[key]

    def finish(s, vblocks):
        m = jnp.max(s, axis=-1, keepdims=True)
        p = jnp.exp2(s - m).astype(BF16)
        pcat = jnp.concatenate([p[:BLK], p[BLK:]], axis=1)
        rhs = jnp.concatenate([vb[0] for vb in vblocks] + [vb[1] for vb in vblocks], axis=0)
        pv = jnp.dot(pcat, rhs, preferred_element_type=F32)
        mm = jnp.where(first_head, m[:BLK], m[BLK:])
        return pv[:, :W], pv[:, W:], mm

    def rows_of(ref, pieces):
        return jnp.concatenate([ref[ph, lo:lo + n, :] for ph, lo, n in pieces], axis=0)

    def keys_of(ref, prev, cur):
        return rows_of(ref, cur) if prev is None else rows_of(ref, prev + cur)

    def store(g, pieces, vals):
        at = 0
        for ph, lo, n in pieces:
            for ref, val in zip((acc_ref, l_ref, m_ref), vals):
                ref[g, ph, lo:lo + n, :] = val[at:at + n]
            at += n

    M = q0_ref.shape[1]
    g2_blocks = [[(r, 0, BLK)] for r in range(N_PHASE)]
    g1_blocks = {(r4, n): [(4 * q4 + r4, 32 * n, 32) for q4 in range(4)]
                 for r4 in range(4) for n in range(M // 32)}
    g0_blocks = [[(r, 8 * n, 8) for r in range(N_PHASE)] for n in range(M // 8)]

    for r in range(N_PHASE):
        qf_ref[r] = q0_ref[r].astype(F32)
        kf_ref[r] = k0_ref[r].astype(F32)
        vf_ref[r] = v0_ref[r].astype(F32)

    work = []
    for n, pieces in enumerate(g0_blocks):
        work.append((0, pieces, g0_blocks[n - 1] if n else None, (qf_ref, kf_ref, vf_ref), True))
    for (r4, n), pieces in g1_blocks.items():
        work.append((1, pieces, g1_blocks[(r4, n - 1)] if n else None, (q1_ref, k1_ref, v1_ref), False))
    for pieces in g2_blocks:
        work.append((2, pieces, None, (q2_ref, k2_ref, v2_ref), False))

    def merge(r, acc2, l2, m2):
        m0, m1 = m_ref[0, r], m_ref[1, r]
        mx = jnp.maximum(jnp.maximum(m0, m1), m2)
        w0, w1, w2 = jnp.exp2(m0 - mx), jnp.exp2(m1 - mx), jnp.exp2(m2 - mx)
        num = w0 * acc_ref[0, r] + w1 * acc_ref[1, r] + w2 * acc2
        den = w0 * l_ref[0, r] + w1 * l_ref[1, r] + w2 * l2
        z = z_ref[r].astype(F32)
        o_ref[r] = (num * z / (den * (1.0 + jnp.exp2(z * (-LOG2E))))).astype(BF16)

    pending = []
    for item in work + [None] * AHEAD:
        if item is not None:
            g, pieces, prev, (q_r, k_r, _), cast = item
            q, k = rows_of(q_r, pieces), keys_of(k_r, prev, pieces)
            if cast:
                q, k = q.astype(BF16), k.astype(BF16)
            pending.append((item, logits(q, k, g, prev is None)))
        if item is None or len(pending) > AHEAD:
            (g_p, pieces_p, prev_p, (_, _, v_r), cast_p), s_p = pending.pop(0)
            key_blocks = ([] if prev_p is None else [prev_p]) + [pieces_p]
            vals = finish(s_p, [value_blocks(g_p, v_r, kb, cast_p) for kb in key_blocks])
            if g_p == 2:
                merge(pieces_p[0][0], *vals)
            else:
                store(g_p, pieces_p, vals)


def _attention(P, bm):
    B, _, M, _ = P.shape
    HP = HEADS // 2
    W = 2 * HEAD_DIM

    def spec(kind, g):
        base = (kind * 3 + g) * HP
        return pl.BlockSpec((None, N_PHASE, M, W), lambda b, hp, base=base: (b, 0, 0, base + hp))

    in_specs = [spec(kind, g) for kind in range(3) for g in range(3)]
    in_specs.append(pl.BlockSpec((None, N_PHASE, M, W), lambda b, hp: (b, 0, 0, 9 * HP + hp)))
    in_specs.append(pl.BlockSpec((3, None, 2 * BLK, 2 * BLK), lambda b, hp: (0, hp, 0, 0)))
    return pl.pallas_call(
        _attn_kernel,
        grid=(B, HP),
        in_specs=in_specs,
        out_specs=pl.BlockSpec((None, N_PHASE, M, W), lambda b, hp: (b, 0, 0, hp)),
        out_shape=jax.ShapeDtypeStruct((B, N_PHASE, M, HEADS * HEAD_DIM), BF16),
        scratch_shapes=[pltpu.VMEM((N_PHASE, M, W), F32) for _ in range(3)]
        + [pltpu.VMEM((2, N_PHASE, M, W), F32) for _ in range(3)],
        compiler_params=_cparams(("parallel", "parallel")),
        name="dilated_attention",
    )(*([P] * 10), bm)


def _t5_bucket(dist):
    max_exact = REL_BUCKETS // 2
    n = jnp.maximum(dist, 1).astype(F32)
    large = max_exact + (jnp.log(n / max_exact) / math.log(REL_MAX_DIST / max_exact)
                         * (REL_BUCKETS - max_exact)).astype(jnp.int32)
    large = jnp.minimum(large, REL_BUCKETS - 1)
    return jnp.where(dist < max_exact, dist, large)


def _bias_tables(rel_bias):
    a = np.arange(BLK)
    pos = (16 * (a % 8) + a // 8, 4 * (a % 32) + a // 32, a)
    back = np.stack([np.concatenate([BLK + p[:, None] - p[None, :], p[:, None] - p[None, :]], axis=1)
                     for p in pos])
    valid = (back >= 0) & (back <= BLK)
    dist = np.clip(back, 0, BLK) * np.asarray(DILATIONS)[:, None, None]
    bucket = jnp.where(jnp.asarray(valid), _t5_bucket(jnp.asarray(dist, jnp.int32)), REL_BUCKETS)
    onehot = (bucket[..., None] == jnp.arange(REL_BUCKETS + 1)).astype(F32)
    ext = jnp.concatenate([rel_bias.astype(F32), jnp.full((1, HEADS), NEG, F32)], axis=0)
    t = jnp.einsum("gijc,ch->ghij", onehot, ext * math.log2(math.e),
                   precision=lax.Precision.HIGHEST)
    return t.reshape(3, HEADS // 2, 2 * BLK, 2 * BLK)


def _out_proj_kernel(o_ref, w_ref, g_ref, x_ref, *rest, mh):
    prm_in, y_ref, prm_out = rest[:11], rest[11], rest[12:]
    n_pair = prm_out[-1].shape[0]

    def of_pair(refs, k):
        return [r.at[2 * k:2 * k + 2] if r.shape[0] == 2 * n_pair else r.at[k] for r in refs]

    half = [_ssm_param_stage_a(*of_pair(prm_in[:7], k)) for k in range(n_pair)]
    o = o_ref[...].reshape(N_PHASE * mh, o_ref.shape[-1])
    h = jnp.dot(o, w_ref[...].astype(BF16), preferred_element_type=F32)
    for k in range(n_pair):
        _ssm_param_stage_b(*half[k], *of_pair(prm_in[7:] + prm_out, k))
    ms = jnp.mean(h * h, axis=-1, keepdims=True)
    y = h * lax.rsqrt(ms + EPS) * g_ref[...]
    y_ref[...] = x_ref[...] + y.reshape(y_ref.shape)


def _out_proj(O, w, g, xp, a_re, a_im, log_dt, b_re, b_im, c_re, c_im, d_skip, *, mh=64):
    B, _, M, D = xp.shape
    G, P, C, Q = SSM_N_GROUPS, SSM_STATE, SSM_GROUP, N_PHASE
    QC = Q * C
    nm = M // mh
    n_pair = (G // 2) // (B * nm)
    assert B * nm * n_pair == G // 2
    f = lambda t: t.astype(F32)
    a_re, a_im, log_dt, b_re, b_im, c_re, c_im = map(f, (a_re, a_im, log_dt, b_re, b_im, c_re, c_im))
    even = (jnp.arange(G) % 2 == 0)[:, None, None]

    def lane_half(c):
        return jnp.concatenate([jnp.where(even, c, 0.0), jnp.where(even, 0.0, c)], axis=-1)

    pair = lambda blk: pl.BlockSpec((2 * n_pair,) + blk, lambda b, m: (b * nm + m,) + (0,) * len(blk))
    one = lambda blk: pl.BlockSpec((n_pair,) + blk, lambda b, m: (b * nm + m,) + (0,) * len(blk))
    outs = pl.pallas_call(
        functools.partial(_out_proj_kernel, mh=mh),
        grid=(B, nm),
        in_specs=[
            pl.BlockSpec((None, N_PHASE, mh, O.shape[-1]), lambda b, m: (b, 0, m, 0)),
            pl.BlockSpec(w.shape, lambda b, m: (0, 0)),
            pl.BlockSpec((1, D), lambda b, m: (0, 0)),
            pl.BlockSpec((None, N_PHASE, mh, D), lambda b, m: (b, 0, m, 0)),
            pair((P, 1)), pair((P, 1)), one((1, 2 * P)), one((1, 2 * P)), one((1, 2 * P)),
            pair((P, C)), pair((P, C)), pair((C, P)), pair((C, P)), pair((C, 2 * P)), pair((C, 2 * P)),
        ],
        out_specs=[pl.BlockSpec((None, N_PHASE, mh, D), lambda b, m: (b, 0, m, 0)),
                   pair((QC, QC)), pair((P, QC)), pair((P, QC)), one((2 * QC, 2 * P)),
                   one((2 * QC, 2 * P)), one((1, 2 * P)), one((1, 2 * P))],
        out_shape=[jax.ShapeDtypeStruct(xp.shape, F32),
                   jax.ShapeDtypeStruct((G, QC, QC), BF16), jax.ShapeDtypeStruct((G, P, QC), BF16),
                   jax.ShapeDtypeStruct((G, P, QC), BF16),
                   jax.ShapeDtypeStruct((G // 2, 2 * QC, 2 * P), BF16),
                   jax.ShapeDtypeStruct((G // 2, 2 * QC, 2 * P), BF16),
                   jax.ShapeDtypeStruct((G // 2, 1, 2 * P), F32),
                   jax.ShapeDtypeStruct((G // 2, 1, 2 * P), F32)],
        compiler_params=_cparams(("parallel", "parallel")),
        name="attn_out_proj",
    )(O, w, g.reshape(1, D), xp,
      a_re.reshape(G, P, 1), a_im.reshape(G, P, 1),
      a_re.reshape(G // 2, 1, 2 * P), a_im.reshape(G // 2, 1, 2 * P),
      jnp.repeat(log_dt, P).reshape(G // 2, 1, 2 * P), b_re, b_im, c_re, c_im,
      lane_half(c_re), lane_half(c_im))
    keys = ("m", "b_re", "b_im", "c_re", "c_im", "a_re", "a_im")
    prm = dict(zip(keys, outs[1:]))
    prm["d"] = d_skip.astype(F32).reshape(G * C, 1)
    return outs[0], prm


def _ssm_in_proj_kernel(x_ref, g_ref, w_ref, o_ref, h_ref):
    M = x_ref.shape[1]
    nt = (((1,), (1,)), ((), ()))

    def weights():
        return w_ref[...].T.astype(BF16)

    @pl.when(pl.program_id(1) == 0)
    def _():
        wt = weights()
        for c in range(0, N_PHASE, NORM_CHUNK):
            x = x_ref[c:c + NORM_CHUNK].reshape(NORM_CHUNK * M, -1)
            ms = jnp.mean(x * x, axis=-1, keepdims=True)
            hc = (x * lax.rsqrt(ms + EPS) * g_ref[...]).astype(BF16)
            h_ref[c * M:(c + NORM_CHUNK) * M, :] = hc
            res = lax.dot_general(wt, hc, nt, preferred_element_type=F32)
            for s in range(NORM_CHUNK):
                o_ref[c + s] = res[:, s * M:(s + 1) * M].astype(BF16)

    @pl.when(pl.program_id(1) > 0)
    def _():
        res = lax.dot_general(weights(), h_ref[...], nt, preferred_element_type=F32)
        for s in range(N_PHASE):
            o_ref[s] = res[:, s * M:(s + 1) * M].astype(BF16)


def _ssm_in_proj(xp, g, w, *, tn=1024):
    B, _, M, D = xp.shape
    S = N_PHASE * M
    N = w.shape[1]
    return pl.pallas_call(
        _ssm_in_proj_kernel,
        grid=(B, N // tn),
        in_specs=[
            pl.BlockSpec((None, N_PHASE, M, D), lambda b, j: (b, 0, 0, 0)),
            pl.BlockSpec((1, D), lambda b, j: (0, 0)),
            pl.BlockSpec((D, tn), lambda b, j: (0, j)),
        ],
        out_specs=pl.BlockSpec((None, N_PHASE, tn, M), lambda b, j: (b, 0, j, 0)),
        out_shape=jax.ShapeDtypeStruct((B, N_PHASE, N, M), BF16),
        scratch_shapes=[pltpu.VMEM((S, D), BF16)],
        compiler_params=_cparams(("parallel", "arbitrary")),
        name="ssm_in_proj",
    )(xp, g.reshape(1, D), w)


def _ssm_core_kernel(u_ref, m_ref, bre_ref, bim_ref, cre_ref, cim_ref, are_ref, aim_ref,
                     d_ref, y_ref, sre_ref, sim_ref, ym_ref, *, n_chunk, n_pair):
    C = SSM_GROUP
    QC = N_PHASE * C
    nb = u_ref.shape[0]
    N = nb * n_chunk
    n_grp = 2 * n_pair

    def grp(i, b):
        return u_ref[b, :, i * C:(i + 1) * C, :]

    us = [jnp.concatenate([grp(i, b).reshape(QC, n_chunk) for b in range(nb)], axis=1)
          for i in range(n_grp)]

    P = bre_ref.shape[1]
    xre, xim = [], []
    for i in range(n_grp):
        lhs = jnp.concatenate([m_ref[i], bre_ref[i], bim_ref[i]], axis=0)
        r = jnp.dot(lhs, us[i], preferred_element_type=F32)
        ym_ref[i] = r[:QC]
        xre.append(r[QC:QC + P])
        xim.append(r[QC + P:])
    for k in range(n_pair):
        sre_ref[k] = jnp.concatenate(xre[2 * k:2 * k + 2], axis=0).T
        sim_ref[k] = jnp.concatenate(xim[2 * k:2 * k + 2], axis=0).T

    sub = lax.broadcasted_iota(jnp.int32, (8, are_ref.shape[-1]), 0)

    def cmul(xr, xi, yr, yi):
        return xr * yr - xi * yi, xr * yi + xi * yr

    for k in range(n_pair):
        a1 = (are_ref[k], aim_ref[k])
        a2 = cmul(*a1, *a1)
        a4 = cmul(*a2, *a2)
        pows = [a1, a2, cmul(*a2, *a1), a4, cmul(*a4, *a1), cmul(*a4, *a2)]
        pows += [cmul(*pows[5], *a1), cmul(*a4, *a4)]
        apow_r = jnp.zeros(sub.shape, F32)
        apow_i = jnp.zeros(sub.shape, F32)
        for i, (pr, pi) in enumerate(pows):
            apow_r = jnp.where(sub == i, pr, apow_r)
            apow_i = jnp.where(sub == i, pi, apow_i)
        step = {d: (jnp.where(sub >= d, pows[d - 1][0], 0.0), jnp.where(sub >= d, pows[d - 1][1], 0.0))
                for d in (1, 2, 4)}
        carry = [(jnp.zeros(sub.shape, F32), jnp.zeros(sub.shape, F32)) for _ in range(nb)]
        for j in range(n_chunk // 8):
            for b in range(nb):
                rows = slice(b * n_chunk + 8 * j, b * n_chunk + 8 * j + 8)
                er, ei = sre_ref[k, rows, :], sim_ref[k, rows, :]
                for d in (1, 2, 4):
                    dr, di = cmul(*step[d], pltpu.roll(er, d, axis=0), pltpu.roll(ei, d, axis=0))
                    er, ei = er + dr, ei + di
                cr, ci = carry[b]
                dr, di = cmul(apow_r, apow_i, cr, ci)
                fr, fi = er + dr, ei + di
                sre_ref[k, rows, :] = jnp.where(sub == 0, cr, pltpu.roll(fr, 1, axis=0))
                sim_ref[k, rows, :] = jnp.where(sub == 0, ci, pltpu.roll(fi, 1, axis=0))
                carry[b] = (jnp.broadcast_to(fr[7:8], sub.shape), jnp.broadcast_to(fi[7:8], sub.shape))

    nt = (((1,), (1,)), ((), ()))
    for k in range(n_pair):
        inter = (lax.dot_general(cre_ref[k], sre_ref[k].astype(BF16), nt, preferred_element_type=F32)
                 + lax.dot_general(cim_ref[k], sim_ref[k].astype(BF16), nt,
                                   preferred_element_type=F32))
        for i in (2 * k, 2 * k + 1):
            y = ym_ref[i] + inter[(i % 2) * QC:(i % 2 + 1) * QC]
            for b in range(nb):
                yb = y[:, b * n_chunk:(b + 1) * n_chunk].reshape(N_PHASE, C, n_chunk)
                yb = yb + d_ref[i * C:(i + 1) * C] * grp(i, b).astype(F32)
                y_ref[b, :, i * C:(i + 1) * C, :] = yb.astype(BF16)


def _ssm_core(UT, prm, *, n_pair=2):
    B, _, _, n_chunk = UT.shape
    N = B * n_chunk
    G2 = SSM_N_GROUPS // 2
    C2 = 2 * SSM_GROUP
    QC = N_PHASE * SSM_GROUP
    P = SSM_STATE
    blk = lambda *shape: pl.BlockSpec(shape, lambda g: (g,) + (0,) * (len(shape) - 1))
    return pl.pallas_call(
        functools.partial(_ssm_core_kernel, n_chunk=n_chunk, n_pair=n_pair),
        grid=(G2 // n_pair,),
        in_specs=[
            pl.BlockSpec((B, N_PHASE, n_pair * C2, n_chunk), lambda g: (0, 0, g, 0)),
            blk(2 * n_pair, QC, QC), blk(2 * n_pair, P, QC), blk(2 * n_pair, P, QC),
            blk(n_pair, 2 * QC, 2 * P), blk(n_pair, 2 * QC, 2 * P),
            blk(n_pair, 1, 2 * P), blk(n_pair, 1, 2 * P),
            blk(n_pair * C2, 1),
        ],
        out_specs=pl.BlockSpec((B, N_PHASE, n_pair * C2, n_chunk), lambda g: (0, 0, g, 0)),
        out_shape=jax.ShapeDtypeStruct((B, N_PHASE, SSM_N_GROUPS * SSM_GROUP, n_chunk), BF16),
        scratch_shapes=[pltpu.VMEM((n_pair, N, 2 * P), F32), pltpu.VMEM((n_pair, N, 2 * P), F32),
                        pltpu.VMEM((2 * n_pair, QC, N), F32)],
        compiler_params=_cparams(("parallel",)),
        name="ssm_core",
    )(UT, prm["m"], prm["b_re"], prm["b_im"], prm["c_re"], prm["c_im"],
      prm["a_re"], prm["a_im"], prm["d"])


def _ssm_param_stage_a(arc_ref, aic_ref, arr_ref, air_ref, ldr_ref, bre_ref, bim_ref):
    Q, C, P = N_PHASE, SSM_GROUP, SSM_STATE
    QC = Q * C
    hi = lax.Precision.HIGHEST
    tile = (lax.broadcasted_iota(jnp.int32, (C, QC), 1) % C
            == lax.broadcasted_iota(jnp.int32, (C, QC), 0)).astype(F32)

    dt = jnp.exp(ldr_ref[...])
    mag = jnp.exp(arr_ref[...] * dt)
    a1r, a1i = mag * jnp.cos(air_ref[...] * dt), mag * jnp.sin(air_ref[...] * dt)
    a1r_c = jnp.broadcast_to(a1r, (8, 2 * P)).T[:, :1]
    a1i_c = jnp.broadcast_to(a1i, (8, 2 * P)).T[:, :1]

    pows = [(jnp.ones_like(a1r), jnp.zeros_like(a1r))]
    for _ in range(Q):
        qr, qi = pows[-1]
        pows.append((qr * a1r - qi * a1i, qr * a1i + qi * a1r))

    bb_t = []
    for i in range(2):
        a_re, a_im = arc_ref[i], aic_ref[i]
        abr, abi = a1r_c[i * P:(i + 1) * P], a1i_c[i * P:(i + 1) * P]
        den = a_re * a_re + a_im * a_im
        cfr = ((abr - 1.0) * a_re + abi * a_im) / den
        cfi = (abi * a_re - (abr - 1.0) * a_im) / den
        bbr = cfr * bre_ref[i] - cfi * bim_ref[i]
        bbi = cfr * bim_ref[i] + cfi * bre_ref[i]
        bb_t.append((jnp.dot(bbr, tile, precision=hi, preferred_element_type=F32),
                     jnp.dot(bbi, tile, precision=hi, preferred_element_type=F32)))
    return pows, bb_t


def _ssm_param_stage_b(pows, bb_t, cre_ref, cim_ref, cpr_ref, cpi_ref,
                       m_ref, br_ref, bi_ref, cr_ref, ci_ref, ar_ref, ai_ref):
    Q, C, P = N_PHASE, SSM_GROUP, SSM_STATE
    QC = Q * C
    hi = lax.Precision.HIGHEST
    pw_r = jnp.concatenate([jnp.broadcast_to(pows[Q - 1 - s][0], (C, 2 * P)) for s in range(Q)], axis=0).T
    pw_i = jnp.concatenate([jnp.broadcast_to(pows[Q - 1 - s][1], (C, 2 * P)) for s in range(Q)], axis=0).T

    for i in range(2):
        bbr_t, bbi_t = bb_t[i]
        pr, pi = pw_r[i * P:(i + 1) * P], pw_i[i * P:(i + 1) * P]
        bst_r = pr * bbr_t - pi * bbi_t
        bst_i = pr * bbi_t + pi * bbr_t
        br_ref[i] = bst_r.astype(BF16)
        bi_ref[i] = bst_i.astype(BF16)

        kl = (jnp.dot(cre_ref[i], bst_r, precision=hi, preferred_element_type=F32)
              - jnp.dot(cim_ref[i], bst_i, precision=hi, preferred_element_type=F32))
        klz = jnp.concatenate([kl, jnp.zeros_like(kl)], axis=1)
        for t in range(Q):
            sh = (2 * QC - C * (Q - 1 - t)) % (2 * QC)
            row = klz if sh == 0 else pltpu.roll(klz, sh, axis=1)
            m_ref[i, t * C:(t + 1) * C, :] = row[:, :QC].astype(BF16)

    for t in range(Q):
        qr, qi = pows[t + 1]
        for i in range(2):
            rows = slice(i * QC + t * C, i * QC + (t + 1) * C)
            cr_ref[rows, :] = (cpr_ref[i] * qr - cpi_ref[i] * qi).astype(BF16)
            ci_ref[rows, :] = (-(cpr_ref[i] * qi + cpi_ref[i] * qr)).astype(BF16)
    ar_ref[...], ai_ref[...] = pows[Q]


def _ssm_out_kernel(y_ref, z_ref, wg_ref, bg_ref, wo_ref, gp_ref, x_ref, o_hbm, obuf, sem,
                    wgt_ref, wot_ref, *, nb, n_sub):
    s, j = pl.program_id(0), pl.program_id(1)
    nj = pl.num_programs(1)
    step = s * nj + j
    last = pl.num_programs(0) * nj - 1
    slot = step % 2

    def out_copy(slot_, s_, j_):
        return pltpu.make_async_copy(obuf.at[slot_], o_hbm.at[pl.ds(j_ * nb, nb), :, s_, :],
                                     sem.at[slot_])

    @pl.when(step >= 2)
    def _():
        out_copy(slot, (step - 2) // nj, (step - 2) % nj).wait()

    @pl.when(step == 0)
    def _():
        wgt_ref[...] = wg_ref[...].T.astype(BF16)
        wot_ref[...] = wo_ref[...].T.astype(BF16)

    bs = nb // n_sub

    def lanes(ref, k):
        return jnp.concatenate([ref[b] for b in range(k * bs, (k + 1) * bs)], axis=1).astype(F32)

    def glu_in(k):
        g = _gelu_tanh(lanes(y_ref, k))
        return g, jnp.dot(wgt_ref[...], g.astype(BF16), preferred_element_type=F32)

    def proj(k, g, gl):
        y2 = g * _sigmoid(gl + bg_ref[...])
        z = lanes(z_ref, k)
        gated = (y2 * (z * _sigmoid(z))).astype(BF16)
        return jnp.dot(wot_ref[...], gated, preferred_element_type=F32)

    def finish(k, ot):
        ms = jnp.mean(ot * ot, axis=0, keepdims=True)
        nt = (ot * lax.rsqrt(ms + EPS)).T
        rows = slice(k * bs, (k + 1) * bs)
        obuf[slot, rows] = x_ref[rows] + (nt * gp_ref[...]).reshape(bs, -1, nt.shape[-1])

    a = [glu_in(k) for k in range(n_sub)]
    o = [proj(k, *a[k]) for k in range(n_sub)]
    for k in range(n_sub):
        finish(k, o[k])
    out_copy(slot, s, j).start()

    @pl.when(step == last)
    def _():
        out_copy(slot, s, j).wait()

    @pl.when(jnp.logical_and(step == last, last >= 1))
    def _():
        out_copy(1 - slot, (step - 1) // nj, (step - 1) % nj).wait()


def _ssm_out(YT, UT, wg, bg, wo, gp, xp, *, nb=8, n_sub=2):
    B, _, M, D = xp.shape
    E = YT.shape[2]
    out = pl.pallas_call(
        functools.partial(_ssm_out_kernel, nb=nb, n_sub=n_sub),
        grid=(N_PHASE, B // nb),
        in_specs=[
            pl.BlockSpec((nb, None, E, M), lambda s, j: (j, s, 0, 0)),
            pl.BlockSpec((nb, None, E, M), lambda s, j: (j, s, 1, 0)),
            pl.BlockSpec((E, E), lambda s, j: (0, 0)),
            pl.BlockSpec((E, 1), lambda s, j: (0, 0)),
            pl.BlockSpec((E, D), lambda s, j: (0, 0)),
            pl.BlockSpec((1, D), lambda s, j: (0, 0)),
            pl.BlockSpec((nb, None, M, D), lambda s, j: (j, s, 0, 0)),
        ],
        out_specs=pl.BlockSpec(memory_space=pl.ANY),
        out_shape=jax.ShapeDtypeStruct((B, M, N_PHASE, D), F32),
        scratch_shapes=[pltpu.VMEM((2, nb, M, D), F32), pltpu.SemaphoreType.DMA((2,)),
                        pltpu.VMEM((E, E), BF16), pltpu.VMEM((D, E), BF16)],
        compiler_params=_cparams(("arbitrary", "arbitrary")),
        name="ssm_out",
    )(YT, UT, wg, bg.reshape(E, 1), wo, gp.reshape(1, D), xp)
    return out.reshape(B, M * N_PHASE, D)


def kernel(x, rel_bias, attn_pre_norm, attn_w_in, attn_w_out, attn_post_norm, ssm_pre_norm, ssm_w_in, ssm_a_re, ssm_a_im, ssm_log_dt, ssm_b_re, ssm_b_im, ssm_c_re, ssm_c_im, ssm_d, ssm_w_glu, ssm_b_glu, ssm_w_out, ssm_post_norm):
    B, S, D = x.shape
    n_chunk = S // N_PHASE

    P, xp = _norm_proj(x, attn_pre_norm[0], attn_w_in[0], scaled_cols=3 * HEADS * HEAD_DIM,
                       scale=HEAD_DIM ** -0.5 * LOG2E)
    O = _attention(P, _bias_tables(rel_bias))
    xp, prm = _out_proj(O, attn_w_out[0], attn_post_norm[0], xp, ssm_a_re[0], ssm_a_im[0],
                        ssm_log_dt[0], ssm_b_re[0], ssm_b_im[0], ssm_c_re[0], ssm_c_im[0], ssm_d[0])

    UT = _ssm_in_proj(xp, ssm_pre_norm[0], ssm_w_in[0])
    YT = _ssm_core(UT, prm)
    return _ssm_out(YT, UT, ssm_w_glu[0], ssm_b_glu[0], ssm_w_out[0], ssm_post_norm[0], xp)
```

```python
import functools
import math

import numpy as np
import jax
import jax.numpy as jnp
from jax import lax
from jax.experimental import pallas as pl
from jax.experimental.pallas import tpu as pltpu

F32 = jnp.float32
BF16 = jnp.bfloat16

D_MODEL = 1024
HEAD_DIM = 64
HEADS = 16
N_PHASE = 16
BLK = 128
DILATIONS = (1, 4, 16)
REL_BUCKETS = 32
REL_MAX_DIST = 2048
SSM_GROUP = 16
SSM_N_GROUPS = 64
SSM_STATE = 64
EPS = 1e-6
NEG = -1e30
VMEM_LIMIT = 56 * 1024 * 1024
NORM_CHUNK = 4


def _cparams(sem):
    return pltpu.CompilerParams(dimension_semantics=sem, vmem_limit_bytes=VMEM_LIMIT)


LOG2E = math.log2(math.e)


def _sigmoid(v):
    return 1.0 / (1.0 + jnp.exp2(v * (-LOG2E)))


def _gelu_tanh(x):
    k0 = -2.0 * math.sqrt(2.0 / math.pi) * LOG2E
    return x / (1.0 + jnp.exp2(x * (k0 + (k0 * 0.044715) * (x * x))))


def _norm_proj_kernel(x_hbm, g_ref, w_ref, o_ref, xp_hbm, xbuf, h_ref, gsem, osem, *,
                      scaled_tiles, scale):
    b, j = pl.program_id(0), pl.program_id(1)
    n_b, n_j = pl.num_programs(0), pl.num_programs(1)
    slot = b % 2

    def phase_copy(bb, sl, r):
        return pltpu.make_async_copy(x_hbm.at[bb, :, r, :], xbuf.at[sl, r], gsem.at[sl])

    def gather_start(bb, sl):
        lax.fori_loop(0, N_PHASE, lambda r, c: (phase_copy(bb, sl, r).start(), c)[1], 0)

    def gather_wait(bb, sl):
        lax.fori_loop(0, N_PHASE, lambda r, c: (phase_copy(bb, sl, r).wait(), c)[1], 0)

    def write_out(bb, sl):
        return pltpu.make_async_copy(xbuf.at[sl], xp_hbm.at[bb], osem.at[sl])

    M = xbuf.shape[2]

    def weights():
        return (w_ref[...] * jnp.where(j < scaled_tiles, scale, 1.0).astype(F32)).astype(BF16)

    @pl.when(j == 0)
    def _():
        @pl.when(b == 0)
        def _():
            gather_start(0, 0)

        gather_wait(b, slot)

        @pl.when(b >= 1)
        def _():
            write_out(b - 1, 1 - slot).wait()

        @pl.when(b + 1 < n_b)
        def _():
            gather_start(b + 1, 1 - slot)

        write_out(b, slot).start()
        w = weights()
        for c in range(0, N_PHASE, NORM_CHUNK):
            x = xbuf[slot, c:c + NORM_CHUNK].reshape(NORM_CHUNK * M, -1)
            ms = jnp.mean(x * x, axis=-1, keepdims=True)
            hc = (x * lax.rsqrt(ms + EPS) * g_ref[...]).astype(BF16)
            h_ref[c * M:(c + NORM_CHUNK) * M, :] = hc
            res = jnp.dot(hc, w, preferred_element_type=F32)
            o_ref[c:c + NORM_CHUNK] = res.reshape(NORM_CHUNK, M, -1).astype(BF16)

    @pl.when(j > 0)
    def _():
        res = jnp.dot(h_ref[...], weights(), preferred_element_type=F32)
        o_ref[...] = res.reshape(o_ref.shape).astype(BF16)

    @pl.when(jnp.logical_and(b == n_b - 1, j == n_j - 1))
    def _():
        write_out(b, slot).wait()


def _norm_proj(x, g, w, *, scaled_cols, scale, tn=1024):
    B, S, D = x.shape
    M = S // N_PHASE
    N = w.shape[1]
    assert scaled_cols % tn == 0
    return pl.pallas_call(
        functools.partial(_norm_proj_kernel, scaled_tiles=scaled_cols // tn, scale=scale),
        grid=(B, N // tn),
        in_specs=[
            pl.BlockSpec(memory_space=pl.ANY),
            pl.BlockSpec((1, D), lambda b, j: (0, 0)),
            pl.BlockSpec((D, tn), lambda b, j: (0, j)),
        ],
        out_specs=[pl.BlockSpec((None, N_PHASE, M, tn), lambda b, j: (b, 0, 0, j)),
                   pl.BlockSpec(memory_space=pl.ANY)],
        out_shape=[jax.ShapeDtypeStruct((B, N_PHASE, M, N), BF16),
                   jax.ShapeDtypeStruct((B, N_PHASE, M, D), F32)],
        scratch_shapes=[pltpu.VMEM((2, N_PHASE, M, D), F32), pltpu.VMEM((S, D), BF16),
                        pltpu.SemaphoreType.DMA((2,)), pltpu.SemaphoreType.DMA((2,))],
        compiler_params=_cparams(("arbitrary", "arbitrary")),
        name="attn_norm_proj",
    )(x.reshape(B, M, N_PHASE, D), g.reshape(1, D), w)


def _attn_kernel(q0_ref, q1_ref, q2_ref, k0_ref, k1_ref, k2_ref, v0_ref, v1_ref, v2_ref,
                 z_ref, bm_ref, o_ref, qf_ref, kf_ref, vf_ref, acc_ref, l_ref, m_ref):
    W = 2 * HEAD_DIM
    AHEAD = 3
    lane = lax.broadcasted_iota(jnp.int32, (BLK, W), 1)
    first_head = lane < HEAD_DIM

    def logits(q, k, g, cur_only):
        zq = jnp.zeros_like(q)
        qs = jnp.concatenate([jnp.where(first_head, q, zq), jnp.where(first_head, zq, q)], axis=0)
        s = lax.dot_general(qs, k, (((1,), (1,)), ((), ())), preferred_element_type=F32)
        return s + (bm_ref[g, :, BLK:2 * BLK] if cur_only else bm_ref[g])

    ones_a = jnp.where(first_head, 1.0, 0.0).astype(BF16)
    ones_b = jnp.where(first_head, 0.0, 1.0).astype(BF16)
    masked_v = {}

    def value_blocks(g, v_ref, pieces, cast):
        key = (g, tuple(pieces))
        if key not in masked_v:
            v = rows_of(v_ref, pieces)
            v = v.astype(BF16) if cast else v
            zv = jnp.zeros_like(v)
            masked_v[key] = (jnp.concatenate([jnp.where(first_head, v, zv), ones_a], axis=1),
                             jnp.concatenate([jnp.where(first_head, zv, v), ones_b], axis=1))
        return masked_v[key]

    def finish(s, vblocks):
        m = jnp.max(s, axis=-1, keepdims=True)
        p = jnp.exp2(s - m).astype(BF16)
        pcat = jnp.concatenate([p[:BLK], p[BLK:]], axis=1)
        rhs = jnp.concatenate([vb[0] for vb in vblocks] + [vb[1] for vb in vblocks], axis=0)
        pv = jnp.dot(pcat, rhs, preferred_element_type=F32)
        mm = jnp.where(first_head, m[:BLK], m[BLK:])
        return pv[:, :W], pv[:, W:], mm

    def rows_of(ref, pieces):
        return jnp.concatenate([ref[ph, lo:lo + n, :] for ph, lo, n in pieces], axis=0)

    def keys_of(ref, prev, cur):
        return rows_of(ref, cur) if prev is None else rows_of(ref, prev + cur)

    def store(g, pieces, vals):
        at = 0
        for ph, lo, n in pieces:
            for ref, val in zip((acc_ref, l_ref, m_ref), vals):
                ref[g, ph, lo:lo + n, :] = val[at:at + n]
            at += n

    M = q0_ref.shape[1]
    g2_blocks = [[(r, 0, BLK)] for r in range(N_PHASE)]
    g1_blocks = {(r4, n): [(4 * q4 + r4, 32 * n, 32) for q4 in range(4)]
                 for r4 in range(4) for n in range(M // 32)}
    g0_blocks = [[(r, 8 * n, 8) for r in range(N_PHASE)] for n in range(M // 8)]

    for r in range(N_PHASE):
        qf_ref[r] = q0_ref[r].astype(F32)
        kf_ref[r] = k0_ref[r].astype(F32)
        vf_ref[r] = v0_ref[r].astype(F32)

    work = []
    for n, pieces in enumerate(g0_blocks):
        work.append((0, pieces, g0_blocks[n - 1] if n else None, (qf_ref, kf_ref, vf_ref), True))
    for (r4, n), pieces in g1_blocks.items():
        work.append((1, pieces, g1_blocks[(r4, n - 1)] if n else None, (q1_ref, k1_ref, v1_ref), False))
    for pieces in g2_blocks:
        work.append((2, pieces, None, (q2_ref, k2_ref, v2_ref), False))

    def merge(r, acc2, l2, m2):
        m0, m1 = m_ref[0, r], m_ref[1, r]
        mx = jnp.maximum(jnp.maximum(m0, m1), m2)
        w0, w1, w2 = jnp.exp2(m0 - mx), jnp.exp2(m1 - mx), jnp.exp2(m2 - mx)
        num = w0 * acc_ref[0, r] + w1 * acc_ref[1, r] + w2 * acc2
        den = w0 * l_ref[0, r] + w1 * l_ref[1, r] + w2 * l2
        z = z_ref[r].astype(F32)
        o_ref[r] = (num * z / (den * (1.0 + jnp.exp2(z * (-LOG2E))))).astype(BF16)

    pending = []
    for item in work + [None] * AHEAD:
        if item is not None:
            g, pieces, prev, (q_r, k_r, _), cast = item
            q, k = rows_of(q_r, pieces), keys_of(k_r, prev, pieces)
            if cast:
                q, k = q.astype(BF16), k.astype(BF16)
            pending.append((item, logits(q, k, g, prev is None)))
        if item is None or len(pending) > AHEAD:
            (g_p, pieces_p, prev_p, (_, _, v_r), cast_p), s_p = pending.pop(0)
            key_blocks = ([] if prev_p is None else [prev_p]) + [pieces_p]
            vals = finish(s_p, [value_blocks(g_p, v_r, kb, cast_p) for kb in key_blocks])
            if g_p == 2:
                merge(pieces_p[0][0], *vals)
            else:
                store(g_p, pieces_p, vals)


def _attention(P, bm):
    B, _, M, _ = P.shape
    HP = HEADS // 2
    W = 2 * HEAD_DIM

    def spec(kind, g):
        base = (kind * 3 + g) * HP
        return pl.BlockSpec((None, N_PHASE, M, W), lambda b, hp, base=base: (b, 0, 0, base + hp))

    in_specs = [spec(kind, g) for kind in range(3) for g in range(3)]
    in_specs.append(pl.BlockSpec((None, N_PHASE, M, W), lambda b, hp: (b, 0, 0, 9 * HP + hp)))
    in_specs.append(pl.BlockSpec((3, None, 2 * BLK, 2 * BLK), lambda b, hp: (0, hp, 0, 0)))
    return pl.pallas_call(
        _attn_kernel,
        grid=(B, HP),
        in_specs=in_specs,
        out_specs=pl.BlockSpec((None, N_PHASE, M, W), lambda b, hp: (b, 0, 0, hp)),
        out_shape=jax.ShapeDtypeStruct((B, N_PHASE, M, HEADS * HEAD_DIM), BF16),
        scratch_shapes=[pltpu.VMEM((N_PHASE, M, W), F32) for _ in range(3)]
        + [pltpu.VMEM((2, N_PHASE, M, W), F32) for _ in range(3)],
        compiler_params=_cparams(("parallel", "parallel")),
        name="dilated_attention",
    )(*([P] * 10), bm)


def _t5_bucket(dist):
    max_exact = REL_BUCKETS // 2
    n = jnp.maximum(dist, 1).astype(F32)
    large = max_exact + (jnp.log(n / max_exact) / math.log(REL_MAX_DIST / max_exact)
                         * (REL_BUCKETS - max_exact)).astype(jnp.int32)
    large = jnp.minimum(large, REL_BUCKETS - 1)
    return jnp.where(dist < max_exact, dist, large)


def _bias_tables(rel_bias):
    a = np.arange(BLK)
    pos = (16 * (a % 8) + a // 8, 4 * (a % 32) + a // 32, a)
    back = np.stack([np.concatenate([BLK + p[:, None] - p[None, :], p[:, None] - p[None, :]], axis=1)
                     for p in pos])
    valid = (back >= 0) & (back <= BLK)
    dist = np.clip(back, 0, BLK) * np.asarray(DILATIONS)[:, None, None]
    bucket = jnp.where(jnp.asarray(valid), _t5_bucket(jnp.asarray(dist, jnp.int32)), REL_BUCKETS)
    onehot = (bucket[..., None] == jnp.arange(REL_BUCKETS + 1)).astype(F32)
    ext = jnp.concatenate([rel_bias.astype(F32), jnp.full((1, HEADS), NEG, F32)], axis=0)
    t = jnp.einsum("gijc,ch->ghij", onehot, ext * math.log2(math.e),
                   precision=lax.Precision.HIGHEST)
    return t.reshape(3, HEADS // 2, 2 * BLK, 2 * BLK)


def _out_proj_kernel(o_ref, w_ref, g_ref, x_ref, *rest, mh):
    prm_in, y_ref, prm_out = rest[:11], rest[11], rest[12:]
    n_pair = prm_out[-1].shape[0]

    def of_pair(refs, k):
        return [r.at[2 * k:2 * k + 2] if r.shape[0] == 2 * n_pair else r.at[k] for r in refs]

    half = [_ssm_param_stage_a(*of_pair(prm_in[:7], k)) for k in range(n_pair)]
    o = o_ref[...].reshape(N_PHASE * mh, o_ref.shape[-1])
    h = jnp.dot(o, w_ref[...].astype(BF16), preferred_element_type=F32)
    for k in range(n_pair):
        _ssm_param_stage_b(*half[k], *of_pair(prm_in[7:] + prm_out, k))
    ms = jnp.mean(h * h, axis=-1, keepdims=True)
    y = h * lax.rsqrt(ms + EPS) * g_ref[...]
    y_ref[...] = x_ref[...] + y.reshape(y_ref.shape)


def _out_proj(O, w, g, xp, a_re, a_im, log_dt, b_re, b_im, c_re, c_im, d_skip, *, mh=64):
    B, _, M, D = xp.shape
    G, P, C, Q = SSM_N_GROUPS, SSM_STATE, SSM_GROUP, N_PHASE
    QC = Q * C
    nm = M // mh
    n_pair = (G // 2) // (B * nm)
    assert B * nm * n_pair == G // 2
    f = lambda t: t.astype(F32)
    a_re, a_im, log_dt, b_re, b_im, c_re, c_im = map(f, (a_re, a_im, log_dt, b_re, b_im, c_re, c_im))
    even = (jnp.arange(G) % 2 == 0)[:, None, None]

    def lane_half(c):
        return jnp.concatenate([jnp.where(even, c, 0.0), jnp.where(even, 0.0, c)], axis=-1)

    pair = lambda blk: pl.BlockSpec((2 * n_pair,) + blk, lambda b, m: (b * nm + m,) + (0,) * len(blk))
    one = lambda blk: pl.BlockSpec((n_pair,) + blk, lambda b, m: (b * nm + m,) + (0,) * len(blk))
    outs = pl.pallas_call(
        functools.partial(_out_proj_kernel, mh=mh),
        grid=(B, nm),
        in_specs=[
            pl.BlockSpec((None, N_PHASE, mh, O.shape[-1]), lambda b, m: (b, 0, m, 0)),
            pl.BlockSpec(w.shape, lambda b, m: (0, 0)),
            pl.BlockSpec((1, D), lambda b, m: (0, 0)),
            pl.BlockSpec((None, N_PHASE, mh, D), lambda b, m: (b, 0, m, 0)),
            pair((P, 1)), pair((P, 1)), one((1, 2 * P)), one((1, 2 * P)), one((1, 2 * P)),
            pair((P, C)), pair((P, C)), pair((C, P)), pair((C, P)), pair((C, 2 * P)), pair((C, 2 * P)),
        ],
        out_specs=[pl.BlockSpec((None, N_PHASE, mh, D), lambda b, m: (b, 0, m, 0)),
                   pair((QC, QC)), pair((P, QC)), pair((P, QC)), one((2 * QC, 2 * P)),
                   one((2 * QC, 2 * P)), one((1, 2 * P)), one((1, 2 * P))],
        out_shape=[jax.ShapeDtypeStruct(xp.shape, F32),
                   jax.ShapeDtypeStruct((G, QC, QC), BF16), jax.ShapeDtypeStruct((G, P, QC), BF16),
                   jax.ShapeDtypeStruct((G, P, QC), BF16),
                   jax.ShapeDtypeStruct((G // 2, 2 * QC, 2 * P), BF16),
                   jax.ShapeDtypeStruct((G // 2, 2 * QC, 2 * P), BF16),
                   jax.ShapeDtypeStruct((G // 2, 1, 2 * P), F32),
                   jax.ShapeDtypeStruct((G // 2, 1, 2 * P), F32)],
        compiler_params=_cparams(("parallel", "parallel")),
        name="attn_out_proj",
    )(O, w, g.reshape(1, D), xp,
      a_re.reshape(G, P, 1), a_im.reshape(G, P, 1),
      a_re.reshape(G // 2, 1, 2 * P), a_im.reshape(G // 2, 1, 2 * P),
      jnp.repeat(log_dt, P).reshape(G // 2, 1, 2 * P), b_re, b_im, c_re, c_im,
      lane_half(c_re), lane_half(c_im))
    keys = ("m", "b_re", "b_im", "c_re", "c_im", "a_re", "a_im")
    prm = dict(zip(keys, outs[1:]))
    prm["d"] = d_skip.astype(F32).reshape(G * C, 1)
    return outs[0], prm


def _ssm_in_proj_kernel(x_ref, g_ref, w_ref, o_ref, h_ref):
    M = x_ref.shape[1]
    nt = (((1,), (1,)), ((), ()))

    def weights():
        return w_ref[...].T.astype(BF16)

    @pl.when(pl.program_id(1) == 0)
    def _():
        wt = weights()
        for c in range(0, N_PHASE, NORM_CHUNK):
            x = x_ref[c:c + NORM_CHUNK].reshape(NORM_CHUNK * M, -1)
            ms = jnp.mean(x * x, axis=-1, keepdims=True)
            hc = (x * lax.rsqrt(ms + EPS) * g_ref[...]).astype(BF16)
            h_ref[c * M:(c + NORM_CHUNK) * M, :] = hc
            res = lax.dot_general(wt, hc, nt, preferred_element_type=F32)
            for s in range(NORM_CHUNK):
                o_ref[c + s] = res[:, s * M:(s + 1) * M].astype(BF16)

    @pl.when(pl.program_id(1) > 0)
    def _():
        res = lax.dot_general(weights(), h_ref[...], nt, preferred_element_type=F32)
        for s in range(N_PHASE):
            o_ref[s] = res[:, s * M:(s + 1) * M].astype(BF16)


def _ssm_in_proj(xp, g, w, *, tn=1024):
    B, _, M, D = xp.shape
    S = N_PHASE * M
    N = w.shape[1]
    return pl.pallas_call(
        _ssm_in_proj_kernel,
        grid=(B, N // tn),
        in_specs=[
            pl.BlockSpec((None, N_PHASE, M, D), lambda b, j: (b, 0, 0, 0)),
            pl.BlockSpec((1, D), lambda b, j: (0, 0)),
            pl.BlockSpec((D, tn), lambda b, j: (0, j)),
        ],
        out_specs=pl.BlockSpec((None, N_PHASE, tn, M), lambda b, j: (b, 0, j, 0)),
        out_shape=jax.ShapeDtypeStruct((B, N_PHASE, N, M), BF16),
        scratch_shapes=[pltpu.VMEM((S, D), BF16)],
        compiler_params=_cparams(("parallel", "arbitrary")),
        name="ssm_in_proj",
    )(xp, g.reshape(1, D), w)


def _ssm_core_kernel(u_ref, m_ref, bre_ref, bim_ref, cre_ref, cim_ref, are_ref, aim_ref,
                     d_ref, y_ref, sre_ref, sim_ref, ym_ref, *, n_chunk, n_pair):
    C = SSM_GROUP
    QC = N_PHASE * C
    nb = u_ref.shape[0]
    n_grp = 2 * n_pair

    def grp(i, b):
        return u_ref[b, :, i * C:(i + 1) * C, :]

    us = [jnp.concatenate([grp(i, b).reshape(QC, n_chunk) for b in range(nb)], axis=1)
          for i in range(n_grp)]

    P = bre_ref.shape[1]
    xre, xim = [], []
    for i in range(n_grp):
        lhs = jnp.concatenate([m_ref[i], bre_ref[i], bim_ref[i]], axis=0)
        r = jnp.dot(lhs, us[i], preferred_element_type=F32)
        ym_ref[i] = r[:QC]
        xre.append(r[QC:QC + P])
        xim.append(r[QC + P:])
    for k in range(n_pair):
        sre_ref[k] = jnp.concatenate(xre[2 * k:2 * k + 2], axis=0).T
        sim_ref[k] = jnp.concatenate(xim[2 * k:2 * k + 2], axis=0).T

    sub = lax.broadcasted_iota(jnp.int32, (8, are_ref.shape[-1]), 0)

    def cmul(xr, xi, yr, yi):
        return xr * yr - xi * yi, xr * yi + xi * yr

    for k in range(n_pair):
        a1 = (are_ref[k], aim_ref[k])
        a2 = cmul(*a1, *a1)
        a4 = cmul(*a2, *a2)
        pows = [a1, a2, cmul(*a2, *a1), a4, cmul(*a4, *a1), cmul(*a4, *a2)]
        pows += [cmul(*pows[5], *a1), cmul(*a4, *a4)]
        apow_r = jnp.zeros(sub.shape, F32)
        apow_i = jnp.zeros(sub.shape, F32)
        for i, (pr, pi) in enumerate(pows):
            apow_r = jnp.where(sub == i, pr, apow_r)
            apow_i = jnp.where(sub == i, pi, apow_i)
        step = {d: (jnp.where(sub >= d, pows[d - 1][0], 0.0), jnp.where(sub >= d, pows[d - 1][1], 0.0))
                for d in (1, 2, 4)}
        carry = [(jnp.zeros(sub.shape, F32), jnp.zeros(sub.shape, F32)) for _ in range(nb)]
        for j in range(n_chunk // 8):
            for b in range(nb):
                rows = slice(b * n_chunk + 8 * j, b * n_chunk + 8 * j + 8)
                er, ei = sre_ref[k, rows, :], sim_ref[k, rows, :]
                for d in (1, 2, 4):
                    dr, di = cmul(*step[d], pltpu.roll(er, d, axis=0), pltpu.roll(ei, d, axis=0))
                    er, ei = er + dr, ei + di
                cr, ci = carry[b]
                dr, di = cmul(apow_r, apow_i, cr, ci)
                fr, fi = er + dr, ei + di
                sre_ref[k, rows, :] = jnp.where(sub == 0, cr, pltpu.roll(fr, 1, axis=0))
                sim_ref[k, rows, :] = jnp.where(sub == 0, ci, pltpu.roll(fi, 1, axis=0))
                carry[b] = (jnp.broadcast_to(fr[7:8], sub.shape), jnp.broadcast_to(fi[7:8], sub.shape))

    nt = (((1,), (1,)), ((), ()))
    for k in range(n_pair):
        inter = (lax.dot_general(cre_ref[k], sre_ref[k].astype(BF16), nt, preferred_element_type=F32)
                 + lax.dot_general(cim_ref[k], sim_ref[k].astype(BF16), nt,
                                   preferred_element_type=F32))
        for i in (2 * k, 2 * k + 1):
            y = ym_ref[i] + inter[(i % 2) * QC:(i % 2 + 1) * QC]
            for b in range(nb):
                yb = y[:, b * n_chunk:(b + 1) * n_chunk].reshape(N_PHASE, C, n_chunk)
                yb = yb + d_ref[i * C:(i + 1) * C] * grp(i, b).astype(F32)
                y_ref[b, :, i * C:(i + 1) * C, :] = yb.astype(BF16)


def _ssm_core(UT, prm, *, n_pair=2):
    B, _, _, n_chunk = UT.shape
    N = B * n_chunk
    G2 = SSM_N_GROUPS // 2
    C2 = 2 * SSM_GROUP
    QC = N_PHASE * SSM_GROUP
    P = SSM_STATE
    blk = lambda *shape: pl.BlockSpec(shape, lambda g: (g,) + (0,) * (len(shape) - 1))
    return pl.pallas_call(
        functools.partial(_ssm_core_kernel, n_chunk=n_chunk, n_pair=n_pair),
        grid=(G2 // n_pair,),
        in_specs=[
            pl.BlockSpec((B, N_PHASE, n_pair * C2, n_chunk), lambda g: (0, 0, g, 0)),
            blk(2 * n_pair, QC, QC), blk(2 * n_pair, P, QC), blk(2 * n_pair, P, QC),
            blk(n_pair, 2 * QC, 2 * P), blk(n_pair, 2 * QC, 2 * P),
            blk(n_pair, 1, 2 * P), blk(n_pair, 1, 2 * P),
            blk(n_pair * C2, 1),
        ],
        out_specs=pl.BlockSpec((B, N_PHASE, n_pair * C2, n_chunk), lambda g: (0, 0, g, 0)),
        out_shape=jax.ShapeDtypeStruct((B, N_PHASE, SSM_N_GROUPS * SSM_GROUP, n_chunk), BF16),
        scratch_shapes=[pltpu.VMEM((n_pair, N, 2 * P), F32), pltpu.VMEM((n_pair, N, 2 * P), F32),
                        pltpu.VMEM((2 * n_pair, QC, N), F32)],
        compiler_params=_cparams(("parallel",)),
        name="ssm_core",
    )(UT, prm["m"], prm["b_re"], prm["b_im"], prm["c_re"], prm["c_im"],
      prm["a_re"], prm["a_im"], prm["d"])


def _ssm_param_stage_a(arc_ref, aic_ref, arr_ref, air_ref, ldr_ref, bre_ref, bim_ref):
    Q, C, P = N_PHASE, SSM_GROUP, SSM_STATE
    QC = Q * C
    hi = lax.Precision.HIGHEST
    tile = (lax.broadcasted_iota(jnp.int32, (C, QC), 1) % C
            == lax.broadcasted_iota(jnp.int32, (C, QC), 0)).astype(F32)

    dt = jnp.exp(ldr_ref[...])
    mag = jnp.exp(arr_ref[...] * dt)
    a1r, a1i = mag * jnp.cos(air_ref[...] * dt), mag * jnp.sin(air_ref[...] * dt)
    a1r_c = jnp.broadcast_to(a1r, (8, 2 * P)).T[:, :1]
    a1i_c = jnp.broadcast_to(a1i, (8, 2 * P)).T[:, :1]

    pows = [(jnp.ones_like(a1r), jnp.zeros_like(a1r))]
    for _ in range(Q):
        qr, qi = pows[-1]
        pows.append((qr * a1r - qi * a1i, qr * a1i + qi * a1r))

    bb_t = []
    for i in range(2):
        a_re, a_im = arc_ref[i], aic_ref[i]
        abr, abi = a1r_c[i * P:(i + 1) * P], a1i_c[i * P:(i + 1) * P]
        den = a_re * a_re + a_im * a_im
        cfr = ((abr - 1.0) * a_re + abi * a_im) / den
        cfi = (abi * a_re - (abr - 1.0) * a_im) / den
        bbr = cfr * bre_ref[i] - cfi * bim_ref[i]
        bbi = cfr * bim_ref[i] + cfi * bre_ref[i]
        bb_t.append((jnp.dot(bbr, tile, precision=hi, preferred_element_type=F32),
                     jnp.dot(bbi, tile, precision=hi, preferred_element_type=F32)))
    return pows, bb_t


def _ssm_param_stage_b(pows, bb_t, cre_ref, cim_ref, cpr_ref, cpi_ref,
                       m_ref, br_ref, bi_ref, cr_ref, ci_ref, ar_ref, ai_ref):
    Q, C, P = N_PHASE, SSM_GROUP, SSM_STATE
    QC = Q * C
    hi = lax.Precision.HIGHEST
    pw_r = jnp.concatenate([jnp.broadcast_to(pows[Q - 1 - s][0], (C, 2 * P)) for s in range(Q)], axis=0).T
    pw_i = jnp.concatenate([jnp.broadcast_to(pows[Q - 1 - s][1], (C, 2 * P)) for s in range(Q)], axis=0).T

    for i in range(2):
        bbr_t, bbi_t = bb_t[i]
        pr, pi = pw_r[i * P:(i + 1) * P], pw_i[i * P:(i + 1) * P]
        bst_r = pr * bbr_t - pi * bbi_t
        bst_i = pr * bbi_t + pi * bbr_t
        br_ref[i] = bst_r.astype(BF16)
        bi_ref[i] = bst_i.astype(BF16)

        kl = (jnp.dot(cre_ref[i], bst_r, precision=hi, preferred_element_type=F32)
              - jnp.dot(cim_ref[i], bst_i, precision=hi, preferred_element_type=F32))
        klz = jnp.concatenate([kl, jnp.zeros_like(kl)], axis=1)
        for t in range(Q):
            sh = (2 * QC - C * (Q - 1 - t)) % (2 * QC)
            row = klz if sh == 0 else pltpu.roll(klz, sh, axis=1)
            m_ref[i, t * C:(t + 1) * C, :] = row[:, :QC].astype(BF16)

    for t in range(Q):
        qr, qi = pows[t + 1]
        for i in range(2):
            rows = slice(i * QC + t * C, i * QC + (t + 1) * C)
            cr_ref[rows, :] = (cpr_ref[i] * qr - cpi_ref[i] * qi).astype(BF16)
            ci_ref[rows, :] = (-(cpr_ref[i] * qi + cpi_ref[i] * qr)).astype(BF16)
    ar_ref[...], ai_ref[...] = pows[Q]


def _ssm_out_kernel(y_ref, z_ref, wg_ref, bg_ref, wo_ref, gp_ref, x_ref, o_hbm, obuf, sem,
                    wgt_ref, wot_ref, *, nb, n_sub):
    s, j = pl.program_id(0), pl.program_id(1)
    nj = pl.num_programs(1)
    step = s * nj + j
    last = pl.num_programs(0) * nj - 1
    slot = step % 2

    def out_copy(slot_, s_, j_):
        return pltpu.make_async_copy(obuf.at[slot_], o_hbm.at[pl.ds(j_ * nb, nb), :, s_, :],
                                     sem.at[slot_])

    @pl.when(step >= 2)
    def _():
        out_copy(slot, (step - 2) // nj, (step - 2) % nj).wait()

    @pl.when(step == 0)
    def _():
        wgt_ref[...] = wg_ref[...].T.astype(BF16)
        wot_ref[...] = wo_ref[...].T.astype(BF16)

    bs = nb // n_sub

    def lanes(ref, k):
        return jnp.concatenate([ref[b] for b in range(k * bs, (k + 1) * bs)], axis=1).astype(F32)

    def glu_in(k):
        g = _gelu_tanh(lanes(y_ref, k))
        return g, jnp.dot(wgt_ref[...], g.astype(BF16), preferred_element_type=F32)

    def proj(k, g, gl):
        y2 = g * _sigmoid(gl + bg_ref[...])
        z = lanes(z_ref, k)
        gated = (y2 * (z * _sigmoid(z))).astype(BF16)
        return jnp.dot(wot_ref[...], gated, preferred_element_type=F32)

    def finish(k, ot):
        ms = jnp.mean(ot * ot, axis=0, keepdims=True)
        nt = (ot * lax.rsqrt(ms + EPS)).T
        rows = slice(k * bs, (k + 1) * bs)
        obuf[slot, rows] = x_ref[rows] + (nt * gp_ref[...]).reshape(bs, -1, nt.shape[-1])

    a = [glu_in(k) for k in range(n_sub)]
    o = [proj(k, *a[k]) for k in range(n_sub)]
    for k in range(n_sub):
        finish(k, o[k])
    out_copy(slot, s, j).start()

    @pl.when(step == last)
    def _():
        out_copy(slot, s, j).wait()

    @pl.when(jnp.logical_and(step == last, last >= 1))
    def _():
        out_copy(1 - slot, (step - 1) // nj, (step - 1) % nj).wait()


def _ssm_out(YT, UT, wg, bg, wo, gp, xp, *, nb=8, n_sub=2):
    B, _, M, D = xp.shape
    E = YT.shape[2]
    out = pl.pallas_call(
        functools.partial(_ssm_out_kernel, nb=nb, n_sub=n_sub),
        grid=(N_PHASE, B // nb),
        in_specs=[
            pl.BlockSpec((nb, None, E, M), lambda s, j: (j, s, 0, 0)),
            pl.BlockSpec((nb, None, E, M), lambda s, j: (j, s, 1, 0)),
            pl.BlockSpec((E, E), lambda s, j: (0, 0)),
            pl.BlockSpec((E, 1), lambda s, j: (0, 0)),
            pl.BlockSpec((E, D), lambda s, j: (0, 0)),
            pl.BlockSpec((1, D), lambda s, j: (0, 0)),
            pl.BlockSpec((nb, None, M, D), lambda s, j: (j, s, 0, 0)),
        ],
        out_specs=pl.BlockSpec(memory_space=pl.ANY),
        out_shape=jax.ShapeDtypeStruct((B, M, N_PHASE, D), F32),
        scratch_shapes=[pltpu.VMEM((2, nb, M, D), F32), pltpu.SemaphoreType.DMA((2,)),
                        pltpu.VMEM((E, E), BF16), pltpu.VMEM((D, E), BF16)],
        compiler_params=_cparams(("arbitrary", "arbitrary")),
        name="ssm_out",
    )(YT, UT, wg, bg.reshape(E, 1), wo, gp.reshape(1, D), xp)
    return out.reshape(B, M * N_PHASE, D)


def kernel(x, rel_bias, attn_pre_norm, attn_w_in, attn_w_out, attn_post_norm, ssm_pre_norm, ssm_w_in, ssm_a_re, ssm_a_im, ssm_log_dt, ssm_b_re, ssm_b_im, ssm_c_re, ssm_c_im, ssm_d, ssm_w_glu, ssm_b_glu, ssm_w_out, ssm_post_norm):
    P, xp = _norm_proj(x, attn_pre_norm[0], attn_w_in[0], scaled_cols=3 * HEADS * HEAD_DIM,
                       scale=HEAD_DIM ** -0.5 * LOG2E)
    O = _attention(P, _bias_tables(rel_bias))
    xp, prm = _out_proj(O, attn_w_out[0], attn_post_norm[0], xp, ssm_a_re[0], ssm_a_im[0],
                        ssm_log_dt[0], ssm_b_re[0], ssm_b_im[0], ssm_c_re[0], ssm_c_im[0], ssm_d[0])

    UT = _ssm_in_proj(xp, ssm_pre_norm[0], ssm_w_in[0])
    YT = _ssm_core(UT, prm)
    return _ssm_out(YT, UT, ssm_w_glu[0], ssm_b_glu[0], ssm_w_out[0], ssm_post_norm[0], xp)
```

```python
import functools
import math

import numpy as np
import jax
import jax.numpy as jnp
from jax import lax
from jax.experimental import pallas as pl
from jax.experimental.pallas import tpu as pltpu

F32 = jnp.float32
BF16 = jnp.bfloat16

D_MODEL = 1024
HEAD_DIM = 64
HEADS = 16
N_PHASE = 16
BLK = 128
DILATIONS = (1, 4, 16)
REL_BUCKETS = 32
REL_MAX_DIST = 2048
SSM_GROUP = 16
SSM_N_GROUPS = 64
SSM_STATE = 64
EPS = 1e-6
NEG = -1e30
VMEM_LIMIT = 56 * 1024 * 1024
NORM_CHUNK = 4


def _cparams(sem):
    return pltpu.CompilerParams(dimension_semantics=sem, vmem_limit_bytes=VMEM_LIMIT)


LOG2E = math.log2(math.e)


def _sigmoid(v):
    return 1.0 / (1.0 + jnp.exp2(v * (-LOG2E)))


def _gelu_tanh(x):
    k0 = -2.0 * math.sqrt(2.0 / math.pi) * LOG2E
    return x / (1.0 + jnp.exp2(x * (k0 + (k0 * 0.044715) * (x * x))))


def _norm_proj_kernel(x_hbm, g_ref, w_ref, o_ref, xp_hbm, xbuf, h_ref, gsem, osem, *,
                      scaled_tiles, scale):
    b, j = pl.program_id(0), pl.program_id(1)
    n_b, n_j = pl.num_programs(0), pl.num_programs(1)
    slot = b % 2

    def phase_copy(bb, sl, r):
        return pltpu.make_async_copy(x_hbm.at[bb, :, r, :], xbuf.at[sl, r], gsem.at[sl])

    def gather_start(bb, sl):
        lax.fori_loop(0, N_PHASE, lambda r, c: (phase_copy(bb, sl, r).start(), c)[1], 0)

    def gather_wait(bb, sl):
        lax.fori_loop(0, N_PHASE, lambda r, c: (phase_copy(bb, sl, r).wait(), c)[1], 0)

    def write_out(bb, sl):
        return pltpu.make_async_copy(xbuf.at[sl], xp_hbm.at[bb], osem.at[sl])

    M = xbuf.shape[2]

    def weights():
        return (w_ref[...] * jnp.where(j < scaled_tiles, scale, 1.0).astype(F32)).astype(BF16)

    @pl.when(j == 0)
    def _():
        @pl.when(b == 0)
        def _():
            gather_start(0, 0)

        gather_wait(b, slot)

        @pl.when(b >= 1)
        def _():
            write_out(b - 1, 1 - slot).wait()

        @pl.when(b + 1 < n_b)
        def _():
            gather_start(b + 1, 1 - slot)

        write_out(b, slot).start()
        w = weights()
        for c in range(0, N_PHASE, NORM_CHUNK):
            x = xbuf[slot, c:c + NORM_CHUNK].reshape(NORM_CHUNK * M, -1)
            ms = jnp.mean(x * x, axis=-1, keepdims=True)
            hc = (x * lax.rsqrt(ms + EPS) * g_ref[...]).astype(BF16)
            h_ref[c * M:(c + NORM_CHUNK) * M, :] = hc
            res = jnp.dot(hc, w, preferred_element_type=F32)
            o_ref[c:c + NORM_CHUNK] = res.reshape(NORM_CHUNK, M, -1).astype(BF16)

    @pl.when(j > 0)
    def _():
        res = jnp.dot(h_ref[...], weights(), preferred_element_type=F32)
        o_ref[...] = res.reshape(o_ref.shape).astype(BF16)

    @pl.when(jnp.logical_and(b == n_b - 1, j == n_j - 1))
    def _():
        write_out(b, slot).wait()


def _norm_proj(x, g, w, *, scaled_cols, scale, tn=1024):
    B, S, D = x.shape
    M = S // N_PHASE
    N = w.shape[1]
    assert scaled_cols % tn == 0
    return pl.pallas_call(
        functools.partial(_norm_proj_kernel, scaled_tiles=scaled_cols // tn, scale=scale),
        grid=(B, N // tn),
        in_specs=[
            pl.BlockSpec(memory_space=pl.ANY),
            pl.BlockSpec((1, D), lambda b, j: (0, 0)),
            pl.BlockSpec((D, tn), lambda b, j: (0, j)),
        ],
        out_specs=[pl.BlockSpec((None, N_PHASE, M, tn), lambda b, j: (b, 0, 0, j)),
                   pl.BlockSpec(memory_space=pl.ANY)],
        out_shape=[jax.ShapeDtypeStruct((B, N_PHASE, M, N), BF16),
                   jax.ShapeDtypeStruct((B, N_PHASE, M, D), F32)],
        scratch_shapes=[pltpu.VMEM((2, N_PHASE, M, D), F32), pltpu.VMEM((S, D), BF16),
                        pltpu.SemaphoreType.DMA((2,)), pltpu.SemaphoreType.DMA((2,))],
        compiler_params=_cparams(("arbitrary", "arbitrary")),
        name="attn_norm_proj",
    )(x.reshape(B, M, N_PHASE, D), g.reshape(1, D), w)


def _attn_kernel(q0_ref, q1_ref, q2_ref, k0_ref, k1_ref, k2_ref, v0_ref, v1_ref, v2_ref,
                 z_ref, bm_ref, o_ref, qf_ref, kf_ref, vf_ref, acc_ref, l_ref, m_ref):
    W = 2 * HEAD_DIM
    AHEAD = 3
    lane = lax.broadcasted_iota(jnp.int32, (BLK, W), 1)
    first_head = lane < HEAD_DIM

    def logits(q, k, g, cur_only):
        zq = jnp.zeros_like(q)
        qs = jnp.concatenate([jnp.where(first_head, q, zq), jnp.where(first_head, zq, q)], axis=0)
        s = lax.dot_general(qs, k, (((1,), (1,)), ((), ())), preferred_element_type=F32)
        return s + (bm_ref[g, :, BLK:2 * BLK] if cur_only else bm_ref[g])

    ones_a = jnp.where(first_head, 1.0, 0.0).astype(BF16)
    ones_b = jnp.where(first_head, 0.0, 1.0).astype(BF16)
    masked_v = {}

    def value_blocks(g, v_ref, pieces, cast):
        key = (g, tuple(pieces))
        if key not in masked_v:
            v = rows_of(v_ref, pieces)
            v = v.astype(BF16) if cast else v
            zv = jnp.zeros_like(v)
            masked_v[key] = (jnp.concatenate([jnp.where(first_head, v, zv), ones_a], axis=1),
                             jnp.concatenate([jnp.where(first_head, zv, v), ones_b], axis=1))
        return masked_v[key]

    def finish(s, vblocks):
        m = jnp.max(s, axis=-1, keepdims=True)
        p = jnp.exp2(s - m).astype(BF16)
        pcat = jnp.concatenate([p[:BLK], p[BLK:]], axis=1)
        rhs = jnp.concatenate([vb[0] for vb in vblocks] + [vb[1] for vb in vblocks], axis=0)
        pv = jnp.dot(pcat, rhs, preferred_element_type=F32)
        mm = jnp.where(first_head, m[:BLK], m[BLK:])
        return pv[:, :W], pv[:, W:], mm

    def rows_of(ref, pieces):
        return jnp.concatenate([ref[ph, lo:lo + n, :] for ph, lo, n in pieces], axis=0)

    def keys_of(ref, prev, cur):
        return rows_of(ref, cur) if prev is None else rows_of(ref, prev + cur)

    def store(g, pieces, vals):
        at = 0
        for ph, lo, n in pieces:
            for ref, val in zip((acc_ref, l_ref, m_ref), vals):
                ref[g, ph, lo:lo + n, :] = val[at:at + n]
            at += n

    M = q0_ref.shape[1]
    g2_blocks = [[(r, 0, BLK)] for r in range(N_PHASE)]
    g1_blocks = {(r4, n): [(4 * q4 + r4, 32 * n, 32) for q4 in range(4)]
                 for r4 in range(4) for n in range(M // 32)}
    g0_blocks = [[(r, 8 * n, 8) for r in range(N_PHASE)] for n in range(M // 8)]

    for r in range(N_PHASE):
        qf_ref[r] = q0_ref[r].astype(F32)
        kf_ref[r] = k0_ref[r].astype(F32)
        vf_ref[r] = v0_ref[r].astype(F32)

    work = []
    for n, pieces in enumerate(g0_blocks):
        work.append((0, pieces, g0_blocks[n - 1] if n else None, (qf_ref, kf_ref, vf_ref), True))
    for (r4, n), pieces in g1_blocks.items():
        work.append((1, pieces, g1_blocks[(r4, n - 1)] if n else None, (q1_ref, k1_ref, v1_ref), False))
    for pieces in g2_blocks:
        work.append((2, pieces, None, (q2_ref, k2_ref, v2_ref), False))

    def merge(r, acc2, l2, m2):
        m0, m1 = m_ref[0, r], m_ref[1, r]
        mx = jnp.maximum(jnp.maximum(m0, m1), m2)
        w0, w1, w2 = jnp.exp2(m0 - mx), jnp.exp2(m1 - mx), jnp.exp2(m2 - mx)
        num = w0 * acc_ref[0, r] + w1 * acc_ref[1, r] + w2 * acc2
        den = w0 * l_ref[0, r] + w1 * l_ref[1, r] + w2 * l2
        z = z_ref[r].astype(F32)
        o_ref[r] = (num * z / (den * (1.0 + jnp.exp2(z * (-LOG2E))))).astype(BF16)

    pending = []
    for item in work + [None] * AHEAD:
        if item is not None:
            g, pieces, prev, (q_r, k_r, _), cast = item
            q, k = rows_of(q_r, pieces), keys_of(k_r, prev, pieces)
            if cast:
                q, k = q.astype(BF16), k.astype(BF16)
            pending.append((item, logits(q, k, g, prev is None)))
        if item is None or len(pending) > AHEAD:
            (g_p, pieces_p, prev_p, (_, _, v_r), cast_p), s_p = pending.pop(0)
            key_blocks = ([] if prev_p is None else [prev_p]) + [pieces_p]
            vals = finish(s_p, [value_blocks(g_p, v_r, kb, cast_p) for kb in key_blocks])
            if g_p == 2:
                merge(pieces_p[0][0], *vals)
            else:
                store(g_p, pieces_p, vals)


def _attention(P, bm):
    B, _, M, _ = P.shape
    HP = HEADS // 2
    W = 2 * HEAD_DIM

    def spec(kind, g):
        base = (kind * 3 + g) * HP
        return pl.BlockSpec((None, N_PHASE, M, W), lambda b, hp, base=base: (b, 0, 0, base + hp))

    in_specs = [spec(kind, g) for kind in range(3) for g in range(3)]
    in_specs.append(pl.BlockSpec((None, N_PHASE, M, W), lambda b, hp: (b, 0, 0, 9 * HP + hp)))
    in_specs.append(pl.BlockSpec((3, None, 2 * BLK, 2 * BLK), lambda b, hp: (0, hp, 0, 0)))
    return pl.pallas_call(
        _attn_kernel,
        grid=(B, HP),
        in_specs=in_specs,
        out_specs=pl.BlockSpec((None, N_PHASE, M, W), lambda b, hp: (b, 0, 0, hp)),
        out_shape=jax.ShapeDtypeStruct((B, N_PHASE, M, HEADS * HEAD_DIM), BF16),
        scratch_shapes=[pltpu.VMEM((N_PHASE, M, W), F32) for _ in range(3)]
        + [pltpu.VMEM((2, N_PHASE, M, W), F32) for _ in range(3)],
        compiler_params=_cparams(("parallel", "parallel")),
        name="dilated_attention",
    )(*([P] * 10), bm)


def _t5_bucket(dist):
    max_exact = REL_BUCKETS // 2
    n = jnp.maximum(dist, 1).astype(F32)
    large = max_exact + (jnp.log(n / max_exact) / math.log(REL_MAX_DIST / max_exact)
                         * (REL_BUCKETS - max_exact)).astype(jnp.int32)
    large = jnp.minimum(large, REL_BUCKETS - 1)
    return jnp.where(dist < max_exact, dist, large)


def _bias_tables(rel_bias):
    a = np.arange(BLK)
    pos = (16 * (a % 8) + a // 8, 4 * (a % 32) + a // 32, a)
    back = np.stack([np.concatenate([BLK + p[:, None] - p[None, :], p[:, None] - p[None, :]], axis=1)
                     for p in pos])
    valid = (back >= 0) & (back <= BLK)
    dist = np.clip(back, 0, BLK) * np.asarray(DILATIONS)[:, None, None]
    bucket = jnp.where(jnp.asarray(valid), _t5_bucket(jnp.asarray(dist, jnp.int32)), REL_BUCKETS)
    onehot = (bucket[..., None] == jnp.arange(REL_BUCKETS + 1)).astype(F32)
    ext = jnp.concatenate([rel_bias.astype(F32), jnp.full((1, HEADS), NEG, F32)], axis=0)
    t = jnp.einsum("gijc,ch->ghij", onehot, ext * math.log2(math.e),
                   precision=lax.Precision.HIGHEST)
    return t.reshape(3, HEADS // 2, 2 * BLK, 2 * BLK)


def _out_proj_kernel(o_ref, w_ref, g_ref, x_ref, *rest, mh):
    prm_in, y_ref, prm_out = rest[:11], rest[11], rest[12:]
    n_pair = prm_out[-1].shape[0]

    def of_pair(refs, k):
        return [r.at[2 * k:2 * k + 2] if r.shape[0] == 2 * n_pair else r.at[k] for r in refs]

    half = [_ssm_param_stage_a(*of_pair(prm_in[:7], k)) for k in range(n_pair)]
    o = o_ref[...].reshape(N_PHASE * mh, o_ref.shape[-1])
    h = jnp.dot(o, w_ref[...].astype(BF16), preferred_element_type=F32)
    for k in range(n_pair):
        _ssm_param_stage_b(*half[k], *of_pair(prm_in[7:] + prm_out, k))
    ms = jnp.mean(h * h, axis=-1, keepdims=True)
    y = h * lax.rsqrt(ms + EPS) * g_ref[...]
    y_ref[...] = x_ref[...] + y.reshape(y_ref.shape)


def _out_proj(O, w, g, xp, a_re, a_im, log_dt, b_re, b_im, c_re, c_im, d_skip, *, mh=64):
    B, _, M, D = xp.shape
    G, P, C, Q = SSM_N_GROUPS, SSM_STATE, SSM_GROUP, N_PHASE
    QC = Q * C
    nm = M // mh
    n_pair = (G // 2) // (B * nm)
    assert B * nm * n_pair == G // 2
    f = lambda t: t.astype(F32)
    a_re, a_im, log_dt, b_re, b_im, c_re, c_im = map(f, (a_re, a_im, log_dt, b_re, b_im, c_re, c_im))
    even = (jnp.arange(G) % 2 == 0)[:, None, None]

    def lane_half(c):
        return jnp.concatenate([jnp.where(even, c, 0.0), jnp.where(even, 0.0, c)], axis=-1)

    pair = lambda blk: pl.BlockSpec((2 * n_pair,) + blk, lambda b, m: (b * nm + m,) + (0,) * len(blk))
    one = lambda blk: pl.BlockSpec((n_pair,) + blk, lambda b, m: (b * nm + m,) + (0,) * len(blk))
    outs = pl.pallas_call(
        functools.partial(_out_proj_kernel, mh=mh),
        grid=(B, nm),
        in_specs=[
            pl.BlockSpec((None, N_PHASE, mh, O.shape[-1]), lambda b, m: (b, 0, m, 0)),
            pl.BlockSpec(w.shape, lambda b, m: (0, 0)),
            pl.BlockSpec((1, D), lambda b, m: (0, 0)),
            pl.BlockSpec((None, N_PHASE, mh, D), lambda b, m: (b, 0, m, 0)),
            pair((P, 1)), pair((P, 1)), one((1, 2 * P)), one((1, 2 * P)), one((1, 2 * P)),
            pair((P, C)), pair((P, C)), pair((C, P)), pair((C, P)), pair((C, 2 * P)), pair((C, 2 * P)),
        ],
        out_specs=[pl.BlockSpec((None, N_PHASE, mh, D), lambda b, m: (b, 0, m, 0)),
                   pair((QC, QC)), pair((P, QC)), pair((P, QC)), one((2 * QC, 2 * P)),
                   one((2 * QC, 2 * P)), one((1, 2 * P)), one((1, 2 * P))],
        out_shape=[jax.ShapeDtypeStruct(xp.shape, F32),
                   jax.ShapeDtypeStruct((G, QC, QC), BF16), jax.ShapeDtypeStruct((G, P, QC), BF16),
                   jax.ShapeDtypeStruct((G, P, QC), BF16),
                   jax.ShapeDtypeStruct((G // 2, 2 * QC, 2 * P), BF16),
                   jax.ShapeDtypeStruct((G // 2, 2 * QC, 2 * P), BF16),
                   jax.ShapeDtypeStruct((G // 2, 1, 2 * P), F32),
                   jax.ShapeDtypeStruct((G // 2, 1, 2 * P), F32)],
        compiler_params=_cparams(("parallel", "parallel")),
        name="attn_out_proj",
    )(O, w, g.reshape(1, D), xp,
      a_re.reshape(G, P, 1), a_im.reshape(G, P, 1),
      a_re.reshape(G // 2, 1, 2 * P), a_im.reshape(G // 2, 1, 2 * P),
      jnp.repeat(log_dt, P).reshape(G // 2, 1, 2 * P), b_re, b_im, c_re, c_im,
      lane_half(c_re), lane_half(c_im))
    keys = ("m", "b_re", "b_im", "c_re", "c_im", "a_re", "a_im")
    prm = dict(zip(keys, outs[1:]))
    prm["d"] = d_skip.astype(F32).reshape(G * C, 1)
    return outs[0], prm


def _ssm_in_proj_kernel(x_ref, g_ref, w_ref, o_ref, h_ref):
    M = x_ref.shape[1]
    nt = (((1,), (1,)), ((), ()))

    def weights():
        return w_ref[...].T.astype(BF16)

    @pl.when(pl.program_id(1) == 0)
    def _():
        wt = weights()
        for c in range(0, N_PHASE, NORM_CHUNK):
            x = x_ref[c:c + NORM_CHUNK].reshape(NORM_CHUNK * M, -1)
            ms = jnp.mean(x * x, axis=-1, keepdims=True)
            hc = (x * lax.rsqrt(ms + EPS) * g_ref[...]).astype(BF16)
            h_ref[c * M:(c + NORM_CHUNK) * M, :] = hc
            res = lax.dot_general(wt, hc, nt, preferred_element_type=F32)
            for s in range(NORM_CHUNK):
                o_ref[c + s] = res[:, s * M:(s + 1) * M].astype(BF16)

    @pl.when(pl.program_id(1) > 0)
    def _():
        res = lax.dot_general(weights(), h_ref[...], nt, preferred_element_type=F32)
        for s in range(N_PHASE):
            o_ref[s] = res[:, s * M:(s + 1) * M].astype(BF16)


def _ssm_in_proj(xp, g, w, *, tn=1024):
    B, _, M, D = xp.shape
    S = N_PHASE * M
    N = w.shape[1]
    return pl.pallas_call(
        _ssm_in_proj_kernel,
        grid=(B, N // tn),
        in_specs=[
            pl.BlockSpec((None, N_PHASE, M, D), lambda b, j: (b, 0, 0, 0)),
            pl.BlockSpec((1, D), lambda b, j: (0, 0)),
            pl.BlockSpec((D, tn), lambda b, j: (0, j)),
        ],
        out_specs=pl.BlockSpec((None, N_PHASE, tn, M), lambda b, j: (b, 0, j, 0)),
        out_shape=jax.ShapeDtypeStruct((B, N_PHASE, N, M), BF16),
        scratch_shapes=[pltpu.VMEM((S, D), BF16)],
        compiler_params=_cparams(("parallel", "arbitrary")),
        name="ssm_in_proj",
    )(xp, g.reshape(1, D), w)


def _ssm_core_kernel(u_ref, m_ref, bre_ref, bim_ref, cre_ref, cim_ref, are_ref, aim_ref,
                     d_ref, y_ref, sre_ref, sim_ref, ym_ref, *, n_chunk, n_pair):
    C = SSM_GROUP
    QC = N_PHASE * C
    nb = u_ref.shape[0]
    n_grp = 2 * n_pair

    def grp(i, b):
        return u_ref[b, :, i * C:(i + 1) * C, :]

    us = [jnp.concatenate([grp(i, b).reshape(QC, n_chunk) for b in range(nb)], axis=1)
          for i in range(n_grp)]

    P = bre_ref.shape[1]
    xre, xim = [], []
    for i in range(n_grp):
        lhs = jnp.concatenate([m_ref[i], bre_ref[i], bim_ref[i]], axis=0)
        r = jnp.dot(lhs, us[i], preferred_element_type=F32)
        ym_ref[i] = r[:QC]
        xre.append(r[QC:QC + P])
        xim.append(r[QC + P:])
    for k in range(n_pair):
        sre_ref[k] = jnp.concatenate(xre[2 * k:2 * k + 2], axis=0).T
        sim_ref[k] = jnp.concatenate(xim[2 * k:2 * k + 2], axis=0).T

    sub = lax.broadcasted_iota(jnp.int32, (8, are_ref.shape[-1]), 0)

    def cmul(xr, xi, yr, yi):
        return xr * yr - xi * yi, xr * yi + xi * yr

    for k in range(n_pair):
        a1 = (are_ref[k], aim_ref[k])
        a2 = cmul(*a1, *a1)
        a4 = cmul(*a2, *a2)
        pows = [a1, a2, cmul(*a2, *a1), a4, cmul(*a4, *a1), cmul(*a4, *a2)]
        pows += [cmul(*pows[5], *a1), cmul(*a4, *a4)]
        apow_r = jnp.zeros(sub.shape, F32)
        apow_i = jnp.zeros(sub.shape, F32)
        for i, (pr, pi) in enumerate(pows):
            apow_r = jnp.where(sub == i, pr, apow_r)
            apow_i = jnp.where(sub == i, pi, apow_i)
        step = {d: (jnp.where(sub >= d, pows[d - 1][0], 0.0), jnp.where(sub >= d, pows[d - 1][1], 0.0))
                for d in (1, 2, 4)}
        carry = [(jnp.zeros(sub.shape, F32), jnp.zeros(sub.shape, F32)) for _ in range(nb)]
        for j in range(n_chunk // 8):
            for b in range(nb):
                rows = slice(b * n_chunk + 8 * j, b * n_chunk + 8 * j + 8)
                er, ei = sre_ref[k, rows, :], sim_ref[k, rows, :]
                for d in (1, 2, 4):
                    dr, di = cmul(*step[d], pltpu.roll(er, d, axis=0), pltpu.roll(ei, d, axis=0))
                    er, ei = er + dr, ei + di
                cr, ci = carry[b]
                dr, di = cmul(apow_r, apow_i, cr, ci)
                fr, fi = er + dr, ei + di
                sre_ref[k, rows, :] = jnp.where(sub == 0, cr, pltpu.roll(fr, 1, axis=0))
                sim_ref[k, rows, :] = jnp.where(sub == 0, ci, pltpu.roll(fi, 1, axis=0))
                carry[b] = (jnp.broadcast_to(fr[7:8], sub.shape), jnp.broadcast_to(fi[7:8], sub.shape))

    nt = (((1,), (1,)), ((), ()))
    for k in range(n_pair):
        inter = (lax.dot_general(cre_ref[k], sre_ref[k].astype(BF16), nt, preferred_element_type=F32)
                 + lax.dot_general(cim_ref[k], sim_ref[k].astype(BF16), nt,
                                   preferred_element_type=F32))
        for i in (2 * k, 2 * k + 1):
            y = ym_ref[i] + inter[(i % 2) * QC:(i % 2 + 1) * QC]
            for b in range(nb):
                yb = y[:, b * n_chunk:(b + 1) * n_chunk].reshape(N_PHASE, C, n_chunk)
                yb = yb + d_ref[i * C:(i + 1) * C] * grp(i, b).astype(F32)
                y_ref[b, :, i * C:(i + 1) * C, :] = yb.astype(BF16)


def _ssm_core(UT, prm, *, n_pair=2):
    B, _, _, n_chunk = UT.shape
    N = B * n_chunk
    G2 = SSM_N_GROUPS // 2
    C2 = 2 * SSM_GROUP
    QC = N_PHASE * SSM_GROUP
    P = SSM_STATE
    blk = lambda *shape: pl.BlockSpec(shape, lambda g: (g,) + (0,) * (len(shape) - 1))
    return pl.pallas_call(
        functools.partial(_ssm_core_kernel, n_chunk=n_chunk, n_pair=n_pair),
        grid=(G2 // n_pair,),
        in_specs=[
            pl.BlockSpec((B, N_PHASE, n_pair * C2, n_chunk), lambda g: (0, 0, g, 0)),
            blk(2 * n_pair, QC, QC), blk(2 * n_pair, P, QC), blk(2 * n_pair, P, QC),
            blk(n_pair, 2 * QC, 2 * P), blk(n_pair, 2 * QC, 2 * P),
            blk(n_pair, 1, 2 * P), blk(n_pair, 1, 2 * P),
            blk(n_pair * C2, 1),
        ],
        out_specs=pl.BlockSpec((B, N_PHASE, n_pair * C2, n_chunk), lambda g: (0, 0, g, 0)),
        out_shape=jax.ShapeDtypeStruct((B, N_PHASE, SSM_N_GROUPS * SSM_GROUP, n_chunk), BF16),
        scratch_shapes=[pltpu.VMEM((n_pair, N, 2 * P), F32), pltpu.VMEM((n_pair, N, 2 * P), F32),
                        pltpu.VMEM((2 * n_pair, QC, N), F32)],
        compiler_params=_cparams(("parallel",)),
        name="ssm_core",
    )(UT, prm["m"], prm["b_re"], prm["b_im"], prm["c_re"], prm["c_im"],
      prm["a_re"], prm["a_im"], prm["d"])


def _ssm_param_stage_a(arc_ref, aic_ref, arr_ref, air_ref, ldr_ref, bre_ref, bim_ref):
    Q, C, P = N_PHASE, SSM_GROUP, SSM_STATE
    QC = Q * C
    hi = lax.Precision.HIGHEST
    tile = (lax.broadcasted_iota(jnp.int32, (C, QC), 1) % C
            == lax.broadcasted_iota(jnp.int32, (C, QC), 0)).astype(F32)

    dt = jnp.exp(ldr_ref[...])
    mag = jnp.exp(arr_ref[...] * dt)
    a1r, a1i = mag * jnp.cos(air_ref[...] * dt), mag * jnp.sin(air_ref[...] * dt)
    a1r_c = jnp.broadcast_to(a1r, (8, 2 * P)).T[:, :1]
    a1i_c = jnp.broadcast_to(a1i, (8, 2 * P)).T[:, :1]

    pows = [(jnp.ones_like(a1r), jnp.zeros_like(a1r))]
    for _ in range(Q):
        qr, qi = pows[-1]
        pows.append((qr * a1r - qi * a1i, qr * a1i + qi * a1r))

    bb_t = []
    for i in range(2):
        a_re, a_im = arc_ref[i], aic_ref[i]
        abr, abi = a1r_c[i * P:(i + 1) * P], a1i_c[i * P:(i + 1) * P]
        den = a_re * a_re + a_im * a_im
        cfr = ((abr - 1.0) * a_re + abi * a_im) / den
        cfi = (abi * a_re - (abr - 1.0) * a_im) / den
        bbr = cfr * bre_ref[i] - cfi * bim_ref[i]
        bbi = cfr * bim_ref[i] + cfi * bre_ref[i]
        bb_t.append((jnp.dot(bbr, tile, precision=hi, preferred_element_type=F32),
                     jnp.dot(bbi, tile, precision=hi, preferred_element_type=F32)))
    return pows, bb_t


def _ssm_param_stage_b(pows, bb_t, cre_ref, cim_ref, cpr_ref, cpi_ref,
                       m_ref, br_ref, bi_ref, cr_ref, ci_ref, ar_ref, ai_ref):
    Q, C, P = N_PHASE, SSM_GROUP, SSM_STATE
    QC = Q * C
    hi = lax.Precision.HIGHEST
    pw_r = jnp.concatenate([jnp.broadcast_to(pows[Q - 1 - s][0], (C, 2 * P)) for s in range(Q)], axis=0).T
    pw_i = jnp.concatenate([jnp.broadcast_to(pows[Q - 1 - s][1], (C, 2 * P)) for s in range(Q)], axis=0).T

    for i in range(2):
        bbr_t, bbi_t = bb_t[i]
        pr, pi = pw_r[i * P:(i + 1) * P], pw_i[i * P:(i + 1) * P]
        bst_r = pr * bbr_t - pi * bbi_t
        bst_i = pr * bbi_t + pi * bbr_t
        br_ref[i] = bst_r.astype(BF16)
        bi_ref[i] = bst_i.astype(BF16)

        kl = (jnp.dot(cre_ref[i], bst_r, precision=hi, preferred_element_type=F32)
              - jnp.dot(cim_ref[i], bst_i, precision=hi, preferred_element_type=F32))
        klz = jnp.concatenate([kl, jnp.zeros_like(kl)], axis=1)
        for t in range(Q):
            sh = (2 * QC - C * (Q - 1 - t)) % (2 * QC)
            row = klz if sh == 0 else pltpu.roll(klz, sh, axis=1)
            m_ref[i, t * C:(t + 1) * C, :] = row[:, :QC].astype(BF16)

    for t in range(Q):
        qr, qi = pows[t + 1]
        for i in range(2):
            rows = slice(i * QC + t * C, i * QC + (t + 1) * C)
            cr_ref[rows, :] = (cpr_ref[i] * qr - cpi_ref[i] * qi).astype(BF16)
            ci_ref[rows, :] = (-(cpr_ref[i] * qi + cpi_ref[i] * qr)).astype(BF16)
    ar_ref[...], ai_ref[...] = pows[Q]


def _ssm_out_kernel(y_ref, z_ref, wg_ref, bg_ref, wo_ref, gp_ref, x_ref, o_hbm, obuf, sem,
                    wgt_ref, wot_ref, *, nb, n_sub):
    s, j = pl.program_id(0), pl.program_id(1)
    nj = pl.num_programs(1)
    step = s * nj + j
    last = pl.num_programs(0) * nj - 1
    slot = step % 2

    def out_copy(slot_, s_, j_):
        return pltpu.make_async_copy(obuf.at[slot_], o_hbm.at[pl.ds(j_ * nb, nb), :, s_, :],
                                     sem.at[slot_])

    @pl.when(step >= 2)
    def _():
        out_copy(slot, (step - 2) // nj, (step - 2) % nj).wait()

    @pl.when(step == 0)
    def _():
        wgt_ref[...] = wg_ref[...].T.astype(BF16)
        wot_ref[...] = wo_ref[...].T.astype(BF16)

    bs = nb // n_sub

    def lanes(ref, k):
        return jnp.concatenate([ref[b] for b in range(k * bs, (k + 1) * bs)], axis=1).astype(F32)

    def glu_in(k):
        g = _gelu_tanh(lanes(y_ref, k))
        return g, jnp.dot(wgt_ref[...], g.astype(BF16), preferred_element_type=F32)

    def proj(k, g, gl):
        y2 = g * _sigmoid(gl + bg_ref[...])
        z = lanes(z_ref, k)
        gated = (y2 * (z * _sigmoid(z))).astype(BF16)
        return jnp.dot(wot_ref[...], gated, preferred_element_type=F32)

    def finish(k, ot):
        ms = jnp.mean(ot * ot, axis=0, keepdims=True)
        nt = (ot * lax.rsqrt(ms + EPS)).T
        rows = slice(k * bs, (k + 1) * bs)
        obuf[slot, rows] = x_ref[rows] + (nt * gp_ref[...]).reshape(bs, -1, nt.shape[-1])

    a = [glu_in(k) for k in range(n_sub)]
    o = [proj(k, *a[k]) for k in range(n_sub)]
    for k in range(n_sub):
        finish(k, o[k])
    out_copy(slot, s, j).start()

    @pl.when(step == last)
    def _():
        out_copy(slot, s, j).wait()

    @pl.when(jnp.logical_and(step == last, last >= 1))
    def _():
        out_copy(1 - slot, (step - 1) // nj, (step - 1) % nj).wait()


def _ssm_out(YT, UT, wg, bg, wo, gp, xp, *, nb=8, n_sub=2):
    B, _, M, D = xp.shape
    E = YT.shape[2]
    out = pl.pallas_call(
        functools.partial(_ssm_out_kernel, nb=nb, n_sub=n_sub),
        grid=(N_PHASE, B // nb),
        in_specs=[
            pl.BlockSpec((nb, None, E, M), lambda s, j: (j, s, 0, 0)),
            pl.BlockSpec((nb, None, E, M), lambda s, j: (j, s, 1, 0)),
            pl.BlockSpec((E, E), lambda s, j: (0, 0)),
            pl.BlockSpec((E, 1), lambda s, j: (0, 0)),
            pl.BlockSpec((E, D), lambda s, j: (0, 0)),
            pl.BlockSpec((1, D), lambda s, j: (0, 0)),
            pl.BlockSpec((nb, None, M, D), lambda s, j: (j, s, 0, 0)),
        ],
        out_specs=pl.BlockSpec(memory_space=pl.ANY),
        out_shape=jax.ShapeDtypeStruct((B, M, N_PHASE, D), F32),
        scratch_shapes=[pltpu.VMEM((2, nb, M, D), F32), pltpu.SemaphoreType.DMA((2,)),
                        pltpu.VMEM((E, E), BF16), pltpu.VMEM((D, E), BF16)],
        compiler_params=_cparams(("arbitrary", "arbitrary")),
        name="ssm_out",
    )(YT, UT, wg, bg.reshape(E, 1), wo, gp.reshape(1, D), xp)
    return out.reshape(B, M * N_PHASE, D)


def kernel(x, rel_bias, attn_pre_norm, attn_w_in, attn_w_out, attn_post_norm, ssm_pre_norm, ssm_w_in, ssm_a_re, ssm_a_im, ssm_log_dt, ssm_b_re, ssm_b_im, ssm_c_re, ssm_c_im, ssm_d, ssm_w_glu, ssm_b_glu, ssm_w_out, ssm_post_norm):
    col_scale = jnp.where(jnp.arange(attn_w_in.shape[-1]) < 3 * HEADS * HEAD_DIM,
                          HEAD_DIM ** -0.5 * LOG2E, 1.0).astype(F32)
    P, xp = _norm_proj(x, attn_pre_norm[0], (attn_w_in[0] * col_scale).astype(BF16), scaled_cols=0,
                       scale=1.0)
    O = _attention(P, _bias_tables(rel_bias))
    xp, prm = _out_proj(O, attn_w_out[0], attn_post_norm[0], xp, ssm_a_re[0], ssm_a_im[0],
                        ssm_log_dt[0], ssm_b_re[0], ssm_b_im[0], ssm_c_re[0], ssm_c_im[0], ssm_d[0])

    UT = _ssm_in_proj(xp, ssm_pre_norm[0], ssm_w_in[0])
    YT = _ssm_core(UT, prm)
    return _ssm_out(YT, UT, ssm_w_glu[0], ssm_b_glu[0], ssm_w_out[0], ssm_post_norm[0], xp)
```

```python
import functools
import math

import numpy as np
import jax
import jax.numpy as jnp
from jax import lax
from jax.experimental import pallas as pl
from jax.experimental.pallas import tpu as pltpu

F32 = jnp.float32
BF16 = jnp.bfloat16

D_MODEL = 1024
HEAD_DIM = 64
HEADS = 16
N_PHASE = 16
BLK = 128
DILATIONS = (1, 4, 16)
REL_BUCKETS = 32
REL_MAX_DIST = 2048
SSM_GROUP = 16
SSM_N_GROUPS = 64
SSM_STATE = 64
EPS = 1e-6
NEG = -1e30
VMEM_LIMIT = 56 * 1024 * 1024
NORM_CHUNK = 4


def _cparams(sem):
    return pltpu.CompilerParams(dimension_semantics=sem, vmem_limit_bytes=VMEM_LIMIT)


LOG2E = math.log2(math.e)


def _sigmoid(v):
    return 1.0 / (1.0 + jnp.exp2(v * (-LOG2E)))


def _gelu_tanh(x):
    k0 = -2.0 * math.sqrt(2.0 / math.pi) * LOG2E
    return x / (1.0 + jnp.exp2(x * (k0 + (k0 * 0.044715) * (x * x))))


def _norm_proj_kernel(x_hbm, g_ref, w_ref, o_ref, xp_hbm, xbuf, h_ref, gsem, osem, *,
                      scaled_tiles, scale):
    b, j = pl.program_id(0), pl.program_id(1)
    n_b, n_j = pl.num_programs(0), pl.num_programs(1)
    slot = b % 2

    def phase_copy(bb, sl, r):
        return pltpu.make_async_copy(x_hbm.at[bb, :, r, :], xbuf.at[sl, r], gsem.at[sl])

    def gather_start(bb, sl):
        lax.fori_loop(0, N_PHASE, lambda r, c: (phase_copy(bb, sl, r).start(), c)[1], 0)

    def gather_wait(bb, sl):
        lax.fori_loop(0, N_PHASE, lambda r, c: (phase_copy(bb, sl, r).wait(), c)[1], 0)

    def write_out(bb, sl):
        return pltpu.make_async_copy(xbuf.at[sl], xp_hbm.at[bb], osem.at[sl])

    M = xbuf.shape[2]

    def weights():
        return (w_ref[...] * jnp.where(j < scaled_tiles, scale, 1.0).astype(F32)).astype(BF16)

    @pl.when(j == 0)
    def _():
        @pl.when(b == 0)
        def _():
            gather_start(0, 0)

        gather_wait(b, slot)

        @pl.when(b >= 1)
        def _():
            write_out(b - 1, 1 - slot).wait()

        @pl.when(b + 1 < n_b)
        def _():
            gather_start(b + 1, 1 - slot)

        write_out(b, slot).start()
        w = weights()
        for c in range(0, N_PHASE, NORM_CHUNK):
            x = xbuf[slot, c:c + NORM_CHUNK].reshape(NORM_CHUNK * M, -1)
            ms = jnp.mean(x * x, axis=-1, keepdims=True)
            hc = (x * lax.rsqrt(ms + EPS) * g_ref[...]).astype(BF16)
            h_ref[c * M:(c + NORM_CHUNK) * M, :] = hc
            res = jnp.dot(hc, w, preferred_element_type=F32)
            o_ref[c:c + NORM_CHUNK] = res.reshape(NORM_CHUNK, M, -1).astype(BF16)

    @pl.when(j > 0)
    def _():
        res = jnp.dot(h_ref[...], weights(), preferred_element_type=F32)
        o_ref[...] = res.reshape(o_ref.shape).astype(BF16)

    @pl.when(jnp.logical_and(b == n_b - 1, j == n_j - 1))
    def _():
        write_out(b, slot).wait()


def _norm_proj(x, g, w, *, scaled_cols, scale, tn=1024):
    B, S, D = x.shape
    M = S // N_PHASE
    N = w.shape[1]
    assert scaled_cols % tn == 0
    return pl.pallas_call(
        functools.partial(_norm_proj_kernel, scaled_tiles=scaled_cols // tn, scale=scale),
        grid=(B, N // tn),
        in_specs=[
            pl.BlockSpec(memory_space=pl.ANY),
            pl.BlockSpec((1, D), lambda b, j: (0, 0)),
            pl.BlockSpec((D, tn), lambda b, j: (0, j)),
        ],
        out_specs=[pl.BlockSpec((None, N_PHASE, M, tn), lambda b, j: (b, 0, 0, j)),
                   pl.BlockSpec(memory_space=pl.ANY)],
        out_shape=[jax.ShapeDtypeStruct((B, N_PHASE, M, N), BF16),
                   jax.ShapeDtypeStruct((B, N_PHASE, M, D), F32)],
        scratch_shapes=[pltpu.VMEM((2, N_PHASE, M, D), F32), pltpu.VMEM((S, D), BF16),
                        pltpu.SemaphoreType.DMA((2,)), pltpu.SemaphoreType.DMA((2,))],
        compiler_params=_cparams(("arbitrary", "arbitrary")),
        name="attn_norm_proj",
    )(x.reshape(B, M, N_PHASE, D), g.reshape(1, D), w)


def _attn_kernel(q0_ref, q1_ref, q2_ref, k0_ref, k1_ref, k2_ref, v0_ref, v1_ref, v2_ref,
                 z_ref, bm_ref, o_ref, qf_ref, kf_ref, vf_ref, acc_ref, l_ref, m_ref):
    W = 2 * HEAD_DIM
    AHEAD = 3
    lane = lax.broadcasted_iota(jnp.int32, (BLK, W), 1)
    first_head = lane < HEAD_DIM

    def logits(q, k, g, cur_only):
        zq = jnp.zeros_like(q)
        qs = jnp.concatenate([jnp.where(first_head, q, zq), jnp.where(first_head, zq, q)], axis=0)
        s = lax.dot_general(qs, k, (((1,), (1,)), ((), ())), preferred_element_type=F32)
        return s + (bm_ref[g, :, BLK:2 * BLK] if cur_only else bm_ref[g])

    ones_a = jnp.where(first_head, 1.0, 0.0).astype(BF16)
    ones_b = jnp.where(first_head, 0.0, 1.0).astype(BF16)
    masked_v = {}

    def value_blocks(g, v_ref, pieces, cast):
        key = (g, tuple(pieces))
        if key not in masked_v:
            v = rows_of(v_ref, pieces)
            v = v.astype(BF16) if cast else v
            zv = jnp.zeros_like(v)
            masked_v[key] = (jnp.concatenate([jnp.where(first_head, v, zv), ones_a], axis=1),
                             jnp.concatenate([jnp.where(first_head, zv, v), ones_b], axis=1))
        return masked_v[key]

    def finish(s, vblocks):
        m = jnp.max(s, axis=-1, keepdims=True)
        p = jnp.exp2(s - m).astype(BF16)
        pcat = jnp.concatenate([p[:BLK], p[BLK:]], axis=1)
        rhs = jnp.concatenate([vb[0] for vb in vblocks] + [vb[1] for vb in vblocks], axis=0)
        pv = jnp.dot(pcat, rhs, preferred_element_type=F32)
        mm = jnp.where(first_head, m[:BLK], m[BLK:])
        return pv[:, :W], pv[:, W:], mm

    def rows_of(ref, pieces):
        return jnp.concatenate([ref[ph, lo:lo + n, :] for ph, lo, n in pieces], axis=0)

    def keys_of(ref, prev, cur):
        return rows_of(ref, cur) if prev is None else rows_of(ref, prev + cur)

    def store(g, pieces, vals):
        at = 0
        for ph, lo, n in pieces:
            for ref, val in zip((acc_ref, l_ref, m_ref), vals):
                ref[g, ph, lo:lo + n, :] = val[at:at + n]
            at += n

    M = q0_ref.shape[1]
    g2_blocks = [[(r, 0, BLK)] for r in range(N_PHASE)]
    g1_blocks = {(r4, n): [(4 * q4 + r4, 32 * n, 32) for q4 in range(4)]
                 for r4 in range(4) for n in range(M // 32)}
    g0_blocks = [[(r, 8 * n, 8) for r in range(N_PHASE)] for n in range(M // 8)]

    for r in range(N_PHASE):
        qf_ref[r] = q0_ref[r].astype(F32)
        kf_ref[r] = k0_ref[r].astype(F32)
        vf_ref[r] = v0_ref[r].astype(F32)

    work = []
    for n, pieces in enumerate(g0_blocks):
        work.append((0, pieces, g0_blocks[n - 1] if n else None, (qf_ref, kf_ref, vf_ref), True))
    for (r4, n), pieces in g1_blocks.items():
        work.append((1, pieces, g1_blocks[(r4, n - 1)] if n else None, (q1_ref, k1_ref, v1_ref), False))
    for pieces in g2_blocks:
        work.append((2, pieces, None, (q2_ref, k2_ref, v2_ref), False))

    def merge(r, acc2, l2, m2):
        m0, m1 = m_ref[0, r], m_ref[1, r]
        mx = jnp.maximum(jnp.maximum(m0, m1), m2)
        w0, w1, w2 = jnp.exp2(m0 - mx), jnp.exp2(m1 - mx), jnp.exp2(m2 - mx)
        num = w0 * acc_ref[0, r] + w1 * acc_ref[1, r] + w2 * acc2
        den = w0 * l_ref[0, r] + w1 * l_ref[1, r] + w2 * l2
        z = z_ref[r].astype(F32)
        o_ref[r] = (num * z / (den * (1.0 + jnp.exp2(z * (-LOG2E))))).astype(BF16)

    pending = []
    for item in work + [None] * AHEAD:
        if item is not None:
            g, pieces, prev, (q_r, k_r, _), cast = item
            q, k = rows_of(q_r, pieces), keys_of(k_r, prev, pieces)
            if cast:
                q, k = q.astype(BF16), k.astype(BF16)
            pending.append((item, logits(q, k, g, prev is None)))
        if item is None or len(pending) > AHEAD:
            (g_p, pieces_p, prev_p, (_, _, v_r), cast_p), s_p = pending.pop(0)
            key_blocks = ([] if prev_p is None else [prev_p]) + [pieces_p]
            vals = finish(s_p, [value_blocks(g_p, v_r, kb, cast_p) for kb in key_blocks])
            if g_p == 2:
                merge(pieces_p[0][0], *vals)
            else:
                store(g_p, pieces_p, vals)


def _attention(P, bm):
    B, _, M, _ = P.shape
    HP = HEADS // 2
    W = 2 * HEAD_DIM

    def spec(kind, g):
        base = (kind * 3 + g) * HP
        return pl.BlockSpec((None, N_PHASE, M, W), lambda b, hp, base=base: (b, 0, 0, base + hp))

    in_specs = [spec(kind, g) for kind in range(3) for g in range(3)]
    in_specs.append(pl.BlockSpec((None, N_PHASE, M, W), lambda b, hp: (b, 0, 0, 9 * HP + hp)))
    in_specs.append(pl.BlockSpec((3, None, 2 * BLK, 2 * BLK), lambda b, hp: (0, hp, 0, 0)))
    return pl.pallas_call(
        _attn_kernel,
        grid=(B, HP),
        in_specs=in_specs,
        out_specs=pl.BlockSpec((None, N_PHASE, M, W), lambda b, hp: (b, 0, 0, hp)),
        out_shape=jax.ShapeDtypeStruct((B, N_PHASE, M, HEADS * HEAD_DIM), BF16),
        scratch_shapes=[pltpu.VMEM((N_PHASE, M, W), F32) for _ in range(3)]
        + [pltpu.VMEM((2, N_PHASE, M, W), F32) for _ in range(3)],
        compiler_params=_cparams(("parallel", "parallel")),
        name="dilated_attention",
    )(*([P] * 10), bm)


def _t5_bucket(dist):
    max_exact = REL_BUCKETS // 2
    n = jnp.maximum(dist, 1).astype(F32)
    large = max_exact + (jnp.log(n / max_exact) / math.log(REL_MAX_DIST / max_exact)
                         * (REL_BUCKETS - max_exact)).astype(jnp.int32)
    large = jnp.minimum(large, REL_BUCKETS - 1)
    return jnp.where(dist < max_exact, dist, large)


def _bias_tables(rel_bias):
    a = np.arange(BLK)
    pos = (16 * (a % 8) + a // 8, 4 * (a % 32) + a // 32, a)
    back = np.stack([np.concatenate([BLK + p[:, None] - p[None, :], p[:, None] - p[None, :]], axis=1)
                     for p in pos])
    valid = (back >= 0) & (back <= BLK)
    dist = np.clip(back, 0, BLK) * np.asarray(DILATIONS)[:, None, None]
    bucket = jnp.where(jnp.asarray(valid), _t5_bucket(jnp.asarray(dist, jnp.int32)), REL_BUCKETS)
    onehot = (bucket[..., None] == jnp.arange(REL_BUCKETS + 1)).astype(F32)
    ext = jnp.concatenate([rel_bias.astype(F32), jnp.full((1, HEADS), NEG, F32)], axis=0)
    t = jnp.einsum("gijc,ch->ghij", onehot, ext * math.log2(math.e),
                   precision=lax.Precision.HIGHEST)
    return t.reshape(3, HEADS // 2, 2 * BLK, 2 * BLK)


def _out_proj_kernel(o_ref, w_ref, g_ref, x_ref, *rest, mh):
    prm_in, y_ref, prm_out, obuf, xbuf, sem = rest[:11], rest[11], rest[12:19], rest[19], rest[20], rest[21]
    n_pair = prm_out[-1].shape[0]

    nm = pl.num_programs(1)
    step = pl.program_id(0) * nm + pl.program_id(1)
    n_steps = pl.num_programs(0) * nm

    def copies(st):
        rows = pl.ds((st % nm) * mh, mh)
        return (pltpu.make_async_copy(o_ref.at[st // nm, :, rows, :], obuf.at[st % 3], sem.at[0, st % 3]),
                pltpu.make_async_copy(x_ref.at[st // nm, :, rows, :], xbuf.at[st % 3], sem.at[1, st % 3]))

    def start(st):
        for c in copies(st):
            c.start()

    @pl.when(step == 0)
    def _():
        start(step)

    @pl.when(jnp.logical_and(step == 0, n_steps > 1))
    def _():
        start(step + 1)

    @pl.when(step + 2 < n_steps)
    def _():
        start(step + 2)

    for c in copies(step):
        c.wait()
    slot = step % 3

    def of_pair(refs, k):
        return [r.at[2 * k:2 * k + 2] if r.shape[0] == 2 * n_pair else r.at[k] for r in refs]

    half = [_ssm_param_stage_a(*of_pair(prm_in[:7], k)) for k in range(n_pair)]
    o = obuf[slot].reshape(N_PHASE * mh, obuf.shape[-1])
    h = jnp.dot(o, w_ref[...].astype(BF16), preferred_element_type=F32)
    for k in range(n_pair):
        _ssm_param_stage_b(*half[k], *of_pair(prm_in[7:] + prm_out, k))
    ms = jnp.mean(h * h, axis=-1, keepdims=True)
    y = h * lax.rsqrt(ms + EPS) * g_ref[...]
    y_ref[...] = xbuf[slot] + y.reshape(y_ref.shape)


def _out_proj(O, w, g, xp, a_re, a_im, log_dt, b_re, b_im, c_re, c_im, d_skip, *, mh=64):
    B, _, M, D = xp.shape
    G, P, C, Q = SSM_N_GROUPS, SSM_STATE, SSM_GROUP, N_PHASE
    QC = Q * C
    nm = M // mh
    n_pair = (G // 2) // (B * nm)
    assert B * nm * n_pair == G // 2
    f = lambda t: t.astype(F32)
    a_re, a_im, log_dt, b_re, b_im, c_re, c_im = map(f, (a_re, a_im, log_dt, b_re, b_im, c_re, c_im))
    even = (jnp.arange(G) % 2 == 0)[:, None, None]

    def lane_half(c):
        return jnp.concatenate([jnp.where(even, c, 0.0), jnp.where(even, 0.0, c)], axis=-1)

    pair = lambda blk: pl.BlockSpec((2 * n_pair,) + blk, lambda b, m: (b * nm + m,) + (0,) * len(blk))
    one = lambda blk: pl.BlockSpec((n_pair,) + blk, lambda b, m: (b * nm + m,) + (0,) * len(blk))
    outs = pl.pallas_call(
        functools.partial(_out_proj_kernel, mh=mh),
        grid=(B, nm),
        in_specs=[
            pl.BlockSpec(memory_space=pl.ANY),
            pl.BlockSpec(w.shape, lambda b, m: (0, 0)),
            pl.BlockSpec((1, D), lambda b, m: (0, 0)),
            pl.BlockSpec(memory_space=pl.ANY),
            pair((P, 1)), pair((P, 1)), one((1, 2 * P)), one((1, 2 * P)), one((1, 2 * P)),
            pair((P, C)), pair((P, C)), pair((C, P)), pair((C, P)), pair((C, 2 * P)), pair((C, 2 * P)),
        ],
        out_specs=[pl.BlockSpec((None, N_PHASE, mh, D), lambda b, m: (b, 0, m, 0)),
                   pair((QC, QC)), pair((P, QC)), pair((P, QC)), one((2 * QC, 2 * P)),
                   one((2 * QC, 2 * P)), one((1, 2 * P)), one((1, 2 * P))],
        out_shape=[jax.ShapeDtypeStruct(xp.shape, F32),
                   jax.ShapeDtypeStruct((G, QC, QC), BF16), jax.ShapeDtypeStruct((G, P, QC), BF16),
                   jax.ShapeDtypeStruct((G, P, QC), BF16),
                   jax.ShapeDtypeStruct((G // 2, 2 * QC, 2 * P), BF16),
                   jax.ShapeDtypeStruct((G // 2, 2 * QC, 2 * P), BF16),
                   jax.ShapeDtypeStruct((G // 2, 1, 2 * P), F32),
                   jax.ShapeDtypeStruct((G // 2, 1, 2 * P), F32)],
        scratch_shapes=[pltpu.VMEM((3, N_PHASE, mh, O.shape[-1]), BF16),
                        pltpu.VMEM((3, N_PHASE, mh, D), F32), pltpu.SemaphoreType.DMA((2, 3))],
        compiler_params=_cparams(("arbitrary", "arbitrary")),
        name="attn_out_proj",
    )(O, w, g.reshape(1, D), xp,
      a_re.reshape(G, P, 1), a_im.reshape(G, P, 1),
      a_re.reshape(G // 2, 1, 2 * P), a_im.reshape(G // 2, 1, 2 * P),
      jnp.repeat(log_dt, P).reshape(G // 2, 1, 2 * P), b_re, b_im, c_re, c_im,
      lane_half(c_re), lane_half(c_im))
    keys = ("m", "b_re", "b_im", "c_re", "c_im", "a_re", "a_im")
    prm = dict(zip(keys, outs[1:]))
    prm["d"] = d_skip.astype(F32).reshape(G * C, 1)
    return outs[0], prm


def _ssm_in_proj_kernel(x_ref, g_ref, w_ref, o_ref, h_ref):
    M = x_ref.shape[1]
    nt = (((1,), (1,)), ((), ()))

    def weights():
        return w_ref[...].T.astype(BF16)

    @pl.when(pl.program_id(1) == 0)
    def _():
        wt = weights()
        for c in range(0, N_PHASE, NORM_CHUNK):
            x = x_ref[c:c + NORM_CHUNK].reshape(NORM_CHUNK * M, -1)
            ms = jnp.mean(x * x, axis=-1, keepdims=True)
            hc = (x * lax.rsqrt(ms + EPS) * g_ref[...]).astype(BF16)
            h_ref[c * M:(c + NORM_CHUNK) * M, :] = hc
            res = lax.dot_general(wt, hc, nt, preferred_element_type=F32)
            for s in range(NORM_CHUNK):
                o_ref[c + s] = res[:, s * M:(s + 1) * M].astype(BF16)

    @pl.when(pl.program_id(1) > 0)
    def _():
        res = lax.dot_general(weights(), h_ref[...], nt, preferred_element_type=F32)
        for s in range(N_PHASE):
            o_ref[s] = res[:, s * M:(s + 1) * M].astype(BF16)


def _ssm_in_proj(xp, g, w, *, tn=1024):
    B, _, M, D = xp.shape
    S = N_PHASE * M
    N = w.shape[1]
    return pl.pallas_call(
        _ssm_in_proj_kernel,
        grid=(B, N // tn),
        in_specs=[
            pl.BlockSpec((None, N_PHASE, M, D), lambda b, j: (b, 0, 0, 0)),
            pl.BlockSpec((1, D), lambda b, j: (0, 0)),
            pl.BlockSpec((D, tn), lambda b, j: (0, j)),
        ],
        out_specs=pl.BlockSpec((None, N_PHASE, tn, M), lambda b, j: (b, 0, j, 0)),
        out_shape=jax.ShapeDtypeStruct((B, N_PHASE, N, M), BF16),
        scratch_shapes=[pltpu.VMEM((S, D), BF16)],
        compiler_params=_cparams(("parallel", "arbitrary")),
        name="ssm_in_proj",
    )(xp, g.reshape(1, D), w)


def _ssm_core_kernel(u_ref, m_ref, bre_ref, bim_ref, cre_ref, cim_ref, are_ref, aim_ref,
                     d_ref, y_ref, sre_ref, sim_ref, ym_ref, *, n_chunk, n_pair):
    C = SSM_GROUP
    QC = N_PHASE * C
    nb = u_ref.shape[0]
    n_grp = 2 * n_pair

    def grp(i, b):
        return u_ref[b, :, i * C:(i + 1) * C, :]

    us = [jnp.concatenate([grp(i, b).reshape(QC, n_chunk) for b in range(nb)], axis=1)
          for i in range(n_grp)]

    P = bre_ref.shape[1]
    xre, xim = [], []
    for i in range(n_grp):
        lhs = jnp.concatenate([m_ref[i], bre_ref[i], bim_ref[i]], axis=0)
        r = jnp.dot(lhs, us[i], preferred_element_type=F32)
        ym_ref[i] = r[:QC]
        xre.append(r[QC:QC + P])
        xim.append(r[QC + P:])
    for k in range(n_pair):
        sre_ref[k] = jnp.concatenate(xre[2 * k:2 * k + 2], axis=0).T
        sim_ref[k] = jnp.concatenate(xim[2 * k:2 * k + 2], axis=0).T

    sub = lax.broadcasted_iota(jnp.int32, (8, are_ref.shape[-1]), 0)

    def cmul(xr, xi, yr, yi):
        return xr * yr - xi * yi, xr * yi + xi * yr

    for k in range(n_pair):
        a1 = (are_ref[k], aim_ref[k])
        a2 = cmul(*a1, *a1)
        a4 = cmul(*a2, *a2)
        pows = [a1, a2, cmul(*a2, *a1), a4, cmul(*a4, *a1), cmul(*a4, *a2)]
        pows += [cmul(*pows[5], *a1), cmul(*a4, *a4)]
        apow_r = jnp.zeros(sub.shape, F32)
        apow_i = jnp.zeros(sub.shape, F32)
        for i, (pr, pi) in enumerate(pows):
            apow_r = jnp.where(sub == i, pr, apow_r)
            apow_i = jnp.where(sub == i, pi, apow_i)
        step = {d: (jnp.where(sub >= d, pows[d - 1][0], 0.0), jnp.where(sub >= d, pows[d - 1][1], 0.0))
                for d in (1, 2, 4)}
        carry = [(jnp.zeros(sub.shape, F32), jnp.zeros(sub.shape, F32)) for _ in range(nb)]
        for j in range(n_chunk // 8):
            for b in range(nb):
                rows = slice(b * n_chunk + 8 * j, b * n_chunk + 8 * j + 8)
                er, ei = sre_ref[k, rows, :], sim_ref[k, rows, :]
                for d in (1, 2, 4):
                    dr, di = cmul(*step[d], pltpu.roll(er, d, axis=0), pltpu.roll(ei, d, axis=0))
                    er, ei = er + dr, ei + di
                cr, ci = carry[b]
                dr, di = cmul(apow_r, apow_i, cr, ci)
                fr, fi = er + dr, ei + di
                sre_ref[k, rows, :] = jnp.where(sub == 0, cr, pltpu.roll(fr, 1, axis=0))
                sim_ref[k, rows, :] = jnp.where(sub == 0, ci, pltpu.roll(fi, 1, axis=0))
                carry[b] = (jnp.broadcast_to(fr[7:8], sub.shape), jnp.broadcast_to(fi[7:8], sub.shape))

    nt = (((1,), (1,)), ((), ()))
    for k in range(n_pair):
        inter = (lax.dot_general(cre_ref[k], sre_ref[k].astype(BF16), nt, preferred_element_type=F32)
                 + lax.dot_general(cim_ref[k], sim_ref[k].astype(BF16), nt,
                                   preferred_element_type=F32))
        for i in (2 * k, 2 * k + 1):
            y = ym_ref[i] + inter[(i % 2) * QC:(i % 2 + 1) * QC]
            for b in range(nb):
                yb = y[:, b * n_chunk:(b + 1) * n_chunk].reshape(N_PHASE, C, n_chunk)
                yb = yb + d_ref[i * C:(i + 1) * C] * grp(i, b).astype(F32)
                y_ref[b, :, i * C:(i + 1) * C, :] = yb.astype(BF16)


def _ssm_core(UT, prm, *, n_pair=2):
    B, _, _, n_chunk = UT.shape
    N = B * n_chunk
    G2 = SSM_N_GROUPS // 2
    C2 = 2 * SSM_GROUP
    QC = N_PHASE * SSM_GROUP
    P = SSM_STATE
    blk = lambda *shape: pl.BlockSpec(shape, lambda g: (g,) + (0,) * (len(shape) - 1))
    return pl.pallas_call(
        functools.partial(_ssm_core_kernel, n_chunk=n_chunk, n_pair=n_pair),
        grid=(G2 // n_pair,),
        in_specs=[
            pl.BlockSpec((B, N_PHASE, n_pair * C2, n_chunk), lambda g: (0, 0, g, 0)),
            blk(2 * n_pair, QC, QC), blk(2 * n_pair, P, QC), blk(2 * n_pair, P, QC),
            blk(n_pair, 2 * QC, 2 * P), blk(n_pair, 2 * QC, 2 * P),
            blk(n_pair, 1, 2 * P), blk(n_pair, 1, 2 * P),
            blk(n_pair * C2, 1),
        ],
        out_specs=pl.BlockSpec((B, N_PHASE, n_pair * C2, n_chunk), lambda g: (0, 0, g, 0)),
        out_shape=jax.ShapeDtypeStruct((B, N_PHASE, SSM_N_GROUPS * SSM_GROUP, n_chunk), BF16),
        scratch_shapes=[pltpu.VMEM((n_pair, N, 2 * P), F32), pltpu.VMEM((n_pair, N, 2 * P), F32),
                        pltpu.VMEM((2 * n_pair, QC, N), F32)],
        compiler_params=_cparams(("parallel",)),
        name="ssm_core",
    )(UT, prm["m"], prm["b_re"], prm["b_im"], prm["c_re"], prm["c_im"],
      prm["a_re"], prm["a_im"], prm["d"])


def _ssm_param_stage_a(arc_ref, aic_ref, arr_ref, air_ref, ldr_ref, bre_ref, bim_ref):
    Q, C, P = N_PHASE, SSM_GROUP, SSM_STATE
    QC = Q * C
    hi = lax.Precision.HIGHEST
    tile = (lax.broadcasted_iota(jnp.int32, (C, QC), 1) % C
            == lax.broadcasted_iota(jnp.int32, (C, QC), 0)).astype(F32)

    dt = jnp.exp(ldr_ref[...])
    mag = jnp.exp(arr_ref[...] * dt)
    a1r, a1i = mag * jnp.cos(air_ref[...] * dt), mag * jnp.sin(air_ref[...] * dt)
    a1r_c = jnp.broadcast_to(a1r, (8, 2 * P)).T[:, :1]
    a1i_c = jnp.broadcast_to(a1i, (8, 2 * P)).T[:, :1]

    pows = [(jnp.ones_like(a1r), jnp.zeros_like(a1r))]
    for _ in range(Q):
        qr, qi = pows[-1]
        pows.append((qr * a1r - qi * a1i, qr * a1i + qi * a1r))

    bb_t = []
    for i in range(2):
        a_re, a_im = arc_ref[i], aic_ref[i]
        abr, abi = a1r_c[i * P:(i + 1) * P], a1i_c[i * P:(i + 1) * P]
        den = a_re * a_re + a_im * a_im
        cfr = ((abr - 1.0) * a_re + abi * a_im) / den
        cfi = (abi * a_re - (abr - 1.0) * a_im) / den
        bbr = cfr * bre_ref[i] - cfi * bim_ref[i]
        bbi = cfr * bim_ref[i] + cfi * bre_ref[i]
        bb_t.append((jnp.dot(bbr, tile, precision=hi, preferred_element_type=F32),
                     jnp.dot(bbi, tile, precision=hi, preferred_element_type=F32)))
    return pows, bb_t


def _ssm_param_stage_b(pows, bb_t, cre_ref, cim_ref, cpr_ref, cpi_ref,
                       m_ref, br_ref, bi_ref, cr_ref, ci_ref, ar_ref, ai_ref):
    Q, C, P = N_PHASE, SSM_GROUP, SSM_STATE
    QC = Q * C
    hi = lax.Precision.HIGHEST
    pw_r = jnp.concatenate([jnp.broadcast_to(pows[Q - 1 - s][0], (C, 2 * P)) for s in range(Q)], axis=0).T
    pw_i = jnp.concatenate([jnp.broadcast_to(pows[Q - 1 - s][1], (C, 2 * P)) for s in range(Q)], axis=0).T

    for i in range(2):
        bbr_t, bbi_t = bb_t[i]
        pr, pi = pw_r[i * P:(i + 1) * P], pw_i[i * P:(i + 1) * P]
        bst_r = pr * bbr_t - pi * bbi_t
        bst_i = pr * bbi_t + pi * bbr_t
        br_ref[i] = bst_r.astype(BF16)
        bi_ref[i] = bst_i.astype(BF16)

        kl = (jnp.dot(cre_ref[i], bst_r, precision=hi, preferred_element_type=F32)
              - jnp.dot(cim_ref[i], bst_i, precision=hi, preferred_element_type=F32))
        klz = jnp.concatenate([kl, jnp.zeros_like(kl)], axis=1)
        for t in range(Q):
            sh = (2 * QC - C * (Q - 1 - t)) % (2 * QC)
            row = klz if sh == 0 else pltpu.roll(klz, sh, axis=1)
            m_ref[i, t * C:(t + 1) * C, :] = row[:, :QC].astype(BF16)

    for t in range(Q):
        qr, qi = pows[t + 1]
        for i in range(2):
            rows = slice(i * QC + t * C, i * QC + (t + 1) * C)
            cr_ref[rows, :] = (cpr_ref[i] * qr - cpi_ref[i] * qi).astype(BF16)
            ci_ref[rows, :] = (-(cpr_ref[i] * qi + cpi_ref[i] * qr)).astype(BF16)
    ar_ref[...], ai_ref[...] = pows[Q]


def _ssm_out_kernel(y_ref, z_ref, wg_ref, bg_ref, wo_ref, gp_ref, x_ref, o_hbm, obuf, sem,
                    wgt_ref, wot_ref, *, nb, n_sub):
    s, j = pl.program_id(0), pl.program_id(1)
    nj = pl.num_programs(1)
    step = s * nj + j
    last = pl.num_programs(0) * nj - 1
    slot = step % 2

    def out_copy(slot_, s_, j_):
        return pltpu.make_async_copy(obuf.at[slot_], o_hbm.at[pl.ds(j_ * nb, nb), :, s_, :],
                                     sem.at[slot_])

    @pl.when(step >= 2)
    def _():
        out_copy(slot, (step - 2) // nj, (step - 2) % nj).wait()

    @pl.when(step == 0)
    def _():
        wgt_ref[...] = wg_ref[...].T.astype(BF16)
        wot_ref[...] = wo_ref[...].T.astype(BF16)

    bs = nb // n_sub

    def lanes(ref, k):
        return jnp.concatenate([ref[b] for b in range(k * bs, (k + 1) * bs)], axis=1).astype(F32)

    def glu_in(k):
        g = _gelu_tanh(lanes(y_ref, k))
        return g, jnp.dot(wgt_ref[...], g.astype(BF16), preferred_element_type=F32)

    def proj(k, g, gl):
        y2 = g * _sigmoid(gl + bg_ref[...])
        z = lanes(z_ref, k)
        gated = (y2 * (z * _sigmoid(z))).astype(BF16)
        return jnp.dot(wot_ref[...], gated, preferred_element_type=F32)

    def finish(k, ot):
        ms = jnp.mean(ot * ot, axis=0, keepdims=True)
        nt = (ot * lax.rsqrt(ms + EPS)).T
        rows = slice(k * bs, (k + 1) * bs)
        obuf[slot, rows] = x_ref[rows] + (nt * gp_ref[...]).reshape(bs, -1, nt.shape[-1])

    a = [glu_in(k) for k in range(n_sub)]
    o = [proj(k, *a[k]) for k in range(n_sub)]
    for k in range(n_sub):
        finish(k, o[k])
    out_copy(slot, s, j).start()

    @pl.when(step == last)
    def _():
        out_copy(slot, s, j).wait()

    @pl.when(jnp.logical_and(step == last, last >= 1))
    def _():
        out_copy(1 - slot, (step - 1) // nj, (step - 1) % nj).wait()


def _ssm_out(YT, UT, wg, bg, wo, gp, xp, *, nb=8, n_sub=2):
    B, _, M, D = xp.shape
    E = YT.shape[2]
    out = pl.pallas_call(
        functools.partial(_ssm_out_kernel, nb=nb, n_sub=n_sub),
        grid=(N_PHASE, B // nb),
        in_specs=[
            pl.BlockSpec((nb, None, E, M), lambda s, j: (j, s, 0, 0)),
            pl.BlockSpec((nb, None, E, M), lambda s, j: (j, s, 1, 0)),
            pl.BlockSpec((E, E), lambda s, j: (0, 0)),
            pl.BlockSpec((E, 1), lambda s, j: (0, 0)),
            pl.BlockSpec((E, D), lambda s, j: (0, 0)),
            pl.BlockSpec((1, D), lambda s, j: (0, 0)),
            pl.BlockSpec((nb, None, M, D), lambda s, j: (j, s, 0, 0)),
        ],
        out_specs=pl.BlockSpec(memory_space=pl.ANY),
        out_shape=jax.ShapeDtypeStruct((B, M, N_PHASE, D), F32),
        scratch_shapes=[pltpu.VMEM((2, nb, M, D), F32), pltpu.SemaphoreType.DMA((2,)),
                        pltpu.VMEM((E, E), BF16), pltpu.VMEM((D, E), BF16)],
        compiler_params=_cparams(("arbitrary", "arbitrary")),
        name="ssm_out",
    )(YT, UT, wg, bg.reshape(E, 1), wo, gp.reshape(1, D), xp)
    return out.reshape(B, M * N_PHASE, D)


def kernel(x, rel_bias, attn_pre_norm, attn_w_in, attn_w_out, attn_post_norm, ssm_pre_norm, ssm_w_in, ssm_a_re, ssm_a_im, ssm_log_dt, ssm_b_re, ssm_b_im, ssm_c_re, ssm_c_im, ssm_d, ssm_w_glu, ssm_b_glu, ssm_w_out, ssm_post_norm):
    P, xp = _norm_proj(x, attn_pre_norm[0], attn_w_in[0], scaled_cols=3 * HEADS * HEAD_DIM,
                       scale=HEAD_DIM ** -0.5 * LOG2E)
    O = _attention(P, _bias_tables(rel_bias))
    xp, prm = _out_proj(O, attn_w_out[0], attn_post_norm[0], xp, ssm_a_re[0], ssm_a_im[0],
                        ssm_log_dt[0], ssm_b_re[0], ssm_b_im[0], ssm_c_re[0], ssm_c_im[0], ssm_d[0])

    UT = _ssm_in_proj(xp, ssm_pre_norm[0], ssm_w_in[0])
    YT = _ssm_core(UT, prm)
    return _ssm_out(YT, UT, ssm_w_glu[0], ssm_b_glu[0], ssm_w_out[0], ssm_post_norm[0], xp)
```

```python
import functools
import math

import numpy as np
import jax
import jax.numpy as jnp
from jax import lax
from jax.experimental import pallas as pl
from jax.experimental.pallas import tpu as pltpu

F32 = jnp.float32
BF16 = jnp.bfloat16

D_MODEL = 1024
HEAD_DIM = 64
HEADS = 16
N_PHASE = 16
BLK = 128
DILATIONS = (1, 4, 16)
REL_BUCKETS = 32
REL_MAX_DIST = 2048
SSM_GROUP = 16
SSM_N_GROUPS = 64
SSM_STATE = 64
EPS = 1e-6
NEG = -1e30
VMEM_LIMIT = 56 * 1024 * 1024
NORM_CHUNK = 4


def _cparams(sem):
    return pltpu.CompilerParams(dimension_semantics=sem, vmem_limit_bytes=VMEM_LIMIT)


LOG2E = math.log2(math.e)


def _sigmoid(v):
    return 1.0 / (1.0 + jnp.exp2(v * (-LOG2E)))


def _gelu_tanh(x):
    k0 = -2.0 * math.sqrt(2.0 / math.pi) * LOG2E
    return x / (1.0 + jnp.exp2(x * (k0 + (k0 * 0.044715) * (x * x))))


def _norm_proj_kernel(x_hbm, g_ref, w_hbm, o_ref, xp_hbm, xbuf, h_ref, gsem, osem, wbuf, wsem, *,
                      scaled_tiles, scale):
    b, j = pl.program_id(0), pl.program_id(1)
    n_b, n_j = pl.num_programs(0), pl.num_programs(1)
    slot = b % 2

    def phase_copy(bb, sl, r):
        return pltpu.make_async_copy(x_hbm.at[bb, :, r, :], xbuf.at[sl, r], gsem.at[sl])

    def gather_start(bb, sl):
        lax.fori_loop(0, N_PHASE, lambda r, c: (phase_copy(bb, sl, r).start(), c)[1], 0)

    def gather_wait(bb, sl):
        lax.fori_loop(0, N_PHASE, lambda r, c: (phase_copy(bb, sl, r).wait(), c)[1], 0)

    def write_out(bb, sl):
        return pltpu.make_async_copy(xbuf.at[sl], xp_hbm.at[bb], osem.at[sl])

    M = xbuf.shape[2]

    tn = wbuf.shape[-1]
    step = b * n_j + j
    n_steps = n_b * n_j

    def w_copy(st):
        return pltpu.make_async_copy(w_hbm.at[:, pl.ds((st % n_j) * tn, tn)], wbuf.at[st % 3],
                                     wsem.at[st % 3])

    @pl.when(step == 0)
    def _():
        w_copy(step).start()

    @pl.when(jnp.logical_and(step == 0, n_steps > 1))
    def _():
        w_copy(step + 1).start()

    @pl.when(step + 2 < n_steps)
    def _():
        w_copy(step + 2).start()

    w_copy(step).wait()

    def weights():
        return (wbuf[step % 3] * jnp.where(j < scaled_tiles, scale, 1.0).astype(F32)).astype(BF16)

    @pl.when(j == 0)
    def _():
        @pl.when(b == 0)
        def _():
            gather_start(0, 0)

        gather_wait(b, slot)

        @pl.when(b >= 1)
        def _():
            write_out(b - 1, 1 - slot).wait()

        @pl.when(b + 1 < n_b)
        def _():
            gather_start(b + 1, 1 - slot)

        write_out(b, slot).start()
        w = weights()
        for c in range(0, N_PHASE, NORM_CHUNK):
            x = xbuf[slot, c:c + NORM_CHUNK].reshape(NORM_CHUNK * M, -1)
            ms = jnp.mean(x * x, axis=-1, keepdims=True)
            hc = (x * lax.rsqrt(ms + EPS) * g_ref[...]).astype(BF16)
            h_ref[c * M:(c + NORM_CHUNK) * M, :] = hc
            res = jnp.dot(hc, w, preferred_element_type=F32)
            o_ref[c:c + NORM_CHUNK] = res.reshape(NORM_CHUNK, M, -1).astype(BF16)

    @pl.when(j > 0)
    def _():
        res = jnp.dot(h_ref[...], weights(), preferred_element_type=F32)
        o_ref[...] = res.reshape(o_ref.shape).astype(BF16)

    @pl.when(jnp.logical_and(b == n_b - 1, j == n_j - 1))
    def _():
        write_out(b, slot).wait()


def _norm_proj(x, g, w, *, scaled_cols, scale, tn=1024):
    B, S, D = x.shape
    M = S // N_PHASE
    N = w.shape[1]
    assert scaled_cols % tn == 0
    return pl.pallas_call(
        functools.partial(_norm_proj_kernel, scaled_tiles=scaled_cols // tn, scale=scale),
        grid=(B, N // tn),
        in_specs=[
            pl.BlockSpec(memory_space=pl.ANY),
            pl.BlockSpec((1, D), lambda b, j: (0, 0)),
            pl.BlockSpec(memory_space=pl.ANY),
        ],
        out_specs=[pl.BlockSpec((None, N_PHASE, M, tn), lambda b, j: (b, 0, 0, j)),
                   pl.BlockSpec(memory_space=pl.ANY)],
        out_shape=[jax.ShapeDtypeStruct((B, N_PHASE, M, N), BF16),
                   jax.ShapeDtypeStruct((B, N_PHASE, M, D), F32)],
        scratch_shapes=[pltpu.VMEM((2, N_PHASE, M, D), F32), pltpu.VMEM((S, D), BF16),
                        pltpu.SemaphoreType.DMA((2,)), pltpu.SemaphoreType.DMA((2,)),
                        pltpu.VMEM((3, D, tn), F32), pltpu.SemaphoreType.DMA((3,))],
        compiler_params=_cparams(("arbitrary", "arbitrary")),
        name="attn_norm_proj",
    )(x.reshape(B, M, N_PHASE, D), g.reshape(1, D), w)


def _attn_kernel(q0_ref, q1_ref, q2_ref, k0_ref, k1_ref, k2_ref, v0_ref, v1_ref, v2_ref,
                 z_ref, bm_ref, o_ref, qf_ref, kf_ref, vf_ref, acc_ref, l_ref, m_ref):
    W = 2 * HEAD_DIM
    AHEAD = 3
    lane = lax.broadcasted_iota(jnp.int32, (BLK, W), 1)
    first_head = lane < HEAD_DIM

    def logits(q, k, g, cur_only):
        zq = jnp.zeros_like(q)
        qs = jnp.concatenate([jnp.where(first_head, q, zq), jnp.where(first_head, zq, q)], axis=0)
        s = lax.dot_general(qs, k, (((1,), (1,)), ((), ())), preferred_element_type=F32)
        return s + (bm_ref[g, :, BLK:2 * BLK] if cur_only else bm_ref[g])

    ones_a = jnp.where(first_head, 1.0, 0.0).astype(BF16)
    ones_b = jnp.where(first_head, 0.0, 1.0).astype(BF16)
    masked_v = {}

    def value_blocks(g, v_ref, pieces, cast):
        key = (g, tuple(pieces))
        if key not in masked_v:
            v = rows_of(v_ref, pieces)
            v = v.astype(BF16) if cast else v
            zv = jnp.zeros_like(v)
            masked_v[key] = (jnp.concatenate([jnp.where(first_head, v, zv), ones_a], axis=1),
                             jnp.concatenate([jnp.where(first_head, zv, v), ones_b], axis=1))
        return masked_v[key]

    def finish(s, vblocks):
        m = jnp.max(s, axis=-1, keepdims=True)
        p = jnp.exp2(s - m).astype(BF16)
        pcat = jnp.concatenate([p[:BLK], p[BLK:]], axis=1)
        rhs = jnp.concatenate([vb[0] for vb in vblocks] + [vb[1] for vb in vblocks], axis=0)
        pv = jnp.dot(pcat, rhs, preferred_element_type=F32)
        mm = jnp.where(first_head, m[:BLK], m[BLK:])
        return pv[:, :W], pv[:, W:], mm

    def rows_of(ref, pieces):
        return jnp.concatenate([ref[ph, lo:lo + n, :] for ph, lo, n in pieces], axis=0)

    def keys_of(ref, prev, cur):
        return rows_of(ref, cur) if prev is None else rows_of(ref, prev + cur)

    def store(g, pieces, vals):
        at = 0
        for ph, lo, n in pieces:
            for ref, val in zip((acc_ref, l_ref, m_ref), vals):
                ref[g, ph, lo:lo + n, :] = val[at:at + n]
            at += n

    M = q0_ref.shape[1]
    g2_blocks = [[(r, 0, BLK)] for r in range(N_PHASE)]
    g1_blocks = {(r4, n): [(4 * q4 + r4, 32 * n, 32) for q4 in range(4)]
                 for r4 in range(4) for n in range(M // 32)}
    g0_blocks = [[(r, 8 * n, 8) for r in range(N_PHASE)] for n in range(M // 8)]

    for r in range(N_PHASE):
        qf_ref[r] = q0_ref[r].astype(F32)
        kf_ref[r] = k0_ref[r].astype(F32)
        vf_ref[r] = v0_ref[r].astype(F32)

    work = []
    for n, pieces in enumerate(g0_blocks):
        work.append((0, pieces, g0_blocks[n - 1] if n else None, (qf_ref, kf_ref, vf_ref), True))
    for (r4, n), pieces in g1_blocks.items():
        work.append((1, pieces, g1_blocks[(r4, n - 1)] if n else None, (q1_ref, k1_ref, v1_ref), False))
    for pieces in g2_blocks:
        work.append((2, pieces, None, (q2_ref, k2_ref, v2_ref), False))

    def merge(r, acc2, l2, m2):
        m0, m1 = m_ref[0, r], m_ref[1, r]
        mx = jnp.maximum(jnp.maximum(m0, m1), m2)
        w0, w1, w2 = jnp.exp2(m0 - mx), jnp.exp2(m1 - mx), jnp.exp2(m2 - mx)
        num = w0 * acc_ref[0, r] + w1 * acc_ref[1, r] + w2 * acc2
        den = w0 * l_ref[0, r] + w1 * l_ref[1, r] + w2 * l2
        z = z_ref[r].astype(F32)
        o_ref[r] = (num * z / (den * (1.0 + jnp.exp2(z * (-LOG2E))))).astype(BF16)

    pending = []
    for item in work + [None] * AHEAD:
        if item is not None:
            g, pieces, prev, (q_r, k_r, _), cast = item
            q, k = rows_of(q_r, pieces), keys_of(k_r, prev, pieces)
            if cast:
                q, k = q.astype(BF16), k.astype(BF16)
            pending.append((item, logits(q, k, g, prev is None)))
        if item is None or len(pending) > AHEAD:
            (g_p, pieces_p, prev_p, (_, _, v_r), cast_p), s_p = pending.pop(0)
            key_blocks = ([] if prev_p is None else [prev_p]) + [pieces_p]
            vals = finish(s_p, [value_blocks(g_p, v_r, kb, cast_p) for kb in key_blocks])
            if g_p == 2:
                merge(pieces_p[0][0], *vals)
            else:
                store(g_p, pieces_p, vals)


def _attention(P, bm):
    B, _, M, _ = P.shape
    HP = HEADS // 2
    W = 2 * HEAD_DIM

    def spec(kind, g):
        base = (kind * 3 + g) * HP
        return pl.BlockSpec((None, N_PHASE, M, W), lambda b, hp, base=base: (b, 0, 0, base + hp))

    in_specs = [spec(kind, g) for kind in range(3) for g in range(3)]
    in_specs.append(pl.BlockSpec((None, N_PHASE, M, W), lambda b, hp: (b, 0, 0, 9 * HP + hp)))
    in_specs.append(pl.BlockSpec((3, None, 2 * BLK, 2 * BLK), lambda b, hp: (0, hp, 0, 0)))
    return pl.pallas_call(
        _attn_kernel,
        grid=(B, HP),
        in_specs=in_specs,
        out_specs=pl.BlockSpec((None, N_PHASE, M, W), lambda b, hp: (b, 0, 0, hp)),
        out_shape=jax.ShapeDtypeStruct((B, N_PHASE, M, HEADS * HEAD_DIM), BF16),
        scratch_shapes=[pltpu.VMEM((N_PHASE, M, W), F32) for _ in range(3)]
        + [pltpu.VMEM((2, N_PHASE, M, W), F32) for _ in range(3)],
        compiler_params=_cparams(("parallel", "parallel")),
        name="dilated_attention",
    )(*([P] * 10), bm)


def _t5_bucket(dist):
    max_exact = REL_BUCKETS // 2
    n = jnp.maximum(dist, 1).astype(F32)
    large = max_exact + (jnp.log(n / max_exact) / math.log(REL_MAX_DIST / max_exact)
                         * (REL_BUCKETS - max_exact)).astype(jnp.int32)
    large = jnp.minimum(large, REL_BUCKETS - 1)
    return jnp.where(dist < max_exact, dist, large)


def _bias_tables(rel_bias):
    a = np.arange(BLK)
    pos = (16 * (a % 8) + a // 8, 4 * (a % 32) + a // 32, a)
    back = np.stack([np.concatenate([BLK + p[:, None] - p[None, :], p[:, None] - p[None, :]], axis=1)
                     for p in pos])
    valid = (back >= 0) & (back <= BLK)
    dist = np.clip(back, 0, BLK) * np.asarray(DILATIONS)[:, None, None]
    bucket = jnp.where(jnp.asarray(valid), _t5_bucket(jnp.asarray(dist, jnp.int32)), REL_BUCKETS)
    onehot = (bucket[..., None] == jnp.arange(REL_BUCKETS + 1)).astype(F32)
    ext = jnp.concatenate([rel_bias.astype(F32), jnp.full((1, HEADS), NEG, F32)], axis=0)
    t = jnp.einsum("gijc,ch->ghij", onehot, ext * math.log2(math.e),
                   precision=lax.Precision.HIGHEST)
    return t.reshape(3, HEADS // 2, 2 * BLK, 2 * BLK)


def _out_proj_kernel(o_ref, w_ref, g_ref, x_ref, *rest, mh):
    prm_in, y_ref, prm_out, obuf, xbuf, sem = rest[:11], rest[11], rest[12:19], rest[19], rest[20], rest[21]
    n_pair = prm_out[-1].shape[0]

    nm = pl.num_programs(1)
    step = pl.program_id(0) * nm + pl.program_id(1)
    n_steps = pl.num_programs(0) * nm

    def copies(st):
        rows = pl.ds((st % nm) * mh, mh)
        return (pltpu.make_async_copy(o_ref.at[st // nm, :, rows, :], obuf.at[st % 3], sem.at[0, st % 3]),
                pltpu.make_async_copy(x_ref.at[st // nm, :, rows, :], xbuf.at[st % 3], sem.at[1, st % 3]))

    def start(st):
        for c in copies(st):
            c.start()

    @pl.when(step == 0)
    def _():
        start(step)

    @pl.when(jnp.logical_and(step == 0, n_steps > 1))
    def _():
        start(step + 1)

    @pl.when(step + 2 < n_steps)
    def _():
        start(step + 2)

    for c in copies(step):
        c.wait()
    slot = step % 3

    def of_pair(refs, k):
        return [r.at[2 * k:2 * k + 2] if r.shape[0] == 2 * n_pair else r.at[k] for r in refs]

    half = [_ssm_param_stage_a(*of_pair(prm_in[:7], k)) for k in range(n_pair)]
    o = obuf[slot].reshape(N_PHASE * mh, obuf.shape[-1])
    h = jnp.dot(o, w_ref[...].astype(BF16), preferred_element_type=F32)
    for k in range(n_pair):
        _ssm_param_stage_b(*half[k], *of_pair(prm_in[7:] + prm_out, k))
    ms = jnp.mean(h * h, axis=-1, keepdims=True)
    y = h * lax.rsqrt(ms + EPS) * g_ref[...]
    y_ref[...] = xbuf[slot] + y.reshape(y_ref.shape)


def _out_proj(O, w, g, xp, a_re, a_im, log_dt, b_re, b_im, c_re, c_im, d_skip, *, mh=64):
    B, _, M, D = xp.shape
    G, P, C, Q = SSM_N_GROUPS, SSM_STATE, SSM_GROUP, N_PHASE
    QC = Q * C
    nm = M // mh
    n_pair = (G // 2) // (B * nm)
    assert B * nm * n_pair == G // 2
    f = lambda t: t.astype(F32)
    a_re, a_im, log_dt, b_re, b_im, c_re, c_im = map(f, (a_re, a_im, log_dt, b_re, b_im, c_re, c_im))
    even = (jnp.arange(G) % 2 == 0)[:, None, None]

    def lane_half(c):
        return jnp.concatenate([jnp.where(even, c, 0.0), jnp.where(even, 0.0, c)], axis=-1)

    pair = lambda blk: pl.BlockSpec((2 * n_pair,) + blk, lambda b, m: (b * nm + m,) + (0,) * len(blk))
    one = lambda blk: pl.BlockSpec((n_pair,) + blk, lambda b, m: (b * nm + m,) + (0,) * len(blk))
    outs = pl.pallas_call(
        functools.partial(_out_proj_kernel, mh=mh),
        grid=(B, nm),
        in_specs=[
            pl.BlockSpec(memory_space=pl.ANY),
            pl.BlockSpec(w.shape, lambda b, m: (0, 0)),
            pl.BlockSpec((1, D), lambda b, m: (0, 0)),
            pl.BlockSpec(memory_space=pl.ANY),
            pair((P, 1)), pair((P, 1)), one((1, 2 * P)), one((1, 2 * P)), one((1, 2 * P)),
            pair((P, C)), pair((P, C)), pair((C, P)), pair((C, P)), pair((C, 2 * P)), pair((C, 2 * P)),
        ],
        out_specs=[pl.BlockSpec((None, N_PHASE, mh, D), lambda b, m: (b, 0, m, 0)),
                   pair((QC, QC)), pair((P, QC)), pair((P, QC)), one((2 * QC, 2 * P)),
                   one((2 * QC, 2 * P)), one((1, 2 * P)), one((1, 2 * P))],
        out_shape=[jax.ShapeDtypeStruct(xp.shape, F32),
                   jax.ShapeDtypeStruct((G, QC, QC), BF16), jax.ShapeDtypeStruct((G, P, QC), BF16),
                   jax.ShapeDtypeStruct((G, P, QC), BF16),
                   jax.ShapeDtypeStruct((G // 2, 2 * QC, 2 * P), BF16),
                   jax.ShapeDtypeStruct((G // 2, 2 * QC, 2 * P), BF16),
                   jax.ShapeDtypeStruct((G // 2, 1, 2 * P), F32),
                   jax.ShapeDtypeStruct((G // 2, 1, 2 * P), F32)],
        scratch_shapes=[pltpu.VMEM((3, N_PHASE, mh, O.shape[-1]), BF16),
                        pltpu.VMEM((3, N_PHASE, mh, D), F32), pltpu.SemaphoreType.DMA((2, 3))],
        compiler_params=_cparams(("arbitrary", "arbitrary")),
        name="attn_out_proj",
    )(O, w, g.reshape(1, D), xp,
      a_re.reshape(G, P, 1), a_im.reshape(G, P, 1),
      a_re.reshape(G // 2, 1, 2 * P), a_im.reshape(G // 2, 1, 2 * P),
      jnp.repeat(log_dt, P).reshape(G // 2, 1, 2 * P), b_re, b_im, c_re, c_im,
      lane_half(c_re), lane_half(c_im))
    keys = ("m", "b_re", "b_im", "c_re", "c_im", "a_re", "a_im")
    prm = dict(zip(keys, outs[1:]))
    prm["d"] = d_skip.astype(F32).reshape(G * C, 1)
    return outs[0], prm


def _ssm_in_proj_kernel(x_ref, g_ref, w_ref, o_ref, h_ref):
    M = x_ref.shape[1]
    nt = (((1,), (1,)), ((), ()))

    def weights():
        return w_ref[...].T.astype(BF16)

    @pl.when(pl.program_id(1) == 0)
    def _():
        wt = weights()
        for c in range(0, N_PHASE, NORM_CHUNK):
            x = x_ref[c:c + NORM_CHUNK].reshape(NORM_CHUNK * M, -1)
            ms = jnp.mean(x * x, axis=-1, keepdims=True)
            hc = (x * lax.rsqrt(ms + EPS) * g_ref[...]).astype(BF16)
            h_ref[c * M:(c + NORM_CHUNK) * M, :] = hc
            res = lax.dot_general(wt, hc, nt, preferred_element_type=F32)
            for s in range(NORM_CHUNK):
                o_ref[c + s] = res[:, s * M:(s + 1) * M].astype(BF16)

    @pl.when(pl.program_id(1) > 0)
    def _():
        res = lax.dot_general(weights(), h_ref[...], nt, preferred_element_type=F32)
        for s in range(N_PHASE):
            o_ref[s] = res[:, s * M:(s + 1) * M].astype(BF16)


def _ssm_in_proj(xp, g, w, *, tn=1024):
    B, _, M, D = xp.shape
    S = N_PHASE * M
    N = w.shape[1]
    return pl.pallas_call(
        _ssm_in_proj_kernel,
        grid=(B, N // tn),
        in_specs=[
            pl.BlockSpec((None, N_PHASE, M, D), lambda b, j: (b, 0, 0, 0)),
            pl.BlockSpec((1, D), lambda b, j: (0, 0)),
            pl.BlockSpec((D, tn), lambda b, j: (0, j)),
        ],
        out_specs=pl.BlockSpec((None, N_PHASE, tn, M), lambda b, j: (b, 0, j, 0)),
        out_shape=jax.ShapeDtypeStruct((B, N_PHASE, N, M), BF16),
        scratch_shapes=[pltpu.VMEM((S, D), BF16)],
        compiler_params=_cparams(("parallel", "arbitrary")),
        name="ssm_in_proj",
    )(xp, g.reshape(1, D), w)


def _ssm_core_kernel(u_ref, m_ref, bre_ref, bim_ref, cre_ref, cim_ref, are_ref, aim_ref,
                     d_ref, y_ref, sre_ref, sim_ref, ym_ref, *, n_chunk, n_pair):
    C = SSM_GROUP
    QC = N_PHASE * C
    nb = u_ref.shape[0]
    n_grp = 2 * n_pair

    def grp(i, b):
        return u_ref[b, :, i * C:(i + 1) * C, :]

    us = [jnp.concatenate([grp(i, b).reshape(QC, n_chunk) for b in range(nb)], axis=1)
          for i in range(n_grp)]

    P = bre_ref.shape[1]
    xre, xim = [], []
    for i in range(n_grp):
        lhs = jnp.concatenate([m_ref[i], bre_ref[i], bim_ref[i]], axis=0)
        r = jnp.dot(lhs, us[i], preferred_element_type=F32)
        ym_ref[i] = r[:QC]
        xre.append(r[QC:QC + P])
        xim.append(r[QC + P:])
    for k in range(n_pair):
        sre_ref[k] = jnp.concatenate(xre[2 * k:2 * k + 2], axis=0).T
        sim_ref[k] = jnp.concatenate(xim[2 * k:2 * k + 2], axis=0).T

    sub = lax.broadcasted_iota(jnp.int32, (8, are_ref.shape[-1]), 0)

    def cmul(xr, xi, yr, yi):
        return xr * yr - xi * yi, xr * yi + xi * yr

    for k in range(n_pair):
        a1 = (are_ref[k], aim_ref[k])
        a2 = cmul(*a1, *a1)
        a4 = cmul(*a2, *a2)
        pows = [a1, a2, cmul(*a2, *a1), a4, cmul(*a4, *a1), cmul(*a4, *a2)]
        pows += [cmul(*pows[5], *a1), cmul(*a4, *a4)]
        apow_r = jnp.zeros(sub.shape, F32)
        apow_i = jnp.zeros(sub.shape, F32)
        for i, (pr, pi) in enumerate(pows):
            apow_r = jnp.where(sub == i, pr, apow_r)
            apow_i = jnp.where(sub == i, pi, apow_i)
        step = {d: (jnp.where(sub >= d, pows[d - 1][0], 0.0), jnp.where(sub >= d, pows[d - 1][1], 0.0))
                for d in (1, 2, 4)}
        carry = [(jnp.zeros(sub.shape, F32), jnp.zeros(sub.shape, F32)) for _ in range(nb)]
        for j in range(n_chunk // 8):
            for b in range(nb):
                rows = slice(b * n_chunk + 8 * j, b * n_chunk + 8 * j + 8)
                er, ei = sre_ref[k, rows, :], sim_ref[k, rows, :]
                for d in (1, 2, 4):
                    dr, di = cmul(*step[d], pltpu.roll(er, d, axis=0), pltpu.roll(ei, d, axis=0))
                    er, ei = er + dr, ei + di
                cr, ci = carry[b]
                dr, di = cmul(apow_r, apow_i, cr, ci)
                fr, fi = er + dr, ei + di
                sre_ref[k, rows, :] = jnp.where(sub == 0, cr, pltpu.roll(fr, 1, axis=0))
                sim_ref[k, rows, :] = jnp.where(sub == 0, ci, pltpu.roll(fi, 1, axis=0))
                carry[b] = (jnp.broadcast_to(fr[7:8], sub.shape), jnp.broadcast_to(fi[7:8], sub.shape))

    nt = (((1,), (1,)), ((), ()))
    for k in range(n_pair):
        inter = (lax.dot_general(cre_ref[k], sre_ref[k].astype(BF16), nt, preferred_element_type=F32)
                 + lax.dot_general(cim_ref[k], sim_ref[k].astype(BF16), nt,
                                   preferred_element_type=F32))
        for i in (2 * k, 2 * k + 1):
            y = ym_ref[i] + inter[(i % 2) * QC:(i % 2 + 1) * QC]
            for b in range(nb):
                yb = y[:, b * n_chunk:(b + 1) * n_chunk].reshape(N_PHASE, C, n_chunk)
                yb = yb + d_ref[i * C:(i + 1) * C] * grp(i, b).astype(F32)
                y_ref[b, :, i * C:(i + 1) * C, :] = yb.astype(BF16)


def _ssm_core(UT, prm, *, n_pair=2):
    B, _, _, n_chunk = UT.shape
    N = B * n_chunk
    G2 = SSM_N_GROUPS // 2
    C2 = 2 * SSM_GROUP
    QC = N_PHASE * SSM_GROUP
    P = SSM_STATE
    blk = lambda *shape: pl.BlockSpec(shape, lambda g: (g,) + (0,) * (len(shape) - 1))
    return pl.pallas_call(
        functools.partial(_ssm_core_kernel, n_chunk=n_chunk, n_pair=n_pair),
        grid=(G2 // n_pair,),
        in_specs=[
            pl.BlockSpec((B, N_PHASE, n_pair * C2, n_chunk), lambda g: (0, 0, g, 0)),
            blk(2 * n_pair, QC, QC), blk(2 * n_pair, P, QC), blk(2 * n_pair, P, QC),
            blk(n_pair, 2 * QC, 2 * P), blk(n_pair, 2 * QC, 2 * P),
            blk(n_pair, 1, 2 * P), blk(n_pair, 1, 2 * P),
            blk(n_pair * C2, 1),
        ],
        out_specs=pl.BlockSpec((B, N_PHASE, n_pair * C2, n_chunk), lambda g: (0, 0, g, 0)),
        out_shape=jax.ShapeDtypeStruct((B, N_PHASE, SSM_N_GROUPS * SSM_GROUP, n_chunk), BF16),
        scratch_shapes=[pltpu.VMEM((n_pair, N, 2 * P), F32), pltpu.VMEM((n_pair, N, 2 * P), F32),
                        pltpu.VMEM((2 * n_pair, QC, N), F32)],
        compiler_params=_cparams(("parallel",)),
        name="ssm_core",
    )(UT, prm["m"], prm["b_re"], prm["b_im"], prm["c_re"], prm["c_im"],
      prm["a_re"], prm["a_im"], prm["d"])


def _ssm_param_stage_a(arc_ref, aic_ref, arr_ref, air_ref, ldr_ref, bre_ref, bim_ref):
    Q, C, P = N_PHASE, SSM_GROUP, SSM_STATE
    QC = Q * C
    hi = lax.Precision.HIGHEST
    tile = (lax.broadcasted_iota(jnp.int32, (C, QC), 1) % C
            == lax.broadcasted_iota(jnp.int32, (C, QC), 0)).astype(F32)

    dt = jnp.exp(ldr_ref[...])
    mag = jnp.exp(arr_ref[...] * dt)
    a1r, a1i = mag * jnp.cos(air_ref[...] * dt), mag * jnp.sin(air_ref[...] * dt)
    a1r_c = jnp.broadcast_to(a1r, (8, 2 * P)).T[:, :1]
    a1i_c = jnp.broadcast_to(a1i, (8, 2 * P)).T[:, :1]

    pows = [(jnp.ones_like(a1r), jnp.zeros_like(a1r))]
    for _ in range(Q):
        qr, qi = pows[-1]
        pows.append((qr * a1r - qi * a1i, qr * a1i + qi * a1r))

    bb_t = []
    for i in range(2):
        a_re, a_im = arc_ref[i], aic_ref[i]
        abr, abi = a1r_c[i * P:(i + 1) * P], a1i_c[i * P:(i + 1) * P]
        den = a_re * a_re + a_im * a_im
        cfr = ((abr - 1.0) * a_re + abi * a_im) / den
        cfi = (abi * a_re - (abr - 1.0) * a_im) / den
        bbr = cfr * bre_ref[i] - cfi * bim_ref[i]
        bbi = cfr * bim_ref[i] + cfi * bre_ref[i]
        bb_t.append((jnp.dot(bbr, tile, precision=hi, preferred_element_type=F32),
                     jnp.dot(bbi, tile, precision=hi, preferred_element_type=F32)))
    return pows, bb_t


def _ssm_param_stage_b(pows, bb_t, cre_ref, cim_ref, cpr_ref, cpi_ref,
                       m_ref, br_ref, bi_ref, cr_ref, ci_ref, ar_ref, ai_ref):
    Q, C, P = N_PHASE, SSM_GROUP, SSM_STATE
    QC = Q * C
    hi = lax.Precision.HIGHEST
    pw_r = jnp.concatenate([jnp.broadcast_to(pows[Q - 1 - s][0], (C, 2 * P)) for s in range(Q)], axis=0).T
    pw_i = jnp.concatenate([jnp.broadcast_to(pows[Q - 1 - s][1], (C, 2 * P)) for s in range(Q)], axis=0).T

    for i in range(2):
        bbr_t, bbi_t = bb_t[i]
        pr, pi = pw_r[i * P:(i + 1) * P], pw_i[i * P:(i + 1) * P]
        bst_r = pr * bbr_t - pi * bbi_t
        bst_i = pr * bbi_t + pi * bbr_t
        br_ref[i] = bst_r.astype(BF16)
        bi_ref[i] = bst_i.astype(BF16)

        kl = (jnp.dot(cre_ref[i], bst_r, precision=hi, preferred_element_type=F32)
              - jnp.dot(cim_ref[i], bst_i, precision=hi, preferred_element_type=F32))
        klz = jnp.concatenate([kl, jnp.zeros_like(kl)], axis=1)
        for t in range(Q):
            sh = (2 * QC - C * (Q - 1 - t)) % (2 * QC)
            row = klz if sh == 0 else pltpu.roll(klz, sh, axis=1)
            m_ref[i, t * C:(t + 1) * C, :] = row[:, :QC].astype(BF16)

    for t in range(Q):
        qr, qi = pows[t + 1]
        for i in range(2):
            rows = slice(i * QC + t * C, i * QC + (t + 1) * C)
            cr_ref[rows, :] = (cpr_ref[i] * qr - cpi_ref[i] * qi).astype(BF16)
            ci_ref[rows, :] = (-(cpr_ref[i] * qi + cpi_ref[i] * qr)).astype(BF16)
    ar_ref[...], ai_ref[...] = pows[Q]


def _ssm_out_kernel(y_ref, z_ref, wg_ref, bg_ref, wo_ref, gp_ref, x_ref, o_hbm, obuf, sem,
                    wgt_ref, wot_ref, *, nb, n_sub):
    s, j = pl.program_id(0), pl.program_id(1)
    nj = pl.num_programs(1)
    step = s * nj + j
    last = pl.num_programs(0) * nj - 1
    slot = step % 2

    def out_copy(slot_, s_, j_):
        return pltpu.make_async_copy(obuf.at[slot_], o_hbm.at[pl.ds(j_ * nb, nb), :, s_, :],
                                     sem.at[slot_])

    @pl.when(step >= 2)
    def _():
        out_copy(slot, (step - 2) // nj, (step - 2) % nj).wait()

    @pl.when(step == 0)
    def _():
        wgt_ref[...] = wg_ref[...].T.astype(BF16)
        wot_ref[...] = wo_ref[...].T.astype(BF16)

    bs = nb // n_sub

    def lanes(ref, k):
        return jnp.concatenate([ref[b] for b in range(k * bs, (k + 1) * bs)], axis=1).astype(F32)

    def glu_in(k):
        g = _gelu_tanh(lanes(y_ref, k))
        return g, jnp.dot(wgt_ref[...], g.astype(BF16), preferred_element_type=F32)

    def proj(k, g, gl):
        y2 = g * _sigmoid(gl + bg_ref[...])
        z = lanes(z_ref, k)
        gated = (y2 * (z * _sigmoid(z))).astype(BF16)
        return jnp.dot(wot_ref[...], gated, preferred_element_type=F32)

    def finish(k, ot):
        ms = jnp.mean(ot * ot, axis=0, keepdims=True)
        nt = (ot * lax.rsqrt(ms + EPS)).T
        rows = slice(k * bs, (k + 1) * bs)
        obuf[slot, rows] = x_ref[rows] + (nt * gp_ref[...]).reshape(bs, -1, nt.shape[-1])

    a = [glu_in(k) for k in range(n_sub)]
    o = [proj(k, *a[k]) for k in range(n_sub)]
    for k in range(n_sub):
        finish(k, o[k])
    out_copy(slot, s, j).start()

    @pl.when(step == last)
    def _():
        out_copy(slot, s, j).wait()

    @pl.when(jnp.logical_and(step == last, last >= 1))
    def _():
        out_copy(1 - slot, (step - 1) // nj, (step - 1) % nj).wait()


def _ssm_out(YT, UT, wg, bg, wo, gp, xp, *, nb=8, n_sub=2):
    B, _, M, D = xp.shape
    E = YT.shape[2]
    out = pl.pallas_call(
        functools.partial(_ssm_out_kernel, nb=nb, n_sub=n_sub),
        grid=(N_PHASE, B // nb),
        in_specs=[
            pl.BlockSpec((nb, None, E, M), lambda s, j: (j, s, 0, 0)),
            pl.BlockSpec((nb, None, E, M), lambda s, j: (j, s, 1, 0)),
            pl.BlockSpec((E, E), lambda s, j: (0, 0)),
            pl.BlockSpec((E, 1), lambda s, j: (0, 0)),
            pl.BlockSpec((E, D), lambda s, j: (0, 0)),
            pl.BlockSpec((1, D), lambda s, j: (0, 0)),
            pl.BlockSpec((nb, None, M, D), lambda s, j: (j, s, 0, 0)),
        ],
        out_specs=pl.BlockSpec(memory_space=pl.ANY),
        out_shape=jax.ShapeDtypeStruct((B, M, N_PHASE, D), F32),
        scratch_shapes=[pltpu.VMEM((2, nb, M, D), F32), pltpu.SemaphoreType.DMA((2,)),
                        pltpu.VMEM((E, E), BF16), pltpu.VMEM((D, E), BF16)],
        compiler_params=_cparams(("arbitrary", "arbitrary")),
        name="ssm_out",
    )(YT, UT, wg, bg.reshape(E, 1), wo, gp.reshape(1, D), xp)
    return out.reshape(B, M * N_PHASE, D)


def kernel(x, rel_bias, attn_pre_norm, attn_w_in, attn_w_out, attn_post_norm, ssm_pre_norm, ssm_w_in, ssm_a_re, ssm_a_im, ssm_log_dt, ssm_b_re, ssm_b_im, ssm_c_re, ssm_c_im, ssm_d, ssm_w_glu, ssm_b_glu, ssm_w_out, ssm_post_norm):
    P, xp = _norm_proj(x, attn_pre_norm[0], attn_w_in[0], scaled_cols=3 * HEADS * HEAD_DIM,
                       scale=HEAD_DIM ** -0.5 * LOG2E)
    O = _attention(P, _bias_tables(rel_bias))
    xp, prm = _out_proj(O, attn_w_out[0], attn_post_norm[0], xp, ssm_a_re[0], ssm_a_im[0],
                        ssm_log_dt[0], ssm_b_re[0], ssm_b_im[0], ssm_c_re[0], ssm_c_im[0], ssm_d[0])

    UT = _ssm_in_proj(xp, ssm_pre_norm[0], ssm_w_in[0])
    YT = _ssm_core(UT, prm)
    return _ssm_out(YT, UT, ssm_w_glu[0], ssm_b_glu[0], ssm_w_out[0], ssm_post_norm[0], xp)
```
